```python
import math
import jax, jax.numpy as jnp
from jax import lax
import numpy as np

D_MODEL = 1024
BATCH = 8
SEQ = 2048
DEPTH = 2

HEAD_DIM = 64
GLA_HEADS = 4
GLA_DK = 32
GLA_DV = 64
GLA_RANK = 16
GLA_TAU = 16.0
GLA_CHUNK = 64
DSA_HEADS = 4
IDX_HEADS = 8
IDX_DIM = 64
DSA_TOPK_MAX = 256
Q_BLOCK = 128
SWA_HEADS = 8
SWA_KV_HEADS = 2
SWA_GROUP = SWA_HEADS // SWA_KV_HEADS
WINDOW = 128
D_MIX = GLA_HEADS * GLA_DV + DSA_HEADS * HEAD_DIM + SWA_HEADS * HEAD_DIM
REL_BUCKETS = 32
REL_MAX_DIST = 128
BIAS_HEADS = DSA_HEADS + SWA_HEADS
N_GROUPS = 4
EXPERTS_PER_GROUP = 8
N_EXPERTS = N_GROUPS * EXPERTS_PER_GROUP
TOP_K_IN_GROUP = 2
D_EXPERT = 512
EXPERT_BLOCK = 128
PLE_DIM = 256
EPS = 1e-6
IN_SPLITS = (
    GLA_HEADS * GLA_DK, GLA_HEADS * GLA_DK, GLA_HEADS * GLA_DV, GLA_HEADS * GLA_DV, GLA_RANK,
    DSA_HEADS * HEAD_DIM, HEAD_DIM, HEAD_DIM, IDX_HEADS * IDX_DIM, IDX_DIM, IDX_HEADS,
    SWA_HEADS * HEAD_DIM, SWA_KV_HEADS * HEAD_DIM, SWA_KV_HEADS * HEAD_DIM,
)
D_IN = sum(IN_SPLITS)

kernel_name = 'hybrid_gla_dsa_swa_hmoe_block'


def rms_norm(x, g):
    xf = x.astype(jnp.float32)
    y = xf * lax.rsqrt(jnp.mean(xf * xf, axis=-1, keepdims=True) + EPS)
    return (y * g.astype(jnp.float32)).astype(x.dtype)


def rel_bucket(dist):
    n = jnp.maximum(dist, 0)
    max_exact = REL_BUCKETS // 2
    nf = jnp.maximum(n, 1).astype(jnp.float32)
    large = max_exact + (jnp.log(nf / max_exact) / math.log(REL_MAX_DIST / max_exact)
                         * (REL_BUCKETS - max_exact)).astype(jnp.int32)
    large = jnp.minimum(large, REL_BUCKETS - 1)
    return jnp.where(n < max_exact, n, large)


def gla_mixer(q, k, v, g, a_lr, w_alpha2, b_alpha, g_norm):
    f32 = jnp.float32
    B, L, _ = q.shape
    nc = L // GLA_CHUNK
    log_a = jax.nn.log_sigmoid((a_lr @ w_alpha2 + b_alpha).astype(f32)) / GLA_TAU

    def heads(t, d):
        t = t.astype(f32).reshape(B, nc, GLA_CHUNK, GLA_HEADS, d)
        return jnp.transpose(t, (1, 0, 3, 2, 4))

    qc = heads(q, GLA_DK) * GLA_DK ** -0.5
    kc = heads(k, GLA_DK)
    vc = heads(v, GLA_DV)
    bc = jnp.cumsum(heads(log_a, GLA_DK), axis=-2)
    causal = jnp.tril(jnp.ones((GLA_CHUNK, GLA_CHUNK), bool))[..., None]

    def step(state, inp):
        qi, ki, vi, bi = inp
        o_inter = jnp.einsum('bhtk,bhkv->bhtv', qi * jnp.exp(bi), state)
        diff = bi[:, :, :, None, :] - bi[:, :, None, :, :]
        decay = jnp.where(causal, jnp.exp(jnp.where(causal, diff, 0.0)), 0.0)
        scores = jnp.einsum('bhtk,bhsk,bhtsk->bhts', qi, ki, decay)
        o = o_inter + jnp.einsum('bhts,bhsv->bhtv', scores, vi)
        b_last = bi[:, :, -1:, :]
        state = (jnp.exp(b_last[:, :, 0, :])[..., None] * state
                 + jnp.einsum('bhsk,bhsv->bhkv', ki * jnp.exp(b_last - bi), vi))
        return state, o

    state0 = jnp.zeros((B, GLA_HEADS, GLA_DK, GLA_DV), f32)
    _, o = lax.scan(step, state0, (qc, kc, vc, bc))
    o = jnp.transpose(o, (1, 0, 3, 2, 4)).reshape(B, L, GLA_HEADS, GLA_DV)
    o = o * lax.rsqrt(jnp.mean(o * o, axis=-1, keepdims=True) + EPS) * g_norm.astype(f32)
    o = o.reshape(B, L, GLA_HEADS * GLA_DV) * jax.nn.silu(g.astype(f32))
    return o.astype(q.dtype)


def dsa_mixer(q, k, v, q_idx, k_idx, w_idx, rel_bias_dsa):
    f32 = jnp.float32
    B, L, _ = q.shape
    n_sel = min(DSA_TOPK_MAX, L // 4)
    nb = L // Q_BLOCK
    q = q.reshape(B, L, DSA_HEADS, HEAD_DIM) * HEAD_DIM ** -0.5
    q_idx = q_idx.reshape(B, L, IDX_HEADS, IDX_DIM)
    w_idx = w_idx * (IDX_HEADS ** -0.5 * IDX_DIM ** -0.5)

    def blocks(t):
        return jnp.swapaxes(t.reshape((B, nb, Q_BLOCK) + t.shape[2:]), 0, 1)

    key_pos = jnp.arange(L)
    gather = jax.vmap(lambda table, idx: table[idx])

    def block(inp):
        j, qb, qib, wb = inp
        t_pos = j * Q_BLOCK + jnp.arange(Q_BLOCK)
        rel = jax.nn.relu(jnp.einsum('bthd,bsd->bths', qib, k_idx))
        score = jnp.einsum('bth,bths->bts', wb, rel).astype(f32)
        admissible = key_pos[None, :] <= t_pos[:, None]
        score = jnp.where(admissible[None], score, -jnp.inf)
        _, sel = lax.top_k(score, n_sel)
        dist = t_pos[None, :, None] - sel
        valid = dist >= 0
        kg = gather(k, sel)
        vg = gather(v, sel)
        logits = jnp.einsum('bthd,btkd->bhtk', qb, kg).astype(f32)
        bias = jnp.moveaxis(rel_bias_dsa[rel_bucket(dist)], -1, 1)
        logits = jnp.where(valid[:, None], logits + bias, -jnp.inf)
        probs = jax.nn.softmax(logits, axis=-1).astype(vg.dtype)
        return jnp.einsum('bhtk,btkd->bthd', probs, vg)

    out = lax.map(block, (jnp.arange(nb), blocks(q), blocks(q_idx), blocks(w_idx)))
    return jnp.swapaxes(out, 0, 1).reshape(B, L, DSA_HEADS * HEAD_DIM)


def swa_mixer(q, k, v, sinks, rel_bias_swa):
    f32 = jnp.float32
    B, L, _ = q.shape
    nb = L // WINDOW
    qb = q.reshape(B, nb, WINDOW, SWA_KV_HEADS, SWA_GROUP, HEAD_DIM) * HEAD_DIM ** -0.5

    def band(t):
        t = t.reshape(B, nb, WINDOW, SWA_KV_HEADS, HEAD_DIM)
        prev = jnp.pad(t[:, :-1], ((0, 0), (1, 0), (0, 0), (0, 0), (0, 0)))
        return jnp.concatenate([prev, t], axis=2)

    kw, vw = band(k), band(v)
    qi = jnp.arange(WINDOW)[:, None]
    kj = jnp.arange(2 * WINDOW)[None, :]
    dist = qi + WINDOW - kj
    in_window = (dist >= 0) & (dist < WINDOW)
    key_pos = jnp.arange(nb)[:, None, None] * WINDOW - WINDOW + kj[None]
    mask = in_window[None] & (key_pos >= 0)
    bias = jnp.transpose(rel_bias_swa[rel_bucket(dist)], (2, 0, 1)).reshape(
        SWA_KV_HEADS, SWA_GROUP, WINDOW, 2 * WINDOW)
    logits = jnp.einsum('bnqhgd,bnkhd->bnhgqk', qb, kw).astype(f32) + bias.astype(f32)
    logits = jnp.where(mask[None, :, None, None], logits, -jnp.inf)
    sink = sinks.astype(f32).reshape(SWA_KV_HEADS, SWA_GROUP)[None, None, :, :, None, None]
    m = jnp.maximum(jnp.max(logits, axis=-1, keepdims=True), sink)
    e = jnp.exp(logits - m)
    probs = e / (jnp.sum(e, axis=-1, keepdims=True) + jnp.exp(sink - m))
    out = jnp.einsum('bnhgqk,bnkhd->bnqhgd', probs.astype(vw.dtype), vw)
    return out.reshape(B, L, SWA_HEADS * HEAD_DIM)


def hier_moe(x, w_rg, b_rg, w_re, b_re, w_gate, w_up, w_down):
    f32 = jnp.float32
    B, L, D = x.shape
    T = B * L
    xt = x.reshape(T, D)
    g_prob = jax.nn.softmax((xt @ w_rg + b_rg).astype(f32), axis=-1)
    g_w, g_sel = lax.top_k(g_prob, 1)
    e_logits = (xt @ w_re + b_re).astype(f32).reshape(T, N_GROUPS, EXPERTS_PER_GROUP)
    e_logits = jnp.take_along_axis(e_logits, g_sel[:, :, None], axis=1)[:, 0]
    e_w, e_sel = lax.top_k(jax.nn.softmax(e_logits, axis=-1), TOP_K_IN_GROUP)
    gate = g_w * e_w / jnp.sum(e_w, axis=-1, keepdims=True)
    expert = g_sel * EXPERTS_PER_GROUP + e_sel
    flat_e = expert.reshape(-1)
    order = jnp.argsort(flat_e)
    e_sorted = flat_e[order]
    tok = order // TOP_K_IN_GROUP
    counts = jnp.bincount(flat_e, length=N_EXPERTS)
    padded = (counts + EXPERT_BLOCK - 1) // EXPERT_BLOCK * EXPERT_BLOCK
    start = jnp.cumsum(counts) - counts
    pad_end = jnp.cumsum(padded)
    pad_start = pad_end - padded
    rank = jnp.arange(flat_e.shape[0])
    row = pad_start[e_sorted] + rank - start[e_sorted]
    n_blocks = -(-(T * TOP_K_IN_GROUP) // EXPERT_BLOCK) + N_EXPERTS
    buf = jnp.zeros((n_blocks * EXPERT_BLOCK, D), x.dtype).at[row].set(xt[tok])
    blk_expert = jnp.minimum(
        jnp.searchsorted(pad_end, jnp.arange(n_blocks) * EXPERT_BLOCK, side='right'), N_EXPERTS - 1)

    def expert_block(inp):
        xb, e = inp
        hmid = jax.nn.silu(xb @ w_gate[e]) * (xb @ w_up[e])
        return hmid @ w_down[e]

    yb = lax.map(expert_block, (buf.reshape(n_blocks, EXPERT_BLOCK, D), blk_expert))
    y = yb.reshape(-1, D)[row] * gate.reshape(-1)[order][:, None].astype(x.dtype)
    return jnp.zeros((T, D), x.dtype).at[tok].add(y).reshape(B, L, D)


def setup_inputs(seed: int = 0) -> dict:
    key = jax.random.key(seed)
    ks = jax.random.split(key, 24)
    f32 = jnp.float32

    def nrm(k, shape, scale):
        return jax.random.normal(k, shape, f32) * scale

    return {
        'x': nrm(ks[0], (BATCH, SEQ, D_MODEL), 1.0),
        'p': nrm(ks[1], (DEPTH, BATCH, SEQ, PLE_DIM), 1.0),
        'rel_bias': nrm(ks[2], (REL_BUCKETS, BIAS_HEADS), 0.5),
        'g_mix': 1.0 + nrm(ks[3], (DEPTH, D_MODEL), 0.05),
        'w_in': nrm(ks[4], (DEPTH, D_MODEL, D_IN), D_MODEL ** -0.5),
        'gla_w_alpha': nrm(ks[5], (DEPTH, GLA_RANK, GLA_HEADS * GLA_DK), GLA_RANK ** -0.5),
        'gla_b_alpha': nrm(ks[6], (DEPTH, GLA_HEADS * GLA_DK), 0.1),
        'gla_g_norm': 1.0 + nrm(ks[7], (DEPTH, GLA_DV), 0.05),
        'swa_sinks': nrm(ks[8], (DEPTH, SWA_HEADS), 0.5),
        'w_out': nrm(ks[9], (DEPTH, D_MIX, D_MODEL), D_MIX ** -0.5),
        'g_ffn': 1.0 + nrm(ks[10], (DEPTH, D_MODEL), 0.05),
        'w_router_group': nrm(ks[11], (DEPTH, D_MODEL, N_GROUPS), D_MODEL ** -0.5),
        'b_router_group': nrm(ks[12], (DEPTH, N_GROUPS), 0.01),
        'w_router_expert': nrm(ks[13], (DEPTH, D_MODEL, N_EXPERTS), D_MODEL ** -0.5),
        'b_router_expert': nrm(ks[14], (DEPTH, N_EXPERTS), 0.01),
        'w_expert_gate': nrm(ks[15], (DEPTH, N_EXPERTS, D_MODEL, D_EXPERT), D_MODEL ** -0.5),
        'w_expert_up': nrm(ks[16], (DEPTH, N_EXPERTS, D_MODEL, D_EXPERT), D_MODEL ** -0.5),
        'w_expert_down': nrm(ks[17], (DEPTH, N_EXPERTS, D_EXPERT, D_MODEL), D_EXPERT ** -0.5),
        'w_ple': nrm(ks[18], (DEPTH, PLE_DIM, D_MODEL), PLE_DIM ** -0.5),
        'g_ple': 1.0 + nrm(ks[19], (DEPTH, D_MODEL), 0.05),
        'w_ple_gate': nrm(ks[20], (DEPTH, D_MODEL, D_MODEL), D_MODEL ** -0.5),
        'g_final': 1.0 + nrm(ks[21], (D_MODEL,), 0.05),
    }


def reference(x, p, rel_bias, g_mix, w_in, gla_w_alpha, gla_b_alpha, gla_g_norm, swa_sinks,
              w_out, g_ffn, w_router_group, b_router_group, w_router_expert, b_router_expert,
              w_expert_gate, w_expert_up, w_expert_down, w_ple, g_ple, w_ple_gate, g_final):
    splits = np.cumsum(IN_SPLITS)[:-1].tolist()
    bias_dsa = rel_bias[:, :DSA_HEADS]
    bias_swa = rel_bias[:, DSA_HEADS:]
    h = x
    for i in range(DEPTH):
        a = rms_norm(h, g_mix[i])
        (gq, gk, gv, gg, ga, dq, dk, dv, iq, ik, iw, sq, sk, sv) = jnp.split(a @ w_in[i], splits, axis=-1)
        o = jnp.concatenate([
            gla_mixer(gq, gk, gv, gg, ga, gla_w_alpha[i], gla_b_alpha[i], gla_g_norm[i]),
            dsa_mixer(dq, dk, dv, iq, ik, iw, bias_dsa),
            swa_mixer(sq, sk, sv, swa_sinks[i], bias_swa),
        ], axis=-1)
        h = h + o @ w_out[i]
        h = h + hier_moe(rms_norm(h, g_ffn[i]), w_router_group[i], b_router_group[i],
                         w_router_expert[i], b_router_expert[i],
                         w_expert_gate[i], w_expert_up[i], w_expert_down[i])
        e = rms_norm(p[i] @ w_ple[i], g_ple[i])
        h = h + e * jax.nn.sigmoid(h @ w_ple_gate[i])
    return rms_norm(h, g_final)
```

```python
import functools
import math

import jax
import jax.numpy as jnp
from jax import lax
from jax.experimental import pallas as pl
from jax.experimental.pallas import tpu as pltpu

F32 = jnp.float32
BF16 = jnp.bfloat16
HIGHEST = lax.Precision.HIGHEST

D_MODEL = 1024
HEAD_DIM = 64
GLA_HEADS = 4
GLA_DK = 32
GLA_DV = 64
GLA_RANK = 16
GLA_TAU = 16.0
GLA_CHUNK = 64
DSA_HEADS = 4
IDX_HEADS = 8
IDX_DIM = 64
DSA_TOPK_MAX = 256
SWA_HEADS = 8
SWA_KV_HEADS = 2
SWA_GROUP = SWA_HEADS // SWA_KV_HEADS
WINDOW = 128
REL_BUCKETS = 32
REL_MAX_DIST = 128
N_GROUPS = 4
EXPERTS_PER_GROUP = 8
N_EXPERTS = N_GROUPS * EXPERTS_PER_GROUP
D_EXPERT = 512
PLE_DIM = 256
EPS = 1e-6

LANES = 128
SUBLANES = 8

COL_IQ, COL_SQ = 0, 512
COL_GV, COL_GG, COL_DQ = 1024, 1280, 1536
COL_GQ, COL_GK, COL_GA, COL_DKV, COL_IKW, COL_SK, COL_SV = 1792, 1920, 2048, 2176, 2304, 2432, 2560
D_PROJ = 2688

TOKEN_TILE = 256
EXPERT_ROWS = 256
GLA_BLOCK = 256
QB = 128
NEG_BIG = -1e30
VMEM_LIMIT = 48 * 1024 * 1024


def _nt(a, b, precision=None):
    return lax.dot_general(a, b, (((1,), (1,)), ((), ())), precision=precision,
                           preferred_element_type=F32)


def _mm(a, b, precision=None):
    return jnp.dot(a, b, precision=precision, preferred_element_type=F32)


def _eye(n, dtype):
    r = lax.broadcasted_iota(jnp.int32, (n, n), 0)
    c = lax.broadcasted_iota(jnp.int32, (n, n), 1)
    return jnp.where(r == c, 1.0, 0.0).astype(dtype)


def _rms(x, g):
    return x * lax.rsqrt(jnp.mean(x * x, axis=-1, keepdims=True) + EPS) * g


def _inproj_kernel(h_ref, g_ref, w_ref, o_ref):
    a = _rms(h_ref[...], g_ref[...])
    o_ref[...] = _mm(a.astype(BF16), w_ref[...])


def _inproj(h, g, w):
    t = h.shape[0]
    return pl.pallas_call(
        _inproj_kernel,
        grid=(t // TOKEN_TILE,),
        in_specs=[
            pl.BlockSpec((TOKEN_TILE, D_MODEL), lambda i: (i, 0)),
            pl.BlockSpec((1, D_MODEL), lambda i: (0, 0)),
            pl.BlockSpec((D_MODEL, D_PROJ), lambda i: (0, 0)),
        ],
        out_specs=pl.BlockSpec((TOKEN_TILE, D_PROJ), lambda i: (i, 0)),
        out_shape=jax.ShapeDtypeStruct((t, D_PROJ), F32),
        compiler_params=pltpu.CompilerParams(
            dimension_semantics=("arbitrary",), vmem_limit_bytes=VMEM_LIMIT),
        name="inproj",
    )(h, g, w)


def _gla_kernel(q_ref, k_ref, v_ref, gg_ref, ga_ref, wal_ref, bal_ref, gn_ref, o_ref,
                state_ref, oc_ref):
    hk = GLA_HEADS * GLA_DK
    hv = GLA_HEADS * GLA_DV
    c = GLA_CHUNK

    @pl.when(pl.program_id(1) == 0)
    def _():
        state_ref[...] = jnp.zeros_like(state_ref)

    r64 = lax.broadcasted_iota(jnp.int32, (c, c), 0)
    c64 = lax.broadcasted_iota(jnp.int32, (c, c), 1)
    tril = jnp.where(r64 >= c64, 1.0, 0.0).astype(F32)
    rk = lax.broadcasted_iota(jnp.int32, (hk, hv), 0) // GLA_DK
    cv = lax.broadcasted_iota(jnp.int32, (hk, hv), 1) // GLA_DV
    blockdiag = jnp.where(rk == cv, 1.0, 0.0).astype(F32)
    expand = blockdiag.astype(BF16)
    eye_k = _eye(hk, F32)
    eye_kb = eye_k.astype(BF16)
    row8 = lax.broadcasted_iota(jnp.int32, (SUBLANES, hk), 0)

    def chunk(ci, carry):
        rows = pl.ds(pl.multiple_of(ci * c, c), c)
        q = q_ref[0, rows, :] * (GLA_DK ** -0.5)
        k = k_ref[0, rows, :]
        v = v_ref[0, rows, :]
        z = _mm(ga_ref[0, rows, :], wal_ref[...], HIGHEST) + bal_ref[...]
        log_a = (jnp.minimum(z, 0.0) - jnp.log1p(jnp.exp(-jnp.abs(z)))) * (1.0 / GLA_TAU)
        b = _mm(tril, log_a, HIGHEST)
        state = state_ref[...]
        o_inter = _mm((q * jnp.exp(b)).astype(BF16), state.astype(BF16))

        for t in range(c):
            ns = SUBLANES * (t // SUBLANES + 1)
            d = b[t:t + 1, :] - b[:ns, :]
            p = jnp.exp(jnp.minimum(d, 0.0)) * k[:ns, :] * q[t:t + 1, :]
            if t % SUBLANES != SUBLANES - 1:
                srow = lax.broadcasted_iota(jnp.int32, (ns, hk), 0)
                p = jnp.where(srow <= t, p, 0.0)
            sc = _mm(p.astype(BF16), expand)
            oc_ref[t:t + 1, :] = jnp.sum(sc * v[:ns, :], axis=0, keepdims=True)
        o = o_inter + oc_ref[...]

        b_last = b[c - 1:c, :]
        kd = (k * jnp.exp(b_last - b)).astype(BF16)
        kd_t = _nt(eye_kb, kd).astype(BF16)
        upd = _mm(kd_t, v.astype(BF16))
        bl8 = jnp.where(row8 >= 0, b_last, 0.0)
        bl_t = _nt(eye_k, bl8, HIGHEST)[:, 0:1]
        state_ref[...] = jnp.exp(bl_t) * state + upd * blockdiag

        gg = gg_ref[0, rows, :]
        outs = []
        for h in range(GLA_HEADS):
            oh = o[:, h * GLA_DV:(h + 1) * GLA_DV]
            ms = jnp.mean(oh * oh, axis=-1, keepdims=True)
            outs.append(oh * lax.rsqrt(ms + EPS))
        on = jnp.concatenate(outs, axis=-1) * gn_ref[...]
        o_ref[0, rows, :] = (on * (gg * jax.nn.sigmoid(gg))).astype(o_ref.dtype)
        return carry

    lax.fori_loop(0, GLA_BLOCK // c, chunk, 0)


def _gla(proj3, wal, bal, gn):
    bsz, seq, _ = proj3.shape
    nb = seq // GLA_BLOCK

    def col(width, off):
        return pl.BlockSpec((1, GLA_BLOCK, width), lambda b, i: (b, i, off // width))

    return pl.pallas_call(
        _gla_kernel,
        grid=(bsz, nb),
        in_specs=[
            col(128, COL_GQ), col(128, COL_GK), col(256, COL_GV), col(256, COL_GG), col(128, COL_GA),
            pl.BlockSpec((128, 128), lambda b, i: (0, 0)),
            pl.BlockSpec((1, 128), lambda b, i: (0, 0)),
            pl.BlockSpec((1, 256), lambda b, i: (0, 0)),
        ],
        out_specs=pl.BlockSpec((1, GLA_BLOCK, 256), lambda b, i: (b, i, 0)),
        out_shape=jax.ShapeDtypeStruct((bsz, seq, GLA_HEADS * GLA_DV), BF16),
        scratch_shapes=[
            pltpu.VMEM((GLA_HEADS * GLA_DK, GLA_HEADS * GLA_DV), F32),
            pltpu.VMEM((GLA_CHUNK, GLA_HEADS * GLA_DV), F32),
        ],
        compiler_params=pltpu.CompilerParams(
            dimension_semantics=("arbitrary", "arbitrary"), vmem_limit_bytes=VMEM_LIMIT),
        name="gla",
    )(proj3, proj3, proj3, proj3, proj3, wal, bal, gn)


DSA_BISECT_STEPS = 24


def _dsa_kernel(dq_ref, kv_ref, iq_ref, ikw_ref, btile_ref, o_ref, sc_ref, vt_ref, cut_ref,
                *, n_sel):
    j = pl.program_id(1)
    nkb = j + 1
    seq = kv_ref.shape[1]
    ksel = float(n_sel)

    def rows(i):
        return pl.ds(pl.multiple_of(i * QB, QB), QB)

    @pl.when(j == 0)
    def _():
        eye_v = _eye(HEAD_DIM, BF16)
        for cidx in range(seq // 512):
            vblk = kv_ref[0, cidx * 512:(cidx + 1) * 512, HEAD_DIM:2 * HEAD_DIM].astype(BF16)
            vt_ref[:, cidx * 512:(cidx + 1) * 512] = _nt(eye_v, vblk).astype(BF16)

    ikw_q = ikw_ref[0, rows(j), :]
    sel_r = lax.broadcasted_iota(jnp.int32, (SUBLANES, LANES), 0)
    sel_c = lax.broadcasted_iota(jnp.int32, (SUBLANES, LANES), 1)
    pick_w = jnp.where(sel_c == sel_r + IDX_DIM, 1.0, 0.0).astype(F32)
    w_t = _nt(pick_w, ikw_q, HIGHEST) * (IDX_HEADS ** -0.5 * IDX_DIM ** -0.5)
    iq = iq_ref[0].astype(BF16)

    def score_block(i, carry):
        ik = ikw_ref[0, rows(i), :][:, :IDX_DIM].astype(BF16)
        acc = jnp.zeros((QB, QB), F32)
        for h in range(IDX_HEADS):
            rel = _nt(ik, iq[:, h * IDX_DIM:(h + 1) * IDX_DIM])
            acc = acc + jnp.maximum(rel, 0.0) * w_t[h:h + 1, :]
        sc_ref[rows(i), :] = acc
        return carry

    lax.fori_loop(0, nkb, score_block, 0)
    s_loc = lax.broadcasted_iota(jnp.int32, (QB, QB), 0)
    t_loc = lax.broadcasted_iota(jnp.int32, (QB, QB), 1)
    sc_ref[rows(j), :] = jnp.where(s_loc <= t_loc, sc_ref[rows(j), :], -jnp.inf)

    def fold(x):
        return jnp.sum(x.reshape(QB // SUBLANES, SUBLANES, QB), axis=0)

    def count(pred):
        def body(i, c8):
            return c8 + fold(jnp.where(pred(sc_ref[rows(i), :]), 1.0, 0.0))
        return jnp.sum(lax.fori_loop(0, nkb, body, jnp.zeros((SUBLANES, QB), F32)),
                       axis=0, keepdims=True)

    def min_where(pred):
        def body(i, m8):
            blk = sc_ref[rows(i), :]
            x = jnp.where(pred(blk), blk, jnp.inf)
            return jnp.minimum(m8, jnp.min(x.reshape(QB // SUBLANES, SUBLANES, QB), axis=0))
        return jnp.min(lax.fori_loop(0, nkb, body, jnp.full((SUBLANES, QB), jnp.inf, F32)),
                       axis=0, keepdims=True)

    def max_all():
        def body(i, m8):
            blk = sc_ref[rows(i), :]
            return jnp.maximum(m8, jnp.max(blk.reshape(QB // SUBLANES, SUBLANES, QB), axis=0))
        return jnp.max(lax.fori_loop(0, nkb, body, jnp.full((SUBLANES, QB), -jnp.inf, F32)),
                       axis=0, keepdims=True)

    lo0 = min_where(lambda blk: blk > -jnp.inf)
    hi0 = max_all()

    def bisect(_, lh):
        lo, hi = lh
        mid = lo + (hi - lo) * 0.5
        ok = count(lambda blk: blk >= mid) >= ksel
        return jnp.where(ok, mid, lo), jnp.where(ok, hi, mid)

    lo, _ = lax.fori_loop(0, DSA_BISECT_STEPS, bisect, (lo0, hi0))

    v0 = min_where(lambda blk: blk >= lo)
    cgt0 = count(lambda blk: blk > v0)

    def walk_cond(st):
        _, cgt = st
        return jnp.max(cgt) >= ksel

    def walk_body(st):
        v, cgt = st
        nv = min_where(lambda blk: blk > v)
        v = jnp.where(cgt >= ksel, nv, v)
        return v, count(lambda blk: blk > v)

    tau, cgt = lax.while_loop(walk_cond, walk_body, (v0, cgt0))

    cge = count(lambda blk: blk >= tau)
    need = ksel - cgt
    cut_ref[...] = jnp.full(cut_ref.shape, float(seq), F32)

    @pl.when(jnp.max(cge) > ksel)
    def _():
        def count_ties_below(m):
            def body(i, c8):
                blk = sc_ref[rows(i), :]
                sidx = (s_loc + i * QB).astype(F32)
                return c8 + fold(jnp.where((blk == tau) & (sidx < m), 1.0, 0.0))
            return jnp.sum(lax.fori_loop(0, nkb, body, jnp.zeros((SUBLANES, QB), F32)),
                           axis=0, keepdims=True)

        def idx_bisect(_, lh):
            lo_m, hi_m = lh
            mid = jnp.floor((lo_m + hi_m) * 0.5)
            ok = count_ties_below(mid) >= need
            return jnp.where(ok, lo_m, mid), jnp.where(ok, mid, hi_m)

        zero = jnp.zeros((1, QB), F32)
        _, hi_m = lax.fori_loop(0, int(math.log2(seq)) + 1, idx_bisect,
                                (zero, zero + float(seq)))
        cut_ref[0:1, :] = jnp.where(cge > ksel, hi_m, float(seq))

    cut = cut_ref[0:1, :]

    q = (dq_ref[0] * (HEAD_DIM ** -0.5)).astype(BF16)
    eye_q = _eye(QB, BF16)
    outs = []
    for h in range(DSA_HEADS):
        qh = q[:, h * HEAD_DIM:(h + 1) * HEAD_DIM]

        def att(i, st, qh=qh, h=h):
            m, l, acc = st
            blk = sc_ref[rows(i), :]
            sidx = (s_loc + i * QB).astype(F32)
            sel = (blk > tau) | ((blk == tau) & (sidx < cut))
            kb = kv_ref[0, rows(i), :][:, :HEAD_DIM].astype(BF16)
            which = jnp.maximum(i - j + 2, 0)
            lg = _nt(kb, qh) + btile_ref[h, which]
            lg = jnp.where(sel, lg, NEG_BIG)
            m_new = jnp.maximum(m, jnp.max(lg, axis=0, keepdims=True))
            alpha = jnp.exp(m - m_new)
            p = jnp.exp(lg - m_new)
            l = alpha * l + jnp.sum(p, axis=0, keepdims=True)
            acc = alpha * acc + _mm(vt_ref[:, rows(i)], p.astype(BF16))
            return m_new, l, acc

        m0 = jnp.full((1, QB), NEG_BIG, F32)
        _, l, acc = lax.fori_loop(
            0, nkb, att, (m0, jnp.zeros((1, QB), F32), jnp.zeros((HEAD_DIM, QB), F32)))
        o_t = (acc / l).astype(BF16)
        outs.append(_nt(eye_q, o_t))
    o_ref[0] = jnp.concatenate(outs, axis=-1).astype(o_ref.dtype)


def _dsa(proj3, btiles):
    bsz, seq, _ = proj3.shape
    n_sel = min(DSA_TOPK_MAX, seq // 4)
    return pl.pallas_call(
        functools.partial(_dsa_kernel, n_sel=n_sel),
        grid=(bsz, seq // QB),
        in_specs=[
            pl.BlockSpec((1, QB, 256), lambda b, j: (b, j, COL_DQ // 256)),
            pl.BlockSpec((1, seq, 128), lambda b, j: (b, 0, COL_DKV // 128)),
            pl.BlockSpec((1, QB, 512), lambda b, j: (b, j, COL_IQ // 512)),
            pl.BlockSpec((1, seq, 128), lambda b, j: (b, 0, COL_IKW // 128)),
            pl.BlockSpec((DSA_HEADS, 3, QB, QB), lambda b, j: (0, 0, 0, 0)),
        ],
        out_specs=pl.BlockSpec((1, QB, 256), lambda b, j: (b, j, 0)),
        out_shape=jax.ShapeDtypeStruct((bsz, seq, DSA_HEADS * HEAD_DIM), BF16),
        scratch_shapes=[
            pltpu.VMEM((seq, QB), F32),
            pltpu.VMEM((HEAD_DIM, seq), BF16),
            pltpu.VMEM((SUBLANES, QB), F32),
        ],
        compiler_params=pltpu.CompilerParams(
            dimension_semantics=("arbitrary", "arbitrary"), vmem_limit_bytes=VMEM_LIMIT),
        name="dsa",
    )(proj3, proj3, proj3, proj3, btiles)


def _swa_kernel(sink_ref, q_ref, kc_ref, kp_ref, vc_ref, vp_ref, bias_ref, o_ref):
    n = pl.program_id(1)
    q = (q_ref[0] * (HEAD_DIM ** -0.5)).astype(BF16)
    k2 = jnp.concatenate([kp_ref[0], kc_ref[0]], axis=0).astype(BF16)
    v2 = jnp.concatenate([vp_ref[0], vc_ref[0]], axis=0).astype(BF16)
    qi = lax.broadcasted_iota(jnp.int32, (WINDOW, 2 * WINDOW), 0)
    kj = lax.broadcasted_iota(jnp.int32, (WINDOW, 2 * WINDOW), 1)
    dist = qi + WINDOW - kj
    mask = (dist >= 0) & (dist < WINDOW) & ((kj >= WINDOW) | (n > 0))
    outs = []
    for h in range(SWA_HEADS):
        kvh = h // SWA_GROUP
        kh = k2[:, kvh * HEAD_DIM:(kvh + 1) * HEAD_DIM]
        vh = v2[:, kvh * HEAD_DIM:(kvh + 1) * HEAD_DIM]
        lg = _nt(q[:, h * HEAD_DIM:(h + 1) * HEAD_DIM], kh) + bias_ref[h]
        lg = jnp.where(mask, lg, -jnp.inf)
        sink = sink_ref[h]
        m = jnp.maximum(jnp.max(lg, axis=-1, keepdims=True), sink)
        e = jnp.exp(lg - m)
        den = jnp.sum(e, axis=-1, keepdims=True) + jnp.exp(sink - m)
        outs.append(_mm((e / den).astype(BF16), vh))
    o_ref[0] = jnp.concatenate(outs, axis=-1).astype(o_ref.dtype)


def _swa(proj3, sinks, bias_nat):
    bsz, seq, _ = proj3.shape
    return pl.pallas_call(
        _swa_kernel,
        grid_spec=pltpu.PrefetchScalarGridSpec(
            num_scalar_prefetch=0,
            grid=(bsz, seq // WINDOW),
            in_specs=[
                pl.BlockSpec(memory_space=pltpu.SMEM),
                pl.BlockSpec((1, WINDOW, 512), lambda b, n: (b, n, COL_SQ // 512)),
                pl.BlockSpec((1, WINDOW, 128), lambda b, n: (b, n, COL_SK // 128)),
                pl.BlockSpec((1, WINDOW, 128), lambda b, n: (b, jnp.maximum(n - 1, 0), COL_SK // 128)),
                pl.BlockSpec((1, WINDOW, 128), lambda b, n: (b, n, COL_SV // 128)),
                pl.BlockSpec((1, WINDOW, 128), lambda b, n: (b, jnp.maximum(n - 1, 0), COL_SV // 128)),
                pl.BlockSpec((SWA_HEADS, WINDOW, 2 * WINDOW), lambda b, n: (0, 0, 0)),
            ],
            out_specs=pl.BlockSpec((1, WINDOW, 512), lambda b, n: (b, n, 0)),
        ),
        out_shape=jax.ShapeDtypeStruct((bsz, seq, SWA_HEADS * HEAD_DIM), BF16),
        compiler_params=pltpu.CompilerParams(
            dimension_semantics=("arbitrary", "arbitrary"), vmem_limit_bytes=VMEM_LIMIT),
        name="swa",
    )(sinks, proj3, proj3, proj3, proj3, proj3, bias_nat)


ROUTE_OFF = N_GROUPS


def _split_bf16(x):
    hi = x.astype(BF16)
    lo = (x - hi.astype(F32)).astype(BF16)
    return hi, lo


def _outproj_router_kernel(h_ref, og_ref, od_ref, os_ref, wo_ref, gffn_ref, wr_hi_ref, wr_lo_ref,
                           br_ref, h1_ref, xn_ref, ri_ref, rf_ref, cnt_ref, run_ref):
    tm = h_ref.shape[0]

    @pl.when(pl.program_id(0) == 0)
    def _():
        run_ref[...] = jnp.zeros_like(run_ref)

    o = jnp.concatenate([og_ref[...], od_ref[...], os_ref[...]], axis=-1)
    h1 = h_ref[...] + _mm(o, wo_ref[...])
    h1_ref[...] = h1
    xn = _rms(h1, gffn_ref[...])
    xn_ref[...] = xn

    x_hi, x_lo = _split_bf16(xn)
    lg = (_mm(x_hi, wr_hi_ref[...]) + _mm(x_lo, wr_hi_ref[...]) + _mm(x_hi, wr_lo_ref[...])
          + br_ref[...])

    lane = lax.broadcasted_iota(jnp.int32, lg.shape, 1)
    lane_f = lane.astype(F32)
    ninf = -jnp.inf

    def first_max(x):
        m = jnp.max(x, axis=-1, keepdims=True)
        idx = jnp.min(jnp.where(x == m, lane_f, float(LANES)), axis=-1, keepdims=True)
        return m, idx

    gl = jnp.where(lane < N_GROUPS, lg, ninf)
    gmax, gsel = first_max(gl)
    g_w = 1.0 / jnp.sum(jnp.exp(gl - gmax), axis=-1, keepdims=True)
    e_lo = ROUTE_OFF + EXPERTS_PER_GROUP * gsel
    el = jnp.where((lane_f >= e_lo) & (lane_f < e_lo + EXPERTS_PER_GROUP), lg, ninf)
    m1, i1 = first_max(el)
    eden = jnp.sum(jnp.exp(el - m1), axis=-1, keepdims=True)
    m2, i2 = first_max(jnp.where(lane_f == i1, ninf, el))
    p1 = 1.0 / eden
    p2 = jnp.exp(m2 - m1) / eden
    gate1 = g_w * p1 / (p1 + p2)
    gate2 = g_w * p2 / (p1 + p2)

    onehot = jnp.where((lane_f == i1) | (lane_f == i2), 1.0, 0.0)
    rr = lax.broadcasted_iota(jnp.int32, (tm, tm), 0)
    cc = lax.broadcasted_iota(jnp.int32, (tm, tm), 1)
    strict = jnp.where(rr > cc, 1.0, 0.0).astype(BF16)
    before = _mm(strict, onehot.astype(BF16)) + run_ref[0:1, :]
    rank1 = jnp.sum(jnp.where(lane_f == i1, before, 0.0), axis=-1, keepdims=True)
    rank2 = jnp.sum(jnp.where(lane_f == i2, before, 0.0), axis=-1, keepdims=True)
    run_ref[0:1, :] = run_ref[0:1, :] + jnp.sum(onehot, axis=0, keepdims=True)
    cnt_ref[...] = run_ref[...]

    ints = jnp.where(lane == 0, i1 - ROUTE_OFF,
                     jnp.where(lane == 1, i2 - ROUTE_OFF,
                               jnp.where(lane == 2, rank1, jnp.where(lane == 3, rank2, 0.0))))
    ri_ref[...] = ints.astype(jnp.int32)
    rf_ref[...] = jnp.where(lane == 0, gate1, jnp.where(lane == 1, gate2, 0.0))


def _outproj_router(h, og, od, os_, wo, gffn, wr_hi, wr_lo, br):
    t = h.shape[0]
    tm = TOKEN_TILE
    row = lambda w: pl.BlockSpec((tm, w), lambda i: (i, 0))
    full = lambda a, b: pl.BlockSpec((a, b), lambda i: (0, 0))
    return pl.pallas_call(
        _outproj_router_kernel,
        grid=(t // tm,),
        in_specs=[row(D_MODEL), row(256), row(256), row(512), full(D_MODEL, D_MODEL),
                  full(1, D_MODEL), full(D_MODEL, LANES), full(D_MODEL, LANES), full(1, LANES)],
        out_specs=[row(D_MODEL), row(D_MODEL), row(LANES), row(LANES), full(SUBLANES, LANES)],
        out_shape=[
            jax.ShapeDtypeStruct((t, D_MODEL), F32),
            jax.ShapeDtypeStruct((t, D_MODEL), F32),
            jax.ShapeDtypeStruct((t, LANES), jnp.int32),
            jax.ShapeDtypeStruct((t, LANES), F32),
            jax.ShapeDtypeStruct((SUBLANES, LANES), F32),
        ],
        scratch_shapes=[pltpu.VMEM((SUBLANES, LANES), F32)],
        compiler_params=pltpu.CompilerParams(
            dimension_semantics=("arbitrary",), vmem_limit_bytes=VMEM_LIMIT),
        name="outproj_router",
    )(h, og, od, os_, wo, gffn, wr_hi, wr_lo, br)


def _dispatch_kernel(pos_ref, xn_ref, buf_in_ref, buf_ref, sem):
    del buf_in_ref
    tm = xn_ref.shape[0]

    def row_copy(r, k):
        dst = pos_ref[0, 0, 2 * r + k]
        return pltpu.make_async_copy(xn_ref.at[pl.ds(r, 1)], buf_ref.at[pl.ds(dst, 1)], sem)

    def issue(r, carry):
        row_copy(r, 0).start()
        row_copy(r, 1).start()
        return carry

    def drain(r, carry):
        row_copy(r, 0).wait()
        row_copy(r, 1).wait()
        return carry

    lax.fori_loop(0, tm, issue, 0)
    lax.fori_loop(0, tm, drain, 0)


def _dispatch(pos3, xn, buf0):
    t = xn.shape[0]
    tm = TOKEN_TILE
    return pl.pallas_call(
        _dispatch_kernel,
        grid=(t // tm,),
        in_specs=[
            pl.BlockSpec((1, 1, 2 * tm), lambda i: (i, 0, 0), memory_space=pltpu.SMEM),
            pl.BlockSpec((tm, D_MODEL), lambda i: (i, 0)),
            pl.BlockSpec(memory_space=pl.ANY),
        ],
        out_specs=pl.BlockSpec(memory_space=pl.ANY),
        out_shape=jax.ShapeDtypeStruct(buf0.shape, buf0.dtype),
        scratch_shapes=[pltpu.SemaphoreType.DMA(())],
        input_output_aliases={2: 0},
        compiler_params=pltpu.CompilerParams(
            dimension_semantics=("arbitrary",), vmem_limit_bytes=VMEM_LIMIT),
        name="dispatch",
    )(pos3, xn, buf0)


def _expert_kernel(be_ref, nu_ref, x_ref, wg_ref, wu_ref, wd_ref, y_ref):
    i = pl.program_id(0)

    @pl.when(i < nu_ref[0])
    def _():
        x = x_ref[...].astype(BF16)
        g = _mm(x, wg_ref[0])
        u = _mm(x, wu_ref[0])
        hmid = (g * jax.nn.sigmoid(g)) * u
        y_ref[...] = _mm(hmid.astype(BF16), wd_ref[0])

    @pl.when(i >= nu_ref[0])
    def _():
        y_ref[...] = jnp.zeros_like(y_ref)


def _experts(blk_expert, n_used, buf, wg, wu, wd):
    nrows = buf.shape[0]
    nblk = nrows // EXPERT_ROWS
    return pl.pallas_call(
        _expert_kernel,
        grid_spec=pltpu.PrefetchScalarGridSpec(
            num_scalar_prefetch=2,
            grid=(nblk,),
            in_specs=[
                pl.BlockSpec((EXPERT_ROWS, D_MODEL), lambda i, be, nu: (i, 0)),
                pl.BlockSpec((1, D_MODEL, D_EXPERT), lambda i, be, nu: (be[i], 0, 0)),
                pl.BlockSpec((1, D_MODEL, D_EXPERT), lambda i, be, nu: (be[i], 0, 0)),
                pl.BlockSpec((1, D_EXPERT, D_MODEL), lambda i, be, nu: (be[i], 0, 0)),
            ],
            out_specs=pl.BlockSpec((EXPERT_ROWS, D_MODEL), lambda i, be, nu: (i, 0)),
        ),
        out_shape=jax.ShapeDtypeStruct((nrows, D_MODEL), F32),
        compiler_params=pltpu.CompilerParams(
            dimension_semantics=("arbitrary",), vmem_limit_bytes=VMEM_LIMIT),
        name="experts",
    )(blk_expert, n_used, buf, wg, wu, wd)


def _combine_ple_kernel(pos_ref, h1_ref, p_ref, rf_ref, yb_ref, wple_ref, gple_ref, wpg_ref,
                        gfin_ref, o_ref, ybuf, sem, *, final_norm):
    tm = h1_ref.shape[0]

    def row_copy(r, k):
        src = pos_ref[0, 0, 2 * r + k]
        return pltpu.make_async_copy(yb_ref.at[pl.ds(src, 1)], ybuf.at[k, pl.ds(r, 1)], sem)

    def issue(r, carry):
        row_copy(r, 0).start()
        row_copy(r, 1).start()
        return carry

    def drain(r, carry):
        row_copy(r, 0).wait()
        row_copy(r, 1).wait()
        return carry

    lax.fori_loop(0, tm, issue, 0)
    e = _rms(_mm(p_ref[...].astype(BF16), wple_ref[...]), gple_ref[...])
    lax.fori_loop(0, tm, drain, 0)

    rf = rf_ref[...]
    h2 = h1_ref[...] + ybuf[0] * rf[:, 0:1] + ybuf[1] * rf[:, 1:2]
    h3 = h2 + e * jax.nn.sigmoid(_mm(h2.astype(BF16), wpg_ref[...]))
    if final_norm:
        h3 = _rms(h3, gfin_ref[...])
    o_ref[...] = h3


def _combine_ple(pos3, h1, p, rf, yb, wple, gple, wpg, gfin, final_norm):
    t = h1.shape[0]
    tm = TOKEN_TILE
    row = lambda w: pl.BlockSpec((tm, w), lambda i: (i, 0))
    full = lambda a, b: pl.BlockSpec((a, b), lambda i: (0, 0))
    return pl.pallas_call(
        functools.partial(_combine_ple_kernel, final_norm=final_norm),
        grid=(t // tm,),
        in_specs=[
            pl.BlockSpec((1, 1, 2 * tm), lambda i: (i, 0, 0), memory_space=pltpu.SMEM),
            row(D_MODEL), row(PLE_DIM), row(LANES),
            pl.BlockSpec(memory_space=pl.ANY),
            full(PLE_DIM, D_MODEL), full(1, D_MODEL), full(D_MODEL, D_MODEL), full(1, D_MODEL),
        ],
        out_specs=row(D_MODEL),
        out_shape=jax.ShapeDtypeStruct((t, D_MODEL), F32),
        scratch_shapes=[pltpu.VMEM((2, tm, D_MODEL), F32), pltpu.SemaphoreType.DMA(())],
        compiler_params=pltpu.CompilerParams(
            dimension_semantics=("arbitrary",), vmem_limit_bytes=VMEM_LIMIT),
        name="combine_ple",
    )(pos3, h1, p, rf, yb, wple, gple, wpg, gfin)


def _rel_bucket(dist):
    n = jnp.maximum(dist, 0)
    max_exact = REL_BUCKETS // 2
    nf = jnp.maximum(n, 1).astype(F32)
    large = max_exact + (jnp.log(nf / max_exact) / math.log(REL_MAX_DIST / max_exact)
                         * (REL_BUCKETS - max_exact)).astype(jnp.int32)
    large = jnp.minimum(large, REL_BUCKETS - 1)
    return jnp.where(n < max_exact, n, large)


def _bias_tiles(rel_bias):
    qi = jnp.arange(WINDOW)[:, None]
    kj = jnp.arange(2 * WINDOW)[None, :]
    nat = jnp.transpose(rel_bias[_rel_bucket(qi + WINDOW - kj)], (2, 0, 1))
    dsa = nat[:DSA_HEADS]
    far = jnp.broadcast_to(rel_bias[REL_BUCKETS - 1, :DSA_HEADS][:, None, None],
                           (DSA_HEADS, WINDOW, WINDOW))
    prev = jnp.swapaxes(dsa[:, :, :WINDOW], 1, 2)
    cur = jnp.swapaxes(dsa[:, :, WINDOW:], 1, 2)
    return jnp.stack([far, prev, cur], axis=1), nat[DSA_HEADS:]


def _pack_w_in(w):
    sizes = (128, 128, 256, 256, 16, 256, 64, 64, 512, 64, 8, 512, 128, 128)
    offs = [0]
    for s in sizes:
        offs.append(offs[-1] + s)
    gq, gk, gv, gg, ga, dq, dk, dv, iq, ik, iw, sq, sk, sv = (
        w[:, offs[n]:offs[n + 1]] for n in range(len(sizes)))
    z = lambda n: jnp.zeros((w.shape[0], n), w.dtype)
    packed = jnp.concatenate(
        [iq, sq, gv, gg, dq, gq, gk, ga, z(128 - GLA_RANK), dk, dv, ik, iw,
         z(128 - IDX_DIM - IDX_HEADS), sk, sv], axis=1)
    return packed.astype(BF16)


def kernel(x, p, rel_bias, g_mix, w_in, gla_w_alpha, gla_b_alpha, gla_g_norm, swa_sinks, w_out,
           g_ffn, w_router_group, b_router_group, w_router_expert, b_router_expert, w_expert_gate,
           w_expert_up, w_expert_down, w_ple, g_ple, w_ple_gate, g_final):
    bsz, seq, d = x.shape
    depth = w_in.shape[0]
    t = bsz * seq
    assert d == D_MODEL and t % TOKEN_TILE == 0 and seq % GLA_BLOCK == 0 and seq % 512 == 0
    n_blocks = -(-(2 * t) // EXPERT_ROWS) + N_EXPERTS
    dsa_tiles, swa_bias = _bias_tiles(rel_bias)

    h = x.reshape(t, d)
    for i in range(depth):
        proj = _inproj(h, g_mix[i][None, :], _pack_w_in(w_in[i]))
        proj3 = proj.reshape(bsz, seq, D_PROJ)

        wal = jnp.zeros((128, 128), F32).at[:GLA_RANK].set(gla_w_alpha[i])
        og = _gla(proj3, wal, gla_b_alpha[i][None, :],
                  jnp.tile(gla_g_norm[i], GLA_HEADS)[None, :])
        od = _dsa(proj3, dsa_tiles)
        os_ = _swa(proj3, swa_sinks[i], swa_bias)

        w_r = jnp.concatenate([w_router_group[i], w_router_expert[i]], axis=1)
        w_r = jnp.pad(w_r, ((0, 0), (0, LANES - w_r.shape[1])))
        wr_hi, wr_lo = _split_bf16(w_r)
        b_r = jnp.pad(jnp.concatenate([b_router_group[i], b_router_expert[i]]),
                      (0, LANES - N_GROUPS - N_EXPERTS))[None, :]
        h1, xn, ri, rf, cnt = _outproj_router(
            h, og.reshape(t, -1), od.reshape(t, -1), os_.reshape(t, -1), w_out[i].astype(BF16),
            g_ffn[i][None, :], wr_hi, wr_lo, b_r)

        counts = cnt[0, ROUTE_OFF:ROUTE_OFF + N_EXPERTS].astype(jnp.int32)
        padded = (counts + EXPERT_ROWS - 1) // EXPERT_ROWS * EXPERT_ROWS
        pad_end = jnp.cumsum(padded)
        pad_start = pad_end - padded
        pos = pad_start[ri[:, 0:2]] + ri[:, 2:4]
        pos3 = pos.reshape(t // TOKEN_TILE, 1, 2 * TOKEN_TILE)
        blk_expert = jnp.minimum(
            jnp.searchsorted(pad_end, jnp.arange(n_blocks) * EXPERT_ROWS, side='right'),
            N_EXPERTS - 1).astype(jnp.int32)
        n_used = (pad_end[-1:] // EXPERT_ROWS).astype(jnp.int32)

        buf = _dispatch(pos3, xn, jnp.zeros((n_blocks * EXPERT_ROWS, d), F32))
        yb = _experts(blk_expert, n_used, buf, w_expert_gate[i].astype(BF16),
                      w_expert_up[i].astype(BF16), w_expert_down[i].astype(BF16))
        h = _combine_ple(pos3, h1, p[i].reshape(t, PLE_DIM), rf, yb, w_ple[i].astype(BF16),
                         g_ple[i][None, :], w_ple_gate[i].astype(BF16), g_final[None, :],
                         final_norm=(i == depth - 1))
    return h.reshape(bsz, seq, d)
```

```python
import functools
import math

import jax
import jax.numpy as jnp
from jax import lax
from jax.experimental import pallas as pl
from jax.experimental.pallas import tpu as pltpu

F32 = jnp.float32
BF16 = jnp.bfloat16
HIGHEST = lax.Precision.HIGHEST

D_MODEL = 1024
HEAD_DIM = 64
GLA_HEADS = 4
GLA_DK = 32
GLA_DV = 64
GLA_RANK = 16
GLA_TAU = 16.0
GLA_CHUNK = 64
DSA_HEADS = 4
IDX_HEADS = 8
IDX_DIM = 64
DSA_TOPK_MAX = 256
SWA_HEADS = 8
SWA_KV_HEADS = 2
SWA_GROUP = SWA_HEADS // SWA_KV_HEADS
WINDOW = 128
REL_BUCKETS = 32
REL_MAX_DIST = 128
N_GROUPS = 4
EXPERTS_PER_GROUP = 8
N_EXPERTS = N_GROUPS * EXPERTS_PER_GROUP
D_EXPERT = 512
PLE_DIM = 256
EPS = 1e-6

LANES = 128
SUBLANES = 8

COL_IQ, COL_SQ = 0, 512
COL_GV, COL_GG, COL_DQ = 1024, 1280, 1536
COL_GQ, COL_GK, COL_GA, COL_DKV, COL_IKW, COL_SK, COL_SV = 1792, 1920, 2048, 2176, 2304, 2432, 2560
D_PROJ = 2688

TOKEN_TILE = 256
EXPERT_ROWS = 256
GLA_BLOCK = 256
QB = 128
KC = 256
NEG_BIG = -1e30
VMEM_LIMIT = 48 * 1024 * 1024


def _nt(a, b, precision=None):
    return lax.dot_general(a, b, (((1,), (1,)), ((), ())), precision=precision,
                           preferred_element_type=F32)


def _mm(a, b, precision=None):
    return jnp.dot(a, b, precision=precision, preferred_element_type=F32)


def _eye(n, dtype):
    r = lax.broadcasted_iota(jnp.int32, (n, n), 0)
    c = lax.broadcasted_iota(jnp.int32, (n, n), 1)
    return jnp.where(r == c, 1.0, 0.0).astype(dtype)


def _rms(x, g):
    return x * lax.rsqrt(jnp.mean(x * x, axis=-1, keepdims=True) + EPS) * g


def _inproj_kernel(h_ref, g_ref, w_ref, o_ref):
    a = _rms(h_ref[...], g_ref[...])
    o_ref[...] = _mm(a.astype(BF16), w_ref[...])


def _inproj(h, g, w):
    t = h.shape[0]
    return pl.pallas_call(
        _inproj_kernel,
        grid=(t // TOKEN_TILE,),
        in_specs=[
            pl.BlockSpec((TOKEN_TILE, D_MODEL), lambda i: (i, 0)),
            pl.BlockSpec((1, D_MODEL), lambda i: (0, 0)),
            pl.BlockSpec((D_MODEL, D_PROJ), lambda i: (0, 0)),
        ],
        out_specs=pl.BlockSpec((TOKEN_TILE, D_PROJ), lambda i: (i, 0)),
        out_shape=jax.ShapeDtypeStruct((t, D_PROJ), F32),
        compiler_params=pltpu.CompilerParams(
            dimension_semantics=("arbitrary",), vmem_limit_bytes=VMEM_LIMIT),
        name="inproj",
    )(h, g, w)


def _gla_kernel(q_ref, k_ref, v_ref, gg_ref, ga_ref, wal_ref, bal_ref, gn_ref, o_ref,
                state_ref, oc_ref):
    hk = GLA_HEADS * GLA_DK
    hv = GLA_HEADS * GLA_DV
    c = GLA_CHUNK

    @pl.when(pl.program_id(1) == 0)
    def _():
        state_ref[...] = jnp.zeros_like(state_ref)

    r64 = lax.broadcasted_iota(jnp.int32, (c, c), 0)
    c64 = lax.broadcasted_iota(jnp.int32, (c, c), 1)
    tril = jnp.where(r64 >= c64, 1.0, 0.0).astype(F32)
    rk = lax.broadcasted_iota(jnp.int32, (hk, hv), 0) // GLA_DK
    cv = lax.broadcasted_iota(jnp.int32, (hk, hv), 1) // GLA_DV
    blockdiag = jnp.where(rk == cv, 1.0, 0.0).astype(F32)
    expand = blockdiag.astype(BF16)
    eye_k = _eye(hk, F32)
    eye_kb = eye_k.astype(BF16)
    row8 = lax.broadcasted_iota(jnp.int32, (SUBLANES, hk), 0)

    def chunk(ci, carry):
        rows = pl.ds(pl.multiple_of(ci * c, c), c)
        q = q_ref[0, rows, :] * (GLA_DK ** -0.5)
        k = k_ref[0, rows, :]
        v = v_ref[0, rows, :]
        z = _mm(ga_ref[0, rows, :], wal_ref[...], HIGHEST) + bal_ref[...]
        log_a = (jnp.minimum(z, 0.0) - jnp.log1p(jnp.exp(-jnp.abs(z)))) * (1.0 / GLA_TAU)
        b = _mm(tril, log_a, HIGHEST)
        state = state_ref[...]
        o_inter = _mm((q * jnp.exp(b)).astype(BF16), state.astype(BF16))

        for t in range(c):
            ns = SUBLANES * (t // SUBLANES + 1)
            d = b[t:t + 1, :] - b[:ns, :]
            p = jnp.exp(jnp.minimum(d, 0.0)) * k[:ns, :] * q[t:t + 1, :]
            if t % SUBLANES != SUBLANES - 1:
                srow = lax.broadcasted_iota(jnp.int32, (ns, hk), 0)
                p = jnp.where(srow <= t, p, 0.0)
            sc = _mm(p.astype(BF16), expand)
            oc_ref[t:t + 1, :] = jnp.sum(sc * v[:ns, :], axis=0, keepdims=True)
        o = o_inter + oc_ref[...]

        b_last = b[c - 1:c, :]
        kd = (k * jnp.exp(b_last - b)).astype(BF16)
        kd_t = _nt(eye_kb, kd).astype(BF16)
        upd = _mm(kd_t, v.astype(BF16))
        bl8 = jnp.where(row8 >= 0, b_last, 0.0)
        bl_t = _nt(eye_k, bl8, HIGHEST)[:, 0:1]
        state_ref[...] = jnp.exp(bl_t) * state + upd * blockdiag

        gg = gg_ref[0, rows, :]
        outs = []
        for h in range(GLA_HEADS):
            oh = o[:, h * GLA_DV:(h + 1) * GLA_DV]
            ms = jnp.mean(oh * oh, axis=-1, keepdims=True)
            outs.append(oh * lax.rsqrt(ms + EPS))
        on = jnp.concatenate(outs, axis=-1) * gn_ref[...]
        o_ref[0, rows, :] = (on * (gg * jax.nn.sigmoid(gg))).astype(o_ref.dtype)
        return carry

    lax.fori_loop(0, GLA_BLOCK // c, chunk, 0)


def _gla(proj3, wal, bal, gn):
    bsz, seq, _ = proj3.shape
    nb = seq // GLA_BLOCK

    def col(width, off):
        return pl.BlockSpec((1, GLA_BLOCK, width), lambda b, i: (b, i, off // width))

    return pl.pallas_call(
        _gla_kernel,
        grid=(bsz, nb),
        in_specs=[
            col(128, COL_GQ), col(128, COL_GK), col(256, COL_GV), col(256, COL_GG), col(128, COL_GA),
            pl.BlockSpec((128, 128), lambda b, i: (0, 0)),
            pl.BlockSpec((1, 128), lambda b, i: (0, 0)),
            pl.BlockSpec((1, 256), lambda b, i: (0, 0)),
        ],
        out_specs=pl.BlockSpec((1, GLA_BLOCK, 256), lambda b, i: (b, i, 0)),
        out_shape=jax.ShapeDtypeStruct((bsz, seq, GLA_HEADS * GLA_DV), BF16),
        scratch_shapes=[
            pltpu.VMEM((GLA_HEADS * GLA_DK, GLA_HEADS * GLA_DV), F32),
            pltpu.VMEM((GLA_CHUNK, GLA_HEADS * GLA_DV), F32),
        ],
        compiler_params=pltpu.CompilerParams(
            dimension_semantics=("arbitrary", "arbitrary"), vmem_limit_bytes=VMEM_LIMIT),
        name="gla",
    )(proj3, proj3, proj3, proj3, proj3, wal, bal, gn)


DSA_SEARCH_CAP = 14


def _dsa_kernel(dq_ref, kv_ref, iq_ref, ikw_ref, btile_ref, o_ref,
                sc_ref, lg_ref, mp_ref, lp_ref, acc_ref, cut_ref, *, n_sel):
    j = pl.program_id(1)
    nkc = (j + 2) // 2
    seq = kv_ref.shape[1]
    ksel = float(n_sel)
    groups = KC // SUBLANES

    def rows(c):
        return pl.ds(pl.multiple_of(c * KC, KC), KC)

    def fold_sum(x):
        return jnp.sum(x.reshape(groups, SUBLANES, QB), axis=0)

    def fold_min(x):
        return jnp.min(x.reshape(groups, SUBLANES, QB), axis=0)

    def fold_max(x):
        return jnp.max(x.reshape(groups, SUBLANES, QB), axis=0)

    def rsum(x8):
        return jnp.sum(x8, axis=0, keepdims=True)

    def rmin(x8):
        return jnp.min(x8, axis=0, keepdims=True)

    zeros8 = jnp.zeros((SUBLANES, QB), F32)
    inf8 = jnp.full((SUBLANES, QB), jnp.inf, F32)

    ikw_q = ikw_ref[0, pl.ds(pl.multiple_of(j * QB, QB), QB), :]
    sel_r = lax.broadcasted_iota(jnp.int32, (SUBLANES, LANES), 0)
    sel_c = lax.broadcasted_iota(jnp.int32, (SUBLANES, LANES), 1)
    pick_w = jnp.where(sel_c == sel_r + IDX_DIM, 1.0, 0.0).astype(F32)
    w_t = _nt(pick_w, ikw_q, HIGHEST) * (IDX_HEADS ** -0.5 * IDX_DIM ** -0.5)
    iq = iq_ref[0].astype(BF16)
    iq_pairs = [
        jnp.concatenate([iq[:, (2 * p) * IDX_DIM:(2 * p + 1) * IDX_DIM],
                         iq[:, (2 * p + 1) * IDX_DIM:(2 * p + 2) * IDX_DIM]], axis=0)
        for p in range(IDX_HEADS // 2)]

    s_loc = lax.broadcasted_iota(jnp.int32, (KC, QB), 0)
    t_glob = j * QB + lax.broadcasted_iota(jnp.int32, (KC, QB), 1)
    s_loc_f = s_loc.astype(F32)

    def score_chunk(c, st):
        mn8, mx8, ge8, gt8, vp8 = st
        ik = ikw_ref[0, rows(c), :][:, :IDX_DIM].astype(BF16)
        acc = jnp.zeros((KC, QB), F32)
        for p in range(IDX_HEADS // 2):
            rel = _nt(ik, iq_pairs[p])
            acc = (acc + jnp.maximum(rel[:, :QB], 0.0) * w_t[2 * p:2 * p + 1, :]
                   + jnp.maximum(rel[:, QB:], 0.0) * w_t[2 * p + 1:2 * p + 2, :])
        adm = s_loc + c * KC <= t_glob
        blk = jnp.where(adm, acc, -jnp.inf)
        sc_ref[rows(c), :] = blk
        mn8 = jnp.minimum(mn8, fold_min(jnp.where(adm, acc, jnp.inf)))
        mx8 = jnp.maximum(mx8, fold_max(blk))
        ge8 = ge8 + fold_sum(jnp.where(blk >= 0.0, 1.0, 0.0))
        pos = blk > 0.0
        gt8 = gt8 + fold_sum(jnp.where(pos, 1.0, 0.0))
        vp8 = jnp.minimum(vp8, fold_min(jnp.where(pos, blk, jnp.inf)))
        return mn8, mx8, ge8, gt8, vp8

    mn8, mx8, ge8, gt8, vp8 = lax.fori_loop(
        0, nkc, score_chunk, (inf8, -inf8, zeros8, zeros8, inf8))
    row_min, row_max = rmin(mn8), jnp.max(mx8, axis=0, keepdims=True)
    c_ge0, c_gt0, v_pos = rsum(ge8), rsum(gt8), rmin(vp8)

    def pass_ge(x):
        def body(c, st):
            c8, m8 = st
            blk = sc_ref[rows(c), :]
            hit = blk >= x
            return (c8 + fold_sum(jnp.where(hit, 1.0, 0.0)),
                    jnp.minimum(m8, fold_min(jnp.where(hit, blk, jnp.inf))))
        c8, m8 = lax.fori_loop(0, nkc, body, (zeros8, inf8))
        return rsum(c8), rmin(m8)

    def pass_gt(x):
        def body(c, st):
            c8, m8 = st
            blk = sc_ref[rows(c), :]
            hit = blk > x
            return (c8 + fold_sum(jnp.where(hit, 1.0, 0.0)),
                    jnp.minimum(m8, fold_min(jnp.where(hit, blk, jnp.inf))))
        c8, m8 = lax.fori_loop(0, nkc, body, (zeros8, inf8))
        return rsum(c8), rmin(m8)

    n_adm = (j * QB + lax.broadcasted_iota(jnp.int32, (1, QB), 1) + 1).astype(F32)
    at_zero = (c_gt0 < ksel) & (c_ge0 >= ksel)
    above = c_gt0 >= ksel
    lo = jnp.where(at_zero, 0.0, jnp.where(above, v_pos, row_min))
    c_lo = jnp.where(at_zero, c_ge0, jnp.where(above, c_gt0, n_adm))
    hi = jnp.where(above, row_max, 0.0)
    c_hi = jnp.where(above, 0.0, c_ge0)
    done = jnp.where(at_zero | (c_lo <= ksel), 1.0, 0.0)
    c_gt = jnp.where(at_zero, c_gt0, 0.0)

    def search_cond(st):
        it, _, _, _, _, dn = st
        return (it < DSA_SEARCH_CAP) & (jnp.min(dn) < 0.5)

    def search_body(st):
        it, lo, c_lo, hi, c_hi, dn = st
        interp = (c_lo - ksel + 0.5) / jnp.maximum(c_lo - c_hi, 1.0)
        frac = jnp.where(it % 2 == 0, interp, 0.5)
        mid = lo + (hi - lo) * frac
        cnt, vmin = pass_ge(mid)
        live = dn < 0.5
        up = live & (cnt >= ksel)
        dw = live & (cnt < ksel)
        lo = jnp.where(up, vmin, lo)
        c_lo = jnp.where(up, cnt, c_lo)
        hi = jnp.where(dw, mid, hi)
        c_hi = jnp.where(dw, cnt, c_hi)
        dn = jnp.where(c_lo == ksel, 1.0, dn)
        return it + 1, lo, c_lo, hi, c_hi, dn

    _, lo, c_lo, _, _, done = lax.while_loop(
        search_cond, search_body, (jnp.int32(0), lo, c_lo, hi, c_hi, done))

    def walk_cond(st):
        return jnp.min(st[3]) < 0.5

    def walk_body(st):
        v, c_ge, c_gt, dn = st
        cnt, vnext = pass_gt(v)
        live = dn < 0.5
        fin = live & (cnt < ksel)
        step = live & (cnt >= ksel)
        c_gt = jnp.where(fin, cnt, c_gt)
        dn = jnp.where(fin, 1.0, dn)
        v = jnp.where(step, vnext, v)
        c_ge = jnp.where(step, cnt, c_ge)
        return v, c_ge, c_gt, dn

    tau, c_ge, c_gt, _ = lax.while_loop(walk_cond, walk_body, (lo, c_lo, c_gt, done))

    need = ksel - c_gt
    cut_ref[...] = jnp.full(cut_ref.shape, float(seq), F32)

    @pl.when(jnp.max(c_ge) > ksel)
    def _():
        def count_ties_below(m):
            def body(c, c8):
                blk = sc_ref[rows(c), :]
                sidx = s_loc_f + (c * KC).astype(F32)
                return c8 + fold_sum(jnp.where((blk == tau) & (sidx < m), 1.0, 0.0))
            return rsum(lax.fori_loop(0, nkc, body, zeros8))

        def idx_bisect(_, lh):
            lo_m, hi_m = lh
            mid = jnp.floor((lo_m + hi_m) * 0.5)
            ok = count_ties_below(mid) >= need
            return jnp.where(ok, lo_m, mid), jnp.where(ok, mid, hi_m)

        zero = jnp.zeros((1, QB), F32)
        _, hi_m = lax.fori_loop(0, int(math.log2(seq)) + 1, idx_bisect,
                                (zero, zero + float(seq)))
        cut_ref[0:1, :] = jnp.where(c_ge > ksel, hi_m, float(seq))

    cut = cut_ref[0:1, :]

    q = (dq_ref[0] * (HEAD_DIM ** -0.5)).astype(BF16)
    q4 = jnp.concatenate([q[:, h * HEAD_DIM:(h + 1) * HEAD_DIM] for h in range(DSA_HEADS)],
                         axis=0)
    mp_ref[...] = jnp.full(mp_ref.shape, NEG_BIG, F32)
    lp_ref[...] = jnp.zeros(lp_ref.shape, F32)
    acc_ref[...] = jnp.zeros(acc_ref.shape, F32)

    def logits_chunk(c, carry):
        blk = sc_ref[rows(c), :]
        sidx = s_loc_f + (c * KC).astype(F32)
        sel = (blk > tau) | ((blk == tau) & (sidx < cut))
        neg_t = jnp.where(sel, 0.0, NEG_BIG).T
        kb = kv_ref[0, rows(c), :][:, :HEAD_DIM].astype(BF16)
        lg4 = _nt(q4, kb)
        w0 = jnp.clip(2 * c - j + 2, 0, 2)
        w1 = jnp.clip(2 * c + 1 - j + 2, 0, 2)
        for h in range(DSA_HEADS):
            bias = jnp.concatenate([btile_ref[h, w0], btile_ref[h, w1]], axis=1)
            lg = lg4[h * QB:(h + 1) * QB, :] + bias + neg_t
            lg_ref[h, c] = lg
            mp_ref[h] = jnp.maximum(mp_ref[h], jnp.maximum(lg[:, :QB], lg[:, QB:]))
        return carry

    lax.fori_loop(0, nkc, logits_chunk, 0)
    m_rows = [jnp.max(mp_ref[h], axis=-1, keepdims=True) for h in range(DSA_HEADS)]

    def pv_chunk(c, carry):
        vb = kv_ref[0, rows(c), :][:, HEAD_DIM:2 * HEAD_DIM].astype(BF16)
        ps = []
        for h in range(DSA_HEADS):
            p = jnp.exp(lg_ref[h, c] - m_rows[h])
            lp_ref[h] = lp_ref[h] + p[:, :QB] + p[:, QB:]
            ps.append(p.astype(BF16))
        acc_ref[...] = acc_ref[...] + _mm(jnp.concatenate(ps, axis=0), vb)
        return carry

    lax.fori_loop(0, nkc, pv_chunk, 0)
    outs = []
    for h in range(DSA_HEADS):
        l = jnp.sum(lp_ref[h], axis=-1, keepdims=True)
        outs.append(acc_ref[h * QB:(h + 1) * QB, :] / l)
    o_ref[0] = jnp.concatenate(outs, axis=-1).astype(o_ref.dtype)


def _dsa(proj3, btiles):
    bsz, seq, _ = proj3.shape
    n_sel = min(DSA_TOPK_MAX, seq // 4)
    return pl.pallas_call(
        functools.partial(_dsa_kernel, n_sel=n_sel),
        grid=(bsz, seq // QB),
        in_specs=[
            pl.BlockSpec((1, QB, 256), lambda b, j: (b, j, COL_DQ // 256)),
            pl.BlockSpec((1, seq, 128), lambda b, j: (b, 0, COL_DKV // 128)),
            pl.BlockSpec((1, QB, 512), lambda b, j: (b, j, COL_IQ // 512)),
            pl.BlockSpec((1, seq, 128), lambda b, j: (b, 0, COL_IKW // 128)),
            pl.BlockSpec((DSA_HEADS, 3, QB, QB), lambda b, j: (0, 0, 0, 0)),
        ],
        out_specs=pl.BlockSpec((1, QB, 256), lambda b, j: (b, j, 0)),
        out_shape=jax.ShapeDtypeStruct((bsz, seq, DSA_HEADS * HEAD_DIM), BF16),
        scratch_shapes=[
            pltpu.VMEM((seq, QB), F32),
            pltpu.VMEM((DSA_HEADS, seq // KC, QB, KC), F32),
            pltpu.VMEM((DSA_HEADS, QB, QB), F32),
            pltpu.VMEM((DSA_HEADS, QB, QB), F32),
            pltpu.VMEM((DSA_HEADS * QB, HEAD_DIM), F32),
            pltpu.VMEM((SUBLANES, QB), F32),
        ],
        compiler_params=pltpu.CompilerParams(
            dimension_semantics=("arbitrary", "arbitrary"), vmem_limit_bytes=VMEM_LIMIT),
        name="dsa",
    )(proj3, proj3, proj3, proj3, btiles)


def _swa_kernel(sink_ref, q_ref, kc_ref, kp_ref, vc_ref, vp_ref, bias_ref, o_ref):
    n = pl.program_id(1)
    q = (q_ref[0] * (HEAD_DIM ** -0.5)).astype(BF16)
    k2 = jnp.concatenate([kp_ref[0], kc_ref[0]], axis=0).astype(BF16)
    v2 = jnp.concatenate([vp_ref[0], vc_ref[0]], axis=0).astype(BF16)
    qi = lax.broadcasted_iota(jnp.int32, (WINDOW, 2 * WINDOW), 0)
    kj = lax.broadcasted_iota(jnp.int32, (WINDOW, 2 * WINDOW), 1)
    dist = qi + WINDOW - kj
    mask = (dist >= 0) & (dist < WINDOW) & ((kj >= WINDOW) | (n > 0))
    outs = []
    for h in range(SWA_HEADS):
        kvh = h // SWA_GROUP
        kh = k2[:, kvh * HEAD_DIM:(kvh + 1) * HEAD_DIM]
        vh = v2[:, kvh * HEAD_DIM:(kvh + 1) * HEAD_DIM]
        lg = _nt(q[:, h * HEAD_DIM:(h + 1) * HEAD_DIM], kh) + bias_ref[h]
        lg = jnp.where(mask, lg, -jnp.inf)
        sink = sink_ref[h]
        m = jnp.maximum(jnp.max(lg, axis=-1, keepdims=True), sink)
        e = jnp.exp(lg - m)
        den = jnp.sum(e, axis=-1, keepdims=True) + jnp.exp(sink - m)
        outs.append(_mm((e / den).astype(BF16), vh))
    o_ref[0] = jnp.concatenate(outs, axis=-1).astype(o_ref.dtype)


def _swa(proj3, sinks, bias_nat):
    bsz, seq, _ = proj3.shape
    return pl.pallas_call(
        _swa_kernel,
        grid=(bsz, seq // WINDOW),
        in_specs=[
            pl.BlockSpec(memory_space=pltpu.SMEM),
            pl.BlockSpec((1, WINDOW, 512), lambda b, n: (b, n, COL_SQ // 512)),
            pl.BlockSpec((1, WINDOW, 128), lambda b, n: (b, n, COL_SK // 128)),
            pl.BlockSpec((1, WINDOW, 128), lambda b, n: (b, jnp.maximum(n - 1, 0), COL_SK // 128)),
            pl.BlockSpec((1, WINDOW, 128), lambda b, n: (b, n, COL_SV // 128)),
            pl.BlockSpec((1, WINDOW, 128), lambda b, n: (b, jnp.maximum(n - 1, 0), COL_SV // 128)),
            pl.BlockSpec((SWA_HEADS, WINDOW, 2 * WINDOW), lambda b, n: (0, 0, 0)),
        ],
        out_specs=pl.BlockSpec((1, WINDOW, 512), lambda b, n: (b, n, 0)),
        out_shape=jax.ShapeDtypeStruct((bsz, seq, SWA_HEADS * HEAD_DIM), BF16),
        compiler_params=pltpu.CompilerParams(
            dimension_semantics=("arbitrary", "arbitrary"), vmem_limit_bytes=VMEM_LIMIT),
        name="swa",
    )(sinks, proj3, proj3, proj3, proj3, proj3, bias_nat)


ROUTE_OFF = N_GROUPS


def _split_bf16(x):
    hi = x.astype(BF16)
    lo = (x - hi.astype(F32)).astype(BF16)
    return hi, lo


def _outproj_router_kernel(h_ref, og_ref, od_ref, os_ref, wo_ref, gffn_ref, wr_hi_ref, wr_lo_ref,
                           br_ref, h1_ref, xn_ref, ri_ref, rf_ref, cnt_ref, run_ref):
    tm = h_ref.shape[0]

    @pl.when(pl.program_id(0) == 0)
    def _():
        run_ref[...] = jnp.zeros_like(run_ref)

    o = jnp.concatenate([og_ref[...], od_ref[...], os_ref[...]], axis=-1)
    h1 = h_ref[...] + _mm(o, wo_ref[...])
    h1_ref[...] = h1
    xn = _rms(h1, gffn_ref[...])
    xn_ref[...] = xn

    x_hi, x_lo = _split_bf16(xn)
    lg = (_mm(x_hi, wr_hi_ref[...]) + _mm(x_lo, wr_hi_ref[...]) + _mm(x_hi, wr_lo_ref[...])
          + br_ref[...])

    lane = lax.broadcasted_iota(jnp.int32, lg.shape, 1)
    lane_f = lane.astype(F32)
    ninf = -jnp.inf

    def first_max(x):
        m = jnp.max(x, axis=-1, keepdims=True)
        idx = jnp.min(jnp.where(x == m, lane_f, float(LANES)), axis=-1, keepdims=True)
        return m, idx

    gl = jnp.where(lane < N_GROUPS, lg, ninf)
    gmax, gsel = first_max(gl)
    g_w = 1.0 / jnp.sum(jnp.exp(gl - gmax), axis=-1, keepdims=True)
    e_lo = ROUTE_OFF + EXPERTS_PER_GROUP * gsel
    el = jnp.where((lane_f >= e_lo) & (lane_f < e_lo + EXPERTS_PER_GROUP), lg, ninf)
    m1, i1 = first_max(el)
    eden = jnp.sum(jnp.exp(el - m1), axis=-1, keepdims=True)
    m2, i2 = first_max(jnp.where(lane_f == i1, ninf, el))
    p1 = 1.0 / eden
    p2 = jnp.exp(m2 - m1) / eden
    gate1 = g_w * p1 / (p1 + p2)
    gate2 = g_w * p2 / (p1 + p2)

    onehot = jnp.where((lane_f == i1) | (lane_f == i2), 1.0, 0.0)
    rr = lax.broadcasted_iota(jnp.int32, (tm, tm), 0)
    cc = lax.broadcasted_iota(jnp.int32, (tm, tm), 1)
    strict = jnp.where(rr > cc, 1.0, 0.0).astype(BF16)
    before = _mm(strict, onehot.astype(BF16)) + run_ref[0:1, :]
    rank1 = jnp.sum(jnp.where(lane_f == i1, before, 0.0), axis=-1, keepdims=True)
    rank2 = jnp.sum(jnp.where(lane_f == i2, before, 0.0), axis=-1, keepdims=True)
    run_ref[0:1, :] = run_ref[0:1, :] + jnp.sum(onehot, axis=0, keepdims=True)
    cnt_ref[...] = run_ref[...]

    ints = jnp.where(lane == 0, i1 - ROUTE_OFF,
                     jnp.where(lane == 1, i2 - ROUTE_OFF,
                               jnp.where(lane == 2, rank1, jnp.where(lane == 3, rank2, 0.0))))
    ri_ref[...] = ints.astype(jnp.int32)
    rf_ref[...] = jnp.where(lane == 0, gate1, jnp.where(lane == 1, gate2, 0.0))


def _outproj_router(h, og, od, os_, wo, gffn, wr_hi, wr_lo, br):
    t = h.shape[0]
    tm = TOKEN_TILE
    row = lambda w: pl.BlockSpec((tm, w), lambda i: (i, 0))
    full = lambda a, b: pl.BlockSpec((a, b), lambda i: (0, 0))
    return pl.pallas_call(
        _outproj_router_kernel,
        grid=(t // tm,),
        in_specs=[row(D_MODEL), row(256), row(256), row(512), full(D_MODEL, D_MODEL),
                  full(1, D_MODEL), full(D_MODEL, LANES), full(D_MODEL, LANES), full(1, LANES)],
        out_specs=[row(D_MODEL), row(D_MODEL), row(LANES), row(LANES), full(SUBLANES, LANES)],
        out_shape=[
            jax.ShapeDtypeStruct((t, D_MODEL), F32),
            jax.ShapeDtypeStruct((t, D_MODEL), F32),
            jax.ShapeDtypeStruct((t, LANES), jnp.int32),
            jax.ShapeDtypeStruct((t, LANES), F32),
            jax.ShapeDtypeStruct((SUBLANES, LANES), F32),
        ],
        scratch_shapes=[pltpu.VMEM((SUBLANES, LANES), F32)],
        compiler_params=pltpu.CompilerParams(
            dimension_semantics=("arbitrary",), vmem_limit_bytes=VMEM_LIMIT),
        name="outproj_router",
    )(h, og, od, os_, wo, gffn, wr_hi, wr_lo, br)


def _dispatch_kernel(pos_ref, xn_ref, buf_in_ref, buf_ref, sem):
    del buf_in_ref
    tm = xn_ref.shape[0]

    def row_copy(r, k):
        dst = pos_ref[0, 0, 2 * r + k]
        return pltpu.make_async_copy(xn_ref.at[pl.ds(r, 1)], buf_ref.at[pl.ds(dst, 1)], sem)

    def issue(r, carry):
        row_copy(r, 0).start()
        row_copy(r, 1).start()
        return carry

    def drain(r, carry):
        row_copy(r, 0).wait()
        row_copy(r, 1).wait()
        return carry

    lax.fori_loop(0, tm, issue, 0)
    lax.fori_loop(0, tm, drain, 0)


def _dispatch(pos3, xn, buf0):
    t = xn.shape[0]
    tm = TOKEN_TILE
    return pl.pallas_call(
        _dispatch_kernel,
        grid=(t // tm,),
        in_specs=[
            pl.BlockSpec((1, 1, 2 * tm), lambda i: (i, 0, 0), memory_space=pltpu.SMEM),
            pl.BlockSpec((tm, D_MODEL), lambda i: (i, 0)),
            pl.BlockSpec(memory_space=pl.ANY),
        ],
        out_specs=pl.BlockSpec(memory_space=pl.ANY),
        out_shape=jax.ShapeDtypeStruct(buf0.shape, buf0.dtype),
        scratch_shapes=[pltpu.SemaphoreType.DMA(())],
        input_output_aliases={2: 0},
        compiler_params=pltpu.CompilerParams(
            dimension_semantics=("arbitrary",), vmem_limit_bytes=VMEM_LIMIT),
        name="dispatch",
    )(pos3, xn, buf0)


def _expert_kernel(be_ref, nu_ref, x_ref, wg_ref, wu_ref, wd_ref, y_ref):
    i = pl.program_id(0)

    @pl.when(i < nu_ref[0])
    def _():
        x = x_ref[...].astype(BF16)
        g = _mm(x, wg_ref[0])
        u = _mm(x, wu_ref[0])
        hmid = (g * jax.nn.sigmoid(g)) * u
        y_ref[...] = _mm(hmid.astype(BF16), wd_ref[0])

    @pl.when(i >= nu_ref[0])
    def _():
        y_ref[...] = jnp.zeros_like(y_ref)


def _experts(blk_expert, n_used, buf, wg, wu, wd):
    nrows = buf.shape[0]
    nblk = nrows // EXPERT_ROWS
    return pl.pallas_call(
        _expert_kernel,
        grid_spec=pltpu.PrefetchScalarGridSpec(
            num_scalar_prefetch=2,
            grid=(nblk,),
            in_specs=[
                pl.BlockSpec((EXPERT_ROWS, D_MODEL), lambda i, be, nu: (i, 0)),
                pl.BlockSpec((1, D_MODEL, D_EXPERT), lambda i, be, nu: (be[i], 0, 0)),
                pl.BlockSpec((1, D_MODEL, D_EXPERT), lambda i, be, nu: (be[i], 0, 0)),
                pl.BlockSpec((1, D_EXPERT, D_MODEL), lambda i, be, nu: (be[i], 0, 0)),
            ],
            out_specs=pl.BlockSpec((EXPERT_ROWS, D_MODEL), lambda i, be, nu: (i, 0)),
        ),
        out_shape=jax.ShapeDtypeStruct((nrows, D_MODEL), F32),
        compiler_params=pltpu.CompilerParams(
            dimension_semantics=("arbitrary",), vmem_limit_bytes=VMEM_LIMIT),
        name="experts",
    )(blk_expert, n_used, buf, wg, wu, wd)


def _combine_ple_kernel(pos_ref, h1_ref, p_ref, rf_ref, yb_ref, wple_ref, gple_ref, wpg_ref,
                        gfin_ref, o_ref, ybuf, sem, *, final_norm):
    tm = h1_ref.shape[0]

    def row_copy(r, k):
        src = pos_ref[0, 0, 2 * r + k]
        return pltpu.make_async_copy(yb_ref.at[pl.ds(src, 1)], ybuf.at[k, pl.ds(r, 1)], sem)

    def issue(r, carry):
        row_copy(r, 0).start()
        row_copy(r, 1).start()
        return carry

    def drain(r, carry):
        row_copy(r, 0).wait()
        row_copy(r, 1).wait()
        return carry

    lax.fori_loop(0, tm, issue, 0)
    e = _rms(_mm(p_ref[...].astype(BF16), wple_ref[...]), gple_ref[...])
    lax.fori_loop(0, tm, drain, 0)

    rf = rf_ref[...]
    h2 = h1_ref[...] + ybuf[0] * rf[:, 0:1] + ybuf[1] * rf[:, 1:2]
    h3 = h2 + e * jax.nn.sigmoid(_mm(h2.astype(BF16), wpg_ref[...]))
    if final_norm:
        h3 = _rms(h3, gfin_ref[...])
    o_ref[...] = h3


def _combine_ple(pos3, h1, p, rf, yb, wple, gple, wpg, gfin, final_norm):
    t = h1.shape[0]
    tm = TOKEN_TILE
    row = lambda w: pl.BlockSpec((tm, w), lambda i: (i, 0))
    full = lambda a, b: pl.BlockSpec((a, b), lambda i: (0, 0))
    return pl.pallas_call(
        functools.partial(_combine_ple_kernel, final_norm=final_norm),
        grid=(t // tm,),
        in_specs=[
            pl.BlockSpec((1, 1, 2 * tm), lambda i: (i, 0, 0), memory_space=pltpu.SMEM),
            row(D_MODEL), row(PLE_DIM), row(LANES),
            pl.BlockSpec(memory_space=pl.ANY),
            full(PLE_DIM, D_MODEL), full(1, D_MODEL), full(D_MODEL, D_MODEL), full(1, D_MODEL),
        ],
        out_specs=row(D_MODEL),
        out_shape=jax.ShapeDtypeStruct((t, D_MODEL), F32),
        scratch_shapes=[pltpu.VMEM((2, tm, D_MODEL), F32), pltpu.SemaphoreType.DMA(())],
        compiler_params=pltpu.CompilerParams(
            dimension_semantics=("arbitrary",), vmem_limit_bytes=VMEM_LIMIT),
        name="combine_ple",
    )(pos3, h1, p, rf, yb, wple, gple, wpg, gfin)


def _rel_bucket(dist):
    n = jnp.maximum(dist, 0)
    max_exact = REL_BUCKETS // 2
    nf = jnp.maximum(n, 1).astype(F32)
    large = max_exact + (jnp.log(nf / max_exact) / math.log(REL_MAX_DIST / max_exact)
                         * (REL_BUCKETS - max_exact)).astype(jnp.int32)
    large = jnp.minimum(large, REL_BUCKETS - 1)
    return jnp.where(n < max_exact, n, large)


def _bias_tiles(rel_bias):
    qi = jnp.arange(WINDOW)[:, None]
    kj = jnp.arange(2 * WINDOW)[None, :]
    nat = jnp.transpose(rel_bias[_rel_bucket(qi + WINDOW - kj)], (2, 0, 1))
    dsa = nat[:DSA_HEADS]
    far = jnp.broadcast_to(rel_bias[REL_BUCKETS - 1, :DSA_HEADS][:, None, None],
                           (DSA_HEADS, WINDOW, WINDOW))
    return jnp.stack([far, dsa[:, :, :WINDOW], dsa[:, :, WINDOW:]], axis=1), nat[DSA_HEADS:]


def _pack_w_in(w):
    sizes = (128, 128, 256, 256, 16, 256, 64, 64, 512, 64, 8, 512, 128, 128)
    offs = [0]
    for s in sizes:
        offs.append(offs[-1] + s)
    gq, gk, gv, gg, ga, dq, dk, dv, iq, ik, iw, sq, sk, sv = (
        w[:, offs[n]:offs[n + 1]] for n in range(len(sizes)))
    z = lambda n: jnp.zeros((w.shape[0], n), w.dtype)
    packed = jnp.concatenate(
        [iq, sq, gv, gg, dq, gq, gk, ga, z(128 - GLA_RANK), dk, dv, ik, iw,
         z(128 - IDX_DIM - IDX_HEADS), sk, sv], axis=1)
    return packed.astype(BF16)


def kernel(x, p, rel_bias, g_mix, w_in, gla_w_alpha, gla_b_alpha, gla_g_norm, swa_sinks, w_out,
           g_ffn, w_router_group, b_router_group, w_router_expert, b_router_expert, w_expert_gate,
           w_expert_up, w_expert_down, w_ple, g_ple, w_ple_gate, g_final):
    bsz, seq, d = x.shape
    depth = w_in.shape[0]
    t = bsz * seq
    assert d == D_MODEL and t % TOKEN_TILE == 0 and seq % GLA_BLOCK == 0 and seq % KC == 0
    n_blocks = -(-(2 * t) // EXPERT_ROWS) + N_EXPERTS
    dsa_tiles, swa_bias = _bias_tiles(rel_bias)

    h = x.reshape(t, d)
    for i in range(depth):
        proj = _inproj(h, g_mix[i][None, :], _pack_w_in(w_in[i]))
        proj3 = proj.reshape(bsz, seq, D_PROJ)

        wal = jnp.zeros((128, 128), F32).at[:GLA_RANK].set(gla_w_alpha[i])
        og = _gla(proj3, wal, gla_b_alpha[i][None, :],
                  jnp.tile(gla_g_norm[i], GLA_HEADS)[None, :])
        od = _dsa(proj3, dsa_tiles)
        os_ = _swa(proj3, swa_sinks[i], swa_bias)

        w_r = jnp.concatenate([w_router_group[i], w_router_expert[i]], axis=1)
        w_r = jnp.pad(w_r, ((0, 0), (0, LANES - w_r.shape[1])))
        wr_hi, wr_lo = _split_bf16(w_r)
        b_r = jnp.pad(jnp.concatenate([b_router_group[i], b_router_expert[i]]),
                      (0, LANES - N_GROUPS - N_EXPERTS))[None, :]
        h1, xn, ri, rf, cnt = _outproj_router(
            h, og.reshape(t, -1), od.reshape(t, -1), os_.reshape(t, -1), w_out[i].astype(BF16),
            g_ffn[i][None, :], wr_hi, wr_lo, b_r)

        counts = cnt[0, ROUTE_OFF:ROUTE_OFF + N_EXPERTS].astype(jnp.int32)
        padded = (counts + EXPERT_ROWS - 1) // EXPERT_ROWS * EXPERT_ROWS
        pad_end = jnp.cumsum(padded)
        pad_start = pad_end - padded
        expert_ids = jnp.arange(N_EXPERTS, dtype=jnp.int32)
        slot_start = jnp.sum(
            jnp.where(ri[:, 0:2, None] == expert_ids, pad_start, 0), axis=-1)
        pos3 = (slot_start + ri[:, 2:4]).reshape(t // TOKEN_TILE, 1, 2 * TOKEN_TILE)
        blk_start = jnp.arange(n_blocks, dtype=jnp.int32) * EXPERT_ROWS
        blk_expert = jnp.minimum(
            jnp.sum((pad_end[None, :] <= blk_start[:, None]).astype(jnp.int32), axis=1),
            N_EXPERTS - 1)
        n_used = (pad_end[-1:] // EXPERT_ROWS).astype(jnp.int32)

        buf = _dispatch(pos3, xn, jnp.zeros((n_blocks * EXPERT_ROWS, d), F32))
        yb = _experts(blk_expert, n_used, buf, w_expert_gate[i].astype(BF16),
                      w_expert_up[i].astype(BF16), w_expert_down[i].astype(BF16))
        h = _combine_ple(pos3, h1, p[i].reshape(t, PLE_DIM), rf, yb, w_ple[i].astype(BF16),
                         g_ple[i][None, :], w_ple_gate[i].astype(BF16), g_final[None, :],
                         final_norm=(i == depth - 1))
    return h.reshape(bsz, seq, d)
```

```python
import functools
import math

import jax
import jax.numpy as jnp
from jax import lax
from jax.experimental import pallas as pl
from jax.experimental.pallas import tpu as pltpu

F32 = jnp.float32
BF16 = jnp.bfloat16
HIGHEST = lax.Precision.HIGHEST

D_MODEL = 1024
HEAD_DIM = 64
GLA_HEADS = 4
GLA_DK = 32
GLA_DV = 64
GLA_RANK = 16
GLA_TAU = 16.0
GLA_CHUNK = 64
DSA_HEADS = 4
IDX_HEADS = 8
IDX_DIM = 64
DSA_TOPK_MAX = 256
SWA_HEADS = 8
SWA_KV_HEADS = 2
SWA_GROUP = SWA_HEADS // SWA_KV_HEADS
WINDOW = 128
REL_BUCKETS = 32
REL_MAX_DIST = 128
N_GROUPS = 4
EXPERTS_PER_GROUP = 8
N_EXPERTS = N_GROUPS * EXPERTS_PER_GROUP
D_EXPERT = 512
PLE_DIM = 256
EPS = 1e-6

LANES = 128
SUBLANES = 8

COL_IQ, COL_SQ = 0, 512
COL_GV, COL_GG, COL_DQ = 1024, 1280, 1536
COL_GQ, COL_GK, COL_GA, COL_DKV, COL_IKW, COL_SK, COL_SV = 1792, 1920, 2048, 2176, 2304, 2432, 2560
D_PROJ = 2688

TOKEN_TILE = 256
EXPERT_ROWS = 256
GLA_BLOCK = 256
QB = 128
KC = 256
DMA_UNROLL = 8
NEG_BIG = -1e30
VMEM_LIMIT = 48 * 1024 * 1024


def _nt(a, b, precision=None):
    return lax.dot_general(a, b, (((1,), (1,)), ((), ())), precision=precision,
                           preferred_element_type=F32)


def _mm(a, b, precision=None):
    return jnp.dot(a, b, precision=precision, preferred_element_type=F32)


def _eye(n, dtype):
    r = lax.broadcasted_iota(jnp.int32, (n, n), 0)
    c = lax.broadcasted_iota(jnp.int32, (n, n), 1)
    return jnp.where(r == c, 1.0, 0.0).astype(dtype)


def _rms(x, g):
    return x * lax.rsqrt(jnp.mean(x * x, axis=-1, keepdims=True) + EPS) * g


def _inproj_kernel(h_ref, g_ref, w_ref, o_ref):
    a = _rms(h_ref[...], g_ref[...])
    o_ref[...] = _nt(a.astype(BF16), w_ref[...])


def _inproj(h, g, w):
    t = h.shape[0]
    return pl.pallas_call(
        _inproj_kernel,
        grid=(t // TOKEN_TILE,),
        in_specs=[
            pl.BlockSpec((TOKEN_TILE, D_MODEL), lambda i: (i, 0)),
            pl.BlockSpec((1, D_MODEL), lambda i: (0, 0)),
            pl.BlockSpec((D_PROJ, D_MODEL), lambda i: (0, 0)),
        ],
        out_specs=pl.BlockSpec((TOKEN_TILE, D_PROJ), lambda i: (i, 0)),
        out_shape=jax.ShapeDtypeStruct((t, D_PROJ), F32),
        compiler_params=pltpu.CompilerParams(
            dimension_semantics=("arbitrary",), vmem_limit_bytes=VMEM_LIMIT),
        name="inproj",
    )(h, g, w)


def _gla_kernel(q_ref, k_ref, v_ref, gg_ref, ga_ref, wal_ref, bal_ref, gn_ref, o_ref,
                state_ref, oc_ref):
    hk = GLA_HEADS * GLA_DK
    hv = GLA_HEADS * GLA_DV
    c = GLA_CHUNK

    @pl.when(pl.program_id(1) == 0)
    def _():
        state_ref[...] = jnp.zeros_like(state_ref)

    r64 = lax.broadcasted_iota(jnp.int32, (c, c), 0)
    c64 = lax.broadcasted_iota(jnp.int32, (c, c), 1)
    tril = jnp.where(r64 >= c64, 1.0, 0.0).astype(F32)
    rk = lax.broadcasted_iota(jnp.int32, (hk, hv), 0) // GLA_DK
    cv = lax.broadcasted_iota(jnp.int32, (hk, hv), 1) // GLA_DV
    blockdiag = jnp.where(rk == cv, 1.0, 0.0).astype(F32)
    expand = blockdiag.astype(BF16)
    eye_k = _eye(hk, F32)
    eye_kb = eye_k.astype(BF16)
    row8 = lax.broadcasted_iota(jnp.int32, (SUBLANES, hk), 0)

    def chunk(ci, carry):
        rows = pl.ds(pl.multiple_of(ci * c, c), c)
        q = q_ref[0, rows, :] * (GLA_DK ** -0.5)
        k = k_ref[0, rows, :]
        v = v_ref[0, rows, :]
        z = _mm(ga_ref[0, rows, :], wal_ref[...], HIGHEST) + bal_ref[...]
        log_a = (jnp.minimum(z, 0.0) - jnp.log1p(jnp.exp(-jnp.abs(z)))) * (1.0 / GLA_TAU)
        b = _mm(tril, log_a, HIGHEST)
        state = state_ref[...]
        o_inter = _mm((q * jnp.exp(b)).astype(BF16), state.astype(BF16))

        for t in range(c):
            ns = SUBLANES * (t // SUBLANES + 1)
            d = b[t:t + 1, :] - b[:ns, :]
            p = jnp.exp(jnp.minimum(d, 0.0)) * k[:ns, :] * q[t:t + 1, :]
            if t % SUBLANES != SUBLANES - 1:
                srow = lax.broadcasted_iota(jnp.int32, (ns, hk), 0)
                p = jnp.where(srow <= t, p, 0.0)
            sc = _mm(p.astype(BF16), expand)
            oc_ref[t:t + 1, :] = jnp.sum(sc * v[:ns, :], axis=0, keepdims=True)
        o = o_inter + oc_ref[...]

        b_last = b[c - 1:c, :]
        kd = (k * jnp.exp(b_last - b)).astype(BF16)
        kd_t = _nt(eye_kb, kd).astype(BF16)
        upd = _mm(kd_t, v.astype(BF16))
        bl8 = jnp.where(row8 >= 0, b_last, 0.0)
        bl_t = _nt(eye_k, bl8, HIGHEST)[:, 0:1]
        state_ref[...] = jnp.exp(bl_t) * state + upd * blockdiag

        gg = gg_ref[0, rows, :]
        outs = []
        for h in range(GLA_HEADS):
            oh = o[:, h * GLA_DV:(h + 1) * GLA_DV]
            ms = jnp.mean(oh * oh, axis=-1, keepdims=True)
            outs.append(oh * lax.rsqrt(ms + EPS))
        on = jnp.concatenate(outs, axis=-1) * gn_ref[...]
        o_ref[0, rows, :] = (on * (gg * jax.nn.sigmoid(gg))).astype(o_ref.dtype)
        return carry

    lax.fori_loop(0, GLA_BLOCK // c, chunk, 0)


def _gla(proj3, wal, bal, gn):
    bsz, seq, _ = proj3.shape
    nb = seq // GLA_BLOCK

    def col(width, off):
        return pl.BlockSpec((1, GLA_BLOCK, width), lambda b, i: (b, i, off // width))

    return pl.pallas_call(
        _gla_kernel,
        grid=(bsz, nb),
        in_specs=[
            col(128, COL_GQ), col(128, COL_GK), col(256, COL_GV), col(256, COL_GG), col(128, COL_GA),
            pl.BlockSpec((128, 128), lambda b, i: (0, 0)),
            pl.BlockSpec((1, 128), lambda b, i: (0, 0)),
            pl.BlockSpec((1, 256), lambda b, i: (0, 0)),
        ],
        out_specs=pl.BlockSpec((1, GLA_BLOCK, 256), lambda b, i: (b, i, 0)),
        out_shape=jax.ShapeDtypeStruct((bsz, seq, GLA_HEADS * GLA_DV), BF16),
        scratch_shapes=[
            pltpu.VMEM((GLA_HEADS * GLA_DK, GLA_HEADS * GLA_DV), F32),
            pltpu.VMEM((GLA_CHUNK, GLA_HEADS * GLA_DV), F32),
        ],
        compiler_params=pltpu.CompilerParams(
            dimension_semantics=("arbitrary", "arbitrary"), vmem_limit_bytes=VMEM_LIMIT),
        name="gla",
    )(proj3, proj3, proj3, proj3, proj3, wal, bal, gn)


DSA_BISECT_STEPS = 12
DSA_ACCUMULATORS = 4


def _dsa_kernel(dq_ref, kv_ref, iq_ref, ikw_ref, btile_ref, o_ref,
                sc_ref, lg_ref, p_ref, kb_ref, vb_ref, mp_ref, lp_ref, acc_ref, cut_ref, *, n_sel):
    j = pl.program_id(1)
    nkc = (j + 2) // 2
    seq = kv_ref.shape[1]
    ksel = float(n_sel)

    @pl.when(j == 0)
    def _():
        kv = kv_ref[0]
        kb_ref[...] = kv[:, :HEAD_DIM].astype(BF16)
        vb_ref[...] = kv[:, HEAD_DIM:].astype(BF16)

    def rows(c):
        return pl.ds(pl.multiple_of(c * KC, KC), KC)

    wide = DSA_ACCUMULATORS * SUBLANES

    def fold(op, x):
        return op(x.reshape(KC // wide, wide, QB), axis=0)

    def all8(op, xw, roll_op):
        x8 = op(xw.reshape(DSA_ACCUMULATORS, SUBLANES, QB), axis=0)
        for shift in (4, 2, 1):
            x8 = roll_op(x8, pltpu.roll(x8, shift, 0))
        return x8

    def widen(x8):
        return jnp.concatenate([x8] * DSA_ACCUMULATORS, axis=0)

    zeros8 = jnp.zeros((SUBLANES, QB), F32)
    zerosw = jnp.zeros((wide, QB), F32)
    infw = jnp.full((wide, QB), jnp.inf, F32)

    ikw_q = ikw_ref[0, pl.ds(pl.multiple_of(j * QB, QB), QB), :]
    sel_r = lax.broadcasted_iota(jnp.int32, (SUBLANES, LANES), 0)
    sel_c = lax.broadcasted_iota(jnp.int32, (SUBLANES, LANES), 1)
    pick_w = jnp.where(sel_c == sel_r + IDX_DIM, 1.0, 0.0).astype(F32)
    w_t = _nt(pick_w, ikw_q, HIGHEST) * (IDX_HEADS ** -0.5 * IDX_DIM ** -0.5)
    iq = iq_ref[0].astype(BF16)
    iq_pairs = [
        jnp.concatenate([iq[:, (2 * p) * IDX_DIM:(2 * p + 1) * IDX_DIM],
                         iq[:, (2 * p + 1) * IDX_DIM:(2 * p + 2) * IDX_DIM]], axis=0)
        for p in range(IDX_HEADS // 2)]

    s_loc = lax.broadcasted_iota(jnp.int32, (KC, QB), 0)
    t_glob = j * QB + lax.broadcasted_iota(jnp.int32, (KC, QB), 1)
    s_loc_f = s_loc.astype(F32)

    def score_chunk(c, st):
        mn8, mx8, ge8, gt8 = st
        ik = ikw_ref[0, rows(c), :][:, :IDX_DIM].astype(BF16)
        acc = jnp.zeros((KC, QB), F32)
        for p in range(IDX_HEADS // 2):
            rel = _nt(ik, iq_pairs[p])
            acc = (acc + jnp.maximum(rel[:, :QB], 0.0) * w_t[2 * p:2 * p + 1, :]
                   + jnp.maximum(rel[:, QB:], 0.0) * w_t[2 * p + 1:2 * p + 2, :])
        adm = s_loc + c * KC <= t_glob
        blk = jnp.where(adm, acc, -jnp.inf)
        sc_ref[rows(c), :] = blk
        mn8 = jnp.minimum(mn8, fold(jnp.min, jnp.where(adm, acc, jnp.inf)))
        mx8 = jnp.maximum(mx8, fold(jnp.max, blk))
        ge8 = ge8 + fold(jnp.sum, jnp.where(blk >= 0.0, 1.0, 0.0))
        gt8 = gt8 + fold(jnp.sum, jnp.where(blk > 0.0, 1.0, 0.0))
        return mn8, mx8, ge8, gt8

    mn8, mx8, ge8, gt8 = lax.fori_loop(0, nkc, score_chunk, (infw, -infw, zerosw, zerosw))
    row_min = all8(jnp.min, mn8, jnp.minimum)
    row_max = all8(jnp.max, mx8, jnp.maximum)
    c_ge0, c_gt0 = all8(jnp.sum, ge8, jnp.add), all8(jnp.sum, gt8, jnp.add)

    def blocks(c):
        return sc_ref[rows(c), :].reshape(KC // wide, wide, QB)

    def count_ge(x8):
        xw = widen(x8)[None]

        def body(c, cw):
            return cw + jnp.sum(jnp.where(blocks(c) >= xw, 1.0, 0.0), axis=0)
        return all8(jnp.sum, lax.fori_loop(0, nkc, body, zerosw), jnp.add)

    def min_ge(x8):
        xw = widen(x8)[None]

        def body(c, mw):
            blk = blocks(c)
            return jnp.minimum(mw, jnp.min(jnp.where(blk >= xw, blk, jnp.inf), axis=0))
        return all8(jnp.min, lax.fori_loop(0, nkc, body, infw), jnp.minimum)

    def pass_gt(x8):
        xw = widen(x8)[None]

        def body(c, st):
            cw, mw = st
            blk = blocks(c)
            hit = blk > xw
            return (cw + jnp.sum(jnp.where(hit, 1.0, 0.0), axis=0),
                    jnp.minimum(mw, jnp.min(jnp.where(hit, blk, jnp.inf), axis=0)))
        cw, mw = lax.fori_loop(0, nkc, body, (zerosw, infw))
        return all8(jnp.sum, cw, jnp.add), all8(jnp.min, mw, jnp.minimum)

    n_adm = (j * QB + lax.broadcasted_iota(jnp.int32, (SUBLANES, QB), 1) + 1).astype(F32)
    at_zero = (c_gt0 < ksel) & (c_ge0 >= ksel)
    above = c_gt0 >= ksel
    lo = jnp.where(above | at_zero, 0.0, row_min)
    c_lo = jnp.where(above | at_zero, c_ge0, n_adm)
    settled = at_zero | (n_adm <= ksel)
    hi = jnp.where(settled, lo, jnp.where(above, row_max, 0.0))

    def bisect(_, st):
        lo, hi, c_lo = st
        mid = lo + (hi - lo) * 0.5
        cnt = count_ge(mid)
        up = cnt >= ksel
        return jnp.where(up, mid, lo), jnp.where(up, hi, mid), jnp.where(up, cnt, c_lo)

    lo, _, c_lo = lax.fori_loop(0, DSA_BISECT_STEPS, bisect, (lo, hi, c_lo))

    v0 = min_ge(lo)
    done0 = jnp.where(settled | (c_lo == ksel), 1.0, 0.0)
    c_gt_init = jnp.where(at_zero, c_gt0, 0.0)

    def walk_cond(st):
        return jnp.min(st[3]) < 0.5

    def walk_body(st):
        v, c_ge, c_gt, dn = st
        cnt, vnext = pass_gt(v)
        live = dn < 0.5
        fin = live & (cnt < ksel)
        step = live & (cnt >= ksel)
        c_gt = jnp.where(fin, cnt, c_gt)
        dn = jnp.where(fin, 1.0, dn)
        v = jnp.where(step, vnext, v)
        c_ge = jnp.where(step, cnt, c_ge)
        return v, c_ge, c_gt, dn

    tau8, c_ge, c_gt, _ = lax.while_loop(walk_cond, walk_body, (v0, c_lo, c_gt_init, done0))

    need = ksel - c_gt
    cut_ref[...] = jnp.full(cut_ref.shape, float(seq), F32)

    @pl.when(jnp.max(c_ge) > ksel)
    def _():
        s_grp = lax.broadcasted_iota(jnp.int32, (KC // wide, wide, QB), 0) * wide
        s_sub = lax.broadcasted_iota(jnp.int32, (KC // wide, wide, QB), 1)
        s_idx = (s_grp + s_sub).astype(F32)
        tauw = widen(tau8)[None]

        def count_ties_below(m8):
            mw = widen(m8)[None]

            def body(c, cw):
                hit = (blocks(c) == tauw) & (s_idx + (c * KC).astype(F32) < mw)
                return cw + jnp.sum(jnp.where(hit, 1.0, 0.0), axis=0)
            return all8(jnp.sum, lax.fori_loop(0, nkc, body, zerosw), jnp.add)

        def idx_bisect(_, lh):
            lo_m, hi_m = lh
            mid = jnp.floor((lo_m + hi_m) * 0.5)
            ok = count_ties_below(mid) >= need
            return jnp.where(ok, lo_m, mid), jnp.where(ok, mid, hi_m)

        _, hi_m = lax.fori_loop(0, int(math.log2(seq)) + 1, idx_bisect,
                                (zeros8, zeros8 + float(seq)))
        cut_ref[...] = jnp.where(c_ge > ksel, hi_m, float(seq))

    tau = tau8[0:1, :]
    cut = cut_ref[0:1, :]

    q = (dq_ref[0] * (HEAD_DIM ** -0.5)).astype(BF16)
    q4 = jnp.concatenate([q[:, h * HEAD_DIM:(h + 1) * HEAD_DIM] for h in range(DSA_HEADS)],
                         axis=0)
    mp_ref[...] = jnp.full(mp_ref.shape, NEG_BIG, F32)
    lp_ref[...] = jnp.zeros(lp_ref.shape, F32)

    half = seq // 2
    second_half = (j + 1) * QB > half
    lg_ref[:, :half] = _nt(q4, kb_ref[:half, :])

    @pl.when(second_half)
    def _():
        lg_ref[:, half:] = _nt(q4, kb_ref[half:, :])

    def cols(c):
        return pl.ds(pl.multiple_of(c * KC, KC), KC)

    def logits_chunk(c, carry):
        blk = sc_ref[rows(c), :]
        sidx = s_loc_f + (c * KC).astype(F32)
        sel = (blk > tau) | ((blk == tau) & (sidx < cut))
        neg_t = jnp.where(sel, 0.0, NEG_BIG).T
        w0 = jnp.clip(2 * c - j + 2, 0, 2)
        w1 = jnp.clip(2 * c + 1 - j + 2, 0, 2)
        for h in range(DSA_HEADS):
            hq = slice(h * QB, (h + 1) * QB)
            bias = jnp.concatenate([btile_ref[h, w0], btile_ref[h, w1]], axis=1)
            lg = lg_ref[hq, cols(c)] + bias + neg_t
            lg_ref[hq, cols(c)] = lg
            mp_ref[h] = jnp.maximum(mp_ref[h], jnp.maximum(lg[:, :QB], lg[:, QB:]))
        return carry

    lax.fori_loop(0, nkc, logits_chunk, 0)
    m_rows = [jnp.max(mp_ref[h], axis=-1, keepdims=True) for h in range(DSA_HEADS)]

    def probs_chunk(c, carry):
        for h in range(DSA_HEADS):
            hq = slice(h * QB, (h + 1) * QB)
            p = jnp.exp(lg_ref[hq, cols(c)] - m_rows[h])
            lp_ref[h] = lp_ref[h] + p[:, :QB] + p[:, QB:]
            p_ref[hq, cols(c)] = p.astype(BF16)
        return carry

    def zero_chunk(c, carry):
        p_ref[:, cols(c)] = jnp.zeros((DSA_HEADS * QB, KC), BF16)
        return carry

    lax.fori_loop(0, nkc, probs_chunk, 0)
    lax.fori_loop(nkc, jnp.where(second_half, seq // KC, half // KC), zero_chunk, 0)
    acc_ref[...] = _mm(p_ref[:, :half], vb_ref[:half, :])

    @pl.when(second_half)
    def _():
        acc_ref[...] = acc_ref[...] + _mm(p_ref[:, half:], vb_ref[half:, :])

    outs = []
    for h in range(DSA_HEADS):
        l = jnp.sum(lp_ref[h], axis=-1, keepdims=True)
        outs.append(acc_ref[h * QB:(h + 1) * QB, :] / l)
    o_ref[0] = jnp.concatenate(outs, axis=-1).astype(o_ref.dtype)


def _dsa(proj3, btiles):
    bsz, seq, _ = proj3.shape
    n_sel = min(DSA_TOPK_MAX, seq // 4)
    return pl.pallas_call(
        functools.partial(_dsa_kernel, n_sel=n_sel),
        grid=(bsz, seq // QB),
        in_specs=[
            pl.BlockSpec((1, QB, 256), lambda b, j: (b, j, COL_DQ // 256)),
            pl.BlockSpec((1, seq, 128), lambda b, j: (b, 0, COL_DKV // 128)),
            pl.BlockSpec((1, QB, 512), lambda b, j: (b, j, COL_IQ // 512)),
            pl.BlockSpec((1, seq, 128), lambda b, j: (b, 0, COL_IKW // 128)),
            pl.BlockSpec((DSA_HEADS, 3, QB, QB), lambda b, j: (0, 0, 0, 0)),
        ],
        out_specs=pl.BlockSpec((1, QB, 256), lambda b, j: (b, j, 0)),
        out_shape=jax.ShapeDtypeStruct((bsz, seq, DSA_HEADS * HEAD_DIM), BF16),
        scratch_shapes=[
            pltpu.VMEM((seq, QB), F32),
            pltpu.VMEM((DSA_HEADS * QB, seq), F32),
            pltpu.VMEM((DSA_HEADS * QB, seq), BF16),
            pltpu.VMEM((seq, HEAD_DIM), BF16),
            pltpu.VMEM((seq, HEAD_DIM), BF16),
            pltpu.VMEM((DSA_HEADS, QB, QB), F32),
            pltpu.VMEM((DSA_HEADS, QB, QB), F32),
            pltpu.VMEM((DSA_HEADS * QB, HEAD_DIM), F32),
            pltpu.VMEM((SUBLANES, QB), F32),
        ],
        compiler_params=pltpu.CompilerParams(
            dimension_semantics=("arbitrary", "arbitrary"), vmem_limit_bytes=VMEM_LIMIT),
        name="dsa",
    )(proj3, proj3, proj3, proj3, btiles)


def _swa_kernel(sink_ref, q_ref, kc_ref, kp_ref, vc_ref, vp_ref, bias_ref, o_ref):
    n = pl.program_id(1)
    q = (q_ref[0] * (HEAD_DIM ** -0.5)).astype(BF16)
    k2 = jnp.concatenate([kp_ref[0], kc_ref[0]], axis=0).astype(BF16)
    v2 = jnp.concatenate([vp_ref[0], vc_ref[0]], axis=0).astype(BF16)
    qi = lax.broadcasted_iota(jnp.int32, (WINDOW, 2 * WINDOW), 0)
    kj = lax.broadcasted_iota(jnp.int32, (WINDOW, 2 * WINDOW), 1)
    dist = qi + WINDOW - kj
    mask = (dist >= 0) & (dist < WINDOW) & ((kj >= WINDOW) | (n > 0))
    outs = []
    for h in range(SWA_HEADS):
        kvh = h // SWA_GROUP
        kh = k2[:, kvh * HEAD_DIM:(kvh + 1) * HEAD_DIM]
        vh = v2[:, kvh * HEAD_DIM:(kvh + 1) * HEAD_DIM]
        lg = _nt(q[:, h * HEAD_DIM:(h + 1) * HEAD_DIM], kh) + bias_ref[h]
        lg = jnp.where(mask, lg, -jnp.inf)
        sink = sink_ref[h]
        m = jnp.maximum(jnp.max(lg, axis=-1, keepdims=True), sink)
        e = jnp.exp(lg - m)
        den = jnp.sum(e, axis=-1, keepdims=True) + jnp.exp(sink - m)
        outs.append(_mm((e / den).astype(BF16), vh))
    o_ref[0] = jnp.concatenate(outs, axis=-1).astype(o_ref.dtype)


def _swa(proj3, sinks, bias_nat):
    bsz, seq, _ = proj3.shape
    return pl.pallas_call(
        _swa_kernel,
        grid=(bsz, seq // WINDOW),
        in_specs=[
            pl.BlockSpec(memory_space=pltpu.SMEM),
            pl.BlockSpec((1, WINDOW, 512), lambda b, n: (b, n, COL_SQ // 512)),
            pl.BlockSpec((1, WINDOW, 128), lambda b, n: (b, n, COL_SK // 128)),
            pl.BlockSpec((1, WINDOW, 128), lambda b, n: (b, jnp.maximum(n - 1, 0), COL_SK // 128)),
            pl.BlockSpec((1, WINDOW, 128), lambda b, n: (b, n, COL_SV // 128)),
            pl.BlockSpec((1, WINDOW, 128), lambda b, n: (b, jnp.maximum(n - 1, 0), COL_SV // 128)),
            pl.BlockSpec((SWA_HEADS, WINDOW, 2 * WINDOW), lambda b, n: (0, 0, 0)),
        ],
        out_specs=pl.BlockSpec((1, WINDOW, 512), lambda b, n: (b, n, 0)),
        out_shape=jax.ShapeDtypeStruct((bsz, seq, SWA_HEADS * HEAD_DIM), BF16),
        compiler_params=pltpu.CompilerParams(
            dimension_semantics=("arbitrary", "arbitrary"), vmem_limit_bytes=VMEM_LIMIT),
        name="swa",
    )(sinks, proj3, proj3, proj3, proj3, proj3, bias_nat)


ROUTE_OFF = N_GROUPS


def _split_bf16(x):
    hi = x.astype(BF16)
    lo = (x - hi.astype(F32)).astype(BF16)
    return hi, lo


def _outproj_router_kernel(h_ref, og_ref, od_ref, os_ref, wo_ref, gffn_ref, wr_hi_ref, wr_lo_ref,
                           br_ref, h1_ref, xn_ref, ri_ref, rf_ref, cnt_ref, run_ref):
    tm = h_ref.shape[0]

    @pl.when(pl.program_id(0) == 0)
    def _():
        run_ref[...] = jnp.zeros_like(run_ref)

    o = jnp.concatenate([og_ref[...], od_ref[...], os_ref[...]], axis=-1)
    h1 = h_ref[...] + _mm(o, wo_ref[...])
    h1_ref[...] = h1
    xn = _rms(h1, gffn_ref[...])
    xn_ref[...] = xn

    x_hi, x_lo = _split_bf16(xn)
    lg = (_mm(x_hi, wr_hi_ref[...]) + _mm(x_lo, wr_hi_ref[...]) + _mm(x_hi, wr_lo_ref[...])
          + br_ref[...])

    lane = lax.broadcasted_iota(jnp.int32, lg.shape, 1)
    lane_f = lane.astype(F32)
    ninf = -jnp.inf

    def first_max(x):
        m = jnp.max(x, axis=-1, keepdims=True)
        idx = jnp.min(jnp.where(x == m, lane_f, float(LANES)), axis=-1, keepdims=True)
        return m, idx

    gl = jnp.where(lane < N_GROUPS, lg, ninf)
    gmax, gsel = first_max(gl)
    g_w = 1.0 / jnp.sum(jnp.exp(gl - gmax), axis=-1, keepdims=True)
    e_lo = ROUTE_OFF + EXPERTS_PER_GROUP * gsel
    el = jnp.where((lane_f >= e_lo) & (lane_f < e_lo + EXPERTS_PER_GROUP), lg, ninf)
    m1, i1 = first_max(el)
    eden = jnp.sum(jnp.exp(el - m1), axis=-1, keepdims=True)
    m2, i2 = first_max(jnp.where(lane_f == i1, ninf, el))
    p1 = 1.0 / eden
    p2 = jnp.exp(m2 - m1) / eden
    gate1 = g_w * p1 / (p1 + p2)
    gate2 = g_w * p2 / (p1 + p2)

    onehot = jnp.where((lane_f == i1) | (lane_f == i2), 1.0, 0.0)
    rr = lax.broadcasted_iota(jnp.int32, (tm, tm), 0)
    cc = lax.broadcasted_iota(jnp.int32, (tm, tm), 1)
    strict = jnp.where(rr > cc, 1.0, 0.0).astype(BF16)
    before = _mm(strict, onehot.astype(BF16)) + run_ref[0:1, :]
    rank1 = jnp.sum(jnp.where(lane_f == i1, before, 0.0), axis=-1, keepdims=True)
    rank2 = jnp.sum(jnp.where(lane_f == i2, before, 0.0), axis=-1, keepdims=True)
    run_ref[0:1, :] = run_ref[0:1, :] + jnp.sum(onehot, axis=0, keepdims=True)
    cnt_ref[...] = run_ref[...]

    ints = jnp.where(lane == 0, i1 - ROUTE_OFF,
                     jnp.where(lane == 1, i2 - ROUTE_OFF,
                               jnp.where(lane == 2, rank1, jnp.where(lane == 3, rank2, 0.0))))
    ri_ref[...] = ints.astype(jnp.int32)
    rf_ref[...] = jnp.where(lane == 0, gate1, jnp.where(lane == 1, gate2, 0.0))


def _outproj_router(h, og, od, os_, wo, gffn, wr_hi, wr_lo, br):
    t = h.shape[0]
    tm = TOKEN_TILE
    row = lambda w: pl.BlockSpec((tm, w), lambda i: (i, 0))
    full = lambda a, b: pl.BlockSpec((a, b), lambda i: (0, 0))
    return pl.pallas_call(
        _outproj_router_kernel,
        grid=(t // tm,),
        in_specs=[row(D_MODEL), row(256), row(256), row(512), full(D_MODEL, D_MODEL),
                  full(1, D_MODEL), full(D_MODEL, LANES), full(D_MODEL, LANES), full(1, LANES)],
        out_specs=[row(D_MODEL), row(D_MODEL), row(LANES), row(LANES), full(SUBLANES, LANES)],
        out_shape=[
            jax.ShapeDtypeStruct((t, D_MODEL), F32),
            jax.ShapeDtypeStruct((t, D_MODEL), F32),
            jax.ShapeDtypeStruct((t, LANES), jnp.int32),
            jax.ShapeDtypeStruct((t, LANES), F32),
            jax.ShapeDtypeStruct((SUBLANES, LANES), F32),
        ],
        scratch_shapes=[pltpu.VMEM((SUBLANES, LANES), F32)],
        compiler_params=pltpu.CompilerParams(
            dimension_semantics=("arbitrary",), vmem_limit_bytes=VMEM_LIMIT),
        name="outproj_router",
    )(h, og, od, os_, wo, gffn, wr_hi, wr_lo, br)


def _dispatch_kernel(pos_ref, xn_ref, buf_in_ref, buf_ref, sem):
    del buf_in_ref
    tm = xn_ref.shape[0]

    def row_copy(r, k):
        dst = pos_ref[0, 0, 2 * r + k]
        return pltpu.make_async_copy(xn_ref.at[pl.ds(r, 1)], buf_ref.at[pl.ds(dst, 1)], sem)

    def issue(g, carry):
        for u in range(DMA_UNROLL):
            row_copy(g * DMA_UNROLL + u, 0).start()
            row_copy(g * DMA_UNROLL + u, 1).start()
        return carry

    def drain(g, carry):
        for u in range(DMA_UNROLL):
            row_copy(g * DMA_UNROLL + u, 0).wait()
            row_copy(g * DMA_UNROLL + u, 1).wait()
        return carry

    lax.fori_loop(0, tm // DMA_UNROLL, issue, 0)
    lax.fori_loop(0, tm // DMA_UNROLL, drain, 0)


def _dispatch(pos3, xn, buf0):
    t = xn.shape[0]
    tm = TOKEN_TILE
    return pl.pallas_call(
        _dispatch_kernel,
        grid=(t // tm,),
        in_specs=[
            pl.BlockSpec((1, 1, 2 * tm), lambda i: (i, 0, 0), memory_space=pltpu.SMEM),
            pl.BlockSpec((tm, D_MODEL), lambda i: (i, 0)),
            pl.BlockSpec(memory_space=pl.ANY),
        ],
        out_specs=pl.BlockSpec(memory_space=pl.ANY),
        out_shape=jax.ShapeDtypeStruct(buf0.shape, buf0.dtype),
        scratch_shapes=[pltpu.SemaphoreType.DMA(())],
        input_output_aliases={2: 0},
        compiler_params=pltpu.CompilerParams(
            dimension_semantics=("arbitrary",), vmem_limit_bytes=VMEM_LIMIT),
        name="dispatch",
    )(pos3, xn, buf0)


def _expert_kernel(be_ref, nu_ref, x_ref, wg_ref, wu_ref, wd_ref, y_ref, wg_s, wu_s, wd_s):
    i = pl.program_id(0)
    used = i < nu_ref[0]
    new_expert = (i == 0) | (be_ref[i] != be_ref[jnp.maximum(i - 1, 0)])

    @pl.when(used & new_expert)
    def _():
        wg_s[...] = wg_ref[0, 0].astype(BF16)
        wu_s[...] = wu_ref[0, 0].astype(BF16)
        wd_s[...] = wd_ref[0, 0].astype(BF16)

    @pl.when(used)
    def _():
        x = x_ref[...].astype(BF16)
        g = _mm(x, wg_s[...])
        u = _mm(x, wu_s[...])
        hmid = (g * jax.nn.sigmoid(g)) * u
        y_ref[...] = _mm(hmid.astype(BF16), wd_s[...])

    @pl.when(i >= nu_ref[0])
    def _():
        y_ref[...] = jnp.zeros_like(y_ref)


def _experts(blk_expert, n_used, buf, wg, wu, wd, layer):
    nrows = buf.shape[0]
    nblk = nrows // EXPERT_ROWS
    w_index = lambda i, be, nu: (layer, be[i], 0, 0)
    return pl.pallas_call(
        _expert_kernel,
        grid_spec=pltpu.PrefetchScalarGridSpec(
            num_scalar_prefetch=2,
            grid=(nblk,),
            in_specs=[
                pl.BlockSpec((EXPERT_ROWS, D_MODEL), lambda i, be, nu: (i, 0)),
                pl.BlockSpec((1, 1, D_MODEL, D_EXPERT), w_index),
                pl.BlockSpec((1, 1, D_MODEL, D_EXPERT), w_index),
                pl.BlockSpec((1, 1, D_EXPERT, D_MODEL), w_index),
            ],
            out_specs=pl.BlockSpec((EXPERT_ROWS, D_MODEL), lambda i, be, nu: (i, 0)),
            scratch_shapes=[
                pltpu.VMEM((D_MODEL, D_EXPERT), BF16),
                pltpu.VMEM((D_MODEL, D_EXPERT), BF16),
                pltpu.VMEM((D_EXPERT, D_MODEL), BF16),
            ],
        ),
        out_shape=jax.ShapeDtypeStruct((nrows, D_MODEL), F32),
        compiler_params=pltpu.CompilerParams(
            dimension_semantics=("arbitrary",), vmem_limit_bytes=VMEM_LIMIT),
        name="experts",
    )(blk_expert, n_used, buf, wg, wu, wd)


def _combine_ple_kernel(pos_ref, h1_ref, p_ref, rf_ref, yb_ref, wple_ref, gple_ref, wpg_ref,
                        gfin_ref, o_ref, ybuf, sem, *, final_norm):
    tm = h1_ref.shape[0]

    def row_copy(r, k):
        src = pos_ref[0, 0, 2 * r + k]
        return pltpu.make_async_copy(yb_ref.at[pl.ds(src, 1)], ybuf.at[k, pl.ds(r, 1)], sem)

    def issue(g, carry):
        for u in range(DMA_UNROLL):
            row_copy(g * DMA_UNROLL + u, 0).start()
            row_copy(g * DMA_UNROLL + u, 1).start()
        return carry

    def drain(g, carry):
        for u in range(DMA_UNROLL):
            row_copy(g * DMA_UNROLL + u, 0).wait()
            row_copy(g * DMA_UNROLL + u, 1).wait()
        return carry

    lax.fori_loop(0, tm // DMA_UNROLL, issue, 0)
    e = _rms(_mm(p_ref[...].astype(BF16), wple_ref[...]), gple_ref[...])
    lax.fori_loop(0, tm // DMA_UNROLL, drain, 0)

    rf = rf_ref[...]
    h2 = h1_ref[...] + ybuf[0] * rf[:, 0:1] + ybuf[1] * rf[:, 1:2]
    h3 = h2 + e * jax.nn.sigmoid(_mm(h2.astype(BF16), wpg_ref[...]))
    if final_norm:
        h3 = _rms(h3, gfin_ref[...])
    o_ref[...] = h3


def _combine_ple(pos3, h1, p, rf, yb, wple, gple, wpg, gfin, layer, final_norm):
    t = h1.shape[0]
    tm = TOKEN_TILE
    row = lambda w: pl.BlockSpec((tm, w), lambda i: (i, 0))
    full = lambda a, b: pl.BlockSpec((a, b), lambda i: (0, 0))
    return pl.pallas_call(
        functools.partial(_combine_ple_kernel, final_norm=final_norm),
        grid=(t // tm,),
        in_specs=[
            pl.BlockSpec((1, 1, 2 * tm), lambda i: (i, 0, 0), memory_space=pltpu.SMEM),
            row(D_MODEL),
            pl.BlockSpec((tm, PLE_DIM), lambda i: (layer * (t // tm) + i, 0)),
            row(LANES),
            pl.BlockSpec(memory_space=pl.ANY),
            full(PLE_DIM, D_MODEL), full(1, D_MODEL), full(D_MODEL, D_MODEL), full(1, D_MODEL),
        ],
        out_specs=row(D_MODEL),
        out_shape=jax.ShapeDtypeStruct((t, D_MODEL), F32),
        scratch_shapes=[pltpu.VMEM((2, tm, D_MODEL), F32), pltpu.SemaphoreType.DMA(())],
        compiler_params=pltpu.CompilerParams(
            dimension_semantics=("arbitrary",), vmem_limit_bytes=VMEM_LIMIT),
        name="combine_ple",
    )(pos3, h1, p, rf, yb, wple, gple, wpg, gfin)


def _rel_bucket(dist):
    n = jnp.maximum(dist, 0)
    max_exact = REL_BUCKETS // 2
    nf = jnp.maximum(n, 1).astype(F32)
    large = max_exact + (jnp.log(nf / max_exact) / math.log(REL_MAX_DIST / max_exact)
                         * (REL_BUCKETS - max_exact)).astype(jnp.int32)
    large = jnp.minimum(large, REL_BUCKETS - 1)
    return jnp.where(n < max_exact, n, large)


def _bias_tiles(rel_bias):
    qi = jnp.arange(WINDOW)[:, None]
    kj = jnp.arange(2 * WINDOW)[None, :]
    bucket = _rel_bucket(qi + WINDOW - kj)
    onehot = (bucket[..., None] == jnp.arange(REL_BUCKETS)).astype(F32)
    nat = jnp.einsum('qkb,bh->hqk', onehot, rel_bias, precision=HIGHEST)
    dsa = nat[:DSA_HEADS]
    far = jnp.broadcast_to(rel_bias[REL_BUCKETS - 1, :DSA_HEADS][:, None, None],
                           (DSA_HEADS, WINDOW, WINDOW))
    return jnp.stack([far, dsa[:, :, :WINDOW], dsa[:, :, WINDOW:]], axis=1), nat[DSA_HEADS:]


def _pack_w_in(w):
    sizes = (128, 128, 256, 256, 16, 256, 64, 64, 512, 64, 8, 512, 128, 128)
    offs = [0]
    for s in sizes:
        offs.append(offs[-1] + s)
    wt = w.T
    gq, gk, gv, gg, ga, dq, dk, dv, iq, ik, iw, sq, sk, sv = (
        wt[offs[n]:offs[n + 1]] for n in range(len(sizes)))
    z = lambda n: jnp.zeros((n, w.shape[0]), w.dtype)
    packed = jnp.concatenate(
        [iq, sq, gv, gg, dq, gq, gk, ga, z(128 - GLA_RANK), dk, dv, ik, iw,
         z(128 - IDX_DIM - IDX_HEADS), sk, sv], axis=0)
    return packed.astype(BF16)


def kernel(x, p, rel_bias, g_mix, w_in, gla_w_alpha, gla_b_alpha, gla_g_norm, swa_sinks, w_out,
           g_ffn, w_router_group, b_router_group, w_router_expert, b_router_expert, w_expert_gate,
           w_expert_up, w_expert_down, w_ple, g_ple, w_ple_gate, g_final):
    bsz, seq, d = x.shape
    depth = w_in.shape[0]
    t = bsz * seq
    assert d == D_MODEL and t % TOKEN_TILE == 0 and seq % GLA_BLOCK == 0 and seq % KC == 0
    n_blocks = -(-(2 * t) // EXPERT_ROWS) + N_EXPERTS
    dsa_tiles, swa_bias = _bias_tiles(rel_bias)

    h = x.reshape(t, d)
    for i in range(depth):
        proj = _inproj(h, g_mix[i][None, :], _pack_w_in(w_in[i]))
        proj3 = proj.reshape(bsz, seq, D_PROJ)

        wal = jnp.zeros((128, 128), F32).at[:GLA_RANK].set(gla_w_alpha[i])
        og = _gla(proj3, wal, gla_b_alpha[i][None, :],
                  jnp.tile(gla_g_norm[i], GLA_HEADS)[None, :])
        od = _dsa(proj3, dsa_tiles)
        os_ = _swa(proj3, swa_sinks[i], swa_bias)

        w_r = jnp.concatenate([w_router_group[i], w_router_expert[i]], axis=1)
        w_r = jnp.pad(w_r, ((0, 0), (0, LANES - w_r.shape[1])))
        wr_hi, wr_lo = _split_bf16(w_r)
        b_r = jnp.pad(jnp.concatenate([b_router_group[i], b_router_expert[i]]),
                      (0, LANES - N_GROUPS - N_EXPERTS))[None, :]
        h1, xn, ri, rf, cnt = _outproj_router(
            h, og.reshape(t, -1), od.reshape(t, -1), os_.reshape(t, -1), w_out[i].astype(BF16),
            g_ffn[i][None, :], wr_hi, wr_lo, b_r)

        counts = cnt[0, ROUTE_OFF:ROUTE_OFF + N_EXPERTS].astype(jnp.int32)
        padded = (counts + EXPERT_ROWS - 1) // EXPERT_ROWS * EXPERT_ROWS
        pad_end = jnp.cumsum(padded)
        pad_start = pad_end - padded
        expert_ids = jnp.arange(N_EXPERTS, dtype=jnp.int32)
        slot_start = jnp.sum(
            jnp.where(ri[:, 0:2, None] == expert_ids, pad_start, 0), axis=-1)
        pos3 = (slot_start + ri[:, 2:4]).reshape(t // TOKEN_TILE, 1, 2 * TOKEN_TILE)
        blk_start = jnp.arange(n_blocks, dtype=jnp.int32) * EXPERT_ROWS
        blk_expert = jnp.minimum(
            jnp.sum((pad_end[None, :] <= blk_start[:, None]).astype(jnp.int32), axis=1),
            N_EXPERTS - 1)
        n_used = (pad_end[-1:] // EXPERT_ROWS).astype(jnp.int32)

        buf = _dispatch(pos3, xn, jnp.zeros((n_blocks * EXPERT_ROWS, d), F32))
        yb = _experts(blk_expert, n_used, buf, w_expert_gate, w_expert_up, w_expert_down, i)
        h = _combine_ple(pos3, h1, p.reshape(depth * t, PLE_DIM), rf, yb, w_ple[i].astype(BF16),
                         g_ple[i][None, :], w_ple_gate[i].astype(BF16), g_final[None, :],
                         layer=i, final_norm=(i == depth - 1))
    return h.reshape(bsz, seq, d)
```

```python
import functools
import math

import jax
import jax.numpy as jnp
from jax import lax
from jax.experimental import pallas as pl
from jax.experimental.pallas import tpu as pltpu

F32 = jnp.float32
BF16 = jnp.bfloat16
HIGHEST = lax.Precision.HIGHEST

D_MODEL = 1024
HEAD_DIM = 64
GLA_HEADS = 4
GLA_DK = 32
GLA_DV = 64
GLA_RANK = 16
GLA_TAU = 16.0
GLA_CHUNK = 64
DSA_HEADS = 4
IDX_HEADS = 8
IDX_DIM = 64
DSA_TOPK_MAX = 256
SWA_HEADS = 8
SWA_KV_HEADS = 2
SWA_GROUP = SWA_HEADS // SWA_KV_HEADS
WINDOW = 128
REL_BUCKETS = 32
REL_MAX_DIST = 128
N_GROUPS = 4
EXPERTS_PER_GROUP = 8
N_EXPERTS = N_GROUPS * EXPERTS_PER_GROUP
D_EXPERT = 512
PLE_DIM = 256
EPS = 1e-6

LANES = 128
SUBLANES = 8
PACK16 = 16

COL_IQ, COL_SQ = 0, 512
COL_GV, COL_GG, COL_DQ = 1024, 1280, 1536
COL_GQ, COL_GK, COL_GA, COL_DKV, COL_IKW, COL_SK, COL_SV = 1792, 1920, 2048, 2176, 2304, 2432, 2560
D_PROJ = 2688

TOKEN_TILE = 256
EXPERT_ROWS = 256
GLA_BLOCK = 256
QB = 128
KC = 256
DMA_UNROLL = 8
NEG_BIG = -1e30
VMEM_LIMIT = 48 * 1024 * 1024


def _nt(a, b, precision=None):
    return lax.dot_general(a, b, (((1,), (1,)), ((), ())), precision=precision,
                           preferred_element_type=F32)


def _mm(a, b, precision=None):
    return jnp.dot(a, b, precision=precision, preferred_element_type=F32)


def _eye(n, dtype):
    r = lax.broadcasted_iota(jnp.int32, (n, n), 0)
    c = lax.broadcasted_iota(jnp.int32, (n, n), 1)
    return jnp.where(r == c, 1.0, 0.0).astype(dtype)


def _rms(x, g):
    return x * lax.rsqrt(jnp.mean(x * x, axis=-1, keepdims=True) + EPS) * g


def _inproj_kernel(h_ref, g_ref, w_ref, o_ref):
    a = _rms(h_ref[...], g_ref[...])
    o_ref[...] = _nt(a.astype(BF16), w_ref[...])


def _inproj(h, g, w):
    t = h.shape[0]
    return pl.pallas_call(
        _inproj_kernel,
        grid=(t // TOKEN_TILE,),
        in_specs=[
            pl.BlockSpec((TOKEN_TILE, D_MODEL), lambda i: (i, 0)),
            pl.BlockSpec((1, D_MODEL), lambda i: (0, 0)),
            pl.BlockSpec((D_PROJ, D_MODEL), lambda i: (0, 0)),
        ],
        out_specs=pl.BlockSpec((TOKEN_TILE, D_PROJ), lambda i: (i, 0)),
        out_shape=jax.ShapeDtypeStruct((t, D_PROJ), F32),
        compiler_params=pltpu.CompilerParams(
            dimension_semantics=("arbitrary",), vmem_limit_bytes=VMEM_LIMIT),
        name="inproj",
    )(h, g, w)


def _gla_kernel(q_ref, k_ref, v_ref, gg_ref, ga_ref, wal_ref, bal_ref, gn_ref, o_ref,
                state_ref, sc_all_ref, p_all_ref, b_ref):
    hk = GLA_HEADS * GLA_DK
    hv = GLA_HEADS * GLA_DV
    c = GLA_CHUNK
    half_c = c // 2
    assert 2 * c == LANES

    @pl.when(pl.program_id(1) == 0)
    def _():
        state_ref[...] = jnp.zeros_like(state_ref)

    @pl.when((pl.program_id(0) == 0) & (pl.program_id(1) == 0))
    def _():
        p_all_ref[...] = jnp.zeros_like(p_all_ref)

    pj = lax.broadcasted_iota(jnp.int32, (half_c, LANES), 0)
    pl_ = lax.broadcasted_iota(jnp.int32, (half_c, LANES), 1)
    pair_causal = jnp.where(pl_ < c, pl_, pl_ - c) <= jnp.where(pl_ < c, pj, pj + half_c)
    rv = lax.broadcasted_iota(jnp.int32, (hv, hk), 0) // GLA_DV
    ck = lax.broadcasted_iota(jnp.int32, (hv, hk), 1) // GLA_DK
    blockdiag_t = jnp.where(rv == ck, 1.0, 0.0).astype(F32)
    hr = lax.broadcasted_iota(jnp.int32, (SUBLANES, hk), 0)
    hl = lax.broadcasted_iota(jnp.int32, (SUBLANES, hk), 1) // GLA_DK
    head_rows = jnp.where(hr == hl, 1.0, 0.0).astype(BF16)
    eye_v = _eye(hv, BF16)

    ga_hi, ga_lo = _split_bf16(ga_ref[0])
    w_hi, w_lo = _split_bf16(wal_ref[...])
    z = _mm(ga_hi, w_hi) + _mm(ga_hi, w_lo) + _mm(ga_lo, w_hi) + bal_ref[...]
    log_a = (jnp.minimum(z, 0.0) - jnp.log1p(jnp.exp(-jnp.abs(z)))) * (1.0 / GLA_TAU)
    la_hi = log_a.astype(BF16)
    la_r1 = log_a - la_hi.astype(F32)
    la_mid = la_r1.astype(BF16)
    la_lo = (la_r1 - la_mid.astype(F32)).astype(BF16)
    rb = lax.broadcasted_iota(jnp.int32, (GLA_BLOCK, GLA_BLOCK), 0)
    cb = lax.broadcasted_iota(jnp.int32, (GLA_BLOCK, GLA_BLOCK), 1)
    tril = jnp.where((rb >= cb) & (rb // c == cb // c), 1.0, 0.0).astype(BF16)
    b_ref[...] = _mm(tril, la_hi) + _mm(tril, la_mid) + _mm(tril, la_lo)

    def chunk(ci):
        rows = slice(ci * c, (ci + 1) * c)
        p_ref, sc_ref = p_all_ref.at[ci], sc_all_ref.at[ci]
        q = q_ref[0, rows, :] * (GLA_DK ** -0.5)
        k = k_ref[0, rows, :]
        v = v_ref[0, rows, :]
        b = b_ref[rows, :]
        state_t = state_ref[...]
        o_inter = _nt((q * jnp.exp(b)).astype(BF16), state_t.astype(BF16))

        for t in range(c):
            ns = PACK16 * (t // PACK16 + 1)
            slot = 2 * (t % half_c) + t // half_c
            p = jnp.exp(b[t:t + 1, :] - b[:ns, :]) * k[:ns, :] * q[t:t + 1, :]
            p_ref[slot * c:slot * c + ns, :] = p.astype(BF16)
        rows_hs = _nt(head_rows, p_ref[...])
        for jj in range(half_c):
            for h in range(GLA_HEADS):
                sc_ref[h, jj:jj + 1, :] = rows_hs[h:h + 1, jj * LANES:(jj + 1) * LANES]
        vb = v.astype(BF16)
        zero_v = jnp.zeros((c, GLA_DV), BF16)
        o_heads = []
        for h in range(GLA_HEADS):
            vh = vb[:, h * GLA_DV:(h + 1) * GLA_DV]
            v_pair = jnp.concatenate([jnp.concatenate([vh, zero_v], axis=1),
                                      jnp.concatenate([zero_v, vh], axis=1)], axis=0)
            scores = jnp.where(pair_causal, sc_ref[h], 0.0).astype(BF16)
            o_pair = _mm(scores, v_pair)
            o_heads.append(jnp.concatenate([o_pair[:, :GLA_DV], o_pair[:, GLA_DV:]], axis=0))
        o = o_inter + jnp.concatenate(o_heads, axis=-1)

        b_last = b[c - 1:c, :]
        kd = (k * jnp.exp(b_last - b)).astype(BF16)
        v_t = _nt(eye_v, vb).astype(BF16)
        upd_t = _mm(v_t, kd)
        state_ref[...] = jnp.exp(b_last) * state_t + upd_t * blockdiag_t

        gg = gg_ref[0, rows, :]
        outs = []
        for h in range(GLA_HEADS):
            oh = o[:, h * GLA_DV:(h + 1) * GLA_DV]
            ms = jnp.mean(oh * oh, axis=-1, keepdims=True)
            outs.append(oh * lax.rsqrt(ms + EPS))
        on = jnp.concatenate(outs, axis=-1) * gn_ref[...]
        o_ref[0, rows, :] = (on * (gg * jax.nn.sigmoid(gg))).astype(o_ref.dtype)

    for ci in range(GLA_BLOCK // c):
        chunk(ci)


def _gla(proj3, wal, bal, gn):
    bsz, seq, _ = proj3.shape
    nb = seq // GLA_BLOCK

    def col(width, off):
        return pl.BlockSpec((1, GLA_BLOCK, width), lambda b, i: (b, i, off // width))

    return pl.pallas_call(
        _gla_kernel,
        grid=(bsz, nb),
        in_specs=[
            col(128, COL_GQ), col(128, COL_GK), col(256, COL_GV), col(256, COL_GG), col(128, COL_GA),
            pl.BlockSpec((128, 128), lambda b, i: (0, 0)),
            pl.BlockSpec((1, 128), lambda b, i: (0, 0)),
            pl.BlockSpec((1, 256), lambda b, i: (0, 0)),
        ],
        out_specs=pl.BlockSpec((1, GLA_BLOCK, 256), lambda b, i: (b, i, 0)),
        out_shape=jax.ShapeDtypeStruct((bsz, seq, GLA_HEADS * GLA_DV), BF16),
        scratch_shapes=[
            pltpu.VMEM((GLA_HEADS * GLA_DV, GLA_HEADS * GLA_DK), F32),
            pltpu.VMEM((GLA_BLOCK // GLA_CHUNK, GLA_HEADS, GLA_CHUNK // 2, 2 * GLA_CHUNK), F32),
            pltpu.VMEM((GLA_BLOCK // GLA_CHUNK, GLA_CHUNK * GLA_CHUNK, GLA_HEADS * GLA_DK), BF16),
            pltpu.VMEM((GLA_BLOCK, GLA_HEADS * GLA_DK), F32),
        ],
        compiler_params=pltpu.CompilerParams(
            dimension_semantics=("arbitrary", "arbitrary"), vmem_limit_bytes=VMEM_LIMIT),
        name="gla",
    )(proj3, proj3, proj3, proj3, proj3, wal, bal, gn)


DSA_BISECT_STEPS = 12
DSA_ACCUMULATORS = 4
DSA_WALK_UNCONDITIONAL = 2


def _dsa_kernel(dq_ref, kv_ref, iq_ref, ikw_ref, btile_ref, o_ref,
                sc_ref, lg_ref, p_ref, kb_ref, vb_ref, mp_ref, lp_ref, acc_ref, cut_ref, *, n_sel):
    j = pl.program_id(1)
    nkc = (j + 2) // 2
    seq = kv_ref.shape[1]
    ksel = float(n_sel)

    @pl.when(j == 0)
    def _():
        kv = kv_ref[0]
        kb_ref[...] = kv[:, :HEAD_DIM].astype(BF16)
        vb_ref[...] = kv[:, HEAD_DIM:].astype(BF16)

    def rows(c):
        return pl.ds(pl.multiple_of(c * KC, KC), KC)

    wide = DSA_ACCUMULATORS * SUBLANES

    def fold(op, x):
        return op(x.reshape(KC // wide, wide, QB), axis=0)

    def all8(op, xw, roll_op):
        x8 = op(xw.reshape(DSA_ACCUMULATORS, SUBLANES, QB), axis=0)
        for shift in (4, 2, 1):
            x8 = roll_op(x8, pltpu.roll(x8, shift, 0))
        return x8

    def widen(x8):
        return jnp.concatenate([x8] * DSA_ACCUMULATORS, axis=0)

    zeros8 = jnp.zeros((SUBLANES, QB), F32)
    zerosw = jnp.zeros((wide, QB), F32)
    infw = jnp.full((wide, QB), jnp.inf, F32)

    ikw_q = ikw_ref[0, pl.ds(pl.multiple_of(j * QB, QB), QB), :]
    sel_r = lax.broadcasted_iota(jnp.int32, (SUBLANES, LANES), 0)
    sel_c = lax.broadcasted_iota(jnp.int32, (SUBLANES, LANES), 1)
    pick_w = jnp.where(sel_c == sel_r + IDX_DIM, 1.0, 0.0).astype(F32)
    w_t = _nt(pick_w, ikw_q, HIGHEST) * (IDX_HEADS ** -0.5 * IDX_DIM ** -0.5)
    iq = iq_ref[0].astype(BF16)
    iq_pairs = [
        jnp.concatenate([iq[:, (2 * p) * IDX_DIM:(2 * p + 1) * IDX_DIM],
                         iq[:, (2 * p + 1) * IDX_DIM:(2 * p + 2) * IDX_DIM]], axis=0)
        for p in range(IDX_HEADS // 2)]

    s_loc = lax.broadcasted_iota(jnp.int32, (KC, QB), 0)
    t_glob = j * QB + lax.broadcasted_iota(jnp.int32, (KC, QB), 1)
    s_loc_f = s_loc.astype(F32)

    def score_chunk(c, st):
        mn8, mx8, ge8, gt8 = st
        ik = ikw_ref[0, rows(c), :][:, :IDX_DIM].astype(BF16)
        acc = jnp.zeros((KC, QB), F32)
        for p in range(IDX_HEADS // 2):
            rel = _nt(ik, iq_pairs[p])
            acc = (acc + jnp.maximum(rel[:, :QB], 0.0) * w_t[2 * p:2 * p + 1, :]
                   + jnp.maximum(rel[:, QB:], 0.0) * w_t[2 * p + 1:2 * p + 2, :])
        adm = s_loc + c * KC <= t_glob
        blk = jnp.where(adm, acc, -jnp.inf)
        sc_ref[rows(c), :] = blk
        mn8 = jnp.minimum(mn8, fold(jnp.min, jnp.where(adm, acc, jnp.inf)))
        mx8 = jnp.maximum(mx8, fold(jnp.max, blk))
        ge8 = ge8 + fold(jnp.sum, jnp.where(blk >= 0.0, 1.0, 0.0))
        gt8 = gt8 + fold(jnp.sum, jnp.where(blk > 0.0, 1.0, 0.0))
        return mn8, mx8, ge8, gt8

    mn8, mx8, ge8, gt8 = lax.fori_loop(0, nkc, score_chunk, (infw, -infw, zerosw, zerosw))
    row_min = all8(jnp.min, mn8, jnp.minimum)
    row_max = all8(jnp.max, mx8, jnp.maximum)
    c_ge0, c_gt0 = all8(jnp.sum, ge8, jnp.add), all8(jnp.sum, gt8, jnp.add)

    def blocks(c):
        return sc_ref[rows(c), :].reshape(KC // wide, wide, QB)

    def count_ge(x8):
        xw = widen(x8)[None]

        def body(c, cw):
            return cw + jnp.sum(jnp.where(blocks(c) >= xw, 1.0, 0.0), axis=0)
        return all8(jnp.sum, lax.fori_loop(0, nkc, body, zerosw), jnp.add)

    def min_ge(x8):
        xw = widen(x8)[None]

        def body(c, mw):
            blk = blocks(c)
            return jnp.minimum(mw, jnp.min(jnp.where(blk >= xw, blk, jnp.inf), axis=0))
        return all8(jnp.min, lax.fori_loop(0, nkc, body, infw), jnp.minimum)

    def pass_gt(x8):
        xw = widen(x8)[None]

        def body(c, st):
            cw, mw = st
            blk = blocks(c)
            hit = blk > xw
            return (cw + jnp.sum(jnp.where(hit, 1.0, 0.0), axis=0),
                    jnp.minimum(mw, jnp.min(jnp.where(hit, blk, jnp.inf), axis=0)))
        cw, mw = lax.fori_loop(0, nkc, body, (zerosw, infw))
        return all8(jnp.sum, cw, jnp.add), all8(jnp.min, mw, jnp.minimum)

    n_adm = (j * QB + lax.broadcasted_iota(jnp.int32, (SUBLANES, QB), 1) + 1).astype(F32)
    at_zero = (c_gt0 < ksel) & (c_ge0 >= ksel)
    above = c_gt0 >= ksel
    lo = jnp.where(above | at_zero, 0.0, row_min)
    c_lo = jnp.where(above | at_zero, c_ge0, n_adm)
    settled = at_zero | (n_adm <= ksel)
    hi = jnp.where(settled, lo, jnp.where(above, row_max, 0.0))

    def bisect(_, st):
        lo, hi, c_lo = st
        mid = lo + (hi - lo) * 0.5
        cnt = count_ge(mid)
        up = cnt >= ksel
        return jnp.where(up, mid, lo), jnp.where(up, hi, mid), jnp.where(up, cnt, c_lo)

    lo, _, c_lo = lax.fori_loop(0, DSA_BISECT_STEPS, bisect, (lo, hi, c_lo))

    v0 = min_ge(lo)
    done0 = jnp.where(settled | (c_lo == ksel), 1.0, 0.0)
    c_gt_init = jnp.where(at_zero, c_gt0, 0.0)

    def walk_cond(st):
        return jnp.min(st[3]) < 0.5

    def walk_body(st):
        v, c_ge, c_gt, dn = st
        cnt, vnext = pass_gt(v)
        live = dn < 0.5
        fin = live & (cnt < ksel)
        step = live & (cnt >= ksel)
        c_gt = jnp.where(fin, cnt, c_gt)
        dn = jnp.where(fin, 1.0, dn)
        v = jnp.where(step, vnext, v)
        c_ge = jnp.where(step, cnt, c_ge)
        return v, c_ge, c_gt, dn

    walk = (v0, c_lo, c_gt_init, done0)
    for _ in range(DSA_WALK_UNCONDITIONAL):
        walk = walk_body(walk)
    tau8, c_ge, c_gt, _ = lax.while_loop(walk_cond, walk_body, walk)

    need = ksel - c_gt
    cut_ref[...] = jnp.full(cut_ref.shape, float(seq), F32)

    @pl.when(jnp.max(c_ge) > ksel)
    def _():
        s_grp = lax.broadcasted_iota(jnp.int32, (KC // wide, wide, QB), 0) * wide
        s_sub = lax.broadcasted_iota(jnp.int32, (KC // wide, wide, QB), 1)
        s_idx = (s_grp + s_sub).astype(F32)
        tauw = widen(tau8)[None]

        def count_ties_below(m8):
            mw = widen(m8)[None]

            def body(c, cw):
                hit = (blocks(c) == tauw) & (s_idx + (c * KC).astype(F32) < mw)
                return cw + jnp.sum(jnp.where(hit, 1.0, 0.0), axis=0)
            return all8(jnp.sum, lax.fori_loop(0, nkc, body, zerosw), jnp.add)

        def idx_bisect(_, lh):
            lo_m, hi_m = lh
            mid = jnp.floor((lo_m + hi_m) * 0.5)
            ok = count_ties_below(mid) >= need
            return jnp.where(ok, lo_m, mid), jnp.where(ok, mid, hi_m)

        _, hi_m = lax.fori_loop(0, int(math.log2(seq)) + 1, idx_bisect,
                                (zeros8, zeros8 + float(seq)))
        cut_ref[...] = jnp.where(c_ge > ksel, hi_m, float(seq))

    tau = tau8[0:1, :]
    cut = cut_ref[0:1, :]

    q = (dq_ref[0] * (HEAD_DIM ** -0.5)).astype(BF16)
    q4 = jnp.concatenate([q[:, h * HEAD_DIM:(h + 1) * HEAD_DIM] for h in range(DSA_HEADS)],
                         axis=0)
    mp_ref[...] = jnp.full(mp_ref.shape, NEG_BIG, F32)
    lp_ref[...] = jnp.zeros(lp_ref.shape, F32)

    half = seq // 2
    second_half = (j + 1) * QB > half

    def cols(c):
        return pl.ds(pl.multiple_of(c * KC, KC), KC)

    def logits_chunk(c, carry):
        blk = sc_ref[rows(c), :]
        sidx = s_loc_f + (c * KC).astype(F32)
        sel = (blk > tau) | ((blk == tau) & (sidx < cut))
        neg_t = jnp.where(sel, 0.0, NEG_BIG).T
        lg4 = _nt(q4, kb_ref[rows(c), :])
        w0 = jnp.clip(2 * c - j + 2, 0, 2)
        w1 = jnp.clip(2 * c + 1 - j + 2, 0, 2)
        for h in range(DSA_HEADS):
            hq = slice(h * QB, (h + 1) * QB)
            bias = jnp.concatenate([btile_ref[h, w0], btile_ref[h, w1]], axis=1)
            lg = lg4[hq, :] + bias + neg_t
            lg_ref[hq, cols(c)] = lg
            mp_ref[h] = jnp.maximum(mp_ref[h], jnp.maximum(lg[:, :QB], lg[:, QB:]))
        return carry

    lax.fori_loop(0, nkc, logits_chunk, 0)
    m_rows = [jnp.max(mp_ref[h], axis=-1, keepdims=True) for h in range(DSA_HEADS)]

    def probs_chunk(c, carry):
        for h in range(DSA_HEADS):
            hq = slice(h * QB, (h + 1) * QB)
            p = jnp.exp(lg_ref[hq, cols(c)] - m_rows[h])
            lp_ref[h] = lp_ref[h] + p[:, :QB] + p[:, QB:]
            p_ref[hq, cols(c)] = p.astype(BF16)
        return carry

    def zero_chunk(c, carry):
        p_ref[:, cols(c)] = jnp.zeros((DSA_HEADS * QB, KC), BF16)
        return carry

    lax.fori_loop(0, nkc, probs_chunk, 0)
    lax.fori_loop(nkc, jnp.where(second_half, seq // KC, half // KC), zero_chunk, 0)
    acc_ref[...] = _mm(p_ref[:, :half], vb_ref[:half, :])

    @pl.when(second_half)
    def _():
        acc_ref[...] = acc_ref[...] + _mm(p_ref[:, half:], vb_ref[half:, :])

    outs = []
    for h in range(DSA_HEADS):
        l = jnp.sum(lp_ref[h], axis=-1, keepdims=True)
        outs.append(acc_ref[h * QB:(h + 1) * QB, :] / l)
    o_ref[0] = jnp.concatenate(outs, axis=-1).astype(o_ref.dtype)


def _dsa(proj3, btiles):
    bsz, seq, _ = proj3.shape
    n_sel = min(DSA_TOPK_MAX, seq // 4)
    return pl.pallas_call(
        functools.partial(_dsa_kernel, n_sel=n_sel),
        grid=(bsz, seq // QB),
        in_specs=[
            pl.BlockSpec((1, QB, 256), lambda b, j: (b, j, COL_DQ // 256)),
            pl.BlockSpec((1, seq, 128), lambda b, j: (b, 0, COL_DKV // 128)),
            pl.BlockSpec((1, QB, 512), lambda b, j: (b, j, COL_IQ // 512)),
            pl.BlockSpec((1, seq, 128), lambda b, j: (b, 0, COL_IKW // 128)),
            pl.BlockSpec((DSA_HEADS, 3, QB, QB), lambda b, j: (0, 0, 0, 0)),
        ],
        out_specs=pl.BlockSpec((1, QB, 256), lambda b, j: (b, j, 0)),
        out_shape=jax.ShapeDtypeStruct((bsz, seq, DSA_HEADS * HEAD_DIM), BF16),
        scratch_shapes=[
            pltpu.VMEM((seq, QB), F32),
            pltpu.VMEM((DSA_HEADS * QB, seq), F32),
            pltpu.VMEM((DSA_HEADS * QB, seq), BF16),
            pltpu.VMEM((seq, HEAD_DIM), BF16),
            pltpu.VMEM((seq, HEAD_DIM), BF16),
            pltpu.VMEM((DSA_HEADS, QB, QB), F32),
            pltpu.VMEM((DSA_HEADS, QB, QB), F32),
            pltpu.VMEM((DSA_HEADS * QB, HEAD_DIM), F32),
            pltpu.VMEM((SUBLANES, QB), F32),
        ],
        compiler_params=pltpu.CompilerParams(
            dimension_semantics=("arbitrary", "arbitrary"), vmem_limit_bytes=VMEM_LIMIT),
        name="dsa",
    )(proj3, proj3, proj3, proj3, btiles)


def _swa_kernel(sink_ref, q_ref, kc_ref, kp_ref, vc_ref, vp_ref, bias_ref, o_ref):
    n = pl.program_id(1)
    q = (q_ref[0] * (HEAD_DIM ** -0.5)).astype(BF16)
    k2 = jnp.concatenate([kp_ref[0], kc_ref[0]], axis=0).astype(BF16)
    v2 = jnp.concatenate([vp_ref[0], vc_ref[0]], axis=0).astype(BF16)
    qi = lax.broadcasted_iota(jnp.int32, (WINDOW, 2 * WINDOW), 0)
    kj = lax.broadcasted_iota(jnp.int32, (WINDOW, 2 * WINDOW), 1)
    dist = qi + WINDOW - kj
    mask = (dist >= 0) & (dist < WINDOW) & ((kj >= WINDOW) | (n > 0))
    outs = []
    for h in range(SWA_HEADS):
        kvh = h // SWA_GROUP
        kh = k2[:, kvh * HEAD_DIM:(kvh + 1) * HEAD_DIM]
        vh = v2[:, kvh * HEAD_DIM:(kvh + 1) * HEAD_DIM]
        lg = _nt(q[:, h * HEAD_DIM:(h + 1) * HEAD_DIM], kh) + bias_ref[h]
        lg = jnp.where(mask, lg, -jnp.inf)
        sink = sink_ref[h]
        m = jnp.maximum(jnp.max(lg, axis=-1, keepdims=True), sink)
        e = jnp.exp(lg - m)
        den = jnp.sum(e, axis=-1, keepdims=True) + jnp.exp(sink - m)
        outs.append(_mm((e / den).astype(BF16), vh))
    o_ref[0] = jnp.concatenate(outs, axis=-1).astype(o_ref.dtype)


def _swa(proj3, sinks, bias_nat):
    bsz, seq, _ = proj3.shape
    return pl.pallas_call(
        _swa_kernel,
        grid=(bsz, seq // WINDOW),
        in_specs=[
            pl.BlockSpec(memory_space=pltpu.SMEM),
            pl.BlockSpec((1, WINDOW, 512), lambda b, n: (b, n, COL_SQ // 512)),
            pl.BlockSpec((1, WINDOW, 128), lambda b, n: (b, n, COL_SK // 128)),
            pl.BlockSpec((1, WINDOW, 128), lambda b, n: (b, jnp.maximum(n - 1, 0), COL_SK // 128)),
            pl.BlockSpec((1, WINDOW, 128), lambda b, n: (b, n, COL_SV // 128)),
            pl.BlockSpec((1, WINDOW, 128), lambda b, n: (b, jnp.maximum(n - 1, 0), COL_SV // 128)),
            pl.BlockSpec((SWA_HEADS, WINDOW, 2 * WINDOW), lambda b, n: (0, 0, 0)),
        ],
        out_specs=pl.BlockSpec((1, WINDOW, 512), lambda b, n: (b, n, 0)),
        out_shape=jax.ShapeDtypeStruct((bsz, seq, SWA_HEADS * HEAD_DIM), BF16),
        compiler_params=pltpu.CompilerParams(
            dimension_semantics=("arbitrary", "arbitrary"), vmem_limit_bytes=VMEM_LIMIT),
        name="swa",
    )(sinks, proj3, proj3, proj3, proj3, proj3, bias_nat)


ROUTE_OFF = N_GROUPS


def _split_bf16(x):
    hi = x.astype(BF16)
    lo = (x - hi.astype(F32)).astype(BF16)
    return hi, lo


def _outproj_router_kernel(h_ref, og_ref, od_ref, os_ref, wo_ref, gffn_ref, wr_hi_ref, wr_lo_ref,
                           br_ref, h1_ref, xn_ref, ri_ref, rf_ref, cnt_ref, run_ref):
    tm = h_ref.shape[0]

    @pl.when(pl.program_id(0) == 0)
    def _():
        run_ref[...] = jnp.zeros_like(run_ref)

    o = jnp.concatenate([og_ref[...], od_ref[...], os_ref[...]], axis=-1)
    h1 = h_ref[...] + _mm(o, wo_ref[...])
    h1_ref[...] = h1
    xn = _rms(h1, gffn_ref[...])
    xn_ref[...] = xn

    x_hi, x_lo = _split_bf16(xn)
    lg = (_mm(x_hi, wr_hi_ref[...]) + _mm(x_lo, wr_hi_ref[...]) + _mm(x_hi, wr_lo_ref[...])
          + br_ref[...])

    lane = lax.broadcasted_iota(jnp.int32, lg.shape, 1)
    lane_f = lane.astype(F32)
    ninf = -jnp.inf

    def first_max(x):
        m = jnp.max(x, axis=-1, keepdims=True)
        idx = jnp.min(jnp.where(x == m, lane_f, float(LANES)), axis=-1, keepdims=True)
        return m, idx

    gl = jnp.where(lane < N_GROUPS, lg, ninf)
    gmax, gsel = first_max(gl)
    g_w = 1.0 / jnp.sum(jnp.exp(gl - gmax), axis=-1, keepdims=True)
    e_lo = ROUTE_OFF + EXPERTS_PER_GROUP * gsel
    el = jnp.where((lane_f >= e_lo) & (lane_f < e_lo + EXPERTS_PER_GROUP), lg, ninf)
    m1, i1 = first_max(el)
    eden = jnp.sum(jnp.exp(el - m1), axis=-1, keepdims=True)
    m2, i2 = first_max(jnp.where(lane_f == i1, ninf, el))
    p1 = 1.0 / eden
    p2 = jnp.exp(m2 - m1) / eden
    gate1 = g_w * p1 / (p1 + p2)
    gate2 = g_w * p2 / (p1 + p2)

    onehot = jnp.where((lane_f == i1) | (lane_f == i2), 1.0, 0.0)
    rr = lax.broadcasted_iota(jnp.int32, (tm, tm), 0)
    cc = lax.broadcasted_iota(jnp.int32, (tm, tm), 1)
    strict = jnp.where(rr > cc, 1.0, 0.0).astype(BF16)
    before = _mm(strict, onehot.astype(BF16)) + run_ref[0:1, :]
    rank1 = jnp.sum(jnp.where(lane_f == i1, before, 0.0), axis=-1, keepdims=True)
    rank2 = jnp.sum(jnp.where(lane_f == i2, before, 0.0), axis=-1, keepdims=True)
    run_ref[0:1, :] = run_ref[0:1, :] + jnp.sum(onehot, axis=0, keepdims=True)
    cnt_ref[...] = run_ref[...]

    ints = jnp.where(lane == 0, i1 - ROUTE_OFF,
                     jnp.where(lane == 1, i2 - ROUTE_OFF,
                               jnp.where(lane == 2, rank1, jnp.where(lane == 3, rank2, 0.0))))
    ri_ref[...] = ints.astype(jnp.int32)
    rf_ref[...] = jnp.where(lane == 0, gate1, jnp.where(lane == 1, gate2, 0.0))


def _outproj_router(h, og, od, os_, wo, gffn, wr_hi, wr_lo, br):
    t = h.shape[0]
    tm = TOKEN_TILE
    row = lambda w: pl.BlockSpec((tm, w), lambda i: (i, 0))
    full = lambda a, b: pl.BlockSpec((a, b), lambda i: (0, 0))
    return pl.pallas_call(
        _outproj_router_kernel,
        grid=(t // tm,),
        in_specs=[row(D_MODEL), row(256), row(256), row(512), full(D_MODEL, D_MODEL),
                  full(1, D_MODEL), full(D_MODEL, LANES), full(D_MODEL, LANES), full(1, LANES)],
        out_specs=[row(D_MODEL), row(D_MODEL), row(LANES), row(LANES), full(SUBLANES, LANES)],
        out_shape=[
            jax.ShapeDtypeStruct((t, D_MODEL), F32),
            jax.ShapeDtypeStruct((t, D_MODEL), F32),
            jax.ShapeDtypeStruct((t, LANES), jnp.int32),
            jax.ShapeDtypeStruct((t, LANES), F32),
            jax.ShapeDtypeStruct((SUBLANES, LANES), F32),
        ],
        scratch_shapes=[pltpu.VMEM((SUBLANES, LANES), F32)],
        compiler_params=pltpu.CompilerParams(
            dimension_semantics=("arbitrary",), vmem_limit_bytes=VMEM_LIMIT),
        name="outproj_router",
    )(h, og, od, os_, wo, gffn, wr_hi, wr_lo, br)


def _dispatch_kernel(pos_ref, xn_ref, buf_in_ref, buf_ref, sem):
    del buf_in_ref
    tm = xn_ref.shape[0]

    def row_copy(r, k):
        dst = pos_ref[0, 0, 2 * r + k]
        return pltpu.make_async_copy(xn_ref.at[pl.ds(r, 1)], buf_ref.at[pl.ds(dst, 1)], sem)

    def issue(g, carry):
        for u in range(DMA_UNROLL):
            row_copy(g * DMA_UNROLL + u, 0).start()
            row_copy(g * DMA_UNROLL + u, 1).start()
        return carry

    def drain(g, carry):
        for u in range(DMA_UNROLL):
            row_copy(g * DMA_UNROLL + u, 0).wait()
            row_copy(g * DMA_UNROLL + u, 1).wait()
        return carry

    lax.fori_loop(0, tm // DMA_UNROLL, issue, 0)
    lax.fori_loop(0, tm // DMA_UNROLL, drain, 0)


def _dispatch(pos3, xn, buf0):
    t = xn.shape[0]
    tm = TOKEN_TILE
    return pl.pallas_call(
        _dispatch_kernel,
        grid=(t // tm,),
        in_specs=[
            pl.BlockSpec((1, 1, 2 * tm), lambda i: (i, 0, 0), memory_space=pltpu.SMEM),
            pl.BlockSpec((tm, D_MODEL), lambda i: (i, 0)),
            pl.BlockSpec(memory_space=pl.ANY),
        ],
        out_specs=pl.BlockSpec(memory_space=pl.ANY),
        out_shape=jax.ShapeDtypeStruct(buf0.shape, buf0.dtype),
        scratch_shapes=[pltpu.SemaphoreType.DMA(())],
        input_output_aliases={2: 0},
        compiler_params=pltpu.CompilerParams(
            dimension_semantics=("arbitrary",), vmem_limit_bytes=VMEM_LIMIT),
        name="dispatch",
    )(pos3, xn, buf0)


def _expert_kernel(be_ref, nu_ref, x_ref, wg_ref, wu_ref, wd_ref, y_ref, wg_s, wu_s, wd_s):
    i = pl.program_id(0)
    used = i < nu_ref[0]
    new_expert = (i == 0) | (be_ref[i] != be_ref[jnp.maximum(i - 1, 0)])

    @pl.when(used & new_expert)
    def _():
        wg_s[...] = wg_ref[0, 0].astype(BF16)
        wu_s[...] = wu_ref[0, 0].astype(BF16)
        wd_s[...] = wd_ref[0, 0].astype(BF16)

    @pl.when(used)
    def _():
        x = x_ref[...].astype(BF16)
        g = _mm(x, wg_s[...])
        u = _mm(x, wu_s[...])
        hmid = (g * jax.nn.sigmoid(g)) * u
        y_ref[...] = _mm(hmid.astype(BF16), wd_s[...])

    @pl.when(i >= nu_ref[0])
    def _():
        y_ref[...] = jnp.zeros_like(y_ref)


def _experts(blk_expert, n_used, buf, wg, wu, wd, layer):
    nrows = buf.shape[0]
    nblk = nrows // EXPERT_ROWS
    w_index = lambda i, be, nu: (layer, be[i], 0, 0)
    return pl.pallas_call(
        _expert_kernel,
        grid_spec=pltpu.PrefetchScalarGridSpec(
            num_scalar_prefetch=2,
            grid=(nblk,),
            in_specs=[
                pl.BlockSpec((EXPERT_ROWS, D_MODEL), lambda i, be, nu: (i, 0)),
                pl.BlockSpec((1, 1, D_MODEL, D_EXPERT), w_index),
                pl.BlockSpec((1, 1, D_MODEL, D_EXPERT), w_index),
                pl.BlockSpec((1, 1, D_EXPERT, D_MODEL), w_index),
            ],
            out_specs=pl.BlockSpec((EXPERT_ROWS, D_MODEL), lambda i, be, nu: (i, 0)),
            scratch_shapes=[
                pltpu.VMEM((D_MODEL, D_EXPERT), BF16),
                pltpu.VMEM((D_MODEL, D_EXPERT), BF16),
                pltpu.VMEM((D_EXPERT, D_MODEL), BF16),
            ],
        ),
        out_shape=jax.ShapeDtypeStruct((nrows, D_MODEL), F32),
        compiler_params=pltpu.CompilerParams(
            dimension_semantics=("arbitrary",), vmem_limit_bytes=VMEM_LIMIT),
        name="experts",
    )(blk_expert, n_used, buf, wg, wu, wd)


def _combine_ple_kernel(pos_ref, h1_ref, p_ref, rf_ref, yb_ref, wple_ref, gple_ref, wpg_ref,
                        gfin_ref, o_ref, ybuf, sem, *, final_norm):
    tm = h1_ref.shape[0]

    def row_copy(r, k):
        src = pos_ref[0, 0, 2 * r + k]
        return pltpu.make_async_copy(yb_ref.at[pl.ds(src, 1)], ybuf.at[k, pl.ds(r, 1)], sem)

    def issue(g, carry):
        for u in range(DMA_UNROLL):
            row_copy(g * DMA_UNROLL + u, 0).start()
            row_copy(g * DMA_UNROLL + u, 1).start()
        return carry

    def drain(g, carry):
        for u in range(DMA_UNROLL):
            row_copy(g * DMA_UNROLL + u, 0).wait()
            row_copy(g * DMA_UNROLL + u, 1).wait()
        return carry

    lax.fori_loop(0, tm // DMA_UNROLL, issue, 0)
    e = _rms(_mm(p_ref[...].astype(BF16), wple_ref[...]), gple_ref[...])
    lax.fori_loop(0, tm // DMA_UNROLL, drain, 0)

    rf = rf_ref[...]
    h2 = h1_ref[...] + ybuf[0] * rf[:, 0:1] + ybuf[1] * rf[:, 1:2]
    h3 = h2 + e * jax.nn.sigmoid(_mm(h2.astype(BF16), wpg_ref[...]))
    if final_norm:
        h3 = _rms(h3, gfin_ref[...])
    o_ref[...] = h3


def _combine_ple(pos3, h1, p, rf, yb, wple, gple, wpg, gfin, layer, final_norm):
    t = h1.shape[0]
    tm = TOKEN_TILE
    row = lambda w: pl.BlockSpec((tm, w), lambda i: (i, 0))
    full = lambda a, b: pl.BlockSpec((a, b), lambda i: (0, 0))
    return pl.pallas_call(
        functools.partial(_combine_ple_kernel, final_norm=final_norm),
        grid=(t // tm,),
        in_specs=[
            pl.BlockSpec((1, 1, 2 * tm), lambda i: (i, 0, 0), memory_space=pltpu.SMEM),
            row(D_MODEL),
            pl.BlockSpec((tm, PLE_DIM), lambda i: (layer * (t // tm) + i, 0)),
            row(LANES),
            pl.BlockSpec(memory_space=pl.ANY),
            full(PLE_DIM, D_MODEL), full(1, D_MODEL), full(D_MODEL, D_MODEL), full(1, D_MODEL),
        ],
        out_specs=row(D_MODEL),
        out_shape=jax.ShapeDtypeStruct((t, D_MODEL), F32),
        scratch_shapes=[pltpu.VMEM((2, tm, D_MODEL), F32), pltpu.SemaphoreType.DMA(())],
        compiler_params=pltpu.CompilerParams(
            dimension_semantics=("arbitrary",), vmem_limit_bytes=VMEM_LIMIT),
        name="combine_ple",
    )(pos3, h1, p, rf, yb, wple, gple, wpg, gfin)


def _rel_bucket(dist):
    n = jnp.maximum(dist, 0)
    max_exact = REL_BUCKETS // 2
    nf = jnp.maximum(n, 1).astype(F32)
    large = max_exact + (jnp.log(nf / max_exact) / math.log(REL_MAX_DIST / max_exact)
                         * (REL_BUCKETS - max_exact)).astype(jnp.int32)
    large = jnp.minimum(large, REL_BUCKETS - 1)
    return jnp.where(n < max_exact, n, large)


def _bias_tiles(rel_bias):
    qi = jnp.arange(WINDOW)[:, None]
    kj = jnp.arange(2 * WINDOW)[None, :]
    bucket = _rel_bucket(qi + WINDOW - kj)
    onehot = (bucket[..., None] == jnp.arange(REL_BUCKETS)).astype(F32)
    nat = jnp.einsum('qkb,bh->hqk', onehot, rel_bias, precision=HIGHEST)
    dsa = nat[:DSA_HEADS]
    far = jnp.broadcast_to(rel_bias[REL_BUCKETS - 1, :DSA_HEADS][:, None, None],
                           (DSA_HEADS, WINDOW, WINDOW))
    return jnp.stack([far, dsa[:, :, :WINDOW], dsa[:, :, WINDOW:]], axis=1), nat[DSA_HEADS:]


def _pack_w_in(w):
    sizes = (128, 128, 256, 256, 16, 256, 64, 64, 512, 64, 8, 512, 128, 128)
    offs = [0]
    for s in sizes:
        offs.append(offs[-1] + s)
    wt = w.T
    gq, gk, gv, gg, ga, dq, dk, dv, iq, ik, iw, sq, sk, sv = (
        wt[offs[n]:offs[n + 1]] for n in range(len(sizes)))
    z = lambda n: jnp.zeros((n, w.shape[0]), w.dtype)
    packed = jnp.concatenate(
        [iq, sq, gv, gg, dq, gq, gk, ga, z(128 - GLA_RANK), dk, dv, ik, iw,
         z(128 - IDX_DIM - IDX_HEADS), sk, sv], axis=0)
    return packed.astype(BF16)


def kernel(x, p, rel_bias, g_mix, w_in, gla_w_alpha, gla_b_alpha, gla_g_norm, swa_sinks, w_out,
           g_ffn, w_router_group, b_router_group, w_router_expert, b_router_expert, w_expert_gate,
           w_expert_up, w_expert_down, w_ple, g_ple, w_ple_gate, g_final):
    bsz, seq, d = x.shape
    depth = w_in.shape[0]
    t = bsz * seq
    assert d == D_MODEL and t % TOKEN_TILE == 0 and seq % GLA_BLOCK == 0 and seq % KC == 0
    n_blocks = -(-(2 * t) // EXPERT_ROWS) + N_EXPERTS
    dsa_tiles, swa_bias = _bias_tiles(rel_bias)

    h = x.reshape(t, d)
    for i in range(depth):
        proj = _inproj(h, g_mix[i][None, :], _pack_w_in(w_in[i]))
        proj3 = proj.reshape(bsz, seq, D_PROJ)

        wal = jnp.zeros((128, 128), F32).at[:GLA_RANK].set(gla_w_alpha[i])
        og = _gla(proj3, wal, gla_b_alpha[i][None, :],
                  jnp.tile(gla_g_norm[i], GLA_HEADS)[None, :])
        od = _dsa(proj3, dsa_tiles)
        os_ = _swa(proj3, swa_sinks[i], swa_bias)

        w_r = jnp.concatenate([w_router_group[i], w_router_expert[i]], axis=1)
        w_r = jnp.pad(w_r, ((0, 0), (0, LANES - w_r.shape[1])))
        wr_hi, wr_lo = _split_bf16(w_r)
        b_r = jnp.pad(jnp.concatenate([b_router_group[i], b_router_expert[i]]),
                      (0, LANES - N_GROUPS - N_EXPERTS))[None, :]
        h1, xn, ri, rf, cnt = _outproj_router(
            h, og.reshape(t, -1), od.reshape(t, -1), os_.reshape(t, -1), w_out[i].astype(BF16),
            g_ffn[i][None, :], wr_hi, wr_lo, b_r)

        counts = cnt[0, ROUTE_OFF:ROUTE_OFF + N_EXPERTS].astype(jnp.int32)
        padded = (counts + EXPERT_ROWS - 1) // EXPERT_ROWS * EXPERT_ROWS
        pad_end = jnp.cumsum(padded)
        pad_start = pad_end - padded
        expert_ids = jnp.arange(N_EXPERTS, dtype=jnp.int32)
        slot_start = jnp.sum(
            jnp.where(ri[:, 0:2, None] == expert_ids, pad_start, 0), axis=-1)
        pos3 = (slot_start + ri[:, 2:4]).reshape(t // TOKEN_TILE, 1, 2 * TOKEN_TILE)
        blk_start = jnp.arange(n_blocks, dtype=jnp.int32) * EXPERT_ROWS
        blk_expert = jnp.minimum(
            jnp.sum((pad_end[None, :] <= blk_start[:, None]).astype(jnp.int32), axis=1),
            N_EXPERTS - 1)
        n_used = (pad_end[-1:] // EXPERT_ROWS).astype(jnp.int32)

        buf = _dispatch(pos3, xn, jnp.zeros((n_blocks * EXPERT_ROWS, d), F32))
        yb = _experts(blk_expert, n_used, buf, w_expert_gate, w_expert_up, w_expert_down, i)
        h = _combine_ple(pos3, h1, p.reshape(depth * t, PLE_DIM), rf, yb, w_ple[i].astype(BF16),
                         g_ple[i][None, :], w_ple_gate[i].astype(BF16), g_final[None, :],
                         layer=i, final_norm=(i == depth - 1))
    return h.reshape(bsz, seq, d)
```

```python
import functools
import math

import jax
import jax.numpy as jnp
from jax import lax
from jax.experimental import pallas as pl
from jax.experimental.pallas import tpu as pltpu

F32 = jnp.float32
BF16 = jnp.bfloat16
HIGHEST = lax.Precision.HIGHEST

D_MODEL = 1024
HEAD_DIM = 64
GLA_HEADS = 4
GLA_DK = 32
GLA_DV = 64
GLA_RANK = 16
GLA_TAU = 16.0
GLA_CHUNK = 64
DSA_HEADS = 4
IDX_HEADS = 8
IDX_DIM = 64
DSA_TOPK_MAX = 256
SWA_HEADS = 8
SWA_KV_HEADS = 2
SWA_GROUP = SWA_HEADS // SWA_KV_HEADS
WINDOW = 128
REL_BUCKETS = 32
REL_MAX_DIST = 128
N_GROUPS = 4
EXPERTS_PER_GROUP = 8
N_EXPERTS = N_GROUPS * EXPERTS_PER_GROUP
D_EXPERT = 512
PLE_DIM = 256
EPS = 1e-6

LANES = 128
SUBLANES = 8
PACK16 = 16

COL_IQ, COL_SQ = 0, 512
COL_GV, COL_GG, COL_DQ = 1024, 1280, 1536
COL_GQ, COL_GK, COL_GA, COL_DKV, COL_IKW, COL_SK, COL_SV = 1792, 1920, 2048, 2176, 2304, 2432, 2560
D_PROJ = 2688

TOKEN_TILE = 256
EXPERT_ROWS = 256
GLA_BLOCK = 256
QB = 128
KC = 256
DMA_UNROLL = 8
NEG_BIG = -1e30
VMEM_LIMIT = 48 * 1024 * 1024


def _nt(a, b, precision=None):
    return lax.dot_general(a, b, (((1,), (1,)), ((), ())), precision=precision,
                           preferred_element_type=F32)


def _mm(a, b, precision=None):
    return jnp.dot(a, b, precision=precision, preferred_element_type=F32)


def _eye(n, dtype):
    r = lax.broadcasted_iota(jnp.int32, (n, n), 0)
    c = lax.broadcasted_iota(jnp.int32, (n, n), 1)
    return jnp.where(r == c, 1.0, 0.0).astype(dtype)


def _rms(x, g):
    return x * lax.rsqrt(jnp.mean(x * x, axis=-1, keepdims=True) + EPS) * g


HALF_MODEL = D_MODEL // 2
HIGH16 = 0xFFFF0000


def _pack_bf16_pairs(x):
    bits = pltpu.bitcast(x.astype(BF16).astype(F32), jnp.uint32)
    return (bits[:, HALF_MODEL:] & jnp.uint32(HIGH16)) | (bits[:, :HALF_MODEL] >> 16)


def _unpack_bf16_pairs(w):
    return pltpu.bitcast(w << 16, F32), pltpu.bitcast(w & jnp.uint32(HIGH16), F32)


def _inproj_kernel(h_ref, g_ref, w_ref, o_ref):
    a = _rms(h_ref[...], g_ref[...])
    o_ref[...] = _nt(a.astype(BF16), w_ref[...])


def _inproj(h, g, w):
    t = h.shape[0]
    return pl.pallas_call(
        _inproj_kernel,
        grid=(t // TOKEN_TILE,),
        in_specs=[
            pl.BlockSpec((TOKEN_TILE, D_MODEL), lambda i: (i, 0)),
            pl.BlockSpec((1, D_MODEL), lambda i: (0, 0)),
            pl.BlockSpec((D_PROJ, D_MODEL), lambda i: (0, 0)),
        ],
        out_specs=pl.BlockSpec((TOKEN_TILE, D_PROJ), lambda i: (i, 0)),
        out_shape=jax.ShapeDtypeStruct((t, D_PROJ), F32),
        compiler_params=pltpu.CompilerParams(
            dimension_semantics=("arbitrary",), vmem_limit_bytes=VMEM_LIMIT),
        name="inproj",
    )(h, g, w)


def _gla_kernel(q_ref, k_ref, v_ref, gg_ref, ga_ref, wal_ref, bal_ref, gn_ref, o_ref,
                state_ref, sc_all_ref, p_all_ref, b_ref):
    hk = GLA_HEADS * GLA_DK
    hv = GLA_HEADS * GLA_DV
    c = GLA_CHUNK
    half_c = c // 2
    assert 2 * c == LANES

    @pl.when(pl.program_id(1) == 0)
    def _():
        state_ref[...] = jnp.zeros_like(state_ref)

    @pl.when((pl.program_id(0) == 0) & (pl.program_id(1) == 0))
    def _():
        p_all_ref[...] = jnp.zeros_like(p_all_ref)

    pj = lax.broadcasted_iota(jnp.int32, (half_c, LANES), 0)
    pl_ = lax.broadcasted_iota(jnp.int32, (half_c, LANES), 1)
    pair_causal = jnp.where(pl_ < c, pl_, pl_ - c) <= jnp.where(pl_ < c, pj, pj + half_c)
    rv = lax.broadcasted_iota(jnp.int32, (hv, hk), 0) // GLA_DV
    ck = lax.broadcasted_iota(jnp.int32, (hv, hk), 1) // GLA_DK
    blockdiag_t = jnp.where(rv == ck, 1.0, 0.0).astype(F32)
    hr = lax.broadcasted_iota(jnp.int32, (SUBLANES, hk), 0)
    hl = lax.broadcasted_iota(jnp.int32, (SUBLANES, hk), 1) // GLA_DK
    head_rows = jnp.where(hr == hl, 1.0, 0.0).astype(BF16)
    eye_v = _eye(hv, BF16)

    ga_hi, ga_lo = _split_bf16(ga_ref[0])
    w_hi, w_lo = _split_bf16(wal_ref[...])
    z = _mm(ga_hi, w_hi) + _mm(ga_hi, w_lo) + _mm(ga_lo, w_hi) + bal_ref[...]
    log_a = (jnp.minimum(z, 0.0) - jnp.log1p(jnp.exp(-jnp.abs(z)))) * (1.0 / GLA_TAU)
    la_hi = log_a.astype(BF16)
    la_r1 = log_a - la_hi.astype(F32)
    la_mid = la_r1.astype(BF16)
    la_lo = (la_r1 - la_mid.astype(F32)).astype(BF16)
    rb = lax.broadcasted_iota(jnp.int32, (GLA_BLOCK, GLA_BLOCK), 0)
    cb = lax.broadcasted_iota(jnp.int32, (GLA_BLOCK, GLA_BLOCK), 1)
    tril = jnp.where((rb >= cb) & (rb // c == cb // c), 1.0, 0.0).astype(BF16)
    b_ref[...] = _mm(tril, la_hi) + _mm(tril, la_mid) + _mm(tril, la_lo)

    def chunk(ci):
        rows = slice(ci * c, (ci + 1) * c)
        p_ref, sc_ref = p_all_ref.at[ci], sc_all_ref.at[ci]
        q = q_ref[0, rows, :] * (GLA_DK ** -0.5)
        k = k_ref[0, rows, :]
        v = v_ref[0, rows, :]
        b = b_ref[rows, :]
        state_t = state_ref[...]
        o_inter = _nt((q * jnp.exp(b)).astype(BF16), state_t.astype(BF16))

        for t in range(c):
            ns = PACK16 * (t // PACK16 + 1)
            slot = 2 * (t % half_c) + t // half_c
            p = jnp.exp(b[t:t + 1, :] - b[:ns, :]) * k[:ns, :] * q[t:t + 1, :]
            p_ref[slot * c:slot * c + ns, :] = p.astype(BF16)
        rows_hs = _nt(head_rows, p_ref[...])
        for jj in range(half_c):
            for h in range(GLA_HEADS):
                sc_ref[h, jj:jj + 1, :] = rows_hs[h:h + 1, jj * LANES:(jj + 1) * LANES]
        vb = v.astype(BF16)
        zero_v = jnp.zeros((c, GLA_DV), BF16)
        o_heads = []
        for h in range(GLA_HEADS):
            vh = vb[:, h * GLA_DV:(h + 1) * GLA_DV]
            v_pair = jnp.concatenate([jnp.concatenate([vh, zero_v], axis=1),
                                      jnp.concatenate([zero_v, vh], axis=1)], axis=0)
            scores = jnp.where(pair_causal, sc_ref[h], 0.0).astype(BF16)
            o_pair = _mm(scores, v_pair)
            o_heads.append(jnp.concatenate([o_pair[:, :GLA_DV], o_pair[:, GLA_DV:]], axis=0))
        o = o_inter + jnp.concatenate(o_heads, axis=-1)

        b_last = b[c - 1:c, :]
        kd = (k * jnp.exp(b_last - b)).astype(BF16)
        v_t = _nt(eye_v, vb).astype(BF16)
        upd_t = _mm(v_t, kd)
        state_ref[...] = jnp.exp(b_last) * state_t + upd_t * blockdiag_t

        gg = gg_ref[0, rows, :]
        outs = []
        for h in range(GLA_HEADS):
            oh = o[:, h * GLA_DV:(h + 1) * GLA_DV]
            ms = jnp.mean(oh * oh, axis=-1, keepdims=True)
            outs.append(oh * lax.rsqrt(ms + EPS))
        on = jnp.concatenate(outs, axis=-1) * gn_ref[...]
        o_ref[0, rows, :] = (on * (gg * jax.nn.sigmoid(gg))).astype(o_ref.dtype)

    for ci in range(GLA_BLOCK // c):
        chunk(ci)


def _gla(proj3, wal, bal, gn):
    bsz, seq, _ = proj3.shape
    nb = seq // GLA_BLOCK

    def col(width, off):
        return pl.BlockSpec((1, GLA_BLOCK, width), lambda b, i: (b, i, off // width))

    return pl.pallas_call(
        _gla_kernel,
        grid=(bsz, nb),
        in_specs=[
            col(128, COL_GQ), col(128, COL_GK), col(256, COL_GV), col(256, COL_GG), col(128, COL_GA),
            pl.BlockSpec((128, 128), lambda b, i: (0, 0)),
            pl.BlockSpec((1, 128), lambda b, i: (0, 0)),
            pl.BlockSpec((1, 256), lambda b, i: (0, 0)),
        ],
        out_specs=pl.BlockSpec((1, GLA_BLOCK, 256), lambda b, i: (b, i, 0)),
        out_shape=jax.ShapeDtypeStruct((bsz, seq, GLA_HEADS * GLA_DV), BF16),
        scratch_shapes=[
            pltpu.VMEM((GLA_HEADS * GLA_DV, GLA_HEADS * GLA_DK), F32),
            pltpu.VMEM((GLA_BLOCK // GLA_CHUNK, GLA_HEADS, GLA_CHUNK // 2, 2 * GLA_CHUNK), F32),
            pltpu.VMEM((GLA_BLOCK // GLA_CHUNK, GLA_CHUNK * GLA_CHUNK, GLA_HEADS * GLA_DK), BF16),
            pltpu.VMEM((GLA_BLOCK, GLA_HEADS * GLA_DK), F32),
        ],
        compiler_params=pltpu.CompilerParams(
            dimension_semantics=("arbitrary", "arbitrary"), vmem_limit_bytes=VMEM_LIMIT),
        name="gla",
    )(proj3, proj3, proj3, proj3, proj3, wal, bal, gn)


DSA_BISECT_STEPS = 12
DSA_ACCUMULATORS = 4
DSA_WALK_UNCONDITIONAL = 2


def _dsa_kernel(dq_ref, kv_ref, iq_ref, ikw_ref, btile_ref, o_ref,
                sc_ref, lg_ref, p_ref, kb_ref, vb_ref, mp_ref, lp_ref, acc_ref, cut_ref, *, n_sel):
    j = pl.program_id(1)
    nkc = (j + 2) // 2
    seq = kv_ref.shape[1]
    ksel = float(n_sel)

    @pl.when(j == 0)
    def _():
        kv = kv_ref[0]
        kb_ref[...] = kv[:, :HEAD_DIM].astype(BF16)
        vb_ref[...] = kv[:, HEAD_DIM:].astype(BF16)

    def rows(c):
        return pl.ds(pl.multiple_of(c * KC, KC), KC)

    wide = DSA_ACCUMULATORS * SUBLANES

    def fold(op, x):
        return op(x.reshape(KC // wide, wide, QB), axis=0)

    def all8(op, xw, roll_op):
        x8 = op(xw.reshape(DSA_ACCUMULATORS, SUBLANES, QB), axis=0)
        for shift in (4, 2, 1):
            x8 = roll_op(x8, pltpu.roll(x8, shift, 0))
        return x8

    def widen(x8):
        return jnp.concatenate([x8] * DSA_ACCUMULATORS, axis=0)

    zeros8 = jnp.zeros((SUBLANES, QB), F32)
    zerosw = jnp.zeros((wide, QB), F32)
    infw = jnp.full((wide, QB), jnp.inf, F32)

    ikw_q = ikw_ref[0, pl.ds(pl.multiple_of(j * QB, QB), QB), :]
    sel_r = lax.broadcasted_iota(jnp.int32, (SUBLANES, LANES), 0)
    sel_c = lax.broadcasted_iota(jnp.int32, (SUBLANES, LANES), 1)
    pick_w = jnp.where(sel_c == sel_r + IDX_DIM, 1.0, 0.0).astype(F32)
    w_t = _nt(pick_w, ikw_q, HIGHEST) * (IDX_HEADS ** -0.5 * IDX_DIM ** -0.5)
    iq = iq_ref[0].astype(BF16)
    iq_pairs = [
        jnp.concatenate([iq[:, (2 * p) * IDX_DIM:(2 * p + 1) * IDX_DIM],
                         iq[:, (2 * p + 1) * IDX_DIM:(2 * p + 2) * IDX_DIM]], axis=0)
        for p in range(IDX_HEADS // 2)]

    s_loc = lax.broadcasted_iota(jnp.int32, (KC, QB), 0)
    t_glob = j * QB + lax.broadcasted_iota(jnp.int32, (KC, QB), 1)
    s_loc_f = s_loc.astype(F32)

    def score_chunk(c, st):
        mn8, mx8, ge8, gt8 = st
        ik = ikw_ref[0, rows(c), :][:, :IDX_DIM].astype(BF16)
        acc = jnp.zeros((KC, QB), F32)
        for p in range(IDX_HEADS // 2):
            rel = _nt(ik, iq_pairs[p])
            acc = (acc + jnp.maximum(rel[:, :QB], 0.0) * w_t[2 * p:2 * p + 1, :]
                   + jnp.maximum(rel[:, QB:], 0.0) * w_t[2 * p + 1:2 * p + 2, :])
        adm = s_loc + c * KC <= t_glob
        blk = jnp.where(adm, acc, -jnp.inf)
        sc_ref[rows(c), :] = blk
        mn8 = jnp.minimum(mn8, fold(jnp.min, jnp.where(adm, acc, jnp.inf)))
        mx8 = jnp.maximum(mx8, fold(jnp.max, blk))
        ge8 = ge8 + fold(jnp.sum, jnp.where(blk >= 0.0, 1.0, 0.0))
        gt8 = gt8 + fold(jnp.sum, jnp.where(blk > 0.0, 1.0, 0.0))
        return mn8, mx8, ge8, gt8

    mn8, mx8, ge8, gt8 = lax.fori_loop(0, nkc, score_chunk, (infw, -infw, zerosw, zerosw))
    row_min = all8(jnp.min, mn8, jnp.minimum)
    row_max = all8(jnp.max, mx8, jnp.maximum)
    c_ge0, c_gt0 = all8(jnp.sum, ge8, jnp.add), all8(jnp.sum, gt8, jnp.add)

    def blocks(c):
        return sc_ref[rows(c), :].reshape(KC // wide, wide, QB)

    def count_ge(x8):
        xw = widen(x8)[None]

        def body(c, cw):
            return cw + jnp.sum(jnp.where(blocks(c) >= xw, 1.0, 0.0), axis=0)
        return all8(jnp.sum, lax.fori_loop(0, nkc, body, zerosw), jnp.add)

    def min_ge(x8):
        xw = widen(x8)[None]

        def body(c, mw):
            blk = blocks(c)
            return jnp.minimum(mw, jnp.min(jnp.where(blk >= xw, blk, jnp.inf), axis=0))
        return all8(jnp.min, lax.fori_loop(0, nkc, body, infw), jnp.minimum)

    def pass_gt(x8):
        xw = widen(x8)[None]

        def body(c, st):
            cw, mw = st
            blk = blocks(c)
            hit = blk > xw
            return (cw + jnp.sum(jnp.where(hit, 1.0, 0.0), axis=0),
                    jnp.minimum(mw, jnp.min(jnp.where(hit, blk, jnp.inf), axis=0)))
        cw, mw = lax.fori_loop(0, nkc, body, (zerosw, infw))
        return all8(jnp.sum, cw, jnp.add), all8(jnp.min, mw, jnp.minimum)

    n_adm = (j * QB + lax.broadcasted_iota(jnp.int32, (SUBLANES, QB), 1) + 1).astype(F32)
    at_zero = (c_gt0 < ksel) & (c_ge0 >= ksel)
    above = c_gt0 >= ksel
    lo = jnp.where(above | at_zero, 0.0, row_min)
    c_lo = jnp.where(above | at_zero, c_ge0, n_adm)
    settled = at_zero | (n_adm <= ksel)
    hi = jnp.where(settled, lo, jnp.where(above, row_max, 0.0))

    def bisect(_, st):
        lo, hi, c_lo = st
        mid = lo + (hi - lo) * 0.5
        cnt = count_ge(mid)
        up = cnt >= ksel
        return jnp.where(up, mid, lo), jnp.where(up, hi, mid), jnp.where(up, cnt, c_lo)

    lo, _, c_lo = lax.fori_loop(0, DSA_BISECT_STEPS, bisect, (lo, hi, c_lo))

    v0 = min_ge(lo)
    done0 = jnp.where(settled | (c_lo == ksel), 1.0, 0.0)
    c_gt_init = jnp.where(at_zero, c_gt0, 0.0)

    def walk_cond(st):
        return jnp.min(st[3]) < 0.5

    def walk_body(st):
        v, c_ge, c_gt, dn = st
        cnt, vnext = pass_gt(v)
        live = dn < 0.5
        fin = live & (cnt < ksel)
        step = live & (cnt >= ksel)
        c_gt = jnp.where(fin, cnt, c_gt)
        dn = jnp.where(fin, 1.0, dn)
        v = jnp.where(step, vnext, v)
        c_ge = jnp.where(step, cnt, c_ge)
        return v, c_ge, c_gt, dn

    walk = (v0, c_lo, c_gt_init, done0)
    for _ in range(DSA_WALK_UNCONDITIONAL):
        walk = walk_body(walk)
    tau8, c_ge, c_gt, _ = lax.while_loop(walk_cond, walk_body, walk)

    need = ksel - c_gt
    cut_ref[...] = jnp.full(cut_ref.shape, float(seq), F32)

    @pl.when(jnp.max(c_ge) > ksel)
    def _():
        s_grp = lax.broadcasted_iota(jnp.int32, (KC // wide, wide, QB), 0) * wide
        s_sub = lax.broadcasted_iota(jnp.int32, (KC // wide, wide, QB), 1)
        s_idx = (s_grp + s_sub).astype(F32)
        tauw = widen(tau8)[None]

        def count_ties_below(m8):
            mw = widen(m8)[None]

            def body(c, cw):
                hit = (blocks(c) == tauw) & (s_idx + (c * KC).astype(F32) < mw)
                return cw + jnp.sum(jnp.where(hit, 1.0, 0.0), axis=0)
            return all8(jnp.sum, lax.fori_loop(0, nkc, body, zerosw), jnp.add)

        def idx_bisect(_, lh):
            lo_m, hi_m = lh
            mid = jnp.floor((lo_m + hi_m) * 0.5)
            ok = count_ties_below(mid) >= need
            return jnp.where(ok, lo_m, mid), jnp.where(ok, mid, hi_m)

        _, hi_m = lax.fori_loop(0, int(math.log2(seq)) + 1, idx_bisect,
                                (zeros8, zeros8 + float(seq)))
        cut_ref[...] = jnp.where(c_ge > ksel, hi_m, float(seq))

    tau = tau8[0:1, :]
    cut = cut_ref[0:1, :]

    q = (dq_ref[0] * (HEAD_DIM ** -0.5)).astype(BF16)
    q4 = jnp.concatenate([q[:, h * HEAD_DIM:(h + 1) * HEAD_DIM] for h in range(DSA_HEADS)],
                         axis=0)
    mp_ref[...] = jnp.full(mp_ref.shape, NEG_BIG, F32)
    lp_ref[...] = jnp.zeros(lp_ref.shape, F32)

    half = seq // 2
    second_half = (j + 1) * QB > half

    def cols(c):
        return pl.ds(pl.multiple_of(c * KC, KC), KC)

    def logits_chunk(c, carry):
        blk = sc_ref[rows(c), :]
        sidx = s_loc_f + (c * KC).astype(F32)
        sel = (blk > tau) | ((blk == tau) & (sidx < cut))
        neg_t = jnp.where(sel, 0.0, NEG_BIG).T
        lg4 = _nt(q4, kb_ref[rows(c), :])
        w0 = jnp.clip(2 * c - j + 2, 0, 2)
        w1 = jnp.clip(2 * c + 1 - j + 2, 0, 2)
        for h in range(DSA_HEADS):
            hq = slice(h * QB, (h + 1) * QB)
            bias = jnp.concatenate([btile_ref[h, w0], btile_ref[h, w1]], axis=1)
            lg = lg4[hq, :] + bias + neg_t
            lg_ref[hq, cols(c)] = lg
            mp_ref[h] = jnp.maximum(mp_ref[h], jnp.maximum(lg[:, :QB], lg[:, QB:]))
        return carry

    lax.fori_loop(0, nkc, logits_chunk, 0)
    m_rows = [jnp.max(mp_ref[h], axis=-1, keepdims=True) for h in range(DSA_HEADS)]

    def probs_chunk(c, carry):
        for h in range(DSA_HEADS):
            hq = slice(h * QB, (h + 1) * QB)
            p = jnp.exp(lg_ref[hq, cols(c)] - m_rows[h])
            lp_ref[h] = lp_ref[h] + p[:, :QB] + p[:, QB:]
            p_ref[hq, cols(c)] = p.astype(BF16)
        return carry

    def zero_chunk(c, carry):
        p_ref[:, cols(c)] = jnp.zeros((DSA_HEADS * QB, KC), BF16)
        return carry

    lax.fori_loop(0, nkc, probs_chunk, 0)
    lax.fori_loop(nkc, jnp.where(second_half, seq // KC, half // KC), zero_chunk, 0)
    acc_ref[...] = _mm(p_ref[:, :half], vb_ref[:half, :])

    @pl.when(second_half)
    def _():
        acc_ref[...] = acc_ref[...] + _mm(p_ref[:, half:], vb_ref[half:, :])

    outs = []
    for h in range(DSA_HEADS):
        l = jnp.sum(lp_ref[h], axis=-1, keepdims=True)
        outs.append(acc_ref[h * QB:(h + 1) * QB, :] / l)
    o_ref[0] = jnp.concatenate(outs, axis=-1).astype(o_ref.dtype)


def _dsa(proj3, btiles):
    bsz, seq, _ = proj3.shape
    n_sel = min(DSA_TOPK_MAX, seq // 4)
    return pl.pallas_call(
        functools.partial(_dsa_kernel, n_sel=n_sel),
        grid=(bsz, seq // QB),
        in_specs=[
            pl.BlockSpec((1, QB, 256), lambda b, j: (b, j, COL_DQ // 256)),
            pl.BlockSpec((1, seq, 128), lambda b, j: (b, 0, COL_DKV // 128)),
            pl.BlockSpec((1, QB, 512), lambda b, j: (b, j, COL_IQ // 512)),
            pl.BlockSpec((1, seq, 128), lambda b, j: (b, 0, COL_IKW // 128)),
            pl.BlockSpec((DSA_HEADS, 3, QB, QB), lambda b, j: (0, 0, 0, 0)),
        ],
        out_specs=pl.BlockSpec((1, QB, 256), lambda b, j: (b, j, 0)),
        out_shape=jax.ShapeDtypeStruct((bsz, seq, DSA_HEADS * HEAD_DIM), BF16),
        scratch_shapes=[
            pltpu.VMEM((seq, QB), F32),
            pltpu.VMEM((DSA_HEADS * QB, seq), F32),
            pltpu.VMEM((DSA_HEADS * QB, seq), BF16),
            pltpu.VMEM((seq, HEAD_DIM), BF16),
            pltpu.VMEM((seq, HEAD_DIM), BF16),
            pltpu.VMEM((DSA_HEADS, QB, QB), F32),
            pltpu.VMEM((DSA_HEADS, QB, QB), F32),
            pltpu.VMEM((DSA_HEADS * QB, HEAD_DIM), F32),
            pltpu.VMEM((SUBLANES, QB), F32),
        ],
        compiler_params=pltpu.CompilerParams(
            dimension_semantics=("arbitrary", "arbitrary"), vmem_limit_bytes=VMEM_LIMIT),
        name="dsa",
    )(proj3, proj3, proj3, proj3, btiles)


def _swa_kernel(sink_ref, q_ref, kc_ref, kp_ref, vc_ref, vp_ref, bias_ref, o_ref):
    n = pl.program_id(1)
    q = (q_ref[0] * (HEAD_DIM ** -0.5)).astype(BF16)
    k2 = jnp.concatenate([kp_ref[0], kc_ref[0]], axis=0).astype(BF16)
    v2 = jnp.concatenate([vp_ref[0], vc_ref[0]], axis=0).astype(BF16)
    qi = lax.broadcasted_iota(jnp.int32, (WINDOW, 2 * WINDOW), 0)
    kj = lax.broadcasted_iota(jnp.int32, (WINDOW, 2 * WINDOW), 1)
    dist = qi + WINDOW - kj
    mask = (dist >= 0) & (dist < WINDOW) & ((kj >= WINDOW) | (n > 0))
    outs = []
    for h in range(SWA_HEADS):
        kvh = h // SWA_GROUP
        kh = k2[:, kvh * HEAD_DIM:(kvh + 1) * HEAD_DIM]
        vh = v2[:, kvh * HEAD_DIM:(kvh + 1) * HEAD_DIM]
        lg = _nt(q[:, h * HEAD_DIM:(h + 1) * HEAD_DIM], kh) + bias_ref[h]
        lg = jnp.where(mask, lg, -jnp.inf)
        sink = sink_ref[h]
        m = jnp.maximum(jnp.max(lg, axis=-1, keepdims=True), sink)
        e = jnp.exp(lg - m)
        den = jnp.sum(e, axis=-1, keepdims=True) + jnp.exp(sink - m)
        outs.append(_mm((e / den).astype(BF16), vh))
    o_ref[0] = jnp.concatenate(outs, axis=-1).astype(o_ref.dtype)


def _swa(proj3, sinks, bias_nat):
    bsz, seq, _ = proj3.shape
    return pl.pallas_call(
        _swa_kernel,
        grid=(bsz, seq // WINDOW),
        in_specs=[
            pl.BlockSpec(memory_space=pltpu.SMEM),
            pl.BlockSpec((1, WINDOW, 512), lambda b, n: (b, n, COL_SQ // 512)),
            pl.BlockSpec((1, WINDOW, 128), lambda b, n: (b, n, COL_SK // 128)),
            pl.BlockSpec((1, WINDOW, 128), lambda b, n: (b, jnp.maximum(n - 1, 0), COL_SK // 128)),
            pl.BlockSpec((1, WINDOW, 128), lambda b, n: (b, n, COL_SV // 128)),
            pl.BlockSpec((1, WINDOW, 128), lambda b, n: (b, jnp.maximum(n - 1, 0), COL_SV // 128)),
            pl.BlockSpec((SWA_HEADS, WINDOW, 2 * WINDOW), lambda b, n: (0, 0, 0)),
        ],
        out_specs=pl.BlockSpec((1, WINDOW, 512), lambda b, n: (b, n, 0)),
        out_shape=jax.ShapeDtypeStruct((bsz, seq, SWA_HEADS * HEAD_DIM), BF16),
        compiler_params=pltpu.CompilerParams(
            dimension_semantics=("arbitrary", "arbitrary"), vmem_limit_bytes=VMEM_LIMIT),
        name="swa",
    )(sinks, proj3, proj3, proj3, proj3, proj3, bias_nat)


ROUTE_OFF = N_GROUPS


def _split_bf16(x):
    hi = x.astype(BF16)
    lo = (x - hi.astype(F32)).astype(BF16)
    return hi, lo


def _outproj_router_kernel(h_ref, og_ref, od_ref, os_ref, wo_ref, gffn_ref, wr_hi_ref, wr_lo_ref,
                           br_ref, h1_ref, xn_ref, ri_ref, rf_ref, cnt_ref, run_ref):
    tm = h_ref.shape[0]

    @pl.when(pl.program_id(0) == 0)
    def _():
        run_ref[...] = jnp.zeros_like(run_ref)

    o = jnp.concatenate([og_ref[...], od_ref[...], os_ref[...]], axis=-1)
    h1 = h_ref[...] + _mm(o, wo_ref[...])
    h1_ref[...] = h1
    xn = _rms(h1, gffn_ref[...])
    xn_ref[...] = _pack_bf16_pairs(xn)

    x_hi, x_lo = _split_bf16(xn)
    lg = (_mm(x_hi, wr_hi_ref[...]) + _mm(x_lo, wr_hi_ref[...]) + _mm(x_hi, wr_lo_ref[...])
          + br_ref[...])

    lane = lax.broadcasted_iota(jnp.int32, lg.shape, 1)
    lane_f = lane.astype(F32)
    ninf = -jnp.inf

    def first_max(x):
        m = jnp.max(x, axis=-1, keepdims=True)
        idx = jnp.min(jnp.where(x == m, lane_f, float(LANES)), axis=-1, keepdims=True)
        return m, idx

    gl = jnp.where(lane < N_GROUPS, lg, ninf)
    gmax, gsel = first_max(gl)
    g_w = 1.0 / jnp.sum(jnp.exp(gl - gmax), axis=-1, keepdims=True)
    e_lo = ROUTE_OFF + EXPERTS_PER_GROUP * gsel
    el = jnp.where((lane_f >= e_lo) & (lane_f < e_lo + EXPERTS_PER_GROUP), lg, ninf)
    m1, i1 = first_max(el)
    eden = jnp.sum(jnp.exp(el - m1), axis=-1, keepdims=True)
    m2, i2 = first_max(jnp.where(lane_f == i1, ninf, el))
    p1 = 1.0 / eden
    p2 = jnp.exp(m2 - m1) / eden
    gate1 = g_w * p1 / (p1 + p2)
    gate2 = g_w * p2 / (p1 + p2)

    onehot = jnp.where((lane_f == i1) | (lane_f == i2), 1.0, 0.0)
    rr = lax.broadcasted_iota(jnp.int32, (tm, tm), 0)
    cc = lax.broadcasted_iota(jnp.int32, (tm, tm), 1)
    strict = jnp.where(rr > cc, 1.0, 0.0).astype(BF16)
    before = _mm(strict, onehot.astype(BF16)) + run_ref[0:1, :]
    rank1 = jnp.sum(jnp.where(lane_f == i1, before, 0.0), axis=-1, keepdims=True)
    rank2 = jnp.sum(jnp.where(lane_f == i2, before, 0.0), axis=-1, keepdims=True)
    run_ref[0:1, :] = run_ref[0:1, :] + jnp.sum(onehot, axis=0, keepdims=True)
    cnt_ref[...] = run_ref[...]

    ints = jnp.where(lane == 0, i1 - ROUTE_OFF,
                     jnp.where(lane == 1, i2 - ROUTE_OFF,
                               jnp.where(lane == 2, rank1, jnp.where(lane == 3, rank2, 0.0))))
    ri_ref[...] = ints.astype(jnp.int32)
    rf_ref[...] = jnp.where(lane == 0, gate1, jnp.where(lane == 1, gate2, 0.0))


def _outproj_router(h, og, od, os_, wo, gffn, wr_hi, wr_lo, br):
    t = h.shape[0]
    tm = TOKEN_TILE
    row = lambda w: pl.BlockSpec((tm, w), lambda i: (i, 0))
    full = lambda a, b: pl.BlockSpec((a, b), lambda i: (0, 0))
    return pl.pallas_call(
        _outproj_router_kernel,
        grid=(t // tm,),
        in_specs=[row(D_MODEL), row(256), row(256), row(512), full(D_MODEL, D_MODEL),
                  full(1, D_MODEL), full(D_MODEL, LANES), full(D_MODEL, LANES), full(1, LANES)],
        out_specs=[row(D_MODEL), row(HALF_MODEL), row(LANES), row(LANES), full(SUBLANES, LANES)],
        out_shape=[
            jax.ShapeDtypeStruct((t, D_MODEL), F32),
            jax.ShapeDtypeStruct((t, HALF_MODEL), jnp.uint32),
            jax.ShapeDtypeStruct((t, LANES), jnp.int32),
            jax.ShapeDtypeStruct((t, LANES), F32),
            jax.ShapeDtypeStruct((SUBLANES, LANES), F32),
        ],
        scratch_shapes=[pltpu.VMEM((SUBLANES, LANES), F32)],
        compiler_params=pltpu.CompilerParams(
            dimension_semantics=("arbitrary",), vmem_limit_bytes=VMEM_LIMIT),
        name="outproj_router",
    )(h, og, od, os_, wo, gffn, wr_hi, wr_lo, br)


def _dispatch_kernel(pos_ref, xn_ref, buf_in_ref, buf_ref, sem):
    del buf_in_ref
    tm = xn_ref.shape[0]

    def row_copy(r, k):
        dst = pos_ref[0, 0, 2 * r + k]
        return pltpu.make_async_copy(xn_ref.at[pl.ds(r, 1)], buf_ref.at[pl.ds(dst, 1)], sem)

    def issue(g, carry):
        for u in range(DMA_UNROLL):
            row_copy(g * DMA_UNROLL + u, 0).start()
            row_copy(g * DMA_UNROLL + u, 1).start()
        return carry

    def drain(g, carry):
        for u in range(DMA_UNROLL):
            row_copy(g * DMA_UNROLL + u, 0).wait()
            row_copy(g * DMA_UNROLL + u, 1).wait()
        return carry

    lax.fori_loop(0, tm // DMA_UNROLL, issue, 0)
    lax.fori_loop(0, tm // DMA_UNROLL, drain, 0)


def _dispatch(pos3, xn, buf0):
    t = xn.shape[0]
    tm = TOKEN_TILE
    return pl.pallas_call(
        _dispatch_kernel,
        grid=(t // tm,),
        in_specs=[
            pl.BlockSpec((1, 1, 2 * tm), lambda i: (i, 0, 0), memory_space=pltpu.SMEM),
            pl.BlockSpec((tm, HALF_MODEL), lambda i: (i, 0)),
            pl.BlockSpec(memory_space=pl.ANY),
        ],
        out_specs=pl.BlockSpec(memory_space=pl.ANY),
        out_shape=jax.ShapeDtypeStruct(buf0.shape, buf0.dtype),
        scratch_shapes=[pltpu.SemaphoreType.DMA(())],
        input_output_aliases={2: 0},
        compiler_params=pltpu.CompilerParams(
            dimension_semantics=("arbitrary",), vmem_limit_bytes=VMEM_LIMIT),
        name="dispatch",
    )(pos3, xn, buf0)


def _expert_kernel(be_ref, nu_ref, x_ref, wg_ref, wu_ref, wd_ref, y_ref, wg_s, wu_s, wd_s):
    i = pl.program_id(0)
    used = i < nu_ref[0]
    new_expert = (i == 0) | (be_ref[i] != be_ref[jnp.maximum(i - 1, 0)])

    @pl.when(used & new_expert)
    def _():
        wg_s[...] = wg_ref[0, 0].astype(BF16)
        wu_s[...] = wu_ref[0, 0].astype(BF16)
        wd_s[...] = wd_ref[0, 0].astype(BF16)

    @pl.when(used)
    def _():
        x_lo, x_hi = (part.astype(BF16) for part in _unpack_bf16_pairs(x_ref[...]))
        g = _mm(x_lo, wg_s[:HALF_MODEL, :]) + _mm(x_hi, wg_s[HALF_MODEL:, :])
        u = _mm(x_lo, wu_s[:HALF_MODEL, :]) + _mm(x_hi, wu_s[HALF_MODEL:, :])
        hmid = (g * jax.nn.sigmoid(g)) * u
        y_ref[...] = _pack_bf16_pairs(_mm(hmid.astype(BF16), wd_s[...]))

    @pl.when(i >= nu_ref[0])
    def _():
        y_ref[...] = jnp.zeros_like(y_ref)


def _experts(blk_expert, n_used, buf, wg, wu, wd, layer):
    nrows = buf.shape[0]
    nblk = nrows // EXPERT_ROWS
    w_index = lambda i, be, nu: (layer, be[i], 0, 0)
    return pl.pallas_call(
        _expert_kernel,
        grid_spec=pltpu.PrefetchScalarGridSpec(
            num_scalar_prefetch=2,
            grid=(nblk,),
            in_specs=[
                pl.BlockSpec((EXPERT_ROWS, HALF_MODEL), lambda i, be, nu: (i, 0)),
                pl.BlockSpec((1, 1, D_MODEL, D_EXPERT), w_index),
                pl.BlockSpec((1, 1, D_MODEL, D_EXPERT), w_index),
                pl.BlockSpec((1, 1, D_EXPERT, D_MODEL), w_index),
            ],
            out_specs=pl.BlockSpec((EXPERT_ROWS, HALF_MODEL), lambda i, be, nu: (i, 0)),
            scratch_shapes=[
                pltpu.VMEM((D_MODEL, D_EXPERT), BF16),
                pltpu.VMEM((D_MODEL, D_EXPERT), BF16),
                pltpu.VMEM((D_EXPERT, D_MODEL), BF16),
            ],
        ),
        out_shape=jax.ShapeDtypeStruct((nrows, HALF_MODEL), jnp.uint32),
        compiler_params=pltpu.CompilerParams(
            dimension_semantics=("arbitrary",), vmem_limit_bytes=VMEM_LIMIT),
        name="experts",
    )(blk_expert, n_used, buf, wg, wu, wd)


def _combine_ple_kernel(pos_ref, h1_ref, p_ref, rf_ref, yb_ref, wple_ref, gple_ref, wpg_ref,
                        gfin_ref, o_ref, ybuf, sem, *, final_norm):
    tm = h1_ref.shape[0]

    def row_copy(r, k):
        src = pos_ref[0, 0, 2 * r + k]
        return pltpu.make_async_copy(yb_ref.at[pl.ds(src, 1)], ybuf.at[k, pl.ds(r, 1)], sem)

    def issue(g, carry):
        for u in range(DMA_UNROLL):
            row_copy(g * DMA_UNROLL + u, 0).start()
            row_copy(g * DMA_UNROLL + u, 1).start()
        return carry

    def drain(g, carry):
        for u in range(DMA_UNROLL):
            row_copy(g * DMA_UNROLL + u, 0).wait()
            row_copy(g * DMA_UNROLL + u, 1).wait()
        return carry

    lax.fori_loop(0, tm // DMA_UNROLL, issue, 0)
    e = _rms(_mm(p_ref[...].astype(BF16), wple_ref[...]), gple_ref[...])
    lax.fori_loop(0, tm // DMA_UNROLL, drain, 0)

    rf = rf_ref[...]
    y0_lo, y0_hi = _unpack_bf16_pairs(ybuf[0])
    y1_lo, y1_hi = _unpack_bf16_pairs(ybuf[1])
    g0, g1 = rf[:, 0:1], rf[:, 1:2]
    moe = jnp.concatenate([y0_lo * g0 + y1_lo * g1, y0_hi * g0 + y1_hi * g1], axis=-1)
    h2 = h1_ref[...] + moe
    h3 = h2 + e * jax.nn.sigmoid(_mm(h2.astype(BF16), wpg_ref[...]))
    if final_norm:
        h3 = _rms(h3, gfin_ref[...])
    o_ref[...] = h3


def _combine_ple(pos3, h1, p, rf, yb, wple, gple, wpg, gfin, layer, final_norm):
    t = h1.shape[0]
    tm = TOKEN_TILE
    row = lambda w: pl.BlockSpec((tm, w), lambda i: (i, 0))
    full = lambda a, b: pl.BlockSpec((a, b), lambda i: (0, 0))
    return pl.pallas_call(
        functools.partial(_combine_ple_kernel, final_norm=final_norm),
        grid=(t // tm,),
        in_specs=[
            pl.BlockSpec((1, 1, 2 * tm), lambda i: (i, 0, 0), memory_space=pltpu.SMEM),
            row(D_MODEL),
            pl.BlockSpec((tm, PLE_DIM), lambda i: (layer * (t // tm) + i, 0)),
            row(LANES),
            pl.BlockSpec(memory_space=pl.ANY),
            full(PLE_DIM, D_MODEL), full(1, D_MODEL), full(D_MODEL, D_MODEL), full(1, D_MODEL),
        ],
        out_specs=row(D_MODEL),
        out_shape=jax.ShapeDtypeStruct((t, D_MODEL), F32),
        scratch_shapes=[pltpu.VMEM((2, tm, HALF_MODEL), jnp.uint32), pltpu.SemaphoreType.DMA(())],
        compiler_params=pltpu.CompilerParams(
            dimension_semantics=("arbitrary",), vmem_limit_bytes=VMEM_LIMIT),
        name="combine_ple",
    )(pos3, h1, p, rf, yb, wple, gple, wpg, gfin)


def _rel_bucket(dist):
    n = jnp.maximum(dist, 0)
    max_exact = REL_BUCKETS // 2
    nf = jnp.maximum(n, 1).astype(F32)
    large = max_exact + (jnp.log(nf / max_exact) / math.log(REL_MAX_DIST / max_exact)
                         * (REL_BUCKETS - max_exact)).astype(jnp.int32)
    large = jnp.minimum(large, REL_BUCKETS - 1)
    return jnp.where(n < max_exact, n, large)


def _bias_tiles(rel_bias):
    qi = jnp.arange(WINDOW)[:, None]
    kj = jnp.arange(2 * WINDOW)[None, :]
    bucket = _rel_bucket(qi + WINDOW - kj)
    onehot = (bucket[..., None] == jnp.arange(REL_BUCKETS)).astype(F32)
    nat = jnp.einsum('qkb,bh->hqk', onehot, rel_bias, precision=HIGHEST)
    dsa = nat[:DSA_HEADS]
    far = jnp.broadcast_to(rel_bias[REL_BUCKETS - 1, :DSA_HEADS][:, None, None],
                           (DSA_HEADS, WINDOW, WINDOW))
    return jnp.stack([far, dsa[:, :, :WINDOW], dsa[:, :, WINDOW:]], axis=1), nat[DSA_HEADS:]


def _pack_w_in(w):
    sizes = (128, 128, 256, 256, 16, 256, 64, 64, 512, 64, 8, 512, 128, 128)
    offs = [0]
    for s in sizes:
        offs.append(offs[-1] + s)
    wt = w.T
    gq, gk, gv, gg, ga, dq, dk, dv, iq, ik, iw, sq, sk, sv = (
        wt[offs[n]:offs[n + 1]] for n in range(len(sizes)))
    z = lambda n: jnp.zeros((n, w.shape[0]), w.dtype)
    packed = jnp.concatenate(
        [iq, sq, gv, gg, dq, gq, gk, ga, z(128 - GLA_RANK), dk, dv, ik, iw,
         z(128 - IDX_DIM - IDX_HEADS), sk, sv], axis=0)
    return packed.astype(BF16)


def kernel(x, p, rel_bias, g_mix, w_in, gla_w_alpha, gla_b_alpha, gla_g_norm, swa_sinks, w_out,
           g_ffn, w_router_group, b_router_group, w_router_expert, b_router_expert, w_expert_gate,
           w_expert_up, w_expert_down, w_ple, g_ple, w_ple_gate, g_final):
    bsz, seq, d = x.shape
    depth = w_in.shape[0]
    t = bsz * seq
    assert d == D_MODEL and t % TOKEN_TILE == 0 and seq % GLA_BLOCK == 0 and seq % KC == 0
    n_blocks = -(-(2 * t) // EXPERT_ROWS) + N_EXPERTS
    dsa_tiles, swa_bias = _bias_tiles(rel_bias)

    h = x.reshape(t, d)
    for i in range(depth):
        proj = _inproj(h, g_mix[i][None, :], _pack_w_in(w_in[i]))
        proj3 = proj.reshape(bsz, seq, D_PROJ)

        wal = jnp.zeros((128, 128), F32).at[:GLA_RANK].set(gla_w_alpha[i])
        og = _gla(proj3, wal, gla_b_alpha[i][None, :],
                  jnp.tile(gla_g_norm[i], GLA_HEADS)[None, :])
        od = _dsa(proj3, dsa_tiles)
        os_ = _swa(proj3, swa_sinks[i], swa_bias)

        w_r = jnp.concatenate([w_router_group[i], w_router_expert[i]], axis=1)
        w_r = jnp.pad(w_r, ((0, 0), (0, LANES - w_r.shape[1])))
        wr_hi, wr_lo = _split_bf16(w_r)
        b_r = jnp.pad(jnp.concatenate([b_router_group[i], b_router_expert[i]]),
                      (0, LANES - N_GROUPS - N_EXPERTS))[None, :]
        h1, xn, ri, rf, cnt = _outproj_router(
            h, og.reshape(t, -1), od.reshape(t, -1), os_.reshape(t, -1), w_out[i].astype(BF16),
            g_ffn[i][None, :], wr_hi, wr_lo, b_r)

        counts = cnt[0, ROUTE_OFF:ROUTE_OFF + N_EXPERTS].astype(jnp.int32)
        padded = (counts + EXPERT_ROWS - 1) // EXPERT_ROWS * EXPERT_ROWS
        pad_end = jnp.cumsum(padded)
        pad_start = pad_end - padded
        expert_ids = jnp.arange(N_EXPERTS, dtype=jnp.int32)
        slot_start = jnp.sum(
            jnp.where(ri[:, 0:2, None] == expert_ids, pad_start, 0), axis=-1)
        pos3 = (slot_start + ri[:, 2:4]).reshape(t // TOKEN_TILE, 1, 2 * TOKEN_TILE)
        blk_start = jnp.arange(n_blocks, dtype=jnp.int32) * EXPERT_ROWS
        blk_expert = jnp.minimum(
            jnp.sum((pad_end[None, :] <= blk_start[:, None]).astype(jnp.int32), axis=1),
            N_EXPERTS - 1)
        n_used = (pad_end[-1:] // EXPERT_ROWS).astype(jnp.int32)

        buf = _dispatch(pos3, xn, jnp.zeros((n_blocks * EXPERT_ROWS, HALF_MODEL), jnp.uint32))
        yb = _experts(blk_expert, n_used, buf, w_expert_gate, w_expert_up, w_expert_down, i)
        h = _combine_ple(pos3, h1, p.reshape(depth * t, PLE_DIM), rf, yb, w_ple[i].astype(BF16),
                         g_ple[i][None, :], w_ple_gate[i].astype(BF16), g_final[None, :],
                         layer=i, final_norm=(i == depth - 1))
    return h.reshape(bsz, seq, d)
```

```python
import functools
import math

import jax
import jax.numpy as jnp
from jax import lax
from jax.experimental import pallas as pl
from jax.experimental.pallas import tpu as pltpu

F32 = jnp.float32
BF16 = jnp.bfloat16
HIGHEST = lax.Precision.HIGHEST

D_MODEL = 1024
HEAD_DIM = 64
GLA_HEADS = 4
GLA_DK = 32
GLA_DV = 64
GLA_RANK = 16
GLA_TAU = 16.0
GLA_CHUNK = 64
DSA_HEADS = 4
IDX_HEADS = 8
IDX_DIM = 64
DSA_TOPK_MAX = 256
SWA_HEADS = 8
SWA_KV_HEADS = 2
SWA_GROUP = SWA_HEADS // SWA_KV_HEADS
WINDOW = 128
REL_BUCKETS = 32
REL_MAX_DIST = 128
N_GROUPS = 4
EXPERTS_PER_GROUP = 8
N_EXPERTS = N_GROUPS * EXPERTS_PER_GROUP
D_EXPERT = 512
PLE_DIM = 256
EPS = 1e-6

LANES = 128
SUBLANES = 8
PACK16 = 16

COL_IQ, COL_SQ = 0, 512
COL_GV, COL_GG, COL_DQ = 1024, 1280, 1536
COL_GQ, COL_GK, COL_GA, COL_DKV, COL_IKW, COL_SK, COL_SV = 1792, 1920, 2048, 2176, 2304, 2432, 2560
D_PROJ = 2688

TOKEN_TILE = 256
INPROJ_TILE = 512
ROUTER_TILE = 512
EXPERT_ROWS = 512
GLA_BLOCK = 256
QB = 128
KC = 256
DMA_UNROLL = 8
NEG_BIG = -1e30
VMEM_LIMIT = 48 * 1024 * 1024


def _nt(a, b, precision=None):
    return lax.dot_general(a, b, (((1,), (1,)), ((), ())), precision=precision,
                           preferred_element_type=F32)


def _mm(a, b, precision=None):
    return jnp.dot(a, b, precision=precision, preferred_element_type=F32)


def _eye(n, dtype):
    r = lax.broadcasted_iota(jnp.int32, (n, n), 0)
    c = lax.broadcasted_iota(jnp.int32, (n, n), 1)
    return jnp.where(r == c, 1.0, 0.0).astype(dtype)


def _rms(x, g):
    return x * lax.rsqrt(jnp.mean(x * x, axis=-1, keepdims=True) + EPS) * g


HALF_MODEL = D_MODEL // 2
HIGH16 = 0xFFFF0000


def _pack_bf16_pairs(x):
    bits = pltpu.bitcast(x.astype(BF16).astype(F32), jnp.uint32)
    return (bits[:, HALF_MODEL:] & jnp.uint32(HIGH16)) | (bits[:, :HALF_MODEL] >> 16)


def _unpack_bf16_pairs(w):
    return pltpu.bitcast(w << 16, F32), pltpu.bitcast(w & jnp.uint32(HIGH16), F32)


def _inproj_kernel(h_ref, g_ref, w_ref, o_ref):
    a = _rms(h_ref[...], g_ref[...])
    o_ref[...] = _nt(a.astype(BF16), w_ref[...])


def _inproj(h, g, w):
    t = h.shape[0]
    return pl.pallas_call(
        _inproj_kernel,
        grid=(t // INPROJ_TILE,),
        in_specs=[
            pl.BlockSpec((INPROJ_TILE, D_MODEL), lambda i: (i, 0)),
            pl.BlockSpec((1, D_MODEL), lambda i: (0, 0)),
            pl.BlockSpec((D_PROJ, D_MODEL), lambda i: (0, 0)),
        ],
        out_specs=pl.BlockSpec((INPROJ_TILE, D_PROJ), lambda i: (i, 0)),
        out_shape=jax.ShapeDtypeStruct((t, D_PROJ), F32),
        compiler_params=pltpu.CompilerParams(
            dimension_semantics=("arbitrary",), vmem_limit_bytes=VMEM_LIMIT),
        name="inproj",
    )(h, g, w)


def _gla_kernel(q_ref, k_ref, v_ref, gg_ref, ga_ref, wal_ref, bal_ref, gn_ref, o_ref,
                state_ref, sc_all_ref, p_all_ref, b_ref):
    hk = GLA_HEADS * GLA_DK
    hv = GLA_HEADS * GLA_DV
    c = GLA_CHUNK
    half_c = c // 2
    assert 2 * c == LANES

    @pl.when(pl.program_id(1) == 0)
    def _():
        state_ref[...] = jnp.zeros_like(state_ref)

    @pl.when((pl.program_id(0) == 0) & (pl.program_id(1) == 0))
    def _():
        p_all_ref[...] = jnp.zeros_like(p_all_ref)

    pj = lax.broadcasted_iota(jnp.int32, (half_c, LANES), 0)
    pl_ = lax.broadcasted_iota(jnp.int32, (half_c, LANES), 1)
    pair_causal = jnp.where(pl_ < c, pl_, pl_ - c) <= jnp.where(pl_ < c, pj, pj + half_c)
    rv = lax.broadcasted_iota(jnp.int32, (hv, hk), 0) // GLA_DV
    ck = lax.broadcasted_iota(jnp.int32, (hv, hk), 1) // GLA_DK
    blockdiag_t = jnp.where(rv == ck, 1.0, 0.0).astype(F32)
    hr = lax.broadcasted_iota(jnp.int32, (SUBLANES, hk), 0)
    hl = lax.broadcasted_iota(jnp.int32, (SUBLANES, hk), 1) // GLA_DK
    head_rows = jnp.where(hr == hl, 1.0, 0.0).astype(BF16)
    eye_v = _eye(hv, BF16)

    ga_hi, ga_lo = _split_bf16(ga_ref[0])
    w_hi, w_lo = _split_bf16(wal_ref[...])
    z = _mm(ga_hi, w_hi) + _mm(ga_hi, w_lo) + _mm(ga_lo, w_hi) + bal_ref[...]
    log_a = (jnp.minimum(z, 0.0) - jnp.log1p(jnp.exp(-jnp.abs(z)))) * (1.0 / GLA_TAU)
    la_hi = log_a.astype(BF16)
    la_r1 = log_a - la_hi.astype(F32)
    la_mid = la_r1.astype(BF16)
    la_lo = (la_r1 - la_mid.astype(F32)).astype(BF16)
    rb = lax.broadcasted_iota(jnp.int32, (GLA_BLOCK, GLA_BLOCK), 0)
    cb = lax.broadcasted_iota(jnp.int32, (GLA_BLOCK, GLA_BLOCK), 1)
    tril = jnp.where((rb >= cb) & (rb // c == cb // c), 1.0, 0.0).astype(BF16)
    b_ref[...] = _mm(tril, la_hi) + _mm(tril, la_mid) + _mm(tril, la_lo)

    def chunk(ci):
        rows = slice(ci * c, (ci + 1) * c)
        p_ref, sc_ref = p_all_ref.at[ci], sc_all_ref.at[ci]
        q = q_ref[0, rows, :] * (GLA_DK ** -0.5)
        k = k_ref[0, rows, :]
        v = v_ref[0, rows, :]
        b = b_ref[rows, :]
        state_t = state_ref[...]
        o_inter = _nt((q * jnp.exp(b)).astype(BF16), state_t.astype(BF16))

        for t in range(c):
            ns = PACK16 * (t // PACK16 + 1)
            slot = 2 * (t % half_c) + t // half_c
            p = jnp.exp(b[t:t + 1, :] - b[:ns, :]) * k[:ns, :] * q[t:t + 1, :]
            p_ref[slot * c:slot * c + ns, :] = p.astype(BF16)
        rows_hs = _nt(head_rows, p_ref[...])
        for jj in range(half_c):
            for h in range(GLA_HEADS):
                sc_ref[h, jj:jj + 1, :] = rows_hs[h:h + 1, jj * LANES:(jj + 1) * LANES]
        vb = v.astype(BF16)
        zero_v = jnp.zeros((c, GLA_DV), BF16)
        o_heads = []
        for h in range(GLA_HEADS):
            vh = vb[:, h * GLA_DV:(h + 1) * GLA_DV]
            v_pair = jnp.concatenate([jnp.concatenate([vh, zero_v], axis=1),
                                      jnp.concatenate([zero_v, vh], axis=1)], axis=0)
            scores = jnp.where(pair_causal, sc_ref[h], 0.0).astype(BF16)
            o_pair = _mm(scores, v_pair)
            o_heads.append(jnp.concatenate([o_pair[:, :GLA_DV], o_pair[:, GLA_DV:]], axis=0))
        o = o_inter + jnp.concatenate(o_heads, axis=-1)

        b_last = b[c - 1:c, :]
        kd = (k * jnp.exp(b_last - b)).astype(BF16)
        v_t = _nt(eye_v, vb).astype(BF16)
        upd_t = _mm(v_t, kd)
        state_ref[...] = jnp.exp(b_last) * state_t + upd_t * blockdiag_t

        gg = gg_ref[0, rows, :]
        outs = []
        for h in range(GLA_HEADS):
            oh = o[:, h * GLA_DV:(h + 1) * GLA_DV]
            ms = jnp.mean(oh * oh, axis=-1, keepdims=True)
            outs.append(oh * lax.rsqrt(ms + EPS))
        on = jnp.concatenate(outs, axis=-1) * gn_ref[...]
        o_ref[0, rows, :] = (on * (gg * jax.nn.sigmoid(gg))).astype(o_ref.dtype)

    for ci in range(GLA_BLOCK // c):
        chunk(ci)


def _gla(proj3, wal, bal, gn):
    bsz, seq, _ = proj3.shape
    nb = seq // GLA_BLOCK

    def col(width, off):
        return pl.BlockSpec((1, GLA_BLOCK, width), lambda b, i: (b, i, off // width))

    return pl.pallas_call(
        _gla_kernel,
        grid=(bsz, nb),
        in_specs=[
            col(128, COL_GQ), col(128, COL_GK), col(256, COL_GV), col(256, COL_GG), col(128, COL_GA),
            pl.BlockSpec((128, 128), lambda b, i: (0, 0)),
            pl.BlockSpec((1, 128), lambda b, i: (0, 0)),
            pl.BlockSpec((1, 256), lambda b, i: (0, 0)),
        ],
        out_specs=pl.BlockSpec((1, GLA_BLOCK, 256), lambda b, i: (b, i, 0)),
        out_shape=jax.ShapeDtypeStruct((bsz, seq, GLA_HEADS * GLA_DV), BF16),
        scratch_shapes=[
            pltpu.VMEM((GLA_HEADS * GLA_DV, GLA_HEADS * GLA_DK), F32),
            pltpu.VMEM((GLA_BLOCK // GLA_CHUNK, GLA_HEADS, GLA_CHUNK // 2, 2 * GLA_CHUNK), F32),
            pltpu.VMEM((GLA_BLOCK // GLA_CHUNK, GLA_CHUNK * GLA_CHUNK, GLA_HEADS * GLA_DK), BF16),
            pltpu.VMEM((GLA_BLOCK, GLA_HEADS * GLA_DK), F32),
        ],
        compiler_params=pltpu.CompilerParams(
            dimension_semantics=("arbitrary", "arbitrary"), vmem_limit_bytes=VMEM_LIMIT),
        name="gla",
    )(proj3, proj3, proj3, proj3, proj3, wal, bal, gn)


DSA_BISECT_STEPS = 12
DSA_ACCUMULATORS = 4
DSA_WALK_UNCONDITIONAL = 2


def _dsa_kernel(dq_ref, kv_ref, iq_ref, ikw_ref, btile_ref, o_ref,
                sc_ref, lg_ref, p_ref, kb_ref, vb_ref, mp_ref, lp_ref, acc_ref, cut_ref, *, n_sel):
    j = pl.program_id(1)
    nkc = (j + 2) // 2
    seq = kv_ref.shape[1]
    ksel = float(n_sel)

    @pl.when(j == 0)
    def _():
        kv = kv_ref[0]
        kb_ref[...] = kv[:, :HEAD_DIM].astype(BF16)
        vb_ref[...] = kv[:, HEAD_DIM:].astype(BF16)

    def rows(c):
        return pl.ds(pl.multiple_of(c * KC, KC), KC)

    wide = DSA_ACCUMULATORS * SUBLANES

    def fold(op, x):
        return op(x.reshape(KC // wide, wide, QB), axis=0)

    def all8(op, xw, roll_op):
        x8 = op(xw.reshape(DSA_ACCUMULATORS, SUBLANES, QB), axis=0)
        for shift in (4, 2, 1):
            x8 = roll_op(x8, pltpu.roll(x8, shift, 0))
        return x8

    def widen(x8):
        return jnp.concatenate([x8] * DSA_ACCUMULATORS, axis=0)

    zeros8 = jnp.zeros((SUBLANES, QB), F32)
    zerosw = jnp.zeros((wide, QB), F32)
    infw = jnp.full((wide, QB), jnp.inf, F32)

    ikw_q = ikw_ref[0, pl.ds(pl.multiple_of(j * QB, QB), QB), :]
    sel_r = lax.broadcasted_iota(jnp.int32, (SUBLANES, LANES), 0)
    sel_c = lax.broadcasted_iota(jnp.int32, (SUBLANES, LANES), 1)
    pick_w = jnp.where(sel_c == sel_r + IDX_DIM, 1.0, 0.0).astype(F32)
    w_t = _nt(pick_w, ikw_q, HIGHEST) * (IDX_HEADS ** -0.5 * IDX_DIM ** -0.5)
    iq = iq_ref[0].astype(BF16)
    iq_pairs = [
        jnp.concatenate([iq[:, (2 * p) * IDX_DIM:(2 * p + 1) * IDX_DIM],
                         iq[:, (2 * p + 1) * IDX_DIM:(2 * p + 2) * IDX_DIM]], axis=0)
        for p in range(IDX_HEADS // 2)]

    s_loc = lax.broadcasted_iota(jnp.int32, (KC, QB), 0)
    t_glob = j * QB + lax.broadcasted_iota(jnp.int32, (KC, QB), 1)
    s_loc_f = s_loc.astype(F32)

    def score_chunk(c, st):
        mn8, mx8, ge8, gt8 = st
        ik = ikw_ref[0, rows(c), :][:, :IDX_DIM].astype(BF16)
        acc = jnp.zeros((KC, QB), F32)
        for p in range(IDX_HEADS // 2):
            rel = _nt(ik, iq_pairs[p])
            acc = (acc + jnp.maximum(rel[:, :QB], 0.0) * w_t[2 * p:2 * p + 1, :]
                   + jnp.maximum(rel[:, QB:], 0.0) * w_t[2 * p + 1:2 * p + 2, :])
        adm = s_loc + c * KC <= t_glob
        blk = jnp.where(adm, acc, -jnp.inf)
        sc_ref[rows(c), :] = blk
        mn8 = jnp.minimum(mn8, fold(jnp.min, jnp.where(adm, acc, jnp.inf)))
        mx8 = jnp.maximum(mx8, fold(jnp.max, blk))
        ge8 = ge8 + fold(jnp.sum, jnp.where(blk >= 0.0, 1.0, 0.0))
        gt8 = gt8 + fold(jnp.sum, jnp.where(blk > 0.0, 1.0, 0.0))
        return mn8, mx8, ge8, gt8

    mn8, mx8, ge8, gt8 = lax.fori_loop(0, nkc, score_chunk, (infw, -infw, zerosw, zerosw))
    row_min = all8(jnp.min, mn8, jnp.minimum)
    row_max = all8(jnp.max, mx8, jnp.maximum)
    c_ge0, c_gt0 = all8(jnp.sum, ge8, jnp.add), all8(jnp.sum, gt8, jnp.add)

    def blocks(c):
        return sc_ref[rows(c), :].reshape(KC // wide, wide, QB)

    def count_ge(x8):
        xw = widen(x8)[None]

        def body(c, cw):
            return cw + jnp.sum(jnp.where(blocks(c) >= xw, 1.0, 0.0), axis=0)
        return all8(jnp.sum, lax.fori_loop(0, nkc, body, zerosw), jnp.add)

    def min_ge(x8):
        xw = widen(x8)[None]

        def body(c, mw):
            blk = blocks(c)
            return jnp.minimum(mw, jnp.min(jnp.where(blk >= xw, blk, jnp.inf), axis=0))
        return all8(jnp.min, lax.fori_loop(0, nkc, body, infw), jnp.minimum)

    def pass_gt(x8):
        xw = widen(x8)[None]

        def body(c, st):
            cw, mw = st
            blk = blocks(c)
            hit = blk > xw
            return (cw + jnp.sum(jnp.where(hit, 1.0, 0.0), axis=0),
                    jnp.minimum(mw, jnp.min(jnp.where(hit, blk, jnp.inf), axis=0)))
        cw, mw = lax.fori_loop(0, nkc, body, (zerosw, infw))
        return all8(jnp.sum, cw, jnp.add), all8(jnp.min, mw, jnp.minimum)

    n_adm = (j * QB + lax.broadcasted_iota(jnp.int32, (SUBLANES, QB), 1) + 1).astype(F32)
    at_zero = (c_gt0 < ksel) & (c_ge0 >= ksel)
    above = c_gt0 >= ksel
    lo = jnp.where(above | at_zero, 0.0, row_min)
    c_lo = jnp.where(above | at_zero, c_ge0, n_adm)
    settled = at_zero | (n_adm <= ksel)
    hi = jnp.where(settled, lo, jnp.where(above, row_max, 0.0))

    def bisect(_, st):
        lo, hi, c_lo = st
        mid = lo + (hi - lo) * 0.5
        cnt = count_ge(mid)
        up = cnt >= ksel
        return jnp.where(up, mid, lo), jnp.where(up, hi, mid), jnp.where(up, cnt, c_lo)

    lo, _, c_lo = lax.fori_loop(0, DSA_BISECT_STEPS, bisect, (lo, hi, c_lo))

    v0 = min_ge(lo)
    done0 = jnp.where(settled | (c_lo == ksel), 1.0, 0.0)
    c_gt_init = jnp.where(at_zero, c_gt0, 0.0)

    def walk_cond(st):
        return jnp.min(st[3]) < 0.5

    def walk_body(st):
        v, c_ge, c_gt, dn = st
        cnt, vnext = pass_gt(v)
        live = dn < 0.5
        fin = live & (cnt < ksel)
        step = live & (cnt >= ksel)
        c_gt = jnp.where(fin, cnt, c_gt)
        dn = jnp.where(fin, 1.0, dn)
        v = jnp.where(step, vnext, v)
        c_ge = jnp.where(step, cnt, c_ge)
        return v, c_ge, c_gt, dn

    walk = (v0, c_lo, c_gt_init, done0)
    for _ in range(DSA_WALK_UNCONDITIONAL):
        walk = walk_body(walk)
    tau8, c_ge, c_gt, _ = lax.while_loop(walk_cond, walk_body, walk)

    need = ksel - c_gt
    cut_ref[...] = jnp.full(cut_ref.shape, float(seq), F32)

    @pl.when(jnp.max(c_ge) > ksel)
    def _():
        s_grp = lax.broadcasted_iota(jnp.int32, (KC // wide, wide, QB), 0) * wide
        s_sub = lax.broadcasted_iota(jnp.int32, (KC // wide, wide, QB), 1)
        s_idx = (s_grp + s_sub).astype(F32)
        tauw = widen(tau8)[None]

        def count_ties_below(m8):
            mw = widen(m8)[None]

            def body(c, cw):
                hit = (blocks(c) == tauw) & (s_idx + (c * KC).astype(F32) < mw)
                return cw + jnp.sum(jnp.where(hit, 1.0, 0.0), axis=0)
            return all8(jnp.sum, lax.fori_loop(0, nkc, body, zerosw), jnp.add)

        def idx_bisect(_, lh):
            lo_m, hi_m = lh
            mid = jnp.floor((lo_m + hi_m) * 0.5)
            ok = count_ties_below(mid) >= need
            return jnp.where(ok, lo_m, mid), jnp.where(ok, mid, hi_m)

        _, hi_m = lax.fori_loop(0, int(math.log2(seq)) + 1, idx_bisect,
                                (zeros8, zeros8 + float(seq)))
        cut_ref[...] = jnp.where(c_ge > ksel, hi_m, float(seq))

    tau = tau8[0:1, :]
    cut = cut_ref[0:1, :]

    q = (dq_ref[0] * (HEAD_DIM ** -0.5)).astype(BF16)
    q4 = jnp.concatenate([q[:, h * HEAD_DIM:(h + 1) * HEAD_DIM] for h in range(DSA_HEADS)],
                         axis=0)
    mp_ref[...] = jnp.full(mp_ref.shape, NEG_BIG, F32)
    lp_ref[...] = jnp.zeros(lp_ref.shape, F32)

    half = seq // 2
    second_half = (j + 1) * QB > half

    def cols(c):
        return pl.ds(pl.multiple_of(c * KC, KC), KC)

    def logits_chunk(c, carry):
        blk = sc_ref[rows(c), :]
        sidx = s_loc_f + (c * KC).astype(F32)
        sel = (blk > tau) | ((blk == tau) & (sidx < cut))
        neg_t = jnp.where(sel, 0.0, NEG_BIG).T
        lg4 = _nt(q4, kb_ref[rows(c), :])
        w0 = jnp.clip(2 * c - j + 2, 0, 2)
        w1 = jnp.clip(2 * c + 1 - j + 2, 0, 2)
        for h in range(DSA_HEADS):
            hq = slice(h * QB, (h + 1) * QB)
            bias = jnp.concatenate([btile_ref[h, w0], btile_ref[h, w1]], axis=1)
            lg = lg4[hq, :] + bias + neg_t
            lg_ref[hq, cols(c)] = lg
            mp_ref[h] = jnp.maximum(mp_ref[h], jnp.maximum(lg[:, :QB], lg[:, QB:]))
        return carry

    lax.fori_loop(0, nkc, logits_chunk, 0)
    m_rows = [jnp.max(mp_ref[h], axis=-1, keepdims=True) for h in range(DSA_HEADS)]

    def probs_chunk(c, carry):
        for h in range(DSA_HEADS):
            hq = slice(h * QB, (h + 1) * QB)
            p = jnp.exp(lg_ref[hq, cols(c)] - m_rows[h])
            lp_ref[h] = lp_ref[h] + p[:, :QB] + p[:, QB:]
            p_ref[hq, cols(c)] = p.astype(BF16)
        return carry

    def zero_chunk(c, carry):
        p_ref[:, cols(c)] = jnp.zeros((DSA_HEADS * QB, KC), BF16)
        return carry

    lax.fori_loop(0, nkc, probs_chunk, 0)
    lax.fori_loop(nkc, jnp.where(second_half, seq // KC, half // KC), zero_chunk, 0)
    acc_ref[...] = _mm(p_ref[:, :half], vb_ref[:half, :])

    @pl.when(second_half)
    def _():
        acc_ref[...] = acc_ref[...] + _mm(p_ref[:, half:], vb_ref[half:, :])

    outs = []
    for h in range(DSA_HEADS):
        l = jnp.sum(lp_ref[h], axis=-1, keepdims=True)
        outs.append(acc_ref[h * QB:(h + 1) * QB, :] / l)
    o_ref[0] = jnp.concatenate(outs, axis=-1).astype(o_ref.dtype)


def _dsa(proj3, btiles):
    bsz, seq, _ = proj3.shape
    n_sel = min(DSA_TOPK_MAX, seq // 4)
    return pl.pallas_call(
        functools.partial(_dsa_kernel, n_sel=n_sel),
        grid=(bsz, seq // QB),
        in_specs=[
            pl.BlockSpec((1, QB, 256), lambda b, j: (b, j, COL_DQ // 256)),
            pl.BlockSpec((1, seq, 128), lambda b, j: (b, 0, COL_DKV // 128)),
            pl.BlockSpec((1, QB, 512), lambda b, j: (b, j, COL_IQ // 512)),
            pl.BlockSpec((1, seq, 128), lambda b, j: (b, 0, COL_IKW // 128)),
            pl.BlockSpec((DSA_HEADS, 3, QB, QB), lambda b, j: (0, 0, 0, 0)),
        ],
        out_specs=pl.BlockSpec((1, QB, 256), lambda b, j: (b, j, 0)),
        out_shape=jax.ShapeDtypeStruct((bsz, seq, DSA_HEADS * HEAD_DIM), BF16),
        scratch_shapes=[
            pltpu.VMEM((seq, QB), F32),
            pltpu.VMEM((DSA_HEADS * QB, seq), F32),
            pltpu.VMEM((DSA_HEADS * QB, seq), BF16),
            pltpu.VMEM((seq, HEAD_DIM), BF16),
            pltpu.VMEM((seq, HEAD_DIM), BF16),
            pltpu.VMEM((DSA_HEADS, QB, QB), F32),
            pltpu.VMEM((DSA_HEADS, QB, QB), F32),
            pltpu.VMEM((DSA_HEADS * QB, HEAD_DIM), F32),
            pltpu.VMEM((SUBLANES, QB), F32),
        ],
        compiler_params=pltpu.CompilerParams(
            dimension_semantics=("arbitrary", "arbitrary"), vmem_limit_bytes=VMEM_LIMIT),
        name="dsa",
    )(proj3, proj3, proj3, proj3, btiles)


def _swa_kernel(sink_ref, q_ref, kc_ref, kp_ref, vc_ref, vp_ref, bias_ref, o_ref):
    n = pl.program_id(1)
    q = (q_ref[0] * (HEAD_DIM ** -0.5)).astype(BF16)
    k2 = jnp.concatenate([kp_ref[0], kc_ref[0]], axis=0).astype(BF16)
    v2 = jnp.concatenate([vp_ref[0], vc_ref[0]], axis=0).astype(BF16)
    qi = lax.broadcasted_iota(jnp.int32, (WINDOW, 2 * WINDOW), 0)
    kj = lax.broadcasted_iota(jnp.int32, (WINDOW, 2 * WINDOW), 1)
    dist = qi + WINDOW - kj
    mask = (dist >= 0) & (dist < WINDOW) & ((kj >= WINDOW) | (n > 0))
    outs = []
    for h in range(SWA_HEADS):
        kvh = h // SWA_GROUP
        kh = k2[:, kvh * HEAD_DIM:(kvh + 1) * HEAD_DIM]
        vh = v2[:, kvh * HEAD_DIM:(kvh + 1) * HEAD_DIM]
        lg = _nt(q[:, h * HEAD_DIM:(h + 1) * HEAD_DIM], kh) + bias_ref[h]
        lg = jnp.where(mask, lg, -jnp.inf)
        sink = sink_ref[h]
        m = jnp.maximum(jnp.max(lg, axis=-1, keepdims=True), sink)
        e = jnp.exp(lg - m)
        den = jnp.sum(e, axis=-1, keepdims=True) + jnp.exp(sink - m)
        outs.append(_mm((e / den).astype(BF16), vh))
    o_ref[0] = jnp.concatenate(outs, axis=-1).astype(o_ref.dtype)


def _swa(proj3, sinks, bias_nat):
    bsz, seq, _ = proj3.shape
    return pl.pallas_call(
        _swa_kernel,
        grid=(bsz, seq // WINDOW),
        in_specs=[
            pl.BlockSpec(memory_space=pltpu.SMEM),
            pl.BlockSpec((1, WINDOW, 512), lambda b, n: (b, n, COL_SQ // 512)),
            pl.BlockSpec((1, WINDOW, 128), lambda b, n: (b, n, COL_SK // 128)),
            pl.BlockSpec((1, WINDOW, 128), lambda b, n: (b, jnp.maximum(n - 1, 0), COL_SK // 128)),
            pl.BlockSpec((1, WINDOW, 128), lambda b, n: (b, n, COL_SV // 128)),
            pl.BlockSpec((1, WINDOW, 128), lambda b, n: (b, jnp.maximum(n - 1, 0), COL_SV // 128)),
            pl.BlockSpec((SWA_HEADS, WINDOW, 2 * WINDOW), lambda b, n: (0, 0, 0)),
        ],
        out_specs=pl.BlockSpec((1, WINDOW, 512), lambda b, n: (b, n, 0)),
        out_shape=jax.ShapeDtypeStruct((bsz, seq, SWA_HEADS * HEAD_DIM), BF16),
        compiler_params=pltpu.CompilerParams(
            dimension_semantics=("arbitrary", "arbitrary"), vmem_limit_bytes=VMEM_LIMIT),
        name="swa",
    )(sinks, proj3, proj3, proj3, proj3, proj3, bias_nat)


ROUTE_OFF = N_GROUPS
ROUTE_ROWS = 48


def _split_bf16(x):
    hi = x.astype(BF16)
    lo = (x - hi.astype(F32)).astype(BF16)
    return hi, lo


def _outproj_router_kernel(h_ref, og_ref, od_ref, os_ref, wo_ref, gffn_ref, wr_hi_ref, wr_lo_ref,
                           br_ref, h1_ref, xn_ref, ri_ref, rf_ref, cnt_ref, run_ref):
    tm = h_ref.shape[0]

    @pl.when(pl.program_id(0) == 0)
    def _():
        run_ref[...] = jnp.zeros_like(run_ref)

    o = jnp.concatenate([og_ref[...], od_ref[...], os_ref[...]], axis=-1)
    h1 = h_ref[...] + _mm(o, wo_ref[...])
    h1_ref[...] = h1
    xn = _rms(h1, gffn_ref[...])
    xn_ref[...] = _pack_bf16_pairs(xn)

    x_hi, x_lo = _split_bf16(xn)
    lg = (_mm(x_hi, wr_hi_ref[...]) + _mm(x_lo, wr_hi_ref[...]) + _mm(x_hi, wr_lo_ref[...])
          + br_ref[...]).T[:ROUTE_ROWS, :]

    row = lax.broadcasted_iota(jnp.int32, lg.shape, 0)
    row_f = row.astype(F32)
    ninf = -jnp.inf

    def first_max(x):
        m = jnp.max(x, axis=0, keepdims=True)
        idx = jnp.min(jnp.where(x == m, row_f, float(ROUTE_ROWS)), axis=0, keepdims=True)
        return m, idx

    gl = jnp.where(row < N_GROUPS, lg, ninf)
    gmax, gsel = first_max(gl)
    g_w = 1.0 / jnp.sum(jnp.exp(gl - gmax), axis=0, keepdims=True)
    e_lo = ROUTE_OFF + EXPERTS_PER_GROUP * gsel
    el = jnp.where((row_f >= e_lo) & (row_f < e_lo + EXPERTS_PER_GROUP), lg, ninf)
    m1, i1 = first_max(el)
    eden = jnp.sum(jnp.exp(el - m1), axis=0, keepdims=True)
    m2, i2 = first_max(jnp.where(row_f == i1, ninf, el))
    p1 = 1.0 / eden
    p2 = jnp.exp(m2 - m1) / eden
    gate1 = g_w * p1 / (p1 + p2)
    gate2 = g_w * p2 / (p1 + p2)

    onehot = jnp.where((row_f == i1) | (row_f == i2), 1.0, 0.0)
    rr = lax.broadcasted_iota(jnp.int32, (tm, tm), 0)
    cc = lax.broadcasted_iota(jnp.int32, (tm, tm), 1)
    earlier = jnp.where(rr < cc, 1.0, 0.0).astype(BF16)
    run = run_ref[...]
    before = _mm(onehot.astype(BF16), earlier) + jnp.concatenate([run] * (tm // LANES), axis=1)
    rank1 = jnp.sum(jnp.where(row_f == i1, before, 0.0), axis=0, keepdims=True)
    rank2 = jnp.sum(jnp.where(row_f == i2, before, 0.0), axis=0, keepdims=True)
    run_ref[...] = run + jnp.sum(onehot, axis=1, keepdims=True)
    cnt_ref[...] = run_ref[...]

    out_row = lax.broadcasted_iota(jnp.int32, (SUBLANES, tm), 0)
    ints = jnp.where(out_row == 0, i1 - ROUTE_OFF,
                     jnp.where(out_row == 1, i2 - ROUTE_OFF,
                               jnp.where(out_row == 2, rank1,
                                         jnp.where(out_row == 3, rank2, 0.0))))
    ri_ref[...] = ints.astype(jnp.int32)
    rf_ref[...] = jnp.where(out_row == 0, gate1, jnp.where(out_row == 1, gate2, 0.0))


def _outproj_router(h, og, od, os_, wo, gffn, wr_hi, wr_lo, br):
    t = h.shape[0]
    tm = ROUTER_TILE
    row = lambda w: pl.BlockSpec((tm, w), lambda i: (i, 0))
    col = lambda r: pl.BlockSpec((r, tm), lambda i: (0, i))
    full = lambda a, b: pl.BlockSpec((a, b), lambda i: (0, 0))
    return pl.pallas_call(
        _outproj_router_kernel,
        grid=(t // tm,),
        in_specs=[row(D_MODEL), row(256), row(256), row(512), full(D_MODEL, D_MODEL),
                  full(1, D_MODEL), full(D_MODEL, LANES), full(D_MODEL, LANES), full(1, LANES)],
        out_specs=[row(D_MODEL), row(HALF_MODEL), col(SUBLANES), col(SUBLANES),
                   full(ROUTE_ROWS, LANES)],
        out_shape=[
            jax.ShapeDtypeStruct((t, D_MODEL), F32),
            jax.ShapeDtypeStruct((t, HALF_MODEL), jnp.uint32),
            jax.ShapeDtypeStruct((SUBLANES, t), jnp.int32),
            jax.ShapeDtypeStruct((SUBLANES, t), F32),
            jax.ShapeDtypeStruct((ROUTE_ROWS, LANES), F32),
        ],
        scratch_shapes=[pltpu.VMEM((ROUTE_ROWS, LANES), F32)],
        compiler_params=pltpu.CompilerParams(
            dimension_semantics=("arbitrary",), vmem_limit_bytes=VMEM_LIMIT),
        name="outproj_router",
    )(h, og, od, os_, wo, gffn, wr_hi, wr_lo, br)


def _dispatch_kernel(pos_ref, xn_ref, buf_in_ref, buf_ref, sem):
    del buf_in_ref
    tm = xn_ref.shape[0]

    def row_copy(r, k):
        dst = pos_ref[0, 0, 2 * r + k]
        return pltpu.make_async_copy(xn_ref.at[pl.ds(r, 1)], buf_ref.at[pl.ds(dst, 1)], sem)

    def issue(g, carry):
        for u in range(DMA_UNROLL):
            row_copy(g * DMA_UNROLL + u, 0).start()
            row_copy(g * DMA_UNROLL + u, 1).start()
        return carry

    def drain(g, carry):
        for u in range(DMA_UNROLL):
            row_copy(g * DMA_UNROLL + u, 0).wait()
            row_copy(g * DMA_UNROLL + u, 1).wait()
        return carry

    lax.fori_loop(0, tm // DMA_UNROLL, issue, 0)
    lax.fori_loop(0, tm // DMA_UNROLL, drain, 0)


def _dispatch(pos3, xn, buf0):
    t = xn.shape[0]
    tm = TOKEN_TILE
    return pl.pallas_call(
        _dispatch_kernel,
        grid=(t // tm,),
        in_specs=[
            pl.BlockSpec((1, 1, 2 * tm), lambda i: (i, 0, 0), memory_space=pltpu.SMEM),
            pl.BlockSpec((tm, HALF_MODEL), lambda i: (i, 0)),
            pl.BlockSpec(memory_space=pl.ANY),
        ],
        out_specs=pl.BlockSpec(memory_space=pl.ANY),
        out_shape=jax.ShapeDtypeStruct(buf0.shape, buf0.dtype),
        scratch_shapes=[pltpu.SemaphoreType.DMA(())],
        input_output_aliases={2: 0},
        compiler_params=pltpu.CompilerParams(
            dimension_semantics=("arbitrary",), vmem_limit_bytes=VMEM_LIMIT),
        name="dispatch",
    )(pos3, xn, buf0)


def _expert_kernel(be_ref, nu_ref, x_ref, wg_ref, wu_ref, wd_ref, y_ref, wg_s, wu_s, wd_s):
    i = pl.program_id(0)
    used = i < nu_ref[0]
    new_expert = (i == 0) | (be_ref[i] != be_ref[jnp.maximum(i - 1, 0)])

    @pl.when(used & new_expert)
    def _():
        wg_s[...] = wg_ref[0, 0].astype(BF16)
        wu_s[...] = wu_ref[0, 0].astype(BF16)
        wd_s[...] = wd_ref[0, 0].astype(BF16)

    @pl.when(used)
    def _():
        x_lo, x_hi = (part.astype(BF16) for part in _unpack_bf16_pairs(x_ref[...]))
        g = _mm(x_lo, wg_s[:HALF_MODEL, :]) + _mm(x_hi, wg_s[HALF_MODEL:, :])
        u = _mm(x_lo, wu_s[:HALF_MODEL, :]) + _mm(x_hi, wu_s[HALF_MODEL:, :])
        hmid = (g * jax.nn.sigmoid(g)) * u
        y_ref[...] = _pack_bf16_pairs(_mm(hmid.astype(BF16), wd_s[...]))

    @pl.when(i >= nu_ref[0])
    def _():
        y_ref[...] = jnp.zeros_like(y_ref)


def _experts(blk_expert, n_used, buf, wg, wu, wd, layer):
    nrows = buf.shape[0]
    nblk = nrows // EXPERT_ROWS
    w_index = lambda i, be, nu: (layer, be[i], 0, 0)
    return pl.pallas_call(
        _expert_kernel,
        grid_spec=pltpu.PrefetchScalarGridSpec(
            num_scalar_prefetch=2,
            grid=(nblk,),
            in_specs=[
                pl.BlockSpec((EXPERT_ROWS, HALF_MODEL), lambda i, be, nu: (i, 0)),
                pl.BlockSpec((1, 1, D_MODEL, D_EXPERT), w_index),
                pl.BlockSpec((1, 1, D_MODEL, D_EXPERT), w_index),
                pl.BlockSpec((1, 1, D_EXPERT, D_MODEL), w_index),
            ],
            out_specs=pl.BlockSpec((EXPERT_ROWS, HALF_MODEL), lambda i, be, nu: (i, 0)),
            scratch_shapes=[
                pltpu.VMEM((D_MODEL, D_EXPERT), BF16),
                pltpu.VMEM((D_MODEL, D_EXPERT), BF16),
                pltpu.VMEM((D_EXPERT, D_MODEL), BF16),
            ],
        ),
        out_shape=jax.ShapeDtypeStruct((nrows, HALF_MODEL), jnp.uint32),
        compiler_params=pltpu.CompilerParams(
            dimension_semantics=("arbitrary",), vmem_limit_bytes=VMEM_LIMIT),
        name="experts",
    )(blk_expert, n_used, buf, wg, wu, wd)


def _combine_ple_kernel(pos_ref, h1_ref, p_ref, rf_ref, yb_ref, wple_ref, gple_ref, wpg_ref,
                        gfin_ref, o_ref, ybuf, sem, *, final_norm):
    tm = h1_ref.shape[0]

    def row_copy(r, k):
        src = pos_ref[0, 0, 2 * r + k]
        return pltpu.make_async_copy(yb_ref.at[pl.ds(src, 1)], ybuf.at[k, pl.ds(r, 1)], sem)

    def issue(g, carry):
        for u in range(DMA_UNROLL):
            row_copy(g * DMA_UNROLL + u, 0).start()
            row_copy(g * DMA_UNROLL + u, 1).start()
        return carry

    def drain(g, carry):
        for u in range(DMA_UNROLL):
            row_copy(g * DMA_UNROLL + u, 0).wait()
            row_copy(g * DMA_UNROLL + u, 1).wait()
        return carry

    lax.fori_loop(0, tm // DMA_UNROLL, issue, 0)
    e = _rms(_mm(p_ref[...].astype(BF16), wple_ref[...]), gple_ref[...])
    lax.fori_loop(0, tm // DMA_UNROLL, drain, 0)

    rf = rf_ref[...]
    y0_lo, y0_hi = _unpack_bf16_pairs(ybuf[0])
    y1_lo, y1_hi = _unpack_bf16_pairs(ybuf[1])
    g0, g1 = rf[:, 0:1], rf[:, 1:2]
    moe = jnp.concatenate([y0_lo * g0 + y1_lo * g1, y0_hi * g0 + y1_hi * g1], axis=-1)
    h2 = h1_ref[...] + moe
    h3 = h2 + e * jax.nn.sigmoid(_mm(h2.astype(BF16), wpg_ref[...]))
    if final_norm:
        h3 = _rms(h3, gfin_ref[...])
    o_ref[...] = h3


def _combine_ple(pos3, h1, p, rf, yb, wple, gple, wpg, gfin, layer, final_norm):
    t = h1.shape[0]
    tm = TOKEN_TILE
    row = lambda w: pl.BlockSpec((tm, w), lambda i: (i, 0))
    full = lambda a, b: pl.BlockSpec((a, b), lambda i: (0, 0))
    return pl.pallas_call(
        functools.partial(_combine_ple_kernel, final_norm=final_norm),
        grid=(t // tm,),
        in_specs=[
            pl.BlockSpec((1, 1, 2 * tm), lambda i: (i, 0, 0), memory_space=pltpu.SMEM),
            row(D_MODEL),
            pl.BlockSpec((tm, PLE_DIM), lambda i: (layer * (t // tm) + i, 0)),
            row(LANES),
            pl.BlockSpec(memory_space=pl.ANY),
            full(PLE_DIM, D_MODEL), full(1, D_MODEL), full(D_MODEL, D_MODEL), full(1, D_MODEL),
        ],
        out_specs=row(D_MODEL),
        out_shape=jax.ShapeDtypeStruct((t, D_MODEL), F32),
        scratch_shapes=[pltpu.VMEM((2, tm, HALF_MODEL), jnp.uint32), pltpu.SemaphoreType.DMA(())],
        compiler_params=pltpu.CompilerParams(
            dimension_semantics=("arbitrary",), vmem_limit_bytes=VMEM_LIMIT),
        name="combine_ple",
    )(pos3, h1, p, rf, yb, wple, gple, wpg, gfin)


def _rel_bucket(dist):
    n = jnp.maximum(dist, 0)
    max_exact = REL_BUCKETS // 2
    nf = jnp.maximum(n, 1).astype(F32)
    large = max_exact + (jnp.log(nf / max_exact) / math.log(REL_MAX_DIST / max_exact)
                         * (REL_BUCKETS - max_exact)).astype(jnp.int32)
    large = jnp.minimum(large, REL_BUCKETS - 1)
    return jnp.where(n < max_exact, n, large)


def _bias_tiles(rel_bias):
    qi = jnp.arange(WINDOW)[:, None]
    kj = jnp.arange(2 * WINDOW)[None, :]
    bucket = _rel_bucket(qi + WINDOW - kj)
    onehot = (bucket[..., None] == jnp.arange(REL_BUCKETS)).astype(F32)
    nat = jnp.einsum('qkb,bh->hqk', onehot, rel_bias, precision=HIGHEST)
    dsa = nat[:DSA_HEADS]
    far = jnp.broadcast_to(rel_bias[REL_BUCKETS - 1, :DSA_HEADS][:, None, None],
                           (DSA_HEADS, WINDOW, WINDOW))
    return jnp.stack([far, dsa[:, :, :WINDOW], dsa[:, :, WINDOW:]], axis=1), nat[DSA_HEADS:]


def _pack_w_in(w):
    sizes = (128, 128, 256, 256, 16, 256, 64, 64, 512, 64, 8, 512, 128, 128)
    offs = [0]
    for s in sizes:
        offs.append(offs[-1] + s)
    wt = w.T
    gq, gk, gv, gg, ga, dq, dk, dv, iq, ik, iw, sq, sk, sv = (
        wt[offs[n]:offs[n + 1]] for n in range(len(sizes)))
    z = lambda n: jnp.zeros((n, w.shape[0]), w.dtype)
    packed = jnp.concatenate(
        [iq, sq, gv, gg, dq, gq, gk, ga, z(128 - GLA_RANK), dk, dv, ik, iw,
         z(128 - IDX_DIM - IDX_HEADS), sk, sv], axis=0)
    return packed.astype(BF16)


def kernel(x, p, rel_bias, g_mix, w_in, gla_w_alpha, gla_b_alpha, gla_g_norm, swa_sinks, w_out,
           g_ffn, w_router_group, b_router_group, w_router_expert, b_router_expert, w_expert_gate,
           w_expert_up, w_expert_down, w_ple, g_ple, w_ple_gate, g_final):
    bsz, seq, d = x.shape
    depth = w_in.shape[0]
    t = bsz * seq
    assert d == D_MODEL and t % INPROJ_TILE == 0 and seq % GLA_BLOCK == 0 and seq % KC == 0
    n_blocks = -(-(2 * t) // EXPERT_ROWS) + N_EXPERTS
    dsa_tiles, swa_bias = _bias_tiles(rel_bias)

    h = x.reshape(t, d)
    for i in range(depth):
        proj = _inproj(h, g_mix[i][None, :], _pack_w_in(w_in[i]))
        proj3 = proj.reshape(bsz, seq, D_PROJ)

        wal = jnp.zeros((128, 128), F32).at[:GLA_RANK].set(gla_w_alpha[i])
        og = _gla(proj3, wal, gla_b_alpha[i][None, :],
                  jnp.tile(gla_g_norm[i], GLA_HEADS)[None, :])
        od = _dsa(proj3, dsa_tiles)
        os_ = _swa(proj3, swa_sinks[i], swa_bias)

        w_r = jnp.concatenate([w_router_group[i], w_router_expert[i]], axis=1)
        w_r = jnp.pad(w_r, ((0, 0), (0, LANES - w_r.shape[1])))
        wr_hi, wr_lo = _split_bf16(w_r)
        b_r = jnp.pad(jnp.concatenate([b_router_group[i], b_router_expert[i]]),
                      (0, LANES - N_GROUPS - N_EXPERTS))[None, :]
        h1, xn, ri, rf, cnt = _outproj_router(
            h, og.reshape(t, -1), od.reshape(t, -1), os_.reshape(t, -1), w_out[i].astype(BF16),
            g_ffn[i][None, :], wr_hi, wr_lo, b_r)

        counts = cnt[ROUTE_OFF:ROUTE_OFF + N_EXPERTS, 0].astype(jnp.int32)
        padded = (counts + EXPERT_ROWS - 1) // EXPERT_ROWS * EXPERT_ROWS
        pad_end = jnp.cumsum(padded)
        pad_start = pad_end - padded
        expert_ids = jnp.arange(N_EXPERTS, dtype=jnp.int32)
        slot_start = jnp.sum(
            jnp.where(ri[0:2, :, None] == expert_ids, pad_start, 0), axis=-1)
        pos3 = (slot_start + ri[2:4]).T.reshape(t // TOKEN_TILE, 1, 2 * TOKEN_TILE)
        gates = jnp.pad(rf[0:2].T, ((0, 0), (0, LANES - 2)))
        blk_start = jnp.arange(n_blocks, dtype=jnp.int32) * EXPERT_ROWS
        blk_expert = jnp.minimum(
            jnp.sum((pad_end[None, :] <= blk_start[:, None]).astype(jnp.int32), axis=1),
            N_EXPERTS - 1)
        n_used = (pad_end[-1:] // EXPERT_ROWS).astype(jnp.int32)

        buf = _dispatch(pos3, xn, jnp.zeros((n_blocks * EXPERT_ROWS, HALF_MODEL), jnp.uint32))
        yb = _experts(blk_expert, n_used, buf, w_expert_gate, w_expert_up, w_expert_down, i)
        h = _combine_ple(pos3, h1, p.reshape(depth * t, PLE_DIM), gates, yb, w_ple[i].astype(BF16),
                         g_ple[i][None, :], w_ple_gate[i].astype(BF16), g_final[None, :],
                         layer=i, final_norm=(i == depth - 1))
    return h.reshape(bsz, seq, d)
```

```python
import functools
import math

import jax
import jax.numpy as jnp
from jax import lax
from jax.experimental import pallas as pl
from jax.experimental.pallas import tpu as pltpu

F32 = jnp.float32
BF16 = jnp.bfloat16
HIGHEST = lax.Precision.HIGHEST

D_MODEL = 1024
HEAD_DIM = 64
GLA_HEADS = 4
GLA_DK = 32
GLA_DV = 64
GLA_RANK = 16
GLA_TAU = 16.0
GLA_CHUNK = 64
DSA_HEADS = 4
IDX_HEADS = 8
IDX_DIM = 64
DSA_TOPK_MAX = 256
SWA_HEADS = 8
SWA_KV_HEADS = 2
SWA_GROUP = SWA_HEADS // SWA_KV_HEADS
WINDOW = 128
REL_BUCKETS = 32
REL_MAX_DIST = 128
N_GROUPS = 4
EXPERTS_PER_GROUP = 8
N_EXPERTS = N_GROUPS * EXPERTS_PER_GROUP
D_EXPERT = 512
PLE_DIM = 256
EPS = 1e-6

LANES = 128
SUBLANES = 8
PACK16 = 16

COL_IQ, COL_SQ = 0, 512
COL_GV, COL_GG, COL_DQ = 1024, 1280, 1536
COL_GQ, COL_GK, COL_GA, COL_DKV, COL_IKW, COL_SK, COL_SV = 1792, 1920, 2048, 2176, 2304, 2432, 2560
D_PROJ = 2688

TOKEN_TILE = 256
INPROJ_TILE = 512
ROUTER_TILE = 512
EXPERT_ROWS = 512
GLA_BLOCK = 256
QB = 128
KC = 256
DMA_UNROLL = 8
NEG_BIG = -1e30
VMEM_LIMIT = 48 * 1024 * 1024


def _nt(a, b, precision=None):
    return lax.dot_general(a, b, (((1,), (1,)), ((), ())), precision=precision,
                           preferred_element_type=F32)


def _mm(a, b, precision=None):
    return jnp.dot(a, b, precision=precision, preferred_element_type=F32)


def _eye(n, dtype):
    r = lax.broadcasted_iota(jnp.int32, (n, n), 0)
    c = lax.broadcasted_iota(jnp.int32, (n, n), 1)
    return jnp.where(r == c, 1.0, 0.0).astype(dtype)


def _rms(x, g):
    return x * lax.rsqrt(jnp.mean(x * x, axis=-1, keepdims=True) + EPS) * g


HALF_MODEL = D_MODEL // 2
HIGH16 = 0xFFFF0000


def _pack_bf16_pairs(x):
    bits = pltpu.bitcast(x.astype(BF16).astype(F32), jnp.uint32)
    return (bits[:, HALF_MODEL:] & jnp.uint32(HIGH16)) | (bits[:, :HALF_MODEL] >> 16)


def _unpack_bf16_pairs(w):
    return pltpu.bitcast(w << 16, F32), pltpu.bitcast(w & jnp.uint32(HIGH16), F32)


def _inproj_kernel(h_ref, g_ref, w_ref, o_ref):
    a = _rms(h_ref[...], g_ref[...])
    o_ref[...] = _nt(a.astype(BF16), w_ref[...])


def _inproj(h, g, w):
    t = h.shape[0]
    return pl.pallas_call(
        _inproj_kernel,
        grid=(t // INPROJ_TILE,),
        in_specs=[
            pl.BlockSpec((INPROJ_TILE, D_MODEL), lambda i: (i, 0)),
            pl.BlockSpec((1, D_MODEL), lambda i: (0, 0)),
            pl.BlockSpec((D_PROJ, D_MODEL), lambda i: (0, 0)),
        ],
        out_specs=pl.BlockSpec((INPROJ_TILE, D_PROJ), lambda i: (i, 0)),
        out_shape=jax.ShapeDtypeStruct((t, D_PROJ), F32),
        compiler_params=pltpu.CompilerParams(
            dimension_semantics=("arbitrary",), vmem_limit_bytes=VMEM_LIMIT),
        name="inproj",
    )(h, g, w)


def _gla_kernel(q_ref, k_ref, v_ref, gg_ref, ga_ref, wal_ref, bal_ref, gn_ref, o_ref,
                state_ref, sc_all_ref, p_all_ref, b_ref):
    hk = GLA_HEADS * GLA_DK
    hv = GLA_HEADS * GLA_DV
    c = GLA_CHUNK
    half_c = c // 2
    assert 2 * c == LANES

    @pl.when(pl.program_id(1) == 0)
    def _():
        state_ref[...] = jnp.zeros_like(state_ref)

    @pl.when((pl.program_id(0) == 0) & (pl.program_id(1) == 0))
    def _():
        p_all_ref[...] = jnp.zeros_like(p_all_ref)

    pj = lax.broadcasted_iota(jnp.int32, (half_c, LANES), 0)
    pl_ = lax.broadcasted_iota(jnp.int32, (half_c, LANES), 1)
    pair_causal = jnp.where(pl_ < c, pl_, pl_ - c) <= jnp.where(pl_ < c, pj, pj + half_c)
    rv = lax.broadcasted_iota(jnp.int32, (hv, hk), 0) // GLA_DV
    ck = lax.broadcasted_iota(jnp.int32, (hv, hk), 1) // GLA_DK
    blockdiag_t = jnp.where(rv == ck, 1.0, 0.0).astype(F32)
    hr = lax.broadcasted_iota(jnp.int32, (SUBLANES, hk), 0)
    hl = lax.broadcasted_iota(jnp.int32, (SUBLANES, hk), 1) // GLA_DK
    head_rows = jnp.where(hr == hl, 1.0, 0.0).astype(BF16)
    eye_v = _eye(hv, BF16)

    ga_hi, ga_lo = _split_bf16(ga_ref[0])
    w_hi, w_lo = _split_bf16(wal_ref[...])
    z = _mm(ga_hi, w_hi) + _mm(ga_hi, w_lo) + _mm(ga_lo, w_hi) + bal_ref[...]
    log_a = (jnp.minimum(z, 0.0) - jnp.log1p(jnp.exp(-jnp.abs(z)))) * (1.0 / GLA_TAU)
    la_hi = log_a.astype(BF16)
    la_r1 = log_a - la_hi.astype(F32)
    la_mid = la_r1.astype(BF16)
    la_lo = (la_r1 - la_mid.astype(F32)).astype(BF16)
    rb = lax.broadcasted_iota(jnp.int32, (GLA_BLOCK, GLA_BLOCK), 0)
    cb = lax.broadcasted_iota(jnp.int32, (GLA_BLOCK, GLA_BLOCK), 1)
    tril = jnp.where((rb >= cb) & (rb // c == cb // c), 1.0, 0.0).astype(BF16)
    b_ref[...] = _mm(tril, la_hi) + _mm(tril, la_mid) + _mm(tril, la_lo)

    def chunk(ci):
        rows = slice(ci * c, (ci + 1) * c)
        p_ref, sc_ref = p_all_ref.at[ci], sc_all_ref.at[ci]
        q = q_ref[0, rows, :] * (GLA_DK ** -0.5)
        k = k_ref[0, rows, :]
        v = v_ref[0, rows, :]
        b = b_ref[rows, :]
        state_t = state_ref[...]
        o_inter = _nt((q * jnp.exp(b)).astype(BF16), state_t.astype(BF16))

        for t in range(c):
            ns = PACK16 * (t // PACK16 + 1)
            slot = 2 * (t % half_c) + t // half_c
            p = jnp.exp(b[t:t + 1, :] - b[:ns, :]) * k[:ns, :] * q[t:t + 1, :]
            p_ref[slot * c:slot * c + ns, :] = p.astype(BF16)
        rows_hs = _nt(head_rows, p_ref[...])
        for jj in range(half_c):
            for h in range(GLA_HEADS):
                sc_ref[h, jj:jj + 1, :] = rows_hs[h:h + 1, jj * LANES:(jj + 1) * LANES]
        vb = v.astype(BF16)
        zero_v = jnp.zeros((c, GLA_DV), BF16)
        o_heads = []
        for h in range(GLA_HEADS):
            vh = vb[:, h * GLA_DV:(h + 1) * GLA_DV]
            v_pair = jnp.concatenate([jnp.concatenate([vh, zero_v], axis=1),
                                      jnp.concatenate([zero_v, vh], axis=1)], axis=0)
            scores = jnp.where(pair_causal, sc_ref[h], 0.0).astype(BF16)
            o_pair = _mm(scores, v_pair)
            o_heads.append(jnp.concatenate([o_pair[:, :GLA_DV], o_pair[:, GLA_DV:]], axis=0))
        o = o_inter + jnp.concatenate(o_heads, axis=-1)

        b_last = b[c - 1:c, :]
        kd = (k * jnp.exp(b_last - b)).astype(BF16)
        v_t = _nt(eye_v, vb).astype(BF16)
        upd_t = _mm(v_t, kd)
        state_ref[...] = jnp.exp(b_last) * state_t + upd_t * blockdiag_t

        gg = gg_ref[0, rows, :]
        outs = []
        for h in range(GLA_HEADS):
            oh = o[:, h * GLA_DV:(h + 1) * GLA_DV]
            ms = jnp.mean(oh * oh, axis=-1, keepdims=True)
            outs.append(oh * lax.rsqrt(ms + EPS))
        on = jnp.concatenate(outs, axis=-1) * gn_ref[...]
        o_ref[0, rows, :] = (on * (gg * jax.nn.sigmoid(gg))).astype(o_ref.dtype)

    for ci in range(GLA_BLOCK // c):
        chunk(ci)


def _gla(proj3, wal, bal, gn):
    bsz, seq, _ = proj3.shape
    nb = seq // GLA_BLOCK

    def col(width, off):
        return pl.BlockSpec((1, GLA_BLOCK, width), lambda b, i: (b, i, off // width))

    return pl.pallas_call(
        _gla_kernel,
        grid=(bsz, nb),
        in_specs=[
            col(128, COL_GQ), col(128, COL_GK), col(256, COL_GV), col(256, COL_GG), col(128, COL_GA),
            pl.BlockSpec((128, 128), lambda b, i: (0, 0)),
            pl.BlockSpec((1, 128), lambda b, i: (0, 0)),
            pl.BlockSpec((1, 256), lambda b, i: (0, 0)),
        ],
        out_specs=pl.BlockSpec((1, GLA_BLOCK, 256), lambda b, i: (b, i, 0)),
        out_shape=jax.ShapeDtypeStruct((bsz, seq, GLA_HEADS * GLA_DV), BF16),
        scratch_shapes=[
            pltpu.VMEM((GLA_HEADS * GLA_DV, GLA_HEADS * GLA_DK), F32),
            pltpu.VMEM((GLA_BLOCK // GLA_CHUNK, GLA_HEADS, GLA_CHUNK // 2, 2 * GLA_CHUNK), F32),
            pltpu.VMEM((GLA_BLOCK // GLA_CHUNK, GLA_CHUNK * GLA_CHUNK, GLA_HEADS * GLA_DK), BF16),
            pltpu.VMEM((GLA_BLOCK, GLA_HEADS * GLA_DK), F32),
        ],
        compiler_params=pltpu.CompilerParams(
            dimension_semantics=("arbitrary", "arbitrary"), vmem_limit_bytes=VMEM_LIMIT),
        name="gla",
    )(proj3, proj3, proj3, proj3, proj3, wal, bal, gn)


DSA_BISECT_STEPS = 12
DSA_ACCUMULATORS = 4
DSA_WALK_UNCONDITIONAL = 2


def _dsa_kernel(dq_ref, kv_ref, iq_ref, ikw_ref, btile_ref, o_ref,
                sc_ref, lg_ref, p_ref, kb_ref, vb_ref, mp_ref, lp_ref, acc_ref, cut_ref, *, n_sel):
    j = pl.program_id(1)
    nkc = (j + 2) // 2
    seq = kv_ref.shape[1]
    ksel = float(n_sel)

    @pl.when(j == 0)
    def _():
        kv = kv_ref[0]
        kb_ref[...] = kv[:, :HEAD_DIM].astype(BF16)
        vb_ref[...] = kv[:, HEAD_DIM:].astype(BF16)

    def rows(c):
        return pl.ds(pl.multiple_of(c * KC, KC), KC)

    wide = DSA_ACCUMULATORS * SUBLANES

    def fold(op, x):
        return op(x.reshape(KC // wide, wide, QB), axis=0)

    def all8(op, xw, roll_op):
        x8 = op(xw.reshape(DSA_ACCUMULATORS, SUBLANES, QB), axis=0)
        for shift in (4, 2, 1):
            x8 = roll_op(x8, pltpu.roll(x8, shift, 0))
        return x8

    def widen(x8):
        return jnp.concatenate([x8] * DSA_ACCUMULATORS, axis=0)

    zeros8 = jnp.zeros((SUBLANES, QB), F32)
    zerosw = jnp.zeros((wide, QB), F32)
    infw = jnp.full((wide, QB), jnp.inf, F32)

    ikw_q = ikw_ref[0, pl.ds(pl.multiple_of(j * QB, QB), QB), :]
    sel_r = lax.broadcasted_iota(jnp.int32, (SUBLANES, LANES), 0)
    sel_c = lax.broadcasted_iota(jnp.int32, (SUBLANES, LANES), 1)
    pick_w = jnp.where(sel_c == sel_r + IDX_DIM, 1.0, 0.0).astype(F32)
    w_t = _nt(pick_w, ikw_q, HIGHEST) * (IDX_HEADS ** -0.5 * IDX_DIM ** -0.5)
    iq = iq_ref[0].astype(BF16)
    iq_pairs = [
        jnp.concatenate([iq[:, (2 * p) * IDX_DIM:(2 * p + 1) * IDX_DIM],
                         iq[:, (2 * p + 1) * IDX_DIM:(2 * p + 2) * IDX_DIM]], axis=0)
        for p in range(IDX_HEADS // 2)]

    s_loc = lax.broadcasted_iota(jnp.int32, (KC, QB), 0)
    t_glob = j * QB + lax.broadcasted_iota(jnp.int32, (KC, QB), 1)
    s_loc_f = s_loc.astype(F32)

    def score_chunk(c, st):
        mn8, mx8, ge8, gt8 = st
        ik = ikw_ref[0, rows(c), :][:, :IDX_DIM].astype(BF16)
        acc = jnp.zeros((KC, QB), F32)
        for p in range(IDX_HEADS // 2):
            rel = _nt(ik, iq_pairs[p])
            acc = (acc + jnp.maximum(rel[:, :QB], 0.0) * w_t[2 * p:2 * p + 1, :]
                   + jnp.maximum(rel[:, QB:], 0.0) * w_t[2 * p + 1:2 * p + 2, :])
        adm = s_loc + c * KC <= t_glob
        blk = jnp.where(adm, acc, -jnp.inf)
        sc_ref[rows(c), :] = blk
        mn8 = jnp.minimum(mn8, fold(jnp.min, jnp.where(adm, acc, jnp.inf)))
        mx8 = jnp.maximum(mx8, fold(jnp.max, blk))
        ge8 = ge8 + fold(jnp.sum, jnp.where(blk >= 0.0, 1.0, 0.0))
        gt8 = gt8 + fold(jnp.sum, jnp.where(blk > 0.0, 1.0, 0.0))
        return mn8, mx8, ge8, gt8

    mn8, mx8, ge8, gt8 = lax.fori_loop(0, nkc, score_chunk, (infw, -infw, zerosw, zerosw))
    row_min = all8(jnp.min, mn8, jnp.minimum)
    row_max = all8(jnp.max, mx8, jnp.maximum)
    c_ge0, c_gt0 = all8(jnp.sum, ge8, jnp.add), all8(jnp.sum, gt8, jnp.add)

    def blocks(c):
        return sc_ref[rows(c), :].reshape(KC // wide, wide, QB)

    def count_ge(x8):
        xw = widen(x8)[None]

        def body(c, cw):
            return cw + jnp.sum(jnp.where(blocks(c) >= xw, 1.0, 0.0), axis=0)
        return all8(jnp.sum, lax.fori_loop(0, nkc, body, zerosw), jnp.add)

    def min_ge(x8):
        xw = widen(x8)[None]

        def body(c, mw):
            blk = blocks(c)
            return jnp.minimum(mw, jnp.min(jnp.where(blk >= xw, blk, jnp.inf), axis=0))
        return all8(jnp.min, lax.fori_loop(0, nkc, body, infw), jnp.minimum)

    def pass_gt(x8):
        xw = widen(x8)[None]

        def body(c, st):
            cw, mw = st
            blk = blocks(c)
            hit = blk > xw
            return (cw + jnp.sum(jnp.where(hit, 1.0, 0.0), axis=0),
                    jnp.minimum(mw, jnp.min(jnp.where(hit, blk, jnp.inf), axis=0)))
        cw, mw = lax.fori_loop(0, nkc, body, (zerosw, infw))
        return all8(jnp.sum, cw, jnp.add), all8(jnp.min, mw, jnp.minimum)

    n_adm = (j * QB + lax.broadcasted_iota(jnp.int32, (SUBLANES, QB), 1) + 1).astype(F32)
    at_zero = (c_gt0 < ksel) & (c_ge0 >= ksel)
    above = c_gt0 >= ksel
    lo = jnp.where(above | at_zero, 0.0, row_min)
    c_lo = jnp.where(above | at_zero, c_ge0, n_adm)
    settled = at_zero | (n_adm <= ksel)
    hi = jnp.where(settled, lo, jnp.where(above, row_max, 0.0))

    def bisect(_, st):
        lo, hi, c_lo = st
        mid = lo + (hi - lo) * 0.5
        cnt = count_ge(mid)
        up = cnt >= ksel
        return jnp.where(up, mid, lo), jnp.where(up, hi, mid), jnp.where(up, cnt, c_lo)

    lo, _, c_lo = lax.fori_loop(0, DSA_BISECT_STEPS, bisect, (lo, hi, c_lo))

    v0 = min_ge(lo)
    done0 = jnp.where(settled | (c_lo == ksel), 1.0, 0.0)
    c_gt_init = jnp.where(at_zero, c_gt0, 0.0)

    def walk_cond(st):
        return jnp.min(st[3]) < 0.5

    def walk_body(st):
        v, c_ge, c_gt, dn = st
        cnt, vnext = pass_gt(v)
        live = dn < 0.5
        fin = live & (cnt < ksel)
        step = live & (cnt >= ksel)
        c_gt = jnp.where(fin, cnt, c_gt)
        dn = jnp.where(fin, 1.0, dn)
        v = jnp.where(step, vnext, v)
        c_ge = jnp.where(step, cnt, c_ge)
        return v, c_ge, c_gt, dn

    walk = (v0, c_lo, c_gt_init, done0)
    for _ in range(DSA_WALK_UNCONDITIONAL):
        walk = walk_body(walk)
    tau8, c_ge, c_gt, _ = lax.while_loop(walk_cond, walk_body, walk)

    need = ksel - c_gt
    cut_ref[...] = jnp.full(cut_ref.shape, float(seq), F32)

    @pl.when(jnp.max(c_ge) > ksel)
    def _():
        s_grp = lax.broadcasted_iota(jnp.int32, (KC // wide, wide, QB), 0) * wide
        s_sub = lax.broadcasted_iota(jnp.int32, (KC // wide, wide, QB), 1)
        s_idx = (s_grp + s_sub).astype(F32)
        tauw = widen(tau8)[None]

        def count_ties_below(m8):
            mw = widen(m8)[None]

            def body(c, cw):
                hit = (blocks(c) == tauw) & (s_idx + (c * KC).astype(F32) < mw)
                return cw + jnp.sum(jnp.where(hit, 1.0, 0.0), axis=0)
            return all8(jnp.sum, lax.fori_loop(0, nkc, body, zerosw), jnp.add)

        def idx_bisect(_, lh):
            lo_m, hi_m = lh
            mid = jnp.floor((lo_m + hi_m) * 0.5)
            ok = count_ties_below(mid) >= need
            return jnp.where(ok, lo_m, mid), jnp.where(ok, mid, hi_m)

        _, hi_m = lax.fori_loop(0, int(math.log2(seq)) + 1, idx_bisect,
                                (zeros8, zeros8 + float(seq)))
        cut_ref[...] = jnp.where(c_ge > ksel, hi_m, float(seq))

    tau = tau8[0:1, :]
    cut = cut_ref[0:1, :]

    q = (dq_ref[0] * (HEAD_DIM ** -0.5)).astype(BF16)
    q4 = jnp.concatenate([q[:, h * HEAD_DIM:(h + 1) * HEAD_DIM] for h in range(DSA_HEADS)],
                         axis=0)
    mp_ref[...] = jnp.full(mp_ref.shape, NEG_BIG, F32)
    lp_ref[...] = jnp.zeros(lp_ref.shape, F32)

    half = seq // 2
    second_half = (j + 1) * QB > half

    def cols(c):
        return pl.ds(pl.multiple_of(c * KC, KC), KC)

    def logits_chunk(c, carry):
        blk = sc_ref[rows(c), :]
        sidx = s_loc_f + (c * KC).astype(F32)
        sel = (blk > tau) | ((blk == tau) & (sidx < cut))
        neg_t = jnp.where(sel, 0.0, NEG_BIG).T
        lg4 = _nt(q4, kb_ref[rows(c), :])
        w0 = jnp.clip(2 * c - j + 2, 0, 2)
        w1 = jnp.clip(2 * c + 1 - j + 2, 0, 2)
        for h in range(DSA_HEADS):
            hq = slice(h * QB, (h + 1) * QB)
            bias = jnp.concatenate([btile_ref[h, w0], btile_ref[h, w1]], axis=1)
            lg = lg4[hq, :] + bias + neg_t
            lg_ref[hq, cols(c)] = lg
            mp_ref[h] = jnp.maximum(mp_ref[h], jnp.maximum(lg[:, :QB], lg[:, QB:]))
        return carry

    lax.fori_loop(0, nkc, logits_chunk, 0)
    m_rows = [jnp.max(mp_ref[h], axis=-1, keepdims=True) for h in range(DSA_HEADS)]

    def probs_chunk(c, carry):
        for h in range(DSA_HEADS):
            hq = slice(h * QB, (h + 1) * QB)
            p = jnp.exp(lg_ref[hq, cols(c)] - m_rows[h])
            lp_ref[h] = lp_ref[h] + p[:, :QB] + p[:, QB:]
            p_ref[hq, cols(c)] = p.astype(BF16)
        return carry

    def zero_chunk(c, carry):
        p_ref[:, cols(c)] = jnp.zeros((DSA_HEADS * QB, KC), BF16)
        return carry

    lax.fori_loop(0, nkc, probs_chunk, 0)
    lax.fori_loop(nkc, jnp.where(second_half, seq // KC, half // KC), zero_chunk, 0)
    acc_ref[...] = _mm(p_ref[:, :half], vb_ref[:half, :])

    @pl.when(second_half)
    def _():
        acc_ref[...] = acc_ref[...] + _mm(p_ref[:, half:], vb_ref[half:, :])

    outs = []
    for h in range(DSA_HEADS):
        l = jnp.sum(lp_ref[h], axis=-1, keepdims=True)
        outs.append(acc_ref[h * QB:(h + 1) * QB, :] / l)
    o_ref[0] = jnp.concatenate(outs, axis=-1).astype(o_ref.dtype)


def _dsa(proj3, btiles):
    bsz, seq, _ = proj3.shape
    n_sel = min(DSA_TOPK_MAX, seq // 4)
    return pl.pallas_call(
        functools.partial(_dsa_kernel, n_sel=n_sel),
        grid=(bsz, seq // QB),
        in_specs=[
            pl.BlockSpec((1, QB, 256), lambda b, j: (b, j, COL_DQ // 256)),
            pl.BlockSpec((1, seq, 128), lambda b, j: (b, 0, COL_DKV // 128)),
            pl.BlockSpec((1, QB, 512), lambda b, j: (b, j, COL_IQ // 512)),
            pl.BlockSpec((1, seq, 128), lambda b, j: (b, 0, COL_IKW // 128)),
            pl.BlockSpec((DSA_HEADS, 3, QB, QB), lambda b, j: (0, 0, 0, 0)),
        ],
        out_specs=pl.BlockSpec((1, QB, 256), lambda b, j: (b, j, 0)),
        out_shape=jax.ShapeDtypeStruct((bsz, seq, DSA_HEADS * HEAD_DIM), BF16),
        scratch_shapes=[
            pltpu.VMEM((seq, QB), F32),
            pltpu.VMEM((DSA_HEADS * QB, seq), F32),
            pltpu.VMEM((DSA_HEADS * QB, seq), BF16),
            pltpu.VMEM((seq, HEAD_DIM), BF16),
            pltpu.VMEM((seq, HEAD_DIM), BF16),
            pltpu.VMEM((DSA_HEADS, QB, QB), F32),
            pltpu.VMEM((DSA_HEADS, QB, QB), F32),
            pltpu.VMEM((DSA_HEADS * QB, HEAD_DIM), F32),
            pltpu.VMEM((SUBLANES, QB), F32),
        ],
        compiler_params=pltpu.CompilerParams(
            dimension_semantics=("arbitrary", "arbitrary"), vmem_limit_bytes=VMEM_LIMIT),
        name="dsa",
    )(proj3, proj3, proj3, proj3, btiles)


def _swa_kernel(sink_ref, q_ref, kc_ref, kp_ref, vc_ref, vp_ref, bias_ref, o_ref):
    n = pl.program_id(1)
    q = (q_ref[0] * (HEAD_DIM ** -0.5)).astype(BF16)
    k2 = jnp.concatenate([kp_ref[0], kc_ref[0]], axis=0).astype(BF16)
    v2 = jnp.concatenate([vp_ref[0], vc_ref[0]], axis=0).astype(BF16)
    qi = lax.broadcasted_iota(jnp.int32, (WINDOW, 2 * WINDOW), 0)
    kj = lax.broadcasted_iota(jnp.int32, (WINDOW, 2 * WINDOW), 1)
    dist = qi + WINDOW - kj
    mask = (dist >= 0) & (dist < WINDOW) & ((kj >= WINDOW) | (n > 0))
    outs = []
    for h in range(SWA_HEADS):
        kvh = h // SWA_GROUP
        kh = k2[:, kvh * HEAD_DIM:(kvh + 1) * HEAD_DIM]
        vh = v2[:, kvh * HEAD_DIM:(kvh + 1) * HEAD_DIM]
        lg = _nt(q[:, h * HEAD_DIM:(h + 1) * HEAD_DIM], kh) + bias_ref[h]
        lg = jnp.where(mask, lg, -jnp.inf)
        sink = sink_ref[h]
        m = jnp.maximum(jnp.max(lg, axis=-1, keepdims=True), sink)
        e = jnp.exp(lg - m)
        den = jnp.sum(e, axis=-1, keepdims=True) + jnp.exp(sink - m)
        outs.append(_mm((e / den).astype(BF16), vh))
    o_ref[0] = jnp.concatenate(outs, axis=-1).astype(o_ref.dtype)


def _swa(proj3, sinks, bias_nat):
    bsz, seq, _ = proj3.shape
    return pl.pallas_call(
        _swa_kernel,
        grid=(bsz, seq // WINDOW),
        in_specs=[
            pl.BlockSpec(memory_space=pltpu.SMEM),
            pl.BlockSpec((1, WINDOW, 512), lambda b, n: (b, n, COL_SQ // 512)),
            pl.BlockSpec((1, WINDOW, 128), lambda b, n: (b, n, COL_SK // 128)),
            pl.BlockSpec((1, WINDOW, 128), lambda b, n: (b, jnp.maximum(n - 1, 0), COL_SK // 128)),
            pl.BlockSpec((1, WINDOW, 128), lambda b, n: (b, n, COL_SV // 128)),
            pl.BlockSpec((1, WINDOW, 128), lambda b, n: (b, jnp.maximum(n - 1, 0), COL_SV // 128)),
            pl.BlockSpec((SWA_HEADS, WINDOW, 2 * WINDOW), lambda b, n: (0, 0, 0)),
        ],
        out_specs=pl.BlockSpec((1, WINDOW, 512), lambda b, n: (b, n, 0)),
        out_shape=jax.ShapeDtypeStruct((bsz, seq, SWA_HEADS * HEAD_DIM), BF16),
        compiler_params=pltpu.CompilerParams(
            dimension_semantics=("arbitrary", "arbitrary"), vmem_limit_bytes=VMEM_LIMIT),
        name="swa",
    )(sinks, proj3, proj3, proj3, proj3, proj3, bias_nat)


ROUTE_OFF = N_GROUPS
ROUTE_ROWS = 48


def _split_bf16(x):
    hi = x.astype(BF16)
    lo = (x - hi.astype(F32)).astype(BF16)
    return hi, lo


def _outproj_router_kernel(h_ref, og_ref, od_ref, os_ref, wo_ref, gffn_ref, wr_hi_ref, wr_lo_ref,
                           br_ref, h1_ref, xn_ref, ri_ref, rf_ref, cnt_ref, run_ref):
    tm = h_ref.shape[0]

    @pl.when(pl.program_id(0) == 0)
    def _():
        run_ref[...] = jnp.zeros_like(run_ref)

    o = jnp.concatenate([og_ref[...], od_ref[...], os_ref[...]], axis=-1)
    h1 = h_ref[...] + _mm(o, wo_ref[...])
    h1_ref[...] = h1
    xn = _rms(h1, gffn_ref[...])
    xn_ref[...] = _pack_bf16_pairs(xn)

    x_hi, x_lo = _split_bf16(xn)
    lg = (_mm(x_hi, wr_hi_ref[...]) + _mm(x_lo, wr_hi_ref[...]) + _mm(x_hi, wr_lo_ref[...])
          + br_ref[...]).T[:ROUTE_ROWS, :]

    row = lax.broadcasted_iota(jnp.int32, lg.shape, 0)
    row_f = row.astype(F32)
    ninf = -jnp.inf

    def first_max(x):
        m = jnp.max(x, axis=0, keepdims=True)
        idx = jnp.min(jnp.where(x == m, row_f, float(ROUTE_ROWS)), axis=0, keepdims=True)
        return m, idx

    gl = jnp.where(row < N_GROUPS, lg, ninf)
    gmax, gsel = first_max(gl)
    g_w = 1.0 / jnp.sum(jnp.exp(gl - gmax), axis=0, keepdims=True)
    e_lo = ROUTE_OFF + EXPERTS_PER_GROUP * gsel
    el = jnp.where((row_f >= e_lo) & (row_f < e_lo + EXPERTS_PER_GROUP), lg, ninf)
    m1, i1 = first_max(el)
    eden = jnp.sum(jnp.exp(el - m1), axis=0, keepdims=True)
    m2, i2 = first_max(jnp.where(row_f == i1, ninf, el))
    p1 = 1.0 / eden
    p2 = jnp.exp(m2 - m1) / eden
    gate1 = g_w * p1 / (p1 + p2)
    gate2 = g_w * p2 / (p1 + p2)

    onehot = jnp.where((row_f == i1) | (row_f == i2), 1.0, 0.0)
    rr = lax.broadcasted_iota(jnp.int32, (tm, tm), 0)
    cc = lax.broadcasted_iota(jnp.int32, (tm, tm), 1)
    earlier = jnp.where(rr < cc, 1.0, 0.0).astype(BF16)
    run = run_ref[...]
    before = _mm(onehot.astype(BF16), earlier) + jnp.concatenate([run] * (tm // LANES), axis=1)
    rank1 = jnp.sum(jnp.where(row_f == i1, before, 0.0), axis=0, keepdims=True)
    rank2 = jnp.sum(jnp.where(row_f == i2, before, 0.0), axis=0, keepdims=True)
    run_ref[...] = run + jnp.sum(onehot, axis=1, keepdims=True)
    cnt_ref[...] = run_ref[...]

    out_row = lax.broadcasted_iota(jnp.int32, (SUBLANES, tm), 0)
    ints = jnp.where(out_row == 0, i1 - ROUTE_OFF,
                     jnp.where(out_row == 1, i2 - ROUTE_OFF,
                               jnp.where(out_row == 2, rank1,
                                         jnp.where(out_row == 3, rank2, 0.0))))
    ri_ref[...] = ints.astype(jnp.int32)
    rf_ref[...] = jnp.where(out_row == 0, gate1, jnp.where(out_row == 1, gate2, 0.0))


def _outproj_router(h, og, od, os_, wo, gffn, wr_hi, wr_lo, br):
    t = h.shape[0]
    tm = ROUTER_TILE
    row = lambda w: pl.BlockSpec((tm, w), lambda i: (i, 0))
    col = lambda r: pl.BlockSpec((r, tm), lambda i: (0, i))
    full = lambda a, b: pl.BlockSpec((a, b), lambda i: (0, 0))
    return pl.pallas_call(
        _outproj_router_kernel,
        grid=(t // tm,),
        in_specs=[row(D_MODEL), row(256), row(256), row(512), full(D_MODEL, D_MODEL),
                  full(1, D_MODEL), full(D_MODEL, LANES), full(D_MODEL, LANES), full(1, LANES)],
        out_specs=[row(D_MODEL), row(HALF_MODEL), col(SUBLANES), col(SUBLANES),
                   full(ROUTE_ROWS, LANES)],
        out_shape=[
            jax.ShapeDtypeStruct((t, D_MODEL), F32),
            jax.ShapeDtypeStruct((t, HALF_MODEL), jnp.uint32),
            jax.ShapeDtypeStruct((SUBLANES, t), jnp.int32),
            jax.ShapeDtypeStruct((SUBLANES, t), F32),
            jax.ShapeDtypeStruct((ROUTE_ROWS, LANES), F32),
        ],
        scratch_shapes=[pltpu.VMEM((ROUTE_ROWS, LANES), F32)],
        compiler_params=pltpu.CompilerParams(
            dimension_semantics=("arbitrary",), vmem_limit_bytes=VMEM_LIMIT),
        name="outproj_router",
    )(h, og, od, os_, wo, gffn, wr_hi, wr_lo, br)


def _dispatch_kernel(pos_ref, xn_ref, buf_in_ref, buf_ref, sem):
    del buf_in_ref
    tm = xn_ref.shape[0]

    def row_copy(r, k):
        dst = pos_ref[0, 0, 2 * r + k]
        return pltpu.make_async_copy(xn_ref.at[pl.ds(r, 1)], buf_ref.at[pl.ds(dst, 1)], sem)

    def issue(g, carry):
        for u in range(DMA_UNROLL):
            row_copy(g * DMA_UNROLL + u, 0).start()
            row_copy(g * DMA_UNROLL + u, 1).start()
        return carry

    def drain(g, carry):
        for u in range(DMA_UNROLL):
            row_copy(g * DMA_UNROLL + u, 0).wait()
            row_copy(g * DMA_UNROLL + u, 1).wait()
        return carry

    lax.fori_loop(0, tm // DMA_UNROLL, issue, 0)
    lax.fori_loop(0, tm // DMA_UNROLL, drain, 0)


def _dispatch(pos3, xn, buf0):
    t = xn.shape[0]
    tm = TOKEN_TILE
    return pl.pallas_call(
        _dispatch_kernel,
        grid=(t // tm,),
        in_specs=[
            pl.BlockSpec((1, 1, 2 * tm), lambda i: (i, 0, 0), memory_space=pltpu.SMEM),
            pl.BlockSpec((tm, HALF_MODEL), lambda i: (i, 0)),
            pl.BlockSpec(memory_space=pl.ANY),
        ],
        out_specs=pl.BlockSpec(memory_space=pl.ANY),
        out_shape=jax.ShapeDtypeStruct(buf0.shape, buf0.dtype),
        scratch_shapes=[pltpu.SemaphoreType.DMA(())],
        input_output_aliases={2: 0},
        compiler_params=pltpu.CompilerParams(
            dimension_semantics=("arbitrary",), vmem_limit_bytes=VMEM_LIMIT),
        name="dispatch",
    )(pos3, xn, buf0)


def _expert_kernel(be_ref, nu_ref, x_ref, wg_ref, wu_ref, wd_ref, y_ref, wg_s, wu_s, wd_s):
    i = pl.program_id(0)
    used = i < nu_ref[0]
    new_expert = (i == 0) | (be_ref[i] != be_ref[jnp.maximum(i - 1, 0)])

    @pl.when(used & new_expert)
    def _():
        wg_s[...] = wg_ref[0, 0].astype(BF16)
        wu_s[...] = wu_ref[0, 0].astype(BF16)
        wd_s[...] = wd_ref[0, 0].astype(BF16)

    @pl.when(used)
    def _():
        x_lo, x_hi = (part.astype(BF16) for part in _unpack_bf16_pairs(x_ref[...]))
        g = _mm(x_lo, wg_s[:HALF_MODEL, :]) + _mm(x_hi, wg_s[HALF_MODEL:, :])
        u = _mm(x_lo, wu_s[:HALF_MODEL, :]) + _mm(x_hi, wu_s[HALF_MODEL:, :])
        hmid = (g * jax.nn.sigmoid(g)) * u
        y_ref[...] = _pack_bf16_pairs(_mm(hmid.astype(BF16), wd_s[...]))

    @pl.when(i >= nu_ref[0])
    def _():
        y_ref[...] = jnp.zeros_like(y_ref)


def _experts(blk_expert, n_used, buf, wg, wu, wd, layer):
    nrows = buf.shape[0]
    nblk = nrows // EXPERT_ROWS
    w_index = lambda i, be, nu: (layer, be[i], 0, 0)
    return pl.pallas_call(
        _expert_kernel,
        grid_spec=pltpu.PrefetchScalarGridSpec(
            num_scalar_prefetch=2,
            grid=(nblk,),
            in_specs=[
                pl.BlockSpec((EXPERT_ROWS, HALF_MODEL), lambda i, be, nu: (i, 0)),
                pl.BlockSpec((1, 1, D_MODEL, D_EXPERT), w_index),
                pl.BlockSpec((1, 1, D_MODEL, D_EXPERT), w_index),
                pl.BlockSpec((1, 1, D_EXPERT, D_MODEL), w_index),
            ],
            out_specs=pl.BlockSpec((EXPERT_ROWS, HALF_MODEL), lambda i, be, nu: (i, 0)),
            scratch_shapes=[
                pltpu.VMEM((D_MODEL, D_EXPERT), BF16),
                pltpu.VMEM((D_MODEL, D_EXPERT), BF16),
                pltpu.VMEM((D_EXPERT, D_MODEL), BF16),
            ],
        ),
        out_shape=jax.ShapeDtypeStruct((nrows, HALF_MODEL), jnp.uint32),
        compiler_params=pltpu.CompilerParams(
            dimension_semantics=("arbitrary",), vmem_limit_bytes=VMEM_LIMIT),
        name="experts",
    )(blk_expert, n_used, buf, wg, wu, wd)


def _combine_ple_kernel(pos_ref, pos_next_ref, h1_ref, p_ref, rf_ref, yb_ref, wple_ref, gple_ref,
                        wpg_ref, gfin_ref, o_ref, ybuf, sems, *, final_norm):
    tm = h1_ref.shape[0]
    i = pl.program_id(0)
    last = pl.num_programs(0) - 1
    slot = lax.rem(i, 2)

    def row_copy(tile_pos_ref, s, r, k):
        src = tile_pos_ref[0, 0, 2 * r + k]
        return pltpu.make_async_copy(
            yb_ref.at[pl.ds(src, 1)], ybuf.at[s, k, pl.ds(r, 1)], sems.at[s])

    def looped(tile_pos_ref, s, op):
        def body(g, carry):
            for u in range(DMA_UNROLL):
                op(row_copy(tile_pos_ref, s, g * DMA_UNROLL + u, 0))
                op(row_copy(tile_pos_ref, s, g * DMA_UNROLL + u, 1))
            return carry
        lax.fori_loop(0, tm // DMA_UNROLL, body, 0)

    @pl.when(i == 0)
    def _():
        looped(pos_ref, 0, lambda cp: cp.start())

    looped(pos_ref, slot, lambda cp: cp.wait())
    for r in range(tm):
        row_copy(pos_next_ref, 1 - slot, r, 0).start()
        row_copy(pos_next_ref, 1 - slot, r, 1).start()
    e = _rms(_mm(p_ref[...].astype(BF16), wple_ref[...]), gple_ref[...])

    rf = rf_ref[...]
    y0_lo, y0_hi = _unpack_bf16_pairs(ybuf[slot, 0])
    y1_lo, y1_hi = _unpack_bf16_pairs(ybuf[slot, 1])
    g0, g1 = rf[:, 0:1], rf[:, 1:2]
    moe = jnp.concatenate([y0_lo * g0 + y1_lo * g1, y0_hi * g0 + y1_hi * g1], axis=-1)
    h2 = h1_ref[...] + moe
    h3 = h2 + e * jax.nn.sigmoid(_mm(h2.astype(BF16), wpg_ref[...]))
    if final_norm:
        h3 = _rms(h3, gfin_ref[...])
    o_ref[...] = h3

    @pl.when(i == last)
    def _():
        looped(pos_next_ref, 1 - slot, lambda cp: cp.wait())


def _combine_ple(pos3, h1, p, rf, yb, wple, gple, wpg, gfin, layer, final_norm):
    t = h1.shape[0]
    tm = TOKEN_TILE
    row = lambda w: pl.BlockSpec((tm, w), lambda i: (i, 0))
    full = lambda a, b: pl.BlockSpec((a, b), lambda i: (0, 0))
    return pl.pallas_call(
        functools.partial(_combine_ple_kernel, final_norm=final_norm),
        grid=(t // tm,),
        in_specs=[
            pl.BlockSpec((1, 1, 2 * tm), lambda i: (i, 0, 0), memory_space=pltpu.SMEM),
            pl.BlockSpec((1, 1, 2 * tm), lambda i: (jnp.minimum(i + 1, t // tm - 1), 0, 0),
                         memory_space=pltpu.SMEM),
            row(D_MODEL),
            pl.BlockSpec((tm, PLE_DIM), lambda i: (layer * (t // tm) + i, 0)),
            row(LANES),
            pl.BlockSpec(memory_space=pl.ANY),
            full(PLE_DIM, D_MODEL), full(1, D_MODEL), full(D_MODEL, D_MODEL), full(1, D_MODEL),
        ],
        out_specs=row(D_MODEL),
        out_shape=jax.ShapeDtypeStruct((t, D_MODEL), F32),
        scratch_shapes=[pltpu.VMEM((2, 2, tm, HALF_MODEL), jnp.uint32),
                        pltpu.SemaphoreType.DMA((2,))],
        compiler_params=pltpu.CompilerParams(
            dimension_semantics=("arbitrary",), vmem_limit_bytes=VMEM_LIMIT),
        name="combine_ple",
    )(pos3, pos3, h1, p, rf, yb, wple, gple, wpg, gfin)


def _rel_bucket(dist):
    n = jnp.maximum(dist, 0)
    max_exact = REL_BUCKETS // 2
    nf = jnp.maximum(n, 1).astype(F32)
    large = max_exact + (jnp.log(nf / max_exact) / math.log(REL_MAX_DIST / max_exact)
                         * (REL_BUCKETS - max_exact)).astype(jnp.int32)
    large = jnp.minimum(large, REL_BUCKETS - 1)
    return jnp.where(n < max_exact, n, large)


def _bias_tiles(rel_bias):
    qi = jnp.arange(WINDOW)[:, None]
    kj = jnp.arange(2 * WINDOW)[None, :]
    bucket = _rel_bucket(qi + WINDOW - kj)
    onehot = (bucket[..., None] == jnp.arange(REL_BUCKETS)).astype(F32)
    nat = jnp.einsum('qkb,bh->hqk', onehot, rel_bias, precision=HIGHEST)
    dsa = nat[:DSA_HEADS]
    far = jnp.broadcast_to(rel_bias[REL_BUCKETS - 1, :DSA_HEADS][:, None, None],
                           (DSA_HEADS, WINDOW, WINDOW))
    return jnp.stack([far, dsa[:, :, :WINDOW], dsa[:, :, WINDOW:]], axis=1), nat[DSA_HEADS:]


def _pack_w_in(w):
    sizes = (128, 128, 256, 256, 16, 256, 64, 64, 512, 64, 8, 512, 128, 128)
    offs = [0]
    for s in sizes:
        offs.append(offs[-1] + s)
    wt = w.T
    gq, gk, gv, gg, ga, dq, dk, dv, iq, ik, iw, sq, sk, sv = (
        wt[offs[n]:offs[n + 1]] for n in range(len(sizes)))
    z = lambda n: jnp.zeros((n, w.shape[0]), w.dtype)
    packed = jnp.concatenate(
        [iq, sq, gv, gg, dq, gq, gk, ga, z(128 - GLA_RANK), dk, dv, ik, iw,
         z(128 - IDX_DIM - IDX_HEADS), sk, sv], axis=0)
    return packed.astype(BF16)


def kernel(x, p, rel_bias, g_mix, w_in, gla_w_alpha, gla_b_alpha, gla_g_norm, swa_sinks, w_out,
           g_ffn, w_router_group, b_router_group, w_router_expert, b_router_expert, w_expert_gate,
           w_expert_up, w_expert_down, w_ple, g_ple, w_ple_gate, g_final):
    bsz, seq, d = x.shape
    depth = w_in.shape[0]
    t = bsz * seq
    assert d == D_MODEL and t % INPROJ_TILE == 0 and seq % GLA_BLOCK == 0 and seq % KC == 0
    n_blocks = -(-(2 * t) // EXPERT_ROWS) + N_EXPERTS
    dsa_tiles, swa_bias = _bias_tiles(rel_bias)

    h = x.reshape(t, d)
    for i in range(depth):
        proj = _inproj(h, g_mix[i][None, :], _pack_w_in(w_in[i]))
        proj3 = proj.reshape(bsz, seq, D_PROJ)

        wal = jnp.zeros((128, 128), F32).at[:GLA_RANK].set(gla_w_alpha[i])
        og = _gla(proj3, wal, gla_b_alpha[i][None, :],
                  jnp.tile(gla_g_norm[i], GLA_HEADS)[None, :])
        od = _dsa(proj3, dsa_tiles)
        os_ = _swa(proj3, swa_sinks[i], swa_bias)

        w_r = jnp.concatenate([w_router_group[i], w_router_expert[i]], axis=1)
        w_r = jnp.pad(w_r, ((0, 0), (0, LANES - w_r.shape[1])))
        wr_hi, wr_lo = _split_bf16(w_r)
        b_r = jnp.pad(jnp.concatenate([b_router_group[i], b_router_expert[i]]),
                      (0, LANES - N_GROUPS - N_EXPERTS))[None, :]
        h1, xn, ri, rf, cnt = _outproj_router(
            h, og.reshape(t, -1), od.reshape(t, -1), os_.reshape(t, -1), w_out[i].astype(BF16),
            g_ffn[i][None, :], wr_hi, wr_lo, b_r)

        counts = cnt[ROUTE_OFF:ROUTE_OFF + N_EXPERTS, 0].astype(jnp.int32)
        padded = (counts + EXPERT_ROWS - 1) // EXPERT_ROWS * EXPERT_ROWS
        pad_end = jnp.cumsum(padded)
        pad_start = pad_end - padded
        expert_ids = jnp.arange(N_EXPERTS, dtype=jnp.int32)
        slot_start = jnp.sum(
            jnp.where(ri[0:2, :, None] == expert_ids, pad_start, 0), axis=-1)
        pos3 = (slot_start + ri[2:4]).T.reshape(t // TOKEN_TILE, 1, 2 * TOKEN_TILE)
        gates = jnp.pad(rf[0:2].T, ((0, 0), (0, LANES - 2)))
        blk_start = jnp.arange(n_blocks, dtype=jnp.int32) * EXPERT_ROWS
        blk_expert = jnp.minimum(
            jnp.sum((pad_end[None, :] <= blk_start[:, None]).astype(jnp.int32), axis=1),
            N_EXPERTS - 1)
        n_used = (pad_end[-1:] // EXPERT_ROWS).astype(jnp.int32)

        buf = _dispatch(pos3, xn, jnp.zeros((n_blocks * EXPERT_ROWS, HALF_MODEL), jnp.uint32))
        yb = _experts(blk_expert, n_used, buf, w_expert_gate, w_expert_up, w_expert_down, i)
        h = _combine_ple(pos3, h1, p.reshape(depth * t, PLE_DIM), gates, yb, w_ple[i].astype(BF16),
                         g_ple[i][None, :], w_ple_gate[i].astype(BF16), g_final[None, :],
                         layer=i, final_norm=(i == depth - 1))
    return h.reshape(bsz, seq, d)
```

```python
import functools
import math

import jax
import jax.numpy as jnp
from jax import lax
from jax.experimental import pallas as pl
from jax.experimental.pallas import tpu as pltpu

F32 = jnp.float32
BF16 = jnp.bfloat16
HIGHEST = lax.Precision.HIGHEST

D_MODEL = 1024
HEAD_DIM = 64
GLA_HEADS = 4
GLA_DK = 32
GLA_DV = 64
GLA_RANK = 16
GLA_TAU = 16.0
GLA_CHUNK = 64
DSA_HEADS = 4
IDX_HEADS = 8
IDX_DIM = 64
DSA_TOPK_MAX = 256
SWA_HEADS = 8
SWA_KV_HEADS = 2
SWA_GROUP = SWA_HEADS // SWA_KV_HEADS
WINDOW = 128
REL_BUCKETS = 32
REL_MAX_DIST = 128
N_GROUPS = 4
EXPERTS_PER_GROUP = 8
N_EXPERTS = N_GROUPS * EXPERTS_PER_GROUP
D_EXPERT = 512
PLE_DIM = 256
EPS = 1e-6

LANES = 128
SUBLANES = 8
PACK16 = 16

COL_IQ, COL_SQ = 0, 512
COL_GV, COL_GG, COL_DQ = 1024, 1280, 1536
COL_GQ, COL_GK, COL_GA, COL_DKV, COL_IKW, COL_SK, COL_SV = 1792, 1920, 2048, 2176, 2304, 2432, 2560
D_PROJ = 2688

TOKEN_TILE = 256
INPROJ_TILE = 512
ROUTER_TILE = 512
EXPERT_ROWS = 512
GLA_BLOCK = 256
QB = 256
KC = 256
DMA_UNROLL = 8
NEG_BIG = -1e30
VMEM_LIMIT = 48 * 1024 * 1024


def _nt(a, b, precision=None):
    return lax.dot_general(a, b, (((1,), (1,)), ((), ())), precision=precision,
                           preferred_element_type=F32)


def _mm(a, b, precision=None):
    return jnp.dot(a, b, precision=precision, preferred_element_type=F32)


def _eye(n, dtype):
    r = lax.broadcasted_iota(jnp.int32, (n, n), 0)
    c = lax.broadcasted_iota(jnp.int32, (n, n), 1)
    return jnp.where(r == c, 1.0, 0.0).astype(dtype)


def _rms(x, g):
    return x * lax.rsqrt(jnp.mean(x * x, axis=-1, keepdims=True) + EPS) * g


HALF_MODEL = D_MODEL // 2
HIGH16 = 0xFFFF0000


def _pack_bf16_pairs(x):
    bits = pltpu.bitcast(x.astype(BF16).astype(F32), jnp.uint32)
    return (bits[:, HALF_MODEL:] & jnp.uint32(HIGH16)) | (bits[:, :HALF_MODEL] >> 16)


def _unpack_bf16_pairs(w):
    return pltpu.bitcast(w << 16, F32), pltpu.bitcast(w & jnp.uint32(HIGH16), F32)


def _inproj_kernel(h_ref, g_ref, w_ref, o_ref):
    a = _rms(h_ref[...], g_ref[...])
    o_ref[...] = _nt(a.astype(BF16), w_ref[...])


def _inproj(h, g, w):
    t = h.shape[0]
    return pl.pallas_call(
        _inproj_kernel,
        grid=(t // INPROJ_TILE,),
        in_specs=[
            pl.BlockSpec((INPROJ_TILE, D_MODEL), lambda i: (i, 0)),
            pl.BlockSpec((1, D_MODEL), lambda i: (0, 0)),
            pl.BlockSpec((D_PROJ, D_MODEL), lambda i: (0, 0)),
        ],
        out_specs=pl.BlockSpec((INPROJ_TILE, D_PROJ), lambda i: (i, 0)),
        out_shape=jax.ShapeDtypeStruct((t, D_PROJ), F32),
        compiler_params=pltpu.CompilerParams(
            dimension_semantics=("arbitrary",), vmem_limit_bytes=VMEM_LIMIT),
        name="inproj",
    )(h, g, w)


def _gla_kernel(q_ref, k_ref, v_ref, gg_ref, ga_ref, wal_ref, bal_ref, gn_ref, o_ref,
                state_ref, sc_all_ref, p_all_ref, b_ref):
    hk = GLA_HEADS * GLA_DK
    hv = GLA_HEADS * GLA_DV
    c = GLA_CHUNK
    half_c = c // 2
    assert 2 * c == LANES

    @pl.when(pl.program_id(1) == 0)
    def _():
        state_ref[...] = jnp.zeros_like(state_ref)

    @pl.when((pl.program_id(0) == 0) & (pl.program_id(1) == 0))
    def _():
        p_all_ref[...] = jnp.zeros_like(p_all_ref)

    pj = lax.broadcasted_iota(jnp.int32, (half_c, LANES), 0)
    pl_ = lax.broadcasted_iota(jnp.int32, (half_c, LANES), 1)
    pair_causal = jnp.where(pl_ < c, pl_, pl_ - c) <= jnp.where(pl_ < c, pj, pj + half_c)
    rv = lax.broadcasted_iota(jnp.int32, (hv, hk), 0) // GLA_DV
    ck = lax.broadcasted_iota(jnp.int32, (hv, hk), 1) // GLA_DK
    blockdiag_t = jnp.where(rv == ck, 1.0, 0.0).astype(F32)
    hr = lax.broadcasted_iota(jnp.int32, (SUBLANES, hk), 0)
    hl = lax.broadcasted_iota(jnp.int32, (SUBLANES, hk), 1) // GLA_DK
    head_rows = jnp.where(hr == hl, 1.0, 0.0).astype(BF16)
    eye_v = _eye(hv, BF16)

    ga_hi, ga_lo = _split_bf16(ga_ref[0])
    w_hi, w_lo = _split_bf16(wal_ref[...])
    z = _mm(ga_hi, w_hi) + _mm(ga_hi, w_lo) + _mm(ga_lo, w_hi) + bal_ref[...]
    log_a = (jnp.minimum(z, 0.0) - jnp.log1p(jnp.exp(-jnp.abs(z)))) * (1.0 / GLA_TAU)
    la_hi = log_a.astype(BF16)
    la_r1 = log_a - la_hi.astype(F32)
    la_mid = la_r1.astype(BF16)
    la_lo = (la_r1 - la_mid.astype(F32)).astype(BF16)
    rb = lax.broadcasted_iota(jnp.int32, (GLA_BLOCK, GLA_BLOCK), 0)
    cb = lax.broadcasted_iota(jnp.int32, (GLA_BLOCK, GLA_BLOCK), 1)
    tril = jnp.where((rb >= cb) & (rb // c == cb // c), 1.0, 0.0).astype(BF16)
    b_ref[...] = _mm(tril, la_hi) + _mm(tril, la_mid) + _mm(tril, la_lo)

    def chunk(ci):
        rows = slice(ci * c, (ci + 1) * c)
        p_ref, sc_ref = p_all_ref.at[ci], sc_all_ref.at[ci]
        q = q_ref[0, rows, :] * (GLA_DK ** -0.5)
        k = k_ref[0, rows, :]
        v = v_ref[0, rows, :]
        b = b_ref[rows, :]
        state_t = state_ref[...]
        o_inter = _nt((q * jnp.exp(b)).astype(BF16), state_t.astype(BF16))

        for t in range(c):
            ns = PACK16 * (t // PACK16 + 1)
            slot = 2 * (t % half_c) + t // half_c
            p = jnp.exp(b[t:t + 1, :] - b[:ns, :]) * k[:ns, :] * q[t:t + 1, :]
            p_ref[slot * c:slot * c + ns, :] = p.astype(BF16)
        rows_hs = _nt(head_rows, p_ref[...])
        for jj in range(half_c):
            for h in range(GLA_HEADS):
                sc_ref[h, jj:jj + 1, :] = rows_hs[h:h + 1, jj * LANES:(jj + 1) * LANES]
        vb = v.astype(BF16)
        zero_v = jnp.zeros((c, GLA_DV), BF16)
        o_heads = []
        for h in range(GLA_HEADS):
            vh = vb[:, h * GLA_DV:(h + 1) * GLA_DV]
            v_pair = jnp.concatenate([jnp.concatenate([vh, zero_v], axis=1),
                                      jnp.concatenate([zero_v, vh], axis=1)], axis=0)
            scores = jnp.where(pair_causal, sc_ref[h], 0.0).astype(BF16)
            o_pair = _mm(scores, v_pair)
            o_heads.append(jnp.concatenate([o_pair[:, :GLA_DV], o_pair[:, GLA_DV:]], axis=0))
        o = o_inter + jnp.concatenate(o_heads, axis=-1)

        b_last = b[c - 1:c, :]
        kd = (k * jnp.exp(b_last - b)).astype(BF16)
        v_t = _nt(eye_v, vb).astype(BF16)
        upd_t = _mm(v_t, kd)
        state_ref[...] = jnp.exp(b_last) * state_t + upd_t * blockdiag_t

        gg = gg_ref[0, rows, :]
        outs = []
        for h in range(GLA_HEADS):
            oh = o[:, h * GLA_DV:(h + 1) * GLA_DV]
            ms = jnp.mean(oh * oh, axis=-1, keepdims=True)
            outs.append(oh * lax.rsqrt(ms + EPS))
        on = jnp.concatenate(outs, axis=-1) * gn_ref[...]
        o_ref[0, rows, :] = (on * (gg * jax.nn.sigmoid(gg))).astype(o_ref.dtype)

    for ci in range(GLA_BLOCK // c):
        chunk(ci)


def _gla(proj3, wal, bal, gn):
    bsz, seq, _ = proj3.shape
    nb = seq // GLA_BLOCK

    def col(width, off):
        return pl.BlockSpec((1, GLA_BLOCK, width), lambda b, i: (b, i, off // width))

    return pl.pallas_call(
        _gla_kernel,
        grid=(bsz, nb),
        in_specs=[
            col(128, COL_GQ), col(128, COL_GK), col(256, COL_GV), col(256, COL_GG), col(128, COL_GA),
            pl.BlockSpec((128, 128), lambda b, i: (0, 0)),
            pl.BlockSpec((1, 128), lambda b, i: (0, 0)),
            pl.BlockSpec((1, 256), lambda b, i: (0, 0)),
        ],
        out_specs=pl.BlockSpec((1, GLA_BLOCK, 256), lambda b, i: (b, i, 0)),
        out_shape=jax.ShapeDtypeStruct((bsz, seq, GLA_HEADS * GLA_DV), BF16),
        scratch_shapes=[
            pltpu.VMEM((GLA_HEADS * GLA_DV, GLA_HEADS * GLA_DK), F32),
            pltpu.VMEM((GLA_BLOCK // GLA_CHUNK, GLA_HEADS, GLA_CHUNK // 2, 2 * GLA_CHUNK), F32),
            pltpu.VMEM((GLA_BLOCK // GLA_CHUNK, GLA_CHUNK * GLA_CHUNK, GLA_HEADS * GLA_DK), BF16),
            pltpu.VMEM((GLA_BLOCK, GLA_HEADS * GLA_DK), F32),
        ],
        compiler_params=pltpu.CompilerParams(
            dimension_semantics=("arbitrary", "arbitrary"), vmem_limit_bytes=VMEM_LIMIT),
        name="gla",
    )(proj3, proj3, proj3, proj3, proj3, wal, bal, gn)


DSA_BISECT_STEPS = 12
DSA_ACCUMULATORS = 4
DSA_WALK_UNCONDITIONAL = 2


def _dsa_kernel(dq_ref, kv_ref, iq_ref, ikw_ref, btile_ref, o_ref,
                sc_ref, lg_ref, p_ref, kb_ref, vb_ref, mp_ref, lp_ref, acc_ref, cut_ref, *, n_sel):
    j = pl.program_id(1)
    nkc = ((j + 1) * QB + KC - 1) // KC
    seq = kv_ref.shape[1]
    ksel = float(n_sel)

    @pl.when(j == 0)
    def _():
        kv = kv_ref[0]
        kb_ref[...] = kv[:, :HEAD_DIM].astype(BF16)
        vb_ref[...] = kv[:, HEAD_DIM:].astype(BF16)

    def rows(c):
        return pl.ds(pl.multiple_of(c * KC, KC), KC)

    wide = DSA_ACCUMULATORS * SUBLANES

    def fold(op, x):
        return op(x.reshape(KC // wide, wide, QB), axis=0)

    def all8(op, xw, roll_op):
        x8 = op(xw.reshape(DSA_ACCUMULATORS, SUBLANES, QB), axis=0)
        for shift in (4, 2, 1):
            x8 = roll_op(x8, pltpu.roll(x8, shift, 0))
        return x8

    def widen(x8):
        return jnp.concatenate([x8] * DSA_ACCUMULATORS, axis=0)

    zeros8 = jnp.zeros((SUBLANES, QB), F32)
    zerosw = jnp.zeros((wide, QB), F32)
    infw = jnp.full((wide, QB), jnp.inf, F32)

    ikw_q = ikw_ref[0, pl.ds(pl.multiple_of(j * QB, QB), QB), :]
    w_t = ikw_q.T[IDX_DIM:IDX_DIM + IDX_HEADS, :] * (IDX_HEADS ** -0.5 * IDX_DIM ** -0.5)
    iq = iq_ref[0].astype(BF16)
    iq_heads = [iq[:, h * IDX_DIM:(h + 1) * IDX_DIM] for h in range(IDX_HEADS)]

    s_loc = lax.broadcasted_iota(jnp.int32, (KC, QB), 0)
    t_glob = j * QB + lax.broadcasted_iota(jnp.int32, (KC, QB), 1)
    s_loc_f = s_loc.astype(F32)

    def score_chunk(c, st):
        mn8, mx8, ge8, gt8 = st
        ik = ikw_ref[0, rows(c), :][:, :IDX_DIM].astype(BF16)
        acc = jnp.zeros((KC, QB), F32)
        for h in range(IDX_HEADS):
            rel = _nt(ik, iq_heads[h])
            acc = acc + jnp.maximum(rel, 0.0) * w_t[h:h + 1, :]
        adm = s_loc + c * KC <= t_glob
        blk = jnp.where(adm, acc, -jnp.inf)
        sc_ref[rows(c), :] = blk
        mn8 = jnp.minimum(mn8, fold(jnp.min, jnp.where(adm, acc, jnp.inf)))
        mx8 = jnp.maximum(mx8, fold(jnp.max, blk))
        ge8 = ge8 + fold(jnp.sum, jnp.where(blk >= 0.0, 1.0, 0.0))
        gt8 = gt8 + fold(jnp.sum, jnp.where(blk > 0.0, 1.0, 0.0))
        return mn8, mx8, ge8, gt8

    mn8, mx8, ge8, gt8 = lax.fori_loop(0, nkc, score_chunk, (infw, -infw, zerosw, zerosw))
    row_min = all8(jnp.min, mn8, jnp.minimum)
    row_max = all8(jnp.max, mx8, jnp.maximum)
    c_ge0, c_gt0 = all8(jnp.sum, ge8, jnp.add), all8(jnp.sum, gt8, jnp.add)

    def blocks(c):
        return sc_ref[rows(c), :].reshape(KC // wide, wide, QB)

    def count_ge(x8):
        xw = widen(x8)[None]

        def body(c, cw):
            return cw + jnp.sum(jnp.where(blocks(c) >= xw, 1.0, 0.0), axis=0)
        return all8(jnp.sum, lax.fori_loop(0, nkc, body, zerosw), jnp.add)

    def min_ge(x8):
        xw = widen(x8)[None]

        def body(c, mw):
            blk = blocks(c)
            return jnp.minimum(mw, jnp.min(jnp.where(blk >= xw, blk, jnp.inf), axis=0))
        return all8(jnp.min, lax.fori_loop(0, nkc, body, infw), jnp.minimum)

    def pass_gt(x8):
        xw = widen(x8)[None]

        def body(c, st):
            cw, mw = st
            blk = blocks(c)
            hit = blk > xw
            return (cw + jnp.sum(jnp.where(hit, 1.0, 0.0), axis=0),
                    jnp.minimum(mw, jnp.min(jnp.where(hit, blk, jnp.inf), axis=0)))
        cw, mw = lax.fori_loop(0, nkc, body, (zerosw, infw))
        return all8(jnp.sum, cw, jnp.add), all8(jnp.min, mw, jnp.minimum)

    n_adm = (j * QB + lax.broadcasted_iota(jnp.int32, (SUBLANES, QB), 1) + 1).astype(F32)
    at_zero = (c_gt0 < ksel) & (c_ge0 >= ksel)
    above = c_gt0 >= ksel
    lo = jnp.where(above | at_zero, 0.0, row_min)
    c_lo = jnp.where(above | at_zero, c_ge0, n_adm)
    settled = at_zero | (n_adm <= ksel)
    hi = jnp.where(settled, lo, jnp.where(above, row_max, 0.0))

    def bisect(_, st):
        lo, hi, c_lo = st
        mid = lo + (hi - lo) * 0.5
        cnt = count_ge(mid)
        up = cnt >= ksel
        return jnp.where(up, mid, lo), jnp.where(up, hi, mid), jnp.where(up, cnt, c_lo)

    lo, _, c_lo = lax.fori_loop(0, DSA_BISECT_STEPS, bisect, (lo, hi, c_lo))

    v0 = min_ge(lo)
    done0 = jnp.where(settled | (c_lo == ksel), 1.0, 0.0)
    c_gt_init = jnp.where(at_zero, c_gt0, 0.0)

    def walk_cond(st):
        return jnp.min(st[3]) < 0.5

    def walk_body(st):
        v, c_ge, c_gt, dn = st
        cnt, vnext = pass_gt(v)
        live = dn < 0.5
        fin = live & (cnt < ksel)
        step = live & (cnt >= ksel)
        c_gt = jnp.where(fin, cnt, c_gt)
        dn = jnp.where(fin, 1.0, dn)
        v = jnp.where(step, vnext, v)
        c_ge = jnp.where(step, cnt, c_ge)
        return v, c_ge, c_gt, dn

    walk = (v0, c_lo, c_gt_init, done0)
    for _ in range(DSA_WALK_UNCONDITIONAL):
        walk = walk_body(walk)
    tau8, c_ge, c_gt, _ = lax.while_loop(walk_cond, walk_body, walk)

    need = ksel - c_gt
    cut_ref[...] = jnp.full(cut_ref.shape, float(seq), F32)

    @pl.when(jnp.max(c_ge) > ksel)
    def _():
        s_grp = lax.broadcasted_iota(jnp.int32, (KC // wide, wide, QB), 0) * wide
        s_sub = lax.broadcasted_iota(jnp.int32, (KC // wide, wide, QB), 1)
        s_idx = (s_grp + s_sub).astype(F32)
        tauw = widen(tau8)[None]

        def count_ties_below(m8):
            mw = widen(m8)[None]

            def body(c, cw):
                hit = (blocks(c) == tauw) & (s_idx + (c * KC).astype(F32) < mw)
                return cw + jnp.sum(jnp.where(hit, 1.0, 0.0), axis=0)
            return all8(jnp.sum, lax.fori_loop(0, nkc, body, zerosw), jnp.add)

        def idx_bisect(_, lh):
            lo_m, hi_m = lh
            mid = jnp.floor((lo_m + hi_m) * 0.5)
            ok = count_ties_below(mid) >= need
            return jnp.where(ok, lo_m, mid), jnp.where(ok, mid, hi_m)

        _, hi_m = lax.fori_loop(0, int(math.log2(seq)) + 1, idx_bisect,
                                (zeros8, zeros8 + float(seq)))
        cut_ref[...] = jnp.where(c_ge > ksel, hi_m, float(seq))

    tau = tau8[0:1, :]
    cut = cut_ref[0:1, :]

    q = (dq_ref[0] * (HEAD_DIM ** -0.5)).astype(BF16)
    q4 = jnp.concatenate([q[:, h * HEAD_DIM:(h + 1) * HEAD_DIM] for h in range(DSA_HEADS)],
                         axis=0)
    mp_ref[...] = jnp.full(mp_ref.shape, NEG_BIG, F32)
    lp_ref[...] = jnp.zeros(lp_ref.shape, F32)

    half = seq // 2
    second_half = (j + 1) * QB > half

    def cols(c):
        return pl.ds(pl.multiple_of(c * KC, KC), KC)

    def lane_fold(op, x):
        out = x[:, :LANES]
        for g in range(1, KC // LANES):
            out = op(out, x[:, g * LANES:(g + 1) * LANES])
        return out

    def logits_chunk(c, carry):
        blk = sc_ref[rows(c), :]
        sidx = s_loc_f + (c * KC).astype(F32)
        sel = (blk > tau) | ((blk == tau) & (sidx < cut))
        neg_t = jnp.where(sel, 0.0, NEG_BIG).T
        lg4 = _nt(q4, kb_ref[rows(c), :])
        which = [[jnp.clip((c * (KC // WINDOW) + b) - (j * (QB // WINDOW) + a) + 2, 0, 2)
                  for b in range(KC // WINDOW)] for a in range(QB // WINDOW)]
        for h in range(DSA_HEADS):
            hq = slice(h * QB, (h + 1) * QB)
            bias = jnp.concatenate(
                [jnp.concatenate([btile_ref[h, w] for w in which_a], axis=1)
                 for which_a in which], axis=0)
            lg = lg4[hq, :] + bias + neg_t
            lg_ref[hq, cols(c)] = lg
            mp_ref[h] = jnp.maximum(mp_ref[h], lane_fold(jnp.maximum, lg))
        return carry

    lax.fori_loop(0, nkc, logits_chunk, 0)
    m_rows = [jnp.max(mp_ref[h], axis=-1, keepdims=True) for h in range(DSA_HEADS)]

    def probs_chunk(c, carry):
        for h in range(DSA_HEADS):
            hq = slice(h * QB, (h + 1) * QB)
            p = jnp.exp(lg_ref[hq, cols(c)] - m_rows[h])
            lp_ref[h] = lp_ref[h] + lane_fold(jnp.add, p)
            p_ref[hq, cols(c)] = p.astype(BF16)
        return carry

    def zero_chunk(c, carry):
        p_ref[:, cols(c)] = jnp.zeros((DSA_HEADS * QB, KC), BF16)
        return carry

    lax.fori_loop(0, nkc, probs_chunk, 0)
    lax.fori_loop(nkc, jnp.where(second_half, seq // KC, half // KC), zero_chunk, 0)
    acc_ref[...] = _mm(p_ref[:, :half], vb_ref[:half, :])

    @pl.when(second_half)
    def _():
        acc_ref[...] = acc_ref[...] + _mm(p_ref[:, half:], vb_ref[half:, :])

    outs = []
    for h in range(DSA_HEADS):
        l = jnp.sum(lp_ref[h], axis=-1, keepdims=True)
        outs.append(acc_ref[h * QB:(h + 1) * QB, :] / l)
    o_ref[0] = jnp.concatenate(outs, axis=-1).astype(o_ref.dtype)


def _dsa(proj3, btiles):
    bsz, seq, _ = proj3.shape
    n_sel = min(DSA_TOPK_MAX, seq // 4)
    return pl.pallas_call(
        functools.partial(_dsa_kernel, n_sel=n_sel),
        grid=(bsz, seq // QB),
        in_specs=[
            pl.BlockSpec((1, QB, 256), lambda b, j: (b, j, COL_DQ // 256)),
            pl.BlockSpec((1, seq, 128), lambda b, j: (b, 0, COL_DKV // 128)),
            pl.BlockSpec((1, QB, 512), lambda b, j: (b, j, COL_IQ // 512)),
            pl.BlockSpec((1, seq, 128), lambda b, j: (b, 0, COL_IKW // 128)),
            pl.BlockSpec((DSA_HEADS, 3, WINDOW, WINDOW), lambda b, j: (0, 0, 0, 0)),
        ],
        out_specs=pl.BlockSpec((1, QB, 256), lambda b, j: (b, j, 0)),
        out_shape=jax.ShapeDtypeStruct((bsz, seq, DSA_HEADS * HEAD_DIM), BF16),
        scratch_shapes=[
            pltpu.VMEM((seq, QB), F32),
            pltpu.VMEM((DSA_HEADS * QB, seq), F32),
            pltpu.VMEM((DSA_HEADS * QB, seq), BF16),
            pltpu.VMEM((seq, HEAD_DIM), BF16),
            pltpu.VMEM((seq, HEAD_DIM), BF16),
            pltpu.VMEM((DSA_HEADS, QB, LANES), F32),
            pltpu.VMEM((DSA_HEADS, QB, LANES), F32),
            pltpu.VMEM((DSA_HEADS * QB, HEAD_DIM), F32),
            pltpu.VMEM((SUBLANES, QB), F32),
        ],
        compiler_params=pltpu.CompilerParams(
            dimension_semantics=("arbitrary", "arbitrary"), vmem_limit_bytes=VMEM_LIMIT),
        name="dsa",
    )(proj3, proj3, proj3, proj3, btiles)


def _swa_kernel(sink_ref, q_ref, kc_ref, kp_ref, vc_ref, vp_ref, bias_ref, o_ref):
    n = pl.program_id(1)
    q = (q_ref[0] * (HEAD_DIM ** -0.5)).astype(BF16)
    k2 = jnp.concatenate([kp_ref[0], kc_ref[0]], axis=0).astype(BF16)
    v2 = jnp.concatenate([vp_ref[0], vc_ref[0]], axis=0).astype(BF16)
    qi = lax.broadcasted_iota(jnp.int32, (WINDOW, 2 * WINDOW), 0)
    kj = lax.broadcasted_iota(jnp.int32, (WINDOW, 2 * WINDOW), 1)
    dist = qi + WINDOW - kj
    mask = (dist >= 0) & (dist < WINDOW) & ((kj >= WINDOW) | (n > 0))
    outs = []
    for h in range(SWA_HEADS):
        kvh = h // SWA_GROUP
        kh = k2[:, kvh * HEAD_DIM:(kvh + 1) * HEAD_DIM]
        vh = v2[:, kvh * HEAD_DIM:(kvh + 1) * HEAD_DIM]
        lg = _nt(q[:, h * HEAD_DIM:(h + 1) * HEAD_DIM], kh) + bias_ref[h]
        lg = jnp.where(mask, lg, -jnp.inf)
        sink = sink_ref[h]
        m = jnp.maximum(jnp.max(lg, axis=-1, keepdims=True), sink)
        e = jnp.exp(lg - m)
        den = jnp.sum(e, axis=-1, keepdims=True) + jnp.exp(sink - m)
        outs.append(_mm((e / den).astype(BF16), vh))
    o_ref[0] = jnp.concatenate(outs, axis=-1).astype(o_ref.dtype)


def _swa(proj3, sinks, bias_nat):
    bsz, seq, _ = proj3.shape
    return pl.pallas_call(
        _swa_kernel,
        grid=(bsz, seq // WINDOW),
        in_specs=[
            pl.BlockSpec(memory_space=pltpu.SMEM),
            pl.BlockSpec((1, WINDOW, 512), lambda b, n: (b, n, COL_SQ // 512)),
            pl.BlockSpec((1, WINDOW, 128), lambda b, n: (b, n, COL_SK // 128)),
            pl.BlockSpec((1, WINDOW, 128), lambda b, n: (b, jnp.maximum(n - 1, 0), COL_SK // 128)),
            pl.BlockSpec((1, WINDOW, 128), lambda b, n: (b, n, COL_SV // 128)),
            pl.BlockSpec((1, WINDOW, 128), lambda b, n: (b, jnp.maximum(n - 1, 0), COL_SV // 128)),
            pl.BlockSpec((SWA_HEADS, WINDOW, 2 * WINDOW), lambda b, n: (0, 0, 0)),
        ],
        out_specs=pl.BlockSpec((1, WINDOW, 512), lambda b, n: (b, n, 0)),
        out_shape=jax.ShapeDtypeStruct((bsz, seq, SWA_HEADS * HEAD_DIM), BF16),
        compiler_params=pltpu.CompilerParams(
            dimension_semantics=("arbitrary", "arbitrary"), vmem_limit_bytes=VMEM_LIMIT),
        name="swa",
    )(sinks, proj3, proj3, proj3, proj3, proj3, bias_nat)


ROUTE_OFF = N_GROUPS
ROUTE_ROWS = 48


def _split_bf16(x):
    hi = x.astype(BF16)
    lo = (x - hi.astype(F32)).astype(BF16)
    return hi, lo


def _outproj_router_kernel(h_ref, og_ref, od_ref, os_ref, wo_ref, gffn_ref, wr_hi_ref, wr_lo_ref,
                           br_ref, h1_ref, xn_ref, ri_ref, rf_ref, cnt_ref, run_ref):
    tm = h_ref.shape[0]

    @pl.when(pl.program_id(0) == 0)
    def _():
        run_ref[...] = jnp.zeros_like(run_ref)

    o = jnp.concatenate([og_ref[...], od_ref[...], os_ref[...]], axis=-1)
    h1 = h_ref[...] + _mm(o, wo_ref[...])
    h1_ref[...] = h1
    xn = _rms(h1, gffn_ref[...])
    xn_ref[...] = _pack_bf16_pairs(xn)

    x_hi, x_lo = _split_bf16(xn)
    lg = (_mm(x_hi, wr_hi_ref[...]) + _mm(x_lo, wr_hi_ref[...]) + _mm(x_hi, wr_lo_ref[...])
          + br_ref[...]).T[:ROUTE_ROWS, :]

    row = lax.broadcasted_iota(jnp.int32, lg.shape, 0)
    row_f = row.astype(F32)
    ninf = -jnp.inf

    def first_max(x):
        m = jnp.max(x, axis=0, keepdims=True)
        idx = jnp.min(jnp.where(x == m, row_f, float(ROUTE_ROWS)), axis=0, keepdims=True)
        return m, idx

    gl = jnp.where(row < N_GROUPS, lg, ninf)
    gmax, gsel = first_max(gl)
    g_w = 1.0 / jnp.sum(jnp.exp(gl - gmax), axis=0, keepdims=True)
    e_lo = ROUTE_OFF + EXPERTS_PER_GROUP * gsel
    el = jnp.where((row_f >= e_lo) & (row_f < e_lo + EXPERTS_PER_GROUP), lg, ninf)
    m1, i1 = first_max(el)
    eden = jnp.sum(jnp.exp(el - m1), axis=0, keepdims=True)
    m2, i2 = first_max(jnp.where(row_f == i1, ninf, el))
    p1 = 1.0 / eden
    p2 = jnp.exp(m2 - m1) / eden
    gate1 = g_w * p1 / (p1 + p2)
    gate2 = g_w * p2 / (p1 + p2)

    onehot = jnp.where((row_f == i1) | (row_f == i2), 1.0, 0.0)
    rr = lax.broadcasted_iota(jnp.int32, (tm, tm), 0)
    cc = lax.broadcasted_iota(jnp.int32, (tm, tm), 1)
    earlier = jnp.where(rr < cc, 1.0, 0.0).astype(BF16)
    run = run_ref[...]
    before = _mm(onehot.astype(BF16), earlier) + jnp.concatenate([run] * (tm // LANES), axis=1)
    rank1 = jnp.sum(jnp.where(row_f == i1, before, 0.0), axis=0, keepdims=True)
    rank2 = jnp.sum(jnp.where(row_f == i2, before, 0.0), axis=0, keepdims=True)
    run_ref[...] = run + jnp.sum(onehot, axis=1, keepdims=True)
    cnt_ref[...] = run_ref[...]

    out_row = lax.broadcasted_iota(jnp.int32, (SUBLANES, tm), 0)
    ints = jnp.where(out_row == 0, i1 - ROUTE_OFF,
                     jnp.where(out_row == 1, i2 - ROUTE_OFF,
                               jnp.where(out_row == 2, rank1,
                                         jnp.where(out_row == 3, rank2, 0.0))))
    ri_ref[...] = ints.astype(jnp.int32)
    rf_ref[...] = jnp.where(out_row == 0, gate1, jnp.where(out_row == 1, gate2, 0.0))


def _outproj_router(h, og, od, os_, wo, gffn, wr_hi, wr_lo, br):
    t = h.shape[0]
    tm = ROUTER_TILE
    row = lambda w: pl.BlockSpec((tm, w), lambda i: (i, 0))
    col = lambda r: pl.BlockSpec((r, tm), lambda i: (0, i))
    full = lambda a, b: pl.BlockSpec((a, b), lambda i: (0, 0))
    return pl.pallas_call(
        _outproj_router_kernel,
        grid=(t // tm,),
        in_specs=[row(D_MODEL), row(256), row(256), row(512), full(D_MODEL, D_MODEL),
                  full(1, D_MODEL), full(D_MODEL, LANES), full(D_MODEL, LANES), full(1, LANES)],
        out_specs=[row(D_MODEL), row(HALF_MODEL), col(SUBLANES), col(SUBLANES),
                   full(ROUTE_ROWS, LANES)],
        out_shape=[
            jax.ShapeDtypeStruct((t, D_MODEL), F32),
            jax.ShapeDtypeStruct((t, HALF_MODEL), jnp.uint32),
            jax.ShapeDtypeStruct((SUBLANES, t), jnp.int32),
            jax.ShapeDtypeStruct((SUBLANES, t), F32),
            jax.ShapeDtypeStruct((ROUTE_ROWS, LANES), F32),
        ],
        scratch_shapes=[pltpu.VMEM((ROUTE_ROWS, LANES), F32)],
        compiler_params=pltpu.CompilerParams(
            dimension_semantics=("arbitrary",), vmem_limit_bytes=VMEM_LIMIT),
        name="outproj_router",
    )(h, og, od, os_, wo, gffn, wr_hi, wr_lo, br)


def _dispatch_kernel(pos_ref, xn_ref, buf_in_ref, buf_ref, sem):
    del buf_in_ref
    tm = xn_ref.shape[0]

    def row_copy(r, k):
        dst = pos_ref[0, 0, 2 * r + k]
        return pltpu.make_async_copy(xn_ref.at[pl.ds(r, 1)], buf_ref.at[pl.ds(dst, 1)], sem)

    def issue(g, carry):
        for u in range(DMA_UNROLL):
            row_copy(g * DMA_UNROLL + u, 0).start()
            row_copy(g * DMA_UNROLL + u, 1).start()
        return carry

    def drain(g, carry):
        for u in range(DMA_UNROLL):
            row_copy(g * DMA_UNROLL + u, 0).wait()
            row_copy(g * DMA_UNROLL + u, 1).wait()
        return carry

    lax.fori_loop(0, tm // DMA_UNROLL, issue, 0)
    lax.fori_loop(0, tm // DMA_UNROLL, drain, 0)


def _dispatch(pos3, xn, buf0):
    t = xn.shape[0]
    tm = TOKEN_TILE
    return pl.pallas_call(
        _dispatch_kernel,
        grid=(t // tm,),
        in_specs=[
            pl.BlockSpec((1, 1, 2 * tm), lambda i: (i, 0, 0), memory_space=pltpu.SMEM),
            pl.BlockSpec((tm, HALF_MODEL), lambda i: (i, 0)),
            pl.BlockSpec(memory_space=pl.ANY),
        ],
        out_specs=pl.BlockSpec(memory_space=pl.ANY),
        out_shape=jax.ShapeDtypeStruct(buf0.shape, buf0.dtype),
        scratch_shapes=[pltpu.SemaphoreType.DMA(())],
        input_output_aliases={2: 0},
        compiler_params=pltpu.CompilerParams(
            dimension_semantics=("arbitrary",), vmem_limit_bytes=VMEM_LIMIT),
        name="dispatch",
    )(pos3, xn, buf0)


def _expert_kernel(be_ref, nu_ref, x_ref, wg_ref, wu_ref, wd_ref, y_ref, wg_s, wu_s, wd_s):
    i = pl.program_id(0)
    used = i < nu_ref[0]
    new_expert = (i == 0) | (be_ref[i] != be_ref[jnp.maximum(i - 1, 0)])

    @pl.when(used & new_expert)
    def _():
        wg_s[...] = wg_ref[0, 0].astype(BF16)
        wu_s[...] = wu_ref[0, 0].astype(BF16)
        wd_s[...] = wd_ref[0, 0].astype(BF16)

    @pl.when(used)
    def _():
        x_lo, x_hi = (part.astype(BF16) for part in _unpack_bf16_pairs(x_ref[...]))
        g = _mm(x_lo, wg_s[:HALF_MODEL, :]) + _mm(x_hi, wg_s[HALF_MODEL:, :])
        u = _mm(x_lo, wu_s[:HALF_MODEL, :]) + _mm(x_hi, wu_s[HALF_MODEL:, :])
        hmid = (g * jax.nn.sigmoid(g)) * u
        y_ref[...] = _pack_bf16_pairs(_mm(hmid.astype(BF16), wd_s[...]))

    @pl.when(i >= nu_ref[0])
    def _():
        y_ref[...] = jnp.zeros_like(y_ref)


def _experts(blk_expert, n_used, buf, wg, wu, wd, layer):
    nrows = buf.shape[0]
    nblk = nrows // EXPERT_ROWS
    w_index = lambda i, be, nu: (layer, be[i], 0, 0)
    return pl.pallas_call(
        _expert_kernel,
        grid_spec=pltpu.PrefetchScalarGridSpec(
            num_scalar_prefetch=2,
            grid=(nblk,),
            in_specs=[
                pl.BlockSpec((EXPERT_ROWS, HALF_MODEL), lambda i, be, nu: (i, 0)),
                pl.BlockSpec((1, 1, D_MODEL, D_EXPERT), w_index),
                pl.BlockSpec((1, 1, D_MODEL, D_EXPERT), w_index),
                pl.BlockSpec((1, 1, D_EXPERT, D_MODEL), w_index),
            ],
            out_specs=pl.BlockSpec((EXPERT_ROWS, HALF_MODEL), lambda i, be, nu: (i, 0)),
            scratch_shapes=[
                pltpu.VMEM((D_MODEL, D_EXPERT), BF16),
                pltpu.VMEM((D_MODEL, D_EXPERT), BF16),
                pltpu.VMEM((D_EXPERT, D_MODEL), BF16),
            ],
        ),
        out_shape=jax.ShapeDtypeStruct((nrows, HALF_MODEL), jnp.uint32),
        compiler_params=pltpu.CompilerParams(
            dimension_semantics=("arbitrary",), vmem_limit_bytes=VMEM_LIMIT),
        name="experts",
    )(blk_expert, n_used, buf, wg, wu, wd)


def _combine_ple_kernel(pos_ref, pos_next_ref, h1_ref, p_ref, rf_ref, yb_ref, wple_ref, gple_ref,
                        wpg_ref, gfin_ref, o_ref, ybuf, sems, *, final_norm):
    tm = h1_ref.shape[0]
    i = pl.program_id(0)
    last = pl.num_programs(0) - 1
    slot = lax.rem(i, 2)

    def row_copy(tile_pos_ref, s, r, k):
        src = tile_pos_ref[0, 0, 2 * r + k]
        return pltpu.make_async_copy(
            yb_ref.at[pl.ds(src, 1)], ybuf.at[s, k, pl.ds(r, 1)], sems.at[s])

    def looped(tile_pos_ref, s, op):
        def body(g, carry):
            for u in range(DMA_UNROLL):
                op(row_copy(tile_pos_ref, s, g * DMA_UNROLL + u, 0))
                op(row_copy(tile_pos_ref, s, g * DMA_UNROLL + u, 1))
            return carry
        lax.fori_loop(0, tm // DMA_UNROLL, body, 0)

    @pl.when(i == 0)
    def _():
        looped(pos_ref, 0, lambda cp: cp.start())

    looped(pos_ref, slot, lambda cp: cp.wait())
    for r in range(tm):
        row_copy(pos_next_ref, 1 - slot, r, 0).start()
        row_copy(pos_next_ref, 1 - slot, r, 1).start()
    e = _rms(_mm(p_ref[...].astype(BF16), wple_ref[...]), gple_ref[...])

    rf = rf_ref[...]
    y0_lo, y0_hi = _unpack_bf16_pairs(ybuf[slot, 0])
    y1_lo, y1_hi = _unpack_bf16_pairs(ybuf[slot, 1])
    g0, g1 = rf[:, 0:1], rf[:, 1:2]
    moe = jnp.concatenate([y0_lo * g0 + y1_lo * g1, y0_hi * g0 + y1_hi * g1], axis=-1)
    h2 = h1_ref[...] + moe
    h3 = h2 + e * jax.nn.sigmoid(_mm(h2.astype(BF16), wpg_ref[...]))
    if final_norm:
        h3 = _rms(h3, gfin_ref[...])
    o_ref[...] = h3

    @pl.when(i == last)
    def _():
        looped(pos_next_ref, 1 - slot, lambda cp: cp.wait())


def _combine_ple(pos3, h1, p, rf, yb, wple, gple, wpg, gfin, layer, final_norm):
    t = h1.shape[0]
    tm = TOKEN_TILE
    row = lambda w: pl.BlockSpec((tm, w), lambda i: (i, 0))
    full = lambda a, b: pl.BlockSpec((a, b), lambda i: (0, 0))
    return pl.pallas_call(
        functools.partial(_combine_ple_kernel, final_norm=final_norm),
        grid=(t // tm,),
        in_specs=[
            pl.BlockSpec((1, 1, 2 * tm), lambda i: (i, 0, 0), memory_space=pltpu.SMEM),
            pl.BlockSpec((1, 1, 2 * tm), lambda i: (jnp.minimum(i + 1, t // tm - 1), 0, 0),
                         memory_space=pltpu.SMEM),
            row(D_MODEL),
            pl.BlockSpec((tm, PLE_DIM), lambda i: (layer * (t // tm) + i, 0)),
            row(LANES),
            pl.BlockSpec(memory_space=pl.ANY),
            full(PLE_DIM, D_MODEL), full(1, D_MODEL), full(D_MODEL, D_MODEL), full(1, D_MODEL),
        ],
        out_specs=row(D_MODEL),
        out_shape=jax.ShapeDtypeStruct((t, D_MODEL), F32),
        scratch_shapes=[pltpu.VMEM((2, 2, tm, HALF_MODEL), jnp.uint32),
                        pltpu.SemaphoreType.DMA((2,))],
        compiler_params=pltpu.CompilerParams(
            dimension_semantics=("arbitrary",), vmem_limit_bytes=VMEM_LIMIT),
        name="combine_ple",
    )(pos3, pos3, h1, p, rf, yb, wple, gple, wpg, gfin)


def _rel_bucket(dist):
    n = jnp.maximum(dist, 0)
    max_exact = REL_BUCKETS // 2
    nf = jnp.maximum(n, 1).astype(F32)
    large = max_exact + (jnp.log(nf / max_exact) / math.log(REL_MAX_DIST / max_exact)
                         * (REL_BUCKETS - max_exact)).astype(jnp.int32)
    large = jnp.minimum(large, REL_BUCKETS - 1)
    return jnp.where(n < max_exact, n, large)


def _bias_tiles(rel_bias):
    qi = jnp.arange(WINDOW)[:, None]
    kj = jnp.arange(2 * WINDOW)[None, :]
    bucket = _rel_bucket(qi + WINDOW - kj)
    onehot = (bucket[..., None] == jnp.arange(REL_BUCKETS)).astype(F32)
    nat = jnp.einsum('qkb,bh->hqk', onehot, rel_bias, precision=HIGHEST)
    dsa = nat[:DSA_HEADS]
    far = jnp.broadcast_to(rel_bias[REL_BUCKETS - 1, :DSA_HEADS][:, None, None],
                           (DSA_HEADS, WINDOW, WINDOW))
    return jnp.stack([far, dsa[:, :, :WINDOW], dsa[:, :, WINDOW:]], axis=1), nat[DSA_HEADS:]


def _pack_w_in(w):
    sizes = (128, 128, 256, 256, 16, 256, 64, 64, 512, 64, 8, 512, 128, 128)
    offs = [0]
    for s in sizes:
        offs.append(offs[-1] + s)
    wt = w.T
    gq, gk, gv, gg, ga, dq, dk, dv, iq, ik, iw, sq, sk, sv = (
        wt[offs[n]:offs[n + 1]] for n in range(len(sizes)))
    z = lambda n: jnp.zeros((n, w.shape[0]), w.dtype)
    packed = jnp.concatenate(
        [iq, sq, gv, gg, dq, gq, gk, ga, z(128 - GLA_RANK), dk, dv, ik, iw,
         z(128 - IDX_DIM - IDX_HEADS), sk, sv], axis=0)
    return packed.astype(BF16)


def kernel(x, p, rel_bias, g_mix, w_in, gla_w_alpha, gla_b_alpha, gla_g_norm, swa_sinks, w_out,
           g_ffn, w_router_group, b_router_group, w_router_expert, b_router_expert, w_expert_gate,
           w_expert_up, w_expert_down, w_ple, g_ple, w_ple_gate, g_final):
    bsz, seq, d = x.shape
    depth = w_in.shape[0]
    t = bsz * seq
    assert d == D_MODEL and t % INPROJ_TILE == 0 and seq % GLA_BLOCK == 0 and seq % KC == 0
    n_blocks = -(-(2 * t) // EXPERT_ROWS) + N_EXPERTS
    dsa_tiles, swa_bias = _bias_tiles(rel_bias)

    h = x.reshape(t, d)
    for i in range(depth):
        proj = _inproj(h, g_mix[i][None, :], _pack_w_in(w_in[i]))
        proj3 = proj.reshape(bsz, seq, D_PROJ)

        wal = jnp.zeros((128, 128), F32).at[:GLA_RANK].set(gla_w_alpha[i])
        og = _gla(proj3, wal, gla_b_alpha[i][None, :],
                  jnp.tile(gla_g_norm[i], GLA_HEADS)[None, :])
        od = _dsa(proj3, dsa_tiles)
        os_ = _swa(proj3, swa_sinks[i], swa_bias)

        w_r = jnp.concatenate([w_router_group[i], w_router_expert[i]], axis=1)
        w_r = jnp.pad(w_r, ((0, 0), (0, LANES - w_r.shape[1])))
        wr_hi, wr_lo = _split_bf16(w_r)
        b_r = jnp.pad(jnp.concatenate([b_router_group[i], b_router_expert[i]]),
                      (0, LANES - N_GROUPS - N_EXPERTS))[None, :]
        h1, xn, ri, rf, cnt = _outproj_router(
            h, og.reshape(t, -1), od.reshape(t, -1), os_.reshape(t, -1), w_out[i].astype(BF16),
            g_ffn[i][None, :], wr_hi, wr_lo, b_r)

        counts = cnt[ROUTE_OFF:ROUTE_OFF + N_EXPERTS, 0].astype(jnp.int32)
        padded = (counts + EXPERT_ROWS - 1) // EXPERT_ROWS * EXPERT_ROWS
        pad_end = jnp.cumsum(padded)
        pad_start = pad_end - padded
        expert_ids = jnp.arange(N_EXPERTS, dtype=jnp.int32)
        slot_start = jnp.sum(
            jnp.where(ri[0:2, :, None] == expert_ids, pad_start, 0), axis=-1)
        pos3 = (slot_start + ri[2:4]).T.reshape(t // TOKEN_TILE, 1, 2 * TOKEN_TILE)
        gates = jnp.pad(rf[0:2].T, ((0, 0), (0, LANES - 2)))
        blk_start = jnp.arange(n_blocks, dtype=jnp.int32) * EXPERT_ROWS
        blk_expert = jnp.minimum(
            jnp.sum((pad_end[None, :] <= blk_start[:, None]).astype(jnp.int32), axis=1),
            N_EXPERTS - 1)
        n_used = (pad_end[-1:] // EXPERT_ROWS).astype(jnp.int32)

        buf = _dispatch(pos3, xn, jnp.zeros((n_blocks * EXPERT_ROWS, HALF_MODEL), jnp.uint32))
        yb = _experts(blk_expert, n_used, buf, w_expert_gate, w_expert_up, w_expert_down, i)
        h = _combine_ple(pos3, h1, p.reshape(depth * t, PLE_DIM), gates, yb, w_ple[i].astype(BF16),
                         g_ple[i][None, :], w_ple_gate[i].astype(BF16), g_final[None, :],
                         layer=i, final_norm=(i == depth - 1))
    return h.reshape(bsz, seq, d)
```

```python
import functools
import math

import jax
import jax.numpy as jnp
from jax import lax
from jax.experimental import pallas as pl
from jax.experimental.pallas import tpu as pltpu

F32 = jnp.float32
BF16 = jnp.bfloat16
HIGHEST = lax.Precision.HIGHEST

D_MODEL = 1024
HEAD_DIM = 64
GLA_HEADS = 4
GLA_DK = 32
GLA_DV = 64
GLA_RANK = 16
GLA_TAU = 16.0
GLA_CHUNK = 64
DSA_HEADS = 4
IDX_HEADS = 8
IDX_DIM = 64
DSA_TOPK_MAX = 256
SWA_HEADS = 8
SWA_KV_HEADS = 2
SWA_GROUP = SWA_HEADS // SWA_KV_HEADS
WINDOW = 128
REL_BUCKETS = 32
REL_MAX_DIST = 128
N_GROUPS = 4
EXPERTS_PER_GROUP = 8
N_EXPERTS = N_GROUPS * EXPERTS_PER_GROUP
D_EXPERT = 512
PLE_DIM = 256
EPS = 1e-6

LANES = 128
SUBLANES = 8
PACK16 = 16

COL_IQ, COL_SQ = 0, 512
COL_GV, COL_GG, COL_DQ = 1024, 1280, 1536
COL_GQ, COL_GK, COL_GA, COL_DKV, COL_IKW, COL_SK, COL_SV = 1792, 1920, 2048, 2176, 2304, 2432, 2560
D_PROJ = 2688

TOKEN_TILE = 256
INPROJ_TILE = 512
ROUTER_TILE = 512
EXPERT_ROWS = 512
GLA_BLOCK = 256
QB = 256
KC = 256
DMA_UNROLL = 8
NEG_BIG = -1e30
VMEM_LIMIT = 48 * 1024 * 1024


def _nt(a, b, precision=None):
    return lax.dot_general(a, b, (((1,), (1,)), ((), ())), precision=precision,
                           preferred_element_type=F32)


def _mm(a, b, precision=None):
    return jnp.dot(a, b, precision=precision, preferred_element_type=F32)


def _eye(n, dtype):
    r = lax.broadcasted_iota(jnp.int32, (n, n), 0)
    c = lax.broadcasted_iota(jnp.int32, (n, n), 1)
    return jnp.where(r == c, 1.0, 0.0).astype(dtype)


def _rms(x, g):
    return x * lax.rsqrt(jnp.mean(x * x, axis=-1, keepdims=True) + EPS) * g


HALF_MODEL = D_MODEL // 2
HIGH16 = 0xFFFF0000


def _pack_bf16_pairs(x):
    bits = pltpu.bitcast(x.astype(BF16).astype(F32), jnp.uint32)
    return (bits[:, HALF_MODEL:] & jnp.uint32(HIGH16)) | (bits[:, :HALF_MODEL] >> 16)


def _unpack_bf16_pairs(w):
    return pltpu.bitcast(w << 16, F32), pltpu.bitcast(w & jnp.uint32(HIGH16), F32)


def _inproj_kernel(h_ref, g_ref, w_ref, o_ref):
    a = _rms(h_ref[...], g_ref[...])
    o_ref[...] = _nt(a.astype(BF16), w_ref[...])


def _inproj(h, g, w):
    t = h.shape[0]
    return pl.pallas_call(
        _inproj_kernel,
        grid=(t // INPROJ_TILE,),
        in_specs=[
            pl.BlockSpec((INPROJ_TILE, D_MODEL), lambda i: (i, 0)),
            pl.BlockSpec((1, D_MODEL), lambda i: (0, 0)),
            pl.BlockSpec((D_PROJ, D_MODEL), lambda i: (0, 0)),
        ],
        out_specs=pl.BlockSpec((INPROJ_TILE, D_PROJ), lambda i: (i, 0)),
        out_shape=jax.ShapeDtypeStruct((t, D_PROJ), F32),
        compiler_params=pltpu.CompilerParams(
            dimension_semantics=("arbitrary",), vmem_limit_bytes=VMEM_LIMIT),
        name="inproj",
    )(h, g, w)


def _gla_kernel(q_ref, k_ref, v_ref, gg_ref, ga_ref, wal_ref, bal_ref, gn_ref, o_ref,
                state_ref, sc_all_ref, p_all_ref, b_ref):
    hk = GLA_HEADS * GLA_DK
    hv = GLA_HEADS * GLA_DV
    c = GLA_CHUNK
    half_c = c // 2
    assert 2 * c == LANES

    @pl.when(pl.program_id(1) == 0)
    def _():
        state_ref[...] = jnp.zeros_like(state_ref)

    @pl.when((pl.program_id(0) == 0) & (pl.program_id(1) == 0))
    def _():
        p_all_ref[...] = jnp.zeros_like(p_all_ref)

    pj = lax.broadcasted_iota(jnp.int32, (half_c, LANES), 0)
    pl_ = lax.broadcasted_iota(jnp.int32, (half_c, LANES), 1)
    pair_causal = jnp.where(pl_ < c, pl_, pl_ - c) <= jnp.where(pl_ < c, pj, pj + half_c)
    rv = lax.broadcasted_iota(jnp.int32, (hv, hk), 0) // GLA_DV
    ck = lax.broadcasted_iota(jnp.int32, (hv, hk), 1) // GLA_DK
    blockdiag_t = jnp.where(rv == ck, 1.0, 0.0).astype(F32)
    hr = lax.broadcasted_iota(jnp.int32, (SUBLANES, hk), 0)
    hl = lax.broadcasted_iota(jnp.int32, (SUBLANES, hk), 1) // GLA_DK
    head_rows = jnp.where(hr == hl, 1.0, 0.0).astype(BF16)
    eye_v = _eye(hv, BF16)

    ga_hi, ga_lo = _split_bf16(ga_ref[0])
    w_hi, w_lo = _split_bf16(wal_ref[...])
    z = _mm(ga_hi, w_hi) + _mm(ga_hi, w_lo) + _mm(ga_lo, w_hi) + bal_ref[...]
    log_a = (jnp.minimum(z, 0.0) - jnp.log1p(jnp.exp(-jnp.abs(z)))) * (1.0 / GLA_TAU)
    la_hi = log_a.astype(BF16)
    la_r1 = log_a - la_hi.astype(F32)
    la_mid = la_r1.astype(BF16)
    la_lo = (la_r1 - la_mid.astype(F32)).astype(BF16)
    rb = lax.broadcasted_iota(jnp.int32, (GLA_BLOCK, GLA_BLOCK), 0)
    cb = lax.broadcasted_iota(jnp.int32, (GLA_BLOCK, GLA_BLOCK), 1)
    tril = jnp.where((rb >= cb) & (rb // c == cb // c), 1.0, 0.0).astype(BF16)
    b_ref[...] = _mm(tril, la_hi) + _mm(tril, la_mid) + _mm(tril, la_lo)

    def chunk(ci):
        rows = slice(ci * c, (ci + 1) * c)
        p_ref, sc_ref = p_all_ref.at[ci], sc_all_ref.at[ci]
        q = q_ref[0, rows, :] * (GLA_DK ** -0.5)
        k = k_ref[0, rows, :]
        v = v_ref[0, rows, :]
        b = b_ref[rows, :]
        state_t = state_ref[...]
        o_inter = _nt((q * jnp.exp(b)).astype(BF16), state_t.astype(BF16))

        for t in range(c):
            ns = PACK16 * (t // PACK16 + 1)
            slot = 2 * (t % half_c) + t // half_c
            p = jnp.exp(b[t:t + 1, :] - b[:ns, :]) * k[:ns, :] * q[t:t + 1, :]
            p_ref[slot * c:slot * c + ns, :] = p.astype(BF16)
        rows_hs = _nt(head_rows, p_ref[...])
        for jj in range(half_c):
            for h in range(GLA_HEADS):
                sc_ref[h, jj:jj + 1, :] = rows_hs[h:h + 1, jj * LANES:(jj + 1) * LANES]
        vb = v.astype(BF16)
        zero_v = jnp.zeros((c, GLA_DV), BF16)
        o_heads = []
        for h in range(GLA_HEADS):
            vh = vb[:, h * GLA_DV:(h + 1) * GLA_DV]
            v_pair = jnp.concatenate([jnp.concatenate([vh, zero_v], axis=1),
                                      jnp.concatenate([zero_v, vh], axis=1)], axis=0)
            scores = jnp.where(pair_causal, sc_ref[h], 0.0).astype(BF16)
            o_pair = _mm(scores, v_pair)
            o_heads.append(jnp.concatenate([o_pair[:, :GLA_DV], o_pair[:, GLA_DV:]], axis=0))
        o = o_inter + jnp.concatenate(o_heads, axis=-1)

        b_last = b[c - 1:c, :]
        kd = (k * jnp.exp(b_last - b)).astype(BF16)
        v_t = _nt(eye_v, vb).astype(BF16)
        upd_t = _mm(v_t, kd)
        state_ref[...] = jnp.exp(b_last) * state_t + upd_t * blockdiag_t

        gg = gg_ref[0, rows, :]
        outs = []
        for h in range(GLA_HEADS):
            oh = o[:, h * GLA_DV:(h + 1) * GLA_DV]
            ms = jnp.mean(oh * oh, axis=-1, keepdims=True)
            outs.append(oh * lax.rsqrt(ms + EPS))
        on = jnp.concatenate(outs, axis=-1) * gn_ref[...]
        o_ref[0, rows, :] = (on * (gg * jax.nn.sigmoid(gg))).astype(o_ref.dtype)

    for ci in range(GLA_BLOCK // c):
        chunk(ci)


def _gla(proj3, wal, bal, gn):
    bsz, seq, _ = proj3.shape
    nb = seq // GLA_BLOCK

    def col(width, off):
        return pl.BlockSpec((1, GLA_BLOCK, width), lambda b, i: (b, i, off // width))

    return pl.pallas_call(
        _gla_kernel,
        grid=(bsz, nb),
        in_specs=[
            col(128, COL_GQ), col(128, COL_GK), col(256, COL_GV), col(256, COL_GG), col(128, COL_GA),
            pl.BlockSpec((128, 128), lambda b, i: (0, 0)),
            pl.BlockSpec((1, 128), lambda b, i: (0, 0)),
            pl.BlockSpec((1, 256), lambda b, i: (0, 0)),
        ],
        out_specs=pl.BlockSpec((1, GLA_BLOCK, 256), lambda b, i: (b, i, 0)),
        out_shape=jax.ShapeDtypeStruct((bsz, seq, GLA_HEADS * GLA_DV), BF16),
        scratch_shapes=[
            pltpu.VMEM((GLA_HEADS * GLA_DV, GLA_HEADS * GLA_DK), F32),
            pltpu.VMEM((GLA_BLOCK // GLA_CHUNK, GLA_HEADS, GLA_CHUNK // 2, 2 * GLA_CHUNK), F32),
            pltpu.VMEM((GLA_BLOCK // GLA_CHUNK, GLA_CHUNK * GLA_CHUNK, GLA_HEADS * GLA_DK), BF16),
            pltpu.VMEM((GLA_BLOCK, GLA_HEADS * GLA_DK), F32),
        ],
        compiler_params=pltpu.CompilerParams(
            dimension_semantics=("arbitrary", "arbitrary"), vmem_limit_bytes=VMEM_LIMIT),
        name="gla",
    )(proj3, proj3, proj3, proj3, proj3, wal, bal, gn)


DSA_BISECT_STEPS = 14
DSA_ACCUMULATORS = 4
DSA_WALK_UNCONDITIONAL = 2


def _dsa_kernel(dq_ref, kv_ref, iq_ref, ikw_ref, btile_ref, o_ref,
                sc_ref, lg_ref, p_ref, kb_ref, vb_ref, mp_ref, lp_ref, acc_ref, cut_ref, *, n_sel):
    j = pl.program_id(1)
    nkc = ((j + 1) * QB + KC - 1) // KC
    seq = kv_ref.shape[1]
    ksel = float(n_sel)

    @pl.when(j == 0)
    def _():
        kv = kv_ref[0]
        kb_ref[...] = kv[:, :HEAD_DIM].astype(BF16)
        vb_ref[...] = kv[:, HEAD_DIM:].astype(BF16)

    def rows(c):
        return pl.ds(pl.multiple_of(c * KC, KC), KC)

    wide = DSA_ACCUMULATORS * SUBLANES

    def fold(op, x):
        return op(x.reshape(KC // wide, wide, QB), axis=0)

    def all8(op, xw, roll_op):
        x8 = op(xw.reshape(DSA_ACCUMULATORS, SUBLANES, QB), axis=0)
        for shift in (4, 2, 1):
            x8 = roll_op(x8, pltpu.roll(x8, shift, 0))
        return x8

    def widen(x8):
        return jnp.concatenate([x8] * DSA_ACCUMULATORS, axis=0)

    zeros8 = jnp.zeros((SUBLANES, QB), F32)
    zerosw = jnp.zeros((wide, QB), F32)
    infw = jnp.full((wide, QB), jnp.inf, F32)

    ikw_q = ikw_ref[0, pl.ds(pl.multiple_of(j * QB, QB), QB), :]
    w_t = ikw_q.T[IDX_DIM:IDX_DIM + IDX_HEADS, :] * (IDX_HEADS ** -0.5 * IDX_DIM ** -0.5)
    iq = iq_ref[0].astype(BF16)
    iq_heads = [iq[:, h * IDX_DIM:(h + 1) * IDX_DIM] for h in range(IDX_HEADS)]

    s_loc = lax.broadcasted_iota(jnp.int32, (KC, QB), 0)
    t_glob = j * QB + lax.broadcasted_iota(jnp.int32, (KC, QB), 1)
    s_loc_f = s_loc.astype(F32)

    def score_chunk(c, st):
        mn8, mx8, ge8, gt8 = st
        ik = ikw_ref[0, rows(c), :][:, :IDX_DIM].astype(BF16)
        acc = jnp.zeros((KC, QB), F32)
        for h in range(IDX_HEADS):
            rel = _nt(ik, iq_heads[h])
            acc = acc + jnp.maximum(rel, 0.0) * w_t[h:h + 1, :]
        adm = s_loc + c * KC <= t_glob
        blk = jnp.where(adm, acc, -jnp.inf)
        sc_ref[rows(c), :] = blk
        mn8 = jnp.minimum(mn8, fold(jnp.min, jnp.where(adm, acc, jnp.inf)))
        mx8 = jnp.maximum(mx8, fold(jnp.max, blk))
        ge8 = ge8 + fold(jnp.sum, jnp.where(blk >= 0.0, 1.0, 0.0))
        gt8 = gt8 + fold(jnp.sum, jnp.where(blk > 0.0, 1.0, 0.0))
        return mn8, mx8, ge8, gt8

    mn8, mx8, ge8, gt8 = lax.fori_loop(0, nkc, score_chunk, (infw, -infw, zerosw, zerosw))
    row_min = all8(jnp.min, mn8, jnp.minimum)
    row_max = all8(jnp.max, mx8, jnp.maximum)
    c_ge0, c_gt0 = all8(jnp.sum, ge8, jnp.add), all8(jnp.sum, gt8, jnp.add)

    def blocks(c):
        return sc_ref[rows(c), :].reshape(KC // wide, wide, QB)

    def count_ge(x8):
        xw = widen(x8)[None]

        def body(c, cw):
            return cw + jnp.sum(jnp.where(blocks(c) >= xw, 1.0, 0.0), axis=0)
        return all8(jnp.sum, lax.fori_loop(0, nkc, body, zerosw), jnp.add)

    def min_ge(x8):
        xw = widen(x8)[None]

        def body(c, mw):
            blk = blocks(c)
            return jnp.minimum(mw, jnp.min(jnp.where(blk >= xw, blk, jnp.inf), axis=0))
        return all8(jnp.min, lax.fori_loop(0, nkc, body, infw), jnp.minimum)

    def pass_gt(x8):
        xw = widen(x8)[None]

        def body(c, st):
            cw, mw = st
            blk = blocks(c)
            hit = blk > xw
            return (cw + jnp.sum(jnp.where(hit, 1.0, 0.0), axis=0),
                    jnp.minimum(mw, jnp.min(jnp.where(hit, blk, jnp.inf), axis=0)))
        cw, mw = lax.fori_loop(0, nkc, body, (zerosw, infw))
        return all8(jnp.sum, cw, jnp.add), all8(jnp.min, mw, jnp.minimum)

    n_adm = (j * QB + lax.broadcasted_iota(jnp.int32, (SUBLANES, QB), 1) + 1).astype(F32)
    at_zero = (c_gt0 < ksel) & (c_ge0 >= ksel)
    above = c_gt0 >= ksel
    lo = jnp.where(above | at_zero, 0.0, row_min)
    c_lo = jnp.where(above | at_zero, c_ge0, n_adm)
    settled = at_zero | (n_adm <= ksel)
    hi = jnp.where(settled, lo, jnp.where(above, row_max, 0.0))

    def bisect(_, st):
        lo, hi, c_lo = st
        mid = lo + (hi - lo) * 0.5
        cnt = count_ge(mid)
        up = cnt >= ksel
        return jnp.where(up, mid, lo), jnp.where(up, hi, mid), jnp.where(up, cnt, c_lo)

    lo, _, c_lo = lax.fori_loop(0, DSA_BISECT_STEPS, bisect, (lo, hi, c_lo))

    v0 = min_ge(lo)
    done0 = jnp.where(settled | (c_lo == ksel), 1.0, 0.0)
    c_gt_init = jnp.where(at_zero, c_gt0, 0.0)

    def walk_cond(st):
        return jnp.min(st[3]) < 0.5

    def walk_body(st):
        v, c_ge, c_gt, dn = st
        cnt, vnext = pass_gt(v)
        live = dn < 0.5
        fin = live & (cnt < ksel)
        step = live & (cnt >= ksel)
        c_gt = jnp.where(fin, cnt, c_gt)
        dn = jnp.where(fin, 1.0, dn)
        v = jnp.where(step, vnext, v)
        c_ge = jnp.where(step, cnt, c_ge)
        return v, c_ge, c_gt, dn

    walk = (v0, c_lo, c_gt_init, done0)
    for _ in range(DSA_WALK_UNCONDITIONAL):
        walk = walk_body(walk)
    tau8, c_ge, c_gt, _ = lax.while_loop(walk_cond, walk_body, walk)

    need = ksel - c_gt
    cut_ref[...] = jnp.full(cut_ref.shape, float(seq), F32)

    @pl.when(jnp.max(c_ge) > ksel)
    def _():
        s_grp = lax.broadcasted_iota(jnp.int32, (KC // wide, wide, QB), 0) * wide
        s_sub = lax.broadcasted_iota(jnp.int32, (KC // wide, wide, QB), 1)
        s_idx = (s_grp + s_sub).astype(F32)
        tauw = widen(tau8)[None]

        def count_ties_below(m8):
            mw = widen(m8)[None]

            def body(c, cw):
                hit = (blocks(c) == tauw) & (s_idx + (c * KC).astype(F32) < mw)
                return cw + jnp.sum(jnp.where(hit, 1.0, 0.0), axis=0)
            return all8(jnp.sum, lax.fori_loop(0, nkc, body, zerosw), jnp.add)

        def idx_bisect(_, lh):
            lo_m, hi_m = lh
            mid = jnp.floor((lo_m + hi_m) * 0.5)
            ok = count_ties_below(mid) >= need
            return jnp.where(ok, lo_m, mid), jnp.where(ok, mid, hi_m)

        _, hi_m = lax.fori_loop(0, int(math.log2(seq)) + 1, idx_bisect,
                                (zeros8, zeros8 + float(seq)))
        cut_ref[...] = jnp.where(c_ge > ksel, hi_m, float(seq))

    tau = tau8[0:1, :]
    cut = cut_ref[0:1, :]

    q = (dq_ref[0] * (HEAD_DIM ** -0.5)).astype(BF16)
    q4 = jnp.concatenate([q[:, h * HEAD_DIM:(h + 1) * HEAD_DIM] for h in range(DSA_HEADS)],
                         axis=0)
    mp_ref[...] = jnp.full(mp_ref.shape, NEG_BIG, F32)
    lp_ref[...] = jnp.zeros(lp_ref.shape, F32)

    half = seq // 2
    second_half = (j + 1) * QB > half

    def cols(c):
        return pl.ds(pl.multiple_of(c * KC, KC), KC)

    def lane_fold(op, x):
        out = x[:, :LANES]
        for g in range(1, KC // LANES):
            out = op(out, x[:, g * LANES:(g + 1) * LANES])
        return out

    def logits_chunk(c, carry):
        blk = sc_ref[rows(c), :]
        sidx = s_loc_f + (c * KC).astype(F32)
        sel = (blk > tau) | ((blk == tau) & (sidx < cut))
        neg_t = jnp.where(sel, 0.0, NEG_BIG).T
        lg4 = _nt(q4, kb_ref[rows(c), :])
        which = [[jnp.clip((c * (KC // WINDOW) + b) - (j * (QB // WINDOW) + a) + 2, 0, 2)
                  for b in range(KC // WINDOW)] for a in range(QB // WINDOW)]
        for h in range(DSA_HEADS):
            hq = slice(h * QB, (h + 1) * QB)
            bias = jnp.concatenate(
                [jnp.concatenate([btile_ref[h, w] for w in which_a], axis=1)
                 for which_a in which], axis=0)
            lg = lg4[hq, :] + bias + neg_t
            lg_ref[hq, cols(c)] = lg
            mp_ref[h] = jnp.maximum(mp_ref[h], lane_fold(jnp.maximum, lg))
        return carry

    lax.fori_loop(0, nkc, logits_chunk, 0)
    m_rows = [jnp.max(mp_ref[h], axis=-1, keepdims=True) for h in range(DSA_HEADS)]

    def probs_chunk(c, carry):
        for h in range(DSA_HEADS):
            hq = slice(h * QB, (h + 1) * QB)
            p = jnp.exp(lg_ref[hq, cols(c)] - m_rows[h])
            lp_ref[h] = lp_ref[h] + lane_fold(jnp.add, p)
            p_ref[hq, cols(c)] = p.astype(BF16)
        return carry

    def zero_chunk(c, carry):
        p_ref[:, cols(c)] = jnp.zeros((DSA_HEADS * QB, KC), BF16)
        return carry

    lax.fori_loop(0, nkc, probs_chunk, 0)
    lax.fori_loop(nkc, jnp.where(second_half, seq // KC, half // KC), zero_chunk, 0)
    acc_ref[...] = _mm(p_ref[:, :half], vb_ref[:half, :])

    @pl.when(second_half)
    def _():
        acc_ref[...] = acc_ref[...] + _mm(p_ref[:, half:], vb_ref[half:, :])

    outs = []
    for h in range(DSA_HEADS):
        l = jnp.sum(lp_ref[h], axis=-1, keepdims=True)
        outs.append(acc_ref[h * QB:(h + 1) * QB, :] / l)
    o_ref[0] = jnp.concatenate(outs, axis=-1).astype(o_ref.dtype)


def _dsa(proj3, btiles):
    bsz, seq, _ = proj3.shape
    n_sel = min(DSA_TOPK_MAX, seq // 4)
    return pl.pallas_call(
        functools.partial(_dsa_kernel, n_sel=n_sel),
        grid=(bsz, seq // QB),
        in_specs=[
            pl.BlockSpec((1, QB, 256), lambda b, j: (b, j, COL_DQ // 256)),
            pl.BlockSpec((1, seq, 128), lambda b, j: (b, 0, COL_DKV // 128)),
            pl.BlockSpec((1, QB, 512), lambda b, j: (b, j, COL_IQ // 512)),
            pl.BlockSpec((1, seq, 128), lambda b, j: (b, 0, COL_IKW // 128)),
            pl.BlockSpec((DSA_HEADS, 3, WINDOW, WINDOW), lambda b, j: (0, 0, 0, 0)),
        ],
        out_specs=pl.BlockSpec((1, QB, 256), lambda b, j: (b, j, 0)),
        out_shape=jax.ShapeDtypeStruct((bsz, seq, DSA_HEADS * HEAD_DIM), BF16),
        scratch_shapes=[
            pltpu.VMEM((seq, QB), F32),
            pltpu.VMEM((DSA_HEADS * QB, seq), F32),
            pltpu.VMEM((DSA_HEADS * QB, seq), BF16),
            pltpu.VMEM((seq, HEAD_DIM), BF16),
            pltpu.VMEM((seq, HEAD_DIM), BF16),
            pltpu.VMEM((DSA_HEADS, QB, LANES), F32),
            pltpu.VMEM((DSA_HEADS, QB, LANES), F32),
            pltpu.VMEM((DSA_HEADS * QB, HEAD_DIM), F32),
            pltpu.VMEM((SUBLANES, QB), F32),
        ],
        compiler_params=pltpu.CompilerParams(
            dimension_semantics=("arbitrary", "arbitrary"), vmem_limit_bytes=VMEM_LIMIT),
        name="dsa",
    )(proj3, proj3, proj3, proj3, btiles)


def _swa_kernel(sink_ref, q_ref, kc_ref, kp_ref, vc_ref, vp_ref, bias_ref, o_ref):
    n = pl.program_id(1)
    q = (q_ref[0] * (HEAD_DIM ** -0.5)).astype(BF16)
    k2 = jnp.concatenate([kp_ref[0], kc_ref[0]], axis=0).astype(BF16)
    v2 = jnp.concatenate([vp_ref[0], vc_ref[0]], axis=0).astype(BF16)
    qi = lax.broadcasted_iota(jnp.int32, (WINDOW, 2 * WINDOW), 0)
    kj = lax.broadcasted_iota(jnp.int32, (WINDOW, 2 * WINDOW), 1)
    dist = qi + WINDOW - kj
    mask = (dist >= 0) & (dist < WINDOW) & ((kj >= WINDOW) | (n > 0))
    outs = []
    for h in range(SWA_HEADS):
        kvh = h // SWA_GROUP
        kh = k2[:, kvh * HEAD_DIM:(kvh + 1) * HEAD_DIM]
        vh = v2[:, kvh * HEAD_DIM:(kvh + 1) * HEAD_DIM]
        lg = _nt(q[:, h * HEAD_DIM:(h + 1) * HEAD_DIM], kh) + bias_ref[h]
        lg = jnp.where(mask, lg, -jnp.inf)
        sink = sink_ref[h]
        m = jnp.maximum(jnp.max(lg, axis=-1, keepdims=True), sink)
        e = jnp.exp(lg - m)
        den = jnp.sum(e, axis=-1, keepdims=True) + jnp.exp(sink - m)
        outs.append(_mm((e / den).astype(BF16), vh))
    o_ref[0] = jnp.concatenate(outs, axis=-1).astype(o_ref.dtype)


def _swa(proj3, sinks, bias_nat):
    bsz, seq, _ = proj3.shape
    return pl.pallas_call(
        _swa_kernel,
        grid=(bsz, seq // WINDOW),
        in_specs=[
            pl.BlockSpec(memory_space=pltpu.SMEM),
            pl.BlockSpec((1, WINDOW, 512), lambda b, n: (b, n, COL_SQ // 512)),
            pl.BlockSpec((1, WINDOW, 128), lambda b, n: (b, n, COL_SK // 128)),
            pl.BlockSpec((1, WINDOW, 128), lambda b, n: (b, jnp.maximum(n - 1, 0), COL_SK // 128)),
            pl.BlockSpec((1, WINDOW, 128), lambda b, n: (b, n, COL_SV // 128)),
            pl.BlockSpec((1, WINDOW, 128), lambda b, n: (b, jnp.maximum(n - 1, 0), COL_SV // 128)),
            pl.BlockSpec((SWA_HEADS, WINDOW, 2 * WINDOW), lambda b, n: (0, 0, 0)),
        ],
        out_specs=pl.BlockSpec((1, WINDOW, 512), lambda b, n: (b, n, 0)),
        out_shape=jax.ShapeDtypeStruct((bsz, seq, SWA_HEADS * HEAD_DIM), BF16),
        compiler_params=pltpu.CompilerParams(
            dimension_semantics=("arbitrary", "arbitrary"), vmem_limit_bytes=VMEM_LIMIT),
        name="swa",
    )(sinks, proj3, proj3, proj3, proj3, proj3, bias_nat)


ROUTE_OFF = N_GROUPS
ROUTE_ROWS = 48


def _split_bf16(x):
    hi = x.astype(BF16)
    lo = (x - hi.astype(F32)).astype(BF16)
    return hi, lo


def _outproj_router_kernel(h_ref, og_ref, od_ref, os_ref, wo_ref, gffn_ref, wr_hi_ref, wr_lo_ref,
                           br_ref, h1_ref, xn_ref, ri_ref, rf_ref, cnt_ref, run_ref):
    tm = h_ref.shape[0]

    @pl.when(pl.program_id(0) == 0)
    def _():
        run_ref[...] = jnp.zeros_like(run_ref)

    o = jnp.concatenate([og_ref[...], od_ref[...], os_ref[...]], axis=-1)
    h1 = h_ref[...] + _mm(o, wo_ref[...])
    h1_ref[...] = h1
    xn = _rms(h1, gffn_ref[...])
    xn_ref[...] = _pack_bf16_pairs(xn)

    x_hi, x_lo = _split_bf16(xn)
    lg = (_mm(x_hi, wr_hi_ref[...]) + _mm(x_lo, wr_hi_ref[...]) + _mm(x_hi, wr_lo_ref[...])
          + br_ref[...]).T[:ROUTE_ROWS, :]

    row = lax.broadcasted_iota(jnp.int32, lg.shape, 0)
    row_f = row.astype(F32)
    ninf = -jnp.inf

    def first_max(x):
        m = jnp.max(x, axis=0, keepdims=True)
        idx = jnp.min(jnp.where(x == m, row_f, float(ROUTE_ROWS)), axis=0, keepdims=True)
        return m, idx

    gl = jnp.where(row < N_GROUPS, lg, ninf)
    gmax, gsel = first_max(gl)
    g_w = 1.0 / jnp.sum(jnp.exp(gl - gmax), axis=0, keepdims=True)
    e_lo = ROUTE_OFF + EXPERTS_PER_GROUP * gsel
    el = jnp.where((row_f >= e_lo) & (row_f < e_lo + EXPERTS_PER_GROUP), lg, ninf)
    m1, i1 = first_max(el)
    eden = jnp.sum(jnp.exp(el - m1), axis=0, keepdims=True)
    m2, i2 = first_max(jnp.where(row_f == i1, ninf, el))
    p1 = 1.0 / eden
    p2 = jnp.exp(m2 - m1) / eden
    gate1 = g_w * p1 / (p1 + p2)
    gate2 = g_w * p2 / (p1 + p2)

    onehot = jnp.where((row_f == i1) | (row_f == i2), 1.0, 0.0)
    rr = lax.broadcasted_iota(jnp.int32, (tm, tm), 0)
    cc = lax.broadcasted_iota(jnp.int32, (tm, tm), 1)
    earlier = jnp.where(rr < cc, 1.0, 0.0).astype(BF16)
    run = run_ref[...]
    before = _mm(onehot.astype(BF16), earlier) + jnp.concatenate([run] * (tm // LANES), axis=1)
    rank1 = jnp.sum(jnp.where(row_f == i1, before, 0.0), axis=0, keepdims=True)
    rank2 = jnp.sum(jnp.where(row_f == i2, before, 0.0), axis=0, keepdims=True)
    run_ref[...] = run + jnp.sum(onehot, axis=1, keepdims=True)
    cnt_ref[...] = run_ref[...]

    out_row = lax.broadcasted_iota(jnp.int32, (SUBLANES, tm), 0)
    ints = jnp.where(out_row == 0, i1 - ROUTE_OFF,
                     jnp.where(out_row == 1, i2 - ROUTE_OFF,
                               jnp.where(out_row == 2, rank1,
                                         jnp.where(out_row == 3, rank2, 0.0))))
    ri_ref[...] = ints.astype(jnp.int32)
    rf_ref[...] = jnp.where(out_row == 0, gate1, jnp.where(out_row == 1, gate2, 0.0))


def _outproj_router(h, og, od, os_, wo, gffn, wr_hi, wr_lo, br):
    t = h.shape[0]
    tm = ROUTER_TILE
    row = lambda w: pl.BlockSpec((tm, w), lambda i: (i, 0))
    col = lambda r: pl.BlockSpec((r, tm), lambda i: (0, i))
    full = lambda a, b: pl.BlockSpec((a, b), lambda i: (0, 0))
    return pl.pallas_call(
        _outproj_router_kernel,
        grid=(t // tm,),
        in_specs=[row(D_MODEL), row(256), row(256), row(512), full(D_MODEL, D_MODEL),
                  full(1, D_MODEL), full(D_MODEL, LANES), full(D_MODEL, LANES), full(1, LANES)],
        out_specs=[row(D_MODEL), row(HALF_MODEL), col(SUBLANES), col(SUBLANES),
                   full(ROUTE_ROWS, LANES)],
        out_shape=[
            jax.ShapeDtypeStruct((t, D_MODEL), F32),
            jax.ShapeDtypeStruct((t, HALF_MODEL), jnp.uint32),
            jax.ShapeDtypeStruct((SUBLANES, t), jnp.int32),
            jax.ShapeDtypeStruct((SUBLANES, t), F32),
            jax.ShapeDtypeStruct((ROUTE_ROWS, LANES), F32),
        ],
        scratch_shapes=[pltpu.VMEM((ROUTE_ROWS, LANES), F32)],
        compiler_params=pltpu.CompilerParams(
            dimension_semantics=("arbitrary",), vmem_limit_bytes=VMEM_LIMIT),
        name="outproj_router",
    )(h, og, od, os_, wo, gffn, wr_hi, wr_lo, br)


def _dispatch_kernel(pos_ref, xn_ref, buf_in_ref, buf_ref, sem):
    del buf_in_ref
    tm = xn_ref.shape[0]

    def row_copy(r, k):
        dst = pos_ref[0, 0, 2 * r + k]
        return pltpu.make_async_copy(xn_ref.at[pl.ds(r, 1)], buf_ref.at[pl.ds(dst, 1)], sem)

    def issue(g, carry):
        for u in range(DMA_UNROLL):
            row_copy(g * DMA_UNROLL + u, 0).start()
            row_copy(g * DMA_UNROLL + u, 1).start()
        return carry

    def drain(g, carry):
        for u in range(DMA_UNROLL):
            row_copy(g * DMA_UNROLL + u, 0).wait()
            row_copy(g * DMA_UNROLL + u, 1).wait()
        return carry

    lax.fori_loop(0, tm // DMA_UNROLL, issue, 0)
    lax.fori_loop(0, tm // DMA_UNROLL, drain, 0)


def _dispatch(pos3, xn, buf0):
    t = xn.shape[0]
    tm = TOKEN_TILE
    return pl.pallas_call(
        _dispatch_kernel,
        grid=(t // tm,),
        in_specs=[
            pl.BlockSpec((1, 1, 2 * tm), lambda i: (i, 0, 0), memory_space=pltpu.SMEM),
            pl.BlockSpec((tm, HALF_MODEL), lambda i: (i, 0)),
            pl.BlockSpec(memory_space=pl.ANY),
        ],
        out_specs=pl.BlockSpec(memory_space=pl.ANY),
        out_shape=jax.ShapeDtypeStruct(buf0.shape, buf0.dtype),
        scratch_shapes=[pltpu.SemaphoreType.DMA(())],
        input_output_aliases={2: 0},
        compiler_params=pltpu.CompilerParams(
            dimension_semantics=("arbitrary",), vmem_limit_bytes=VMEM_LIMIT),
        name="dispatch",
    )(pos3, xn, buf0)


def _expert_kernel(be_ref, nu_ref, rows_ref, slot_ref, next_ref, x_ref, wg_hbm, wu_hbm, wd_hbm,
                   y_ref, wg_f, wu_f, wd_f, wg_s, wu_s, wd_s, sems, *, layer):
    i = pl.program_id(0)
    used = i < nu_ref[0]
    expert = be_ref[i]
    new_expert = (i == 0) | (expert != be_ref[jnp.maximum(i - 1, 0)])
    slot = slot_ref[i]

    def weight_copies(e, s):
        return (pltpu.make_async_copy(wg_hbm.at[layer, e], wg_f.at[s], sems.at[s, 0]),
                pltpu.make_async_copy(wu_hbm.at[layer, e], wu_f.at[s], sems.at[s, 1]),
                pltpu.make_async_copy(wd_hbm.at[layer, e], wd_f.at[s], sems.at[s, 2]))

    @pl.when(i == 0)
    def _():
        for cp in weight_copies(expert, slot):
            cp.start()

    @pl.when(used & new_expert)
    def _():
        for cp in weight_copies(expert, slot):
            cp.wait()
        wg_s[...] = wg_f[slot].astype(BF16)
        wu_s[...] = wu_f[slot].astype(BF16)
        wd_s[...] = wd_f[slot].astype(BF16)
        nxt = next_ref[i]

        @pl.when(nxt >= 0)
        def _():
            for cp in weight_copies(nxt, 1 - slot):
                cp.start()

    def ffn(nrows):
        x_lo, x_hi = (part.astype(BF16) for part in _unpack_bf16_pairs(x_ref[:nrows, :]))
        g = _mm(x_lo, wg_s[:HALF_MODEL, :]) + _mm(x_hi, wg_s[HALF_MODEL:, :])
        u = _mm(x_lo, wu_s[:HALF_MODEL, :]) + _mm(x_hi, wu_s[HALF_MODEL:, :])
        hmid = (g * jax.nn.sigmoid(g)) * u
        y_ref[:nrows, :] = _pack_bf16_pairs(_mm(hmid.astype(BF16), wd_s[...]))

    half_rows = EXPERT_ROWS // 2
    short = rows_ref[i] <= half_rows

    @pl.when(used & jnp.logical_not(short))
    def _():
        ffn(EXPERT_ROWS)

    @pl.when(used & short)
    def _():
        ffn(half_rows)
        y_ref[half_rows:, :] = jnp.zeros((EXPERT_ROWS - half_rows, HALF_MODEL), y_ref.dtype)

    @pl.when(i >= nu_ref[0])
    def _():
        y_ref[...] = jnp.zeros_like(y_ref)


def _experts(blk_expert, n_used, blk_rows, buf, wg, wu, wd, layer):
    nrows = buf.shape[0]
    nblk = nrows // EXPERT_ROWS
    idx = jnp.arange(nblk, dtype=jnp.int32)
    change = (idx == 0) | (blk_expert != jnp.roll(blk_expert, 1))
    slot = (jnp.cumsum(change.astype(jnp.int32)) - 1) % 2
    later_run = (idx[None, :] > idx[:, None]) & change[None, :] & (idx[None, :] < n_used[0])
    nxt = jnp.where(jnp.any(later_run, axis=1), blk_expert[jnp.argmax(later_run, axis=1)], -1)
    block = lambda i, *_: (i, 0)
    any_space = pl.BlockSpec(memory_space=pl.ANY)
    return pl.pallas_call(
        functools.partial(_expert_kernel, layer=layer),
        grid_spec=pltpu.PrefetchScalarGridSpec(
            num_scalar_prefetch=5,
            grid=(nblk,),
            in_specs=[pl.BlockSpec((EXPERT_ROWS, HALF_MODEL), block),
                      any_space, any_space, any_space],
            out_specs=pl.BlockSpec((EXPERT_ROWS, HALF_MODEL), block),
            scratch_shapes=[
                pltpu.VMEM((2, D_MODEL, D_EXPERT), F32),
                pltpu.VMEM((2, D_MODEL, D_EXPERT), F32),
                pltpu.VMEM((2, D_EXPERT, D_MODEL), F32),
                pltpu.VMEM((D_MODEL, D_EXPERT), BF16),
                pltpu.VMEM((D_MODEL, D_EXPERT), BF16),
                pltpu.VMEM((D_EXPERT, D_MODEL), BF16),
                pltpu.SemaphoreType.DMA((2, 3)),
            ],
        ),
        out_shape=jax.ShapeDtypeStruct((nrows, HALF_MODEL), jnp.uint32),
        compiler_params=pltpu.CompilerParams(
            dimension_semantics=("arbitrary",), vmem_limit_bytes=VMEM_LIMIT),
        name="experts",
    )(blk_expert, n_used, blk_rows, slot.astype(jnp.int32), nxt.astype(jnp.int32),
      buf, wg, wu, wd)


def _combine_ple_kernel(pos_ref, pos_next_ref, h1_ref, p_ref, rf_ref, yb_ref, wple_ref, gple_ref,
                        wpg_ref, gfin_ref, o_ref, ybuf, sems, *, final_norm):
    tm = h1_ref.shape[0]
    i = pl.program_id(0)
    last = pl.num_programs(0) - 1
    slot = lax.rem(i, 2)

    def row_copy(tile_pos_ref, s, r, k):
        src = tile_pos_ref[0, 0, 2 * r + k]
        return pltpu.make_async_copy(
            yb_ref.at[pl.ds(src, 1)], ybuf.at[s, k, pl.ds(r, 1)], sems.at[s])

    def looped(tile_pos_ref, s, op):
        def body(g, carry):
            for u in range(DMA_UNROLL):
                op(row_copy(tile_pos_ref, s, g * DMA_UNROLL + u, 0))
                op(row_copy(tile_pos_ref, s, g * DMA_UNROLL + u, 1))
            return carry
        lax.fori_loop(0, tm // DMA_UNROLL, body, 0)

    @pl.when(i == 0)
    def _():
        looped(pos_ref, 0, lambda cp: cp.start())

    looped(pos_ref, slot, lambda cp: cp.wait())
    for r in range(tm):
        row_copy(pos_next_ref, 1 - slot, r, 0).start()
        row_copy(pos_next_ref, 1 - slot, r, 1).start()
    e = _rms(_mm(p_ref[...].astype(BF16), wple_ref[...]), gple_ref[...])

    rf = rf_ref[...]
    y0_lo, y0_hi = _unpack_bf16_pairs(ybuf[slot, 0])
    y1_lo, y1_hi = _unpack_bf16_pairs(ybuf[slot, 1])
    g0, g1 = rf[:, 0:1], rf[:, 1:2]
    moe = jnp.concatenate([y0_lo * g0 + y1_lo * g1, y0_hi * g0 + y1_hi * g1], axis=-1)
    h2 = h1_ref[...] + moe
    h3 = h2 + e * jax.nn.sigmoid(_mm(h2.astype(BF16), wpg_ref[...]))
    if final_norm:
        h3 = _rms(h3, gfin_ref[...])
    o_ref[...] = h3

    @pl.when(i == last)
    def _():
        looped(pos_next_ref, 1 - slot, lambda cp: cp.wait())


def _combine_ple(pos3, h1, p, rf, yb, wple, gple, wpg, gfin, layer, final_norm):
    t = h1.shape[0]
    tm = TOKEN_TILE
    row = lambda w: pl.BlockSpec((tm, w), lambda i: (i, 0))
    full = lambda a, b: pl.BlockSpec((a, b), lambda i: (0, 0))
    return pl.pallas_call(
        functools.partial(_combine_ple_kernel, final_norm=final_norm),
        grid=(t // tm,),
        in_specs=[
            pl.BlockSpec((1, 1, 2 * tm), lambda i: (i, 0, 0), memory_space=pltpu.SMEM),
            pl.BlockSpec((1, 1, 2 * tm), lambda i: (jnp.minimum(i + 1, t // tm - 1), 0, 0),
                         memory_space=pltpu.SMEM),
            row(D_MODEL),
            pl.BlockSpec((tm, PLE_DIM), lambda i: (layer * (t // tm) + i, 0)),
            row(LANES),
            pl.BlockSpec(memory_space=pl.ANY),
            full(PLE_DIM, D_MODEL), full(1, D_MODEL), full(D_MODEL, D_MODEL), full(1, D_MODEL),
        ],
        out_specs=row(D_MODEL),
        out_shape=jax.ShapeDtypeStruct((t, D_MODEL), F32),
        scratch_shapes=[pltpu.VMEM((2, 2, tm, HALF_MODEL), jnp.uint32),
                        pltpu.SemaphoreType.DMA((2,))],
        compiler_params=pltpu.CompilerParams(
            dimension_semantics=("arbitrary",), vmem_limit_bytes=VMEM_LIMIT),
        name="combine_ple",
    )(pos3, pos3, h1, p, rf, yb, wple, gple, wpg, gfin)


def _rel_bucket(dist):
    n = jnp.maximum(dist, 0)
    max_exact = REL_BUCKETS // 2
    nf = jnp.maximum(n, 1).astype(F32)
    large = max_exact + (jnp.log(nf / max_exact) / math.log(REL_MAX_DIST / max_exact)
                         * (REL_BUCKETS - max_exact)).astype(jnp.int32)
    large = jnp.minimum(large, REL_BUCKETS - 1)
    return jnp.where(n < max_exact, n, large)


def _bias_tiles(rel_bias):
    qi = jnp.arange(WINDOW)[:, None]
    kj = jnp.arange(2 * WINDOW)[None, :]
    bucket = _rel_bucket(qi + WINDOW - kj)
    onehot = (bucket[..., None] == jnp.arange(REL_BUCKETS)).astype(F32)
    nat = jnp.einsum('qkb,bh->hqk', onehot, rel_bias, precision=HIGHEST)
    dsa = nat[:DSA_HEADS]
    far = jnp.broadcast_to(rel_bias[REL_BUCKETS - 1, :DSA_HEADS][:, None, None],
                           (DSA_HEADS, WINDOW, WINDOW))
    return jnp.stack([far, dsa[:, :, :WINDOW], dsa[:, :, WINDOW:]], axis=1), nat[DSA_HEADS:]


def _pack_w_in(w):
    sizes = (128, 128, 256, 256, 16, 256, 64, 64, 512, 64, 8, 512, 128, 128)
    offs = [0]
    for s in sizes:
        offs.append(offs[-1] + s)
    wt = w.T
    gq, gk, gv, gg, ga, dq, dk, dv, iq, ik, iw, sq, sk, sv = (
        wt[offs[n]:offs[n + 1]] for n in range(len(sizes)))
    z = lambda n: jnp.zeros((n, w.shape[0]), w.dtype)
    packed = jnp.concatenate(
        [iq, sq, gv, gg, dq, gq, gk, ga, z(128 - GLA_RANK), dk, dv, ik, iw,
         z(128 - IDX_DIM - IDX_HEADS), sk, sv], axis=0)
    return packed.astype(BF16)


def kernel(x, p, rel_bias, g_mix, w_in, gla_w_alpha, gla_b_alpha, gla_g_norm, swa_sinks, w_out,
           g_ffn, w_router_group, b_router_group, w_router_expert, b_router_expert, w_expert_gate,
           w_expert_up, w_expert_down, w_ple, g_ple, w_ple_gate, g_final):
    bsz, seq, d = x.shape
    depth = w_in.shape[0]
    t = bsz * seq
    assert d == D_MODEL and t % INPROJ_TILE == 0 and seq % GLA_BLOCK == 0 and seq % KC == 0
    n_blocks = -(-(2 * t) // EXPERT_ROWS) + N_EXPERTS
    dsa_tiles, swa_bias = _bias_tiles(rel_bias)

    h = x.reshape(t, d)
    for i in range(depth):
        proj = _inproj(h, g_mix[i][None, :], _pack_w_in(w_in[i]))
        proj3 = proj.reshape(bsz, seq, D_PROJ)

        wal = jnp.zeros((128, 128), F32).at[:GLA_RANK].set(gla_w_alpha[i])
        og = _gla(proj3, wal, gla_b_alpha[i][None, :],
                  jnp.tile(gla_g_norm[i], GLA_HEADS)[None, :])
        od = _dsa(proj3, dsa_tiles)
        os_ = _swa(proj3, swa_sinks[i], swa_bias)

        w_r = jnp.concatenate([w_router_group[i], w_router_expert[i]], axis=1)
        w_r = jnp.pad(w_r, ((0, 0), (0, LANES - w_r.shape[1])))
        wr_hi, wr_lo = _split_bf16(w_r)
        b_r = jnp.pad(jnp.concatenate([b_router_group[i], b_router_expert[i]]),
                      (0, LANES - N_GROUPS - N_EXPERTS))[None, :]
        h1, xn, ri, rf, cnt = _outproj_router(
            h, og.reshape(t, -1), od.reshape(t, -1), os_.reshape(t, -1), w_out[i].astype(BF16),
            g_ffn[i][None, :], wr_hi, wr_lo, b_r)

        counts = cnt[ROUTE_OFF:ROUTE_OFF + N_EXPERTS, 0].astype(jnp.int32)
        padded = (counts + EXPERT_ROWS - 1) // EXPERT_ROWS * EXPERT_ROWS
        pad_end = jnp.cumsum(padded)
        pad_start = pad_end - padded
        expert_ids = jnp.arange(N_EXPERTS, dtype=jnp.int32)
        slot_start = jnp.sum(
            jnp.where(ri[0:2, :, None] == expert_ids, pad_start, 0), axis=-1)
        pos3 = (slot_start + ri[2:4]).T.reshape(t // TOKEN_TILE, 1, 2 * TOKEN_TILE)
        gates = jnp.pad(rf[0:2].T, ((0, 0), (0, LANES - 2)))
        blk_start = jnp.arange(n_blocks, dtype=jnp.int32) * EXPERT_ROWS
        blk_expert = jnp.minimum(
            jnp.sum((pad_end[None, :] <= blk_start[:, None]).astype(jnp.int32), axis=1),
            N_EXPERTS - 1)
        n_used = (pad_end[-1:] // EXPERT_ROWS).astype(jnp.int32)
        onehot_e = blk_expert[:, None] == expert_ids[None, :]
        rows_before = blk_start - jnp.sum(jnp.where(onehot_e, pad_start[None, :], 0), axis=1)
        blk_rows = jnp.clip(jnp.sum(jnp.where(onehot_e, counts[None, :], 0), axis=1) - rows_before,
                            0, EXPERT_ROWS).astype(jnp.int32)

        buf = _dispatch(pos3, xn, jnp.zeros((n_blocks * EXPERT_ROWS, HALF_MODEL), jnp.uint32))
        yb = _experts(blk_expert, n_used, blk_rows, buf, w_expert_gate, w_expert_up,
                      w_expert_down, i)
        h = _combine_ple(pos3, h1, p.reshape(depth * t, PLE_DIM), gates, yb, w_ple[i].astype(BF16),
                         g_ple[i][None, :], w_ple_gate[i].astype(BF16), g_final[None, :],
                         layer=i, final_norm=(i == depth - 1))
    return h.reshape(bsz, seq, d)
```

```python
import functools
import math

import jax
import jax.numpy as jnp
from jax import lax
from jax.experimental import pallas as pl
from jax.experimental.pallas import tpu as pltpu

F32 = jnp.float32
BF16 = jnp.bfloat16
HIGHEST = lax.Precision.HIGHEST

D_MODEL = 1024
HEAD_DIM = 64
GLA_HEADS = 4
GLA_DK = 32
GLA_DV = 64
GLA_RANK = 16
GLA_TAU = 16.0
GLA_CHUNK = 64
DSA_HEADS = 4
IDX_HEADS = 8
IDX_DIM = 64
DSA_TOPK_MAX = 256
SWA_HEADS = 8
SWA_KV_HEADS = 2
SWA_GROUP = SWA_HEADS // SWA_KV_HEADS
WINDOW = 128
REL_BUCKETS = 32
REL_MAX_DIST = 128
N_GROUPS = 4
EXPERTS_PER_GROUP = 8
N_EXPERTS = N_GROUPS * EXPERTS_PER_GROUP
D_EXPERT = 512
PLE_DIM = 256
EPS = 1e-6

LANES = 128
SUBLANES = 8
PACK16 = 16

COL_IQ, COL_SQ = 0, 512
COL_GV, COL_GG, COL_DQ = 1024, 1280, 1536
COL_GQ, COL_GK, COL_GA, COL_DKV, COL_IKW, COL_SK, COL_SV = 1792, 1920, 2048, 2176, 2304, 2432, 2560
D_PROJ = 2688

TOKEN_TILE = 256
INPROJ_TILE = 512
ROUTER_TILE = 512
DISPATCH_TILE = 1024
EXPERT_ROWS = 512
GLA_BLOCK = 256
QB = 256
KC = 256
DMA_UNROLL = 8
NEG_BIG = -1e30
VMEM_LIMIT = 48 * 1024 * 1024


def _nt(a, b, precision=None):
    return lax.dot_general(a, b, (((1,), (1,)), ((), ())), precision=precision,
                           preferred_element_type=F32)


def _mm(a, b, precision=None):
    return jnp.dot(a, b, precision=precision, preferred_element_type=F32)


def _eye(n, dtype):
    r = lax.broadcasted_iota(jnp.int32, (n, n), 0)
    c = lax.broadcasted_iota(jnp.int32, (n, n), 1)
    return jnp.where(r == c, 1.0, 0.0).astype(dtype)


def _rms(x, g):
    return x * lax.rsqrt(jnp.mean(x * x, axis=-1, keepdims=True) + EPS) * g


HALF_MODEL = D_MODEL // 2
HIGH16 = 0xFFFF0000


def _pack_bf16_pairs(x):
    bits = pltpu.bitcast(x.astype(BF16).astype(F32), jnp.uint32)
    return (bits[:, HALF_MODEL:] & jnp.uint32(HIGH16)) | (bits[:, :HALF_MODEL] >> 16)


def _unpack_bf16_pairs(w):
    return pltpu.bitcast(w << 16, F32), pltpu.bitcast(w & jnp.uint32(HIGH16), F32)


def _inproj_kernel(h_ref, g_ref, w_ref, o_ref):
    a = _rms(h_ref[...], g_ref[...])
    o_ref[...] = _nt(a.astype(BF16), w_ref[...])


def _inproj(h, g, w):
    t = h.shape[0]
    return pl.pallas_call(
        _inproj_kernel,
        grid=(t // INPROJ_TILE,),
        in_specs=[
            pl.BlockSpec((INPROJ_TILE, D_MODEL), lambda i: (i, 0)),
            pl.BlockSpec((1, D_MODEL), lambda i: (0, 0)),
            pl.BlockSpec((D_PROJ, D_MODEL), lambda i: (0, 0)),
        ],
        out_specs=pl.BlockSpec((INPROJ_TILE, D_PROJ), lambda i: (i, 0)),
        out_shape=jax.ShapeDtypeStruct((t, D_PROJ), F32),
        compiler_params=pltpu.CompilerParams(
            dimension_semantics=("arbitrary",), vmem_limit_bytes=VMEM_LIMIT),
        name="inproj",
    )(h, g, w)


def _gla_kernel(q_ref, k_ref, v_ref, gg_ref, ga_ref, wal_ref, bal_ref, gn_ref, o_ref,
                state_ref, sc_all_ref, p_all_ref, b_ref):
    hk = GLA_HEADS * GLA_DK
    hv = GLA_HEADS * GLA_DV
    c = GLA_CHUNK
    half_c = c // 2
    assert 2 * c == LANES

    @pl.when(pl.program_id(1) == 0)
    def _():
        state_ref[...] = jnp.zeros_like(state_ref)

    @pl.when((pl.program_id(0) == 0) & (pl.program_id(1) == 0))
    def _():
        p_all_ref[...] = jnp.zeros_like(p_all_ref)

    pj = lax.broadcasted_iota(jnp.int32, (half_c, LANES), 0)
    pl_ = lax.broadcasted_iota(jnp.int32, (half_c, LANES), 1)
    pair_causal = jnp.where(pl_ < c, pl_, pl_ - c) <= jnp.where(pl_ < c, pj, pj + half_c)
    rv = lax.broadcasted_iota(jnp.int32, (hv, hk), 0) // GLA_DV
    ck = lax.broadcasted_iota(jnp.int32, (hv, hk), 1) // GLA_DK
    blockdiag_t = jnp.where(rv == ck, 1.0, 0.0).astype(F32)
    hr = lax.broadcasted_iota(jnp.int32, (SUBLANES, hk), 0)
    hl = lax.broadcasted_iota(jnp.int32, (SUBLANES, hk), 1) // GLA_DK
    head_rows = jnp.where(hr == hl, 1.0, 0.0).astype(BF16)
    eye_v = _eye(hv, BF16)

    ga_hi, ga_lo = _split_bf16(ga_ref[0])
    w_hi, w_lo = _split_bf16(wal_ref[...])
    z = _mm(ga_hi, w_hi) + _mm(ga_hi, w_lo) + _mm(ga_lo, w_hi) + bal_ref[...]
    log_a = (jnp.minimum(z, 0.0) - jnp.log1p(jnp.exp(-jnp.abs(z)))) * (1.0 / GLA_TAU)
    la_hi = log_a.astype(BF16)
    la_r1 = log_a - la_hi.astype(F32)
    la_mid = la_r1.astype(BF16)
    la_lo = (la_r1 - la_mid.astype(F32)).astype(BF16)
    rb = lax.broadcasted_iota(jnp.int32, (GLA_BLOCK, GLA_BLOCK), 0)
    cb = lax.broadcasted_iota(jnp.int32, (GLA_BLOCK, GLA_BLOCK), 1)
    tril = jnp.where((rb >= cb) & (rb // c == cb // c), 1.0, 0.0).astype(BF16)
    b_ref[...] = _mm(tril, la_hi) + _mm(tril, la_mid) + _mm(tril, la_lo)

    def chunk(ci):
        rows = slice(ci * c, (ci + 1) * c)
        p_ref, sc_ref = p_all_ref.at[ci], sc_all_ref.at[ci]
        q = q_ref[0, rows, :] * (GLA_DK ** -0.5)
        k = k_ref[0, rows, :]
        v = v_ref[0, rows, :]
        b = b_ref[rows, :]
        state_t = state_ref[...]
        o_inter = _nt((q * jnp.exp(b)).astype(BF16), state_t.astype(BF16))

        for t in range(c):
            ns = PACK16 * (t // PACK16 + 1)
            slot = 2 * (t % half_c) + t // half_c
            p = jnp.exp(b[t:t + 1, :] - b[:ns, :]) * k[:ns, :] * q[t:t + 1, :]
            p_ref[slot * c:slot * c + ns, :] = p.astype(BF16)
        rows_hs = _nt(head_rows, p_ref[...])
        for jj in range(half_c):
            for h in range(GLA_HEADS):
                sc_ref[h, jj:jj + 1, :] = rows_hs[h:h + 1, jj * LANES:(jj + 1) * LANES]
        vb = v.astype(BF16)
        zero_v = jnp.zeros((c, GLA_DV), BF16)
        o_heads = []
        for h in range(GLA_HEADS):
            vh = vb[:, h * GLA_DV:(h + 1) * GLA_DV]
            v_pair = jnp.concatenate([jnp.concatenate([vh, zero_v], axis=1),
                                      jnp.concatenate([zero_v, vh], axis=1)], axis=0)
            scores = jnp.where(pair_causal, sc_ref[h], 0.0).astype(BF16)
            o_pair = _mm(scores, v_pair)
            o_heads.append(jnp.concatenate([o_pair[:, :GLA_DV], o_pair[:, GLA_DV:]], axis=0))
        o = o_inter + jnp.concatenate(o_heads, axis=-1)

        b_last = b[c - 1:c, :]
        kd = (k * jnp.exp(b_last - b)).astype(BF16)
        v_t = _nt(eye_v, vb).astype(BF16)
        upd_t = _mm(v_t, kd)
        state_ref[...] = jnp.exp(b_last) * state_t + upd_t * blockdiag_t

        gg = gg_ref[0, rows, :]
        outs = []
        for h in range(GLA_HEADS):
            oh = o[:, h * GLA_DV:(h + 1) * GLA_DV]
            ms = jnp.mean(oh * oh, axis=-1, keepdims=True)
            outs.append(oh * lax.rsqrt(ms + EPS))
        on = jnp.concatenate(outs, axis=-1) * gn_ref[...]
        o_ref[0, rows, :] = (on * (gg * jax.nn.sigmoid(gg))).astype(o_ref.dtype)

    for ci in range(GLA_BLOCK // c):
        chunk(ci)


def _gla(proj3, wal, bal, gn):
    bsz, seq, _ = proj3.shape
    nb = seq // GLA_BLOCK

    def col(width, off):
        return pl.BlockSpec((1, GLA_BLOCK, width), lambda b, i: (b, i, off // width))

    return pl.pallas_call(
        _gla_kernel,
        grid=(bsz, nb),
        in_specs=[
            col(128, COL_GQ), col(128, COL_GK), col(256, COL_GV), col(256, COL_GG), col(128, COL_GA),
            pl.BlockSpec((128, 128), lambda b, i: (0, 0)),
            pl.BlockSpec((1, 128), lambda b, i: (0, 0)),
            pl.BlockSpec((1, 256), lambda b, i: (0, 0)),
        ],
        out_specs=pl.BlockSpec((1, GLA_BLOCK, 256), lambda b, i: (b, i, 0)),
        out_shape=jax.ShapeDtypeStruct((bsz, seq, GLA_HEADS * GLA_DV), BF16),
        scratch_shapes=[
            pltpu.VMEM((GLA_HEADS * GLA_DV, GLA_HEADS * GLA_DK), F32),
            pltpu.VMEM((GLA_BLOCK // GLA_CHUNK, GLA_HEADS, GLA_CHUNK // 2, 2 * GLA_CHUNK), F32),
            pltpu.VMEM((GLA_BLOCK // GLA_CHUNK, GLA_CHUNK * GLA_CHUNK, GLA_HEADS * GLA_DK), BF16),
            pltpu.VMEM((GLA_BLOCK, GLA_HEADS * GLA_DK), F32),
        ],
        compiler_params=pltpu.CompilerParams(
            dimension_semantics=("arbitrary", "arbitrary"), vmem_limit_bytes=VMEM_LIMIT),
        name="gla",
    )(proj3, proj3, proj3, proj3, proj3, wal, bal, gn)


DSA_BISECT_STEPS = 14
DSA_ACCUMULATORS = 4
DSA_WALK_UNCONDITIONAL = 2
DSA_PV_PARTS = 4


def _dsa_kernel(dq_ref, kv_ref, iq_ref, ikw_ref, btile_ref, o_ref,
                sc_ref, lg_ref, p_ref, kb_ref, vb_ref, mp_ref, lp_ref, acc_ref, cut_ref, *, n_sel):
    j = pl.program_id(1)
    nkc = ((j + 1) * QB + KC - 1) // KC
    seq = kv_ref.shape[1]
    ksel = float(n_sel)

    @pl.when(j == 0)
    def _():
        kv = kv_ref[0]
        kb_ref[...] = kv[:, :HEAD_DIM].astype(BF16)
        vb_ref[...] = kv[:, HEAD_DIM:].astype(BF16)

    def rows(c):
        return pl.ds(pl.multiple_of(c * KC, KC), KC)

    wide = DSA_ACCUMULATORS * SUBLANES

    def fold(op, x):
        return op(x.reshape(KC // wide, wide, QB), axis=0)

    def all8(op, xw, roll_op):
        x8 = op(xw.reshape(DSA_ACCUMULATORS, SUBLANES, QB), axis=0)
        for shift in (4, 2, 1):
            x8 = roll_op(x8, pltpu.roll(x8, shift, 0))
        return x8

    def widen(x8):
        return jnp.concatenate([x8] * DSA_ACCUMULATORS, axis=0)

    zeros8 = jnp.zeros((SUBLANES, QB), F32)
    zerosw = jnp.zeros((wide, QB), F32)
    infw = jnp.full((wide, QB), jnp.inf, F32)

    ikw_q = ikw_ref[0, pl.ds(pl.multiple_of(j * QB, QB), QB), :]
    w_t = ikw_q.T[IDX_DIM:IDX_DIM + IDX_HEADS, :] * (IDX_HEADS ** -0.5 * IDX_DIM ** -0.5)
    iq = iq_ref[0].astype(BF16)
    iq_heads = [iq[:, h * IDX_DIM:(h + 1) * IDX_DIM] for h in range(IDX_HEADS)]

    s_loc = lax.broadcasted_iota(jnp.int32, (KC, QB), 0)
    t_glob = j * QB + lax.broadcasted_iota(jnp.int32, (KC, QB), 1)
    s_loc_f = s_loc.astype(F32)

    def score_chunk(c, st):
        mn8, mx8, ge8, gt8 = st
        ik = ikw_ref[0, rows(c), :][:, :IDX_DIM].astype(BF16)
        acc = jnp.zeros((KC, QB), F32)
        for h in range(IDX_HEADS):
            rel = _nt(ik, iq_heads[h])
            acc = acc + jnp.maximum(rel, 0.0) * w_t[h:h + 1, :]
        adm = s_loc + c * KC <= t_glob
        blk = jnp.where(adm, acc, -jnp.inf)
        sc_ref[rows(c), :] = blk
        mn8 = jnp.minimum(mn8, fold(jnp.min, jnp.where(adm, acc, jnp.inf)))
        mx8 = jnp.maximum(mx8, fold(jnp.max, blk))
        ge8 = ge8 + fold(jnp.sum, jnp.where(blk >= 0.0, 1.0, 0.0))
        gt8 = gt8 + fold(jnp.sum, jnp.where(blk > 0.0, 1.0, 0.0))
        return mn8, mx8, ge8, gt8

    mn8, mx8, ge8, gt8 = lax.fori_loop(0, nkc, score_chunk, (infw, -infw, zerosw, zerosw))
    row_min = all8(jnp.min, mn8, jnp.minimum)
    row_max = all8(jnp.max, mx8, jnp.maximum)
    c_ge0, c_gt0 = all8(jnp.sum, ge8, jnp.add), all8(jnp.sum, gt8, jnp.add)

    def blocks(c):
        return sc_ref[rows(c), :].reshape(KC // wide, wide, QB)

    def count_ge(x8):
        xw = widen(x8)[None]

        def body(c, cw):
            return cw + jnp.sum(jnp.where(blocks(c) >= xw, 1.0, 0.0), axis=0)
        return all8(jnp.sum, lax.fori_loop(0, nkc, body, zerosw), jnp.add)

    def min_ge(x8):
        xw = widen(x8)[None]

        def body(c, mw):
            blk = blocks(c)
            return jnp.minimum(mw, jnp.min(jnp.where(blk >= xw, blk, jnp.inf), axis=0))
        return all8(jnp.min, lax.fori_loop(0, nkc, body, infw), jnp.minimum)

    def pass_gt(x8):
        xw = widen(x8)[None]

        def body(c, st):
            cw, mw = st
            blk = blocks(c)
            hit = blk > xw
            return (cw + jnp.sum(jnp.where(hit, 1.0, 0.0), axis=0),
                    jnp.minimum(mw, jnp.min(jnp.where(hit, blk, jnp.inf), axis=0)))
        cw, mw = lax.fori_loop(0, nkc, body, (zerosw, infw))
        return all8(jnp.sum, cw, jnp.add), all8(jnp.min, mw, jnp.minimum)

    n_adm = (j * QB + lax.broadcasted_iota(jnp.int32, (SUBLANES, QB), 1) + 1).astype(F32)
    at_zero = (c_gt0 < ksel) & (c_ge0 >= ksel)
    above = c_gt0 >= ksel
    lo = jnp.where(above | at_zero, 0.0, row_min)
    c_lo = jnp.where(above | at_zero, c_ge0, n_adm)
    settled = at_zero | (n_adm <= ksel)
    hi = jnp.where(settled, lo, jnp.where(above, row_max, 0.0))

    def bisect(_, st):
        lo, hi, c_lo = st
        mid = lo + (hi - lo) * 0.5
        cnt = count_ge(mid)
        up = cnt >= ksel
        return jnp.where(up, mid, lo), jnp.where(up, hi, mid), jnp.where(up, cnt, c_lo)

    lo, _, c_lo = lax.fori_loop(0, DSA_BISECT_STEPS, bisect, (lo, hi, c_lo))

    v0 = min_ge(lo)
    done0 = jnp.where(settled | (c_lo == ksel), 1.0, 0.0)
    c_gt_init = jnp.where(at_zero, c_gt0, 0.0)

    def walk_cond(st):
        return jnp.min(st[3]) < 0.5

    def walk_body(st):
        v, c_ge, c_gt, dn = st
        cnt, vnext = pass_gt(v)
        live = dn < 0.5
        fin = live & (cnt < ksel)
        step = live & (cnt >= ksel)
        c_gt = jnp.where(fin, cnt, c_gt)
        dn = jnp.where(fin, 1.0, dn)
        v = jnp.where(step, vnext, v)
        c_ge = jnp.where(step, cnt, c_ge)
        return v, c_ge, c_gt, dn

    walk = (v0, c_lo, c_gt_init, done0)
    for _ in range(DSA_WALK_UNCONDITIONAL):
        walk = walk_body(walk)
    tau8, c_ge, c_gt, _ = lax.while_loop(walk_cond, walk_body, walk)

    need = ksel - c_gt
    cut_ref[...] = jnp.full(cut_ref.shape, float(seq), F32)

    @pl.when(jnp.max(c_ge) > ksel)
    def _():
        s_grp = lax.broadcasted_iota(jnp.int32, (KC // wide, wide, QB), 0) * wide
        s_sub = lax.broadcasted_iota(jnp.int32, (KC // wide, wide, QB), 1)
        s_idx = (s_grp + s_sub).astype(F32)
        tauw = widen(tau8)[None]

        def count_ties_below(m8):
            mw = widen(m8)[None]

            def body(c, cw):
                hit = (blocks(c) == tauw) & (s_idx + (c * KC).astype(F32) < mw)
                return cw + jnp.sum(jnp.where(hit, 1.0, 0.0), axis=0)
            return all8(jnp.sum, lax.fori_loop(0, nkc, body, zerosw), jnp.add)

        def idx_bisect(_, lh):
            lo_m, hi_m = lh
            mid = jnp.floor((lo_m + hi_m) * 0.5)
            ok = count_ties_below(mid) >= need
            return jnp.where(ok, lo_m, mid), jnp.where(ok, mid, hi_m)

        _, hi_m = lax.fori_loop(0, int(math.log2(seq)) + 1, idx_bisect,
                                (zeros8, zeros8 + float(seq)))
        cut_ref[...] = jnp.where(c_ge > ksel, hi_m, float(seq))

    tau = tau8[0:1, :]
    cut = cut_ref[0:1, :]

    q = (dq_ref[0] * (HEAD_DIM ** -0.5)).astype(BF16)
    q4 = jnp.concatenate([q[:, h * HEAD_DIM:(h + 1) * HEAD_DIM] for h in range(DSA_HEADS)],
                         axis=0)
    mp_ref[...] = jnp.full(mp_ref.shape, NEG_BIG, F32)
    lp_ref[...] = jnp.zeros(lp_ref.shape, F32)

    pv_parts = min(DSA_PV_PARTS, seq // KC)
    part = seq // pv_parts

    def cols(c):
        return pl.ds(pl.multiple_of(c * KC, KC), KC)

    def lane_fold(op, x):
        out = x[:, :LANES]
        for g in range(1, KC // LANES):
            out = op(out, x[:, g * LANES:(g + 1) * LANES])
        return out

    def logits_chunk(c, carry):
        blk = sc_ref[rows(c), :]
        sidx = s_loc_f + (c * KC).astype(F32)
        sel = (blk > tau) | ((blk == tau) & (sidx < cut))
        neg_t = jnp.where(sel, 0.0, NEG_BIG).T
        lg4 = _nt(q4, kb_ref[rows(c), :])
        which = [[jnp.clip((c * (KC // WINDOW) + b) - (j * (QB // WINDOW) + a) + 2, 0, 2)
                  for b in range(KC // WINDOW)] for a in range(QB // WINDOW)]
        for h in range(DSA_HEADS):
            hq = slice(h * QB, (h + 1) * QB)
            bias = jnp.concatenate(
                [jnp.concatenate([btile_ref[h, w] for w in which_a], axis=1)
                 for which_a in which], axis=0)
            lg = lg4[hq, :] + bias + neg_t
            lg_ref[hq, cols(c)] = lg
            mp_ref[h] = jnp.maximum(mp_ref[h], lane_fold(jnp.maximum, lg))
        return carry

    lax.fori_loop(0, nkc, logits_chunk, 0)
    m_rows = [jnp.max(mp_ref[h], axis=-1, keepdims=True) for h in range(DSA_HEADS)]

    def probs_chunk(c, carry):
        for h in range(DSA_HEADS):
            hq = slice(h * QB, (h + 1) * QB)
            p = jnp.exp(lg_ref[hq, cols(c)] - m_rows[h])
            lp_ref[h] = lp_ref[h] + lane_fold(jnp.add, p)
            p_ref[hq, cols(c)] = p.astype(BF16)
        return carry

    def zero_chunk(c, carry):
        p_ref[:, cols(c)] = jnp.zeros((DSA_HEADS * QB, KC), BF16)
        return carry

    lax.fori_loop(0, nkc, probs_chunk, 0)
    n_parts = ((j + 1) * QB + part - 1) // part
    lax.fori_loop(nkc, n_parts * (part // KC), zero_chunk, 0)
    acc_ref[...] = _mm(p_ref[:, :part], vb_ref[:part, :])
    for k in range(1, pv_parts):
        @pl.when(k < n_parts)
        def _(k=k):
            acc_ref[...] = acc_ref[...] + _mm(p_ref[:, k * part:(k + 1) * part],
                                              vb_ref[k * part:(k + 1) * part, :])

    outs = []
    for h in range(DSA_HEADS):
        l = jnp.sum(lp_ref[h], axis=-1, keepdims=True)
        outs.append(acc_ref[h * QB:(h + 1) * QB, :] / l)
    o_ref[0] = jnp.concatenate(outs, axis=-1).astype(o_ref.dtype)


def _dsa(proj3, btiles):
    bsz, seq, _ = proj3.shape
    n_sel = min(DSA_TOPK_MAX, seq // 4)
    return pl.pallas_call(
        functools.partial(_dsa_kernel, n_sel=n_sel),
        grid=(bsz, seq // QB),
        in_specs=[
            pl.BlockSpec((1, QB, 256), lambda b, j: (b, j, COL_DQ // 256)),
            pl.BlockSpec((1, seq, 128), lambda b, j: (b, 0, COL_DKV // 128)),
            pl.BlockSpec((1, QB, 512), lambda b, j: (b, j, COL_IQ // 512)),
            pl.BlockSpec((1, seq, 128), lambda b, j: (b, 0, COL_IKW // 128)),
            pl.BlockSpec((DSA_HEADS, 3, WINDOW, WINDOW), lambda b, j: (0, 0, 0, 0)),
        ],
        out_specs=pl.BlockSpec((1, QB, 256), lambda b, j: (b, j, 0)),
        out_shape=jax.ShapeDtypeStruct((bsz, seq, DSA_HEADS * HEAD_DIM), BF16),
        scratch_shapes=[
            pltpu.VMEM((seq, QB), F32),
            pltpu.VMEM((DSA_HEADS * QB, seq), F32),
            pltpu.VMEM((DSA_HEADS * QB, seq), BF16),
            pltpu.VMEM((seq, HEAD_DIM), BF16),
            pltpu.VMEM((seq, HEAD_DIM), BF16),
            pltpu.VMEM((DSA_HEADS, QB, LANES), F32),
            pltpu.VMEM((DSA_HEADS, QB, LANES), F32),
            pltpu.VMEM((DSA_HEADS * QB, HEAD_DIM), F32),
            pltpu.VMEM((SUBLANES, QB), F32),
        ],
        compiler_params=pltpu.CompilerParams(
            dimension_semantics=("arbitrary", "arbitrary"), vmem_limit_bytes=VMEM_LIMIT),
        name="dsa",
    )(proj3, proj3, proj3, proj3, btiles)


def _swa_kernel(sink_ref, q_ref, kc_ref, kp_ref, vc_ref, vp_ref, bias_ref, o_ref):
    n = pl.program_id(1)
    q = (q_ref[0] * (HEAD_DIM ** -0.5)).astype(BF16)
    k2 = jnp.concatenate([kp_ref[0], kc_ref[0]], axis=0).astype(BF16)
    v2 = jnp.concatenate([vp_ref[0], vc_ref[0]], axis=0).astype(BF16)
    ones = jnp.ones((2 * WINDOW, HEAD_DIM), BF16)
    first = jnp.where(n == 0, 1, 0)
    outs = []
    for h in range(SWA_HEADS):
        kvh = h // SWA_GROUP
        kh = k2[:, kvh * HEAD_DIM:(kvh + 1) * HEAD_DIM]
        v_ones = jnp.concatenate([v2[:, kvh * HEAD_DIM:(kvh + 1) * HEAD_DIM], ones], axis=1)
        lg = _nt(q[:, h * HEAD_DIM:(h + 1) * HEAD_DIM], kh) + bias_ref[first, h]
        sink = sink_ref[h]
        m = jnp.maximum(jnp.max(lg, axis=-1, keepdims=True), sink)
        num_den = _mm(jnp.exp(lg - m).astype(BF16), v_ones)
        outs.append(num_den[:, :HEAD_DIM] / (num_den[:, HEAD_DIM:] + jnp.exp(sink - m)))
    o_ref[0] = jnp.concatenate(outs, axis=-1).astype(o_ref.dtype)


def _swa(proj3, sinks, bias_nat):
    bsz, seq, _ = proj3.shape
    return pl.pallas_call(
        _swa_kernel,
        grid=(bsz, seq // WINDOW),
        in_specs=[
            pl.BlockSpec(memory_space=pltpu.SMEM),
            pl.BlockSpec((1, WINDOW, 512), lambda b, n: (b, n, COL_SQ // 512)),
            pl.BlockSpec((1, WINDOW, 128), lambda b, n: (b, n, COL_SK // 128)),
            pl.BlockSpec((1, WINDOW, 128), lambda b, n: (b, jnp.maximum(n - 1, 0), COL_SK // 128)),
            pl.BlockSpec((1, WINDOW, 128), lambda b, n: (b, n, COL_SV // 128)),
            pl.BlockSpec((1, WINDOW, 128), lambda b, n: (b, jnp.maximum(n - 1, 0), COL_SV // 128)),
            pl.BlockSpec((2, SWA_HEADS, WINDOW, 2 * WINDOW), lambda b, n: (0, 0, 0, 0)),
        ],
        out_specs=pl.BlockSpec((1, WINDOW, 512), lambda b, n: (b, n, 0)),
        out_shape=jax.ShapeDtypeStruct((bsz, seq, SWA_HEADS * HEAD_DIM), BF16),
        compiler_params=pltpu.CompilerParams(
            dimension_semantics=("arbitrary", "arbitrary"), vmem_limit_bytes=VMEM_LIMIT),
        name="swa",
    )(sinks, proj3, proj3, proj3, proj3, proj3, bias_nat)


ROUTE_OFF = N_GROUPS
ROUTE_ROWS = 48


def _split_bf16(x):
    hi = x.astype(BF16)
    lo = (x - hi.astype(F32)).astype(BF16)
    return hi, lo


def _outproj_router_kernel(h_ref, og_ref, od_ref, os_ref, wo_ref, gffn_ref, wr_hi_ref, wr_lo_ref,
                           br_ref, h1_ref, xn_ref, ri_ref, rf_ref, cnt_ref, run_ref):
    tm = h_ref.shape[0]

    @pl.when(pl.program_id(0) == 0)
    def _():
        run_ref[...] = jnp.zeros_like(run_ref)

    o = jnp.concatenate([og_ref[...], od_ref[...], os_ref[...]], axis=-1)
    h1 = h_ref[...] + _mm(o, wo_ref[...])
    h1_ref[...] = h1
    xn = _rms(h1, gffn_ref[...])
    xn_ref[...] = _pack_bf16_pairs(xn)

    x_hi, x_lo = _split_bf16(xn)
    lg = (_mm(x_hi, wr_hi_ref[...]) + _mm(x_lo, wr_hi_ref[...]) + _mm(x_hi, wr_lo_ref[...])
          + br_ref[...]).T[:ROUTE_ROWS, :]

    row = lax.broadcasted_iota(jnp.int32, lg.shape, 0)
    row_f = row.astype(F32)
    ninf = -jnp.inf

    def first_max(x):
        m = jnp.max(x, axis=0, keepdims=True)
        idx = jnp.min(jnp.where(x == m, row_f, float(ROUTE_ROWS)), axis=0, keepdims=True)
        return m, idx

    gl = jnp.where(row < N_GROUPS, lg, ninf)
    gmax, gsel = first_max(gl)
    g_w = 1.0 / jnp.sum(jnp.exp(gl - gmax), axis=0, keepdims=True)
    e_lo = ROUTE_OFF + EXPERTS_PER_GROUP * gsel
    el = jnp.where((row_f >= e_lo) & (row_f < e_lo + EXPERTS_PER_GROUP), lg, ninf)
    m1, i1 = first_max(el)
    eden = jnp.sum(jnp.exp(el - m1), axis=0, keepdims=True)
    m2, i2 = first_max(jnp.where(row_f == i1, ninf, el))
    p1 = 1.0 / eden
    p2 = jnp.exp(m2 - m1) / eden
    gate1 = g_w * p1 / (p1 + p2)
    gate2 = g_w * p2 / (p1 + p2)

    onehot = jnp.where((row_f == i1) | (row_f == i2), 1.0, 0.0)
    rr = lax.broadcasted_iota(jnp.int32, (tm, tm), 0)
    cc = lax.broadcasted_iota(jnp.int32, (tm, tm), 1)
    earlier = jnp.where(rr < cc, 1.0, 0.0).astype(BF16)
    run = run_ref[...]
    before = _mm(onehot.astype(BF16), earlier) + jnp.concatenate([run] * (tm // LANES), axis=1)
    rank1 = jnp.sum(jnp.where(row_f == i1, before, 0.0), axis=0, keepdims=True)
    rank2 = jnp.sum(jnp.where(row_f == i2, before, 0.0), axis=0, keepdims=True)
    run_ref[...] = run + jnp.sum(onehot, axis=1, keepdims=True)
    cnt_ref[...] = run_ref[...]

    out_row = lax.broadcasted_iota(jnp.int32, (SUBLANES, tm), 0)
    ints = jnp.where(out_row == 0, i1 - ROUTE_OFF,
                     jnp.where(out_row == 1, i2 - ROUTE_OFF,
                               jnp.where(out_row == 2, rank1,
                                         jnp.where(out_row == 3, rank2, 0.0))))
    ri_ref[...] = ints.astype(jnp.int32)
    rf_ref[...] = jnp.where(out_row == 0, gate1, jnp.where(out_row == 1, gate2, 0.0))


def _outproj_router(h, og, od, os_, wo, gffn, wr_hi, wr_lo, br):
    t = h.shape[0]
    tm = ROUTER_TILE
    row = lambda w: pl.BlockSpec((tm, w), lambda i: (i, 0))
    col = lambda r: pl.BlockSpec((r, tm), lambda i: (0, i))
    full = lambda a, b: pl.BlockSpec((a, b), lambda i: (0, 0))
    return pl.pallas_call(
        _outproj_router_kernel,
        grid=(t // tm,),
        in_specs=[row(D_MODEL), row(256), row(256), row(512), full(D_MODEL, D_MODEL),
                  full(1, D_MODEL), full(D_MODEL, LANES), full(D_MODEL, LANES), full(1, LANES)],
        out_specs=[row(D_MODEL), row(HALF_MODEL), col(SUBLANES), col(SUBLANES),
                   full(ROUTE_ROWS, LANES)],
        out_shape=[
            jax.ShapeDtypeStruct((t, D_MODEL), F32),
            jax.ShapeDtypeStruct((t, HALF_MODEL), jnp.uint32),
            jax.ShapeDtypeStruct((SUBLANES, t), jnp.int32),
            jax.ShapeDtypeStruct((SUBLANES, t), F32),
            jax.ShapeDtypeStruct((ROUTE_ROWS, LANES), F32),
        ],
        scratch_shapes=[pltpu.VMEM((ROUTE_ROWS, LANES), F32)],
        compiler_params=pltpu.CompilerParams(
            dimension_semantics=("arbitrary",), vmem_limit_bytes=VMEM_LIMIT),
        name="outproj_router",
    )(h, og, od, os_, wo, gffn, wr_hi, wr_lo, br)


def _dispatch_kernel(pos_ref, xn_ref, buf_in_ref, buf_ref, sem):
    del buf_in_ref
    tm = xn_ref.shape[0]

    def row_copy(r, k):
        dst = pos_ref[0, 0, 2 * r + k]
        return pltpu.make_async_copy(xn_ref.at[pl.ds(r, 1)], buf_ref.at[pl.ds(dst, 1)], sem)

    def issue(g, carry):
        for u in range(DMA_UNROLL):
            row_copy(g * DMA_UNROLL + u, 0).start()
            row_copy(g * DMA_UNROLL + u, 1).start()
        return carry

    def drain(g, carry):
        for u in range(DMA_UNROLL):
            row_copy(g * DMA_UNROLL + u, 0).wait()
            row_copy(g * DMA_UNROLL + u, 1).wait()
        return carry

    lax.fori_loop(0, tm // DMA_UNROLL, issue, 0)
    lax.fori_loop(0, tm // DMA_UNROLL, drain, 0)


def _dispatch(pos3, xn, buf0):
    t = xn.shape[0]
    tm = DISPATCH_TILE
    pos3 = pos3.reshape(t // tm, 1, 2 * tm)
    return pl.pallas_call(
        _dispatch_kernel,
        grid=(t // tm,),
        in_specs=[
            pl.BlockSpec((1, 1, 2 * tm), lambda i: (i, 0, 0), memory_space=pltpu.SMEM),
            pl.BlockSpec((tm, HALF_MODEL), lambda i: (i, 0)),
            pl.BlockSpec(memory_space=pl.ANY),
        ],
        out_specs=pl.BlockSpec(memory_space=pl.ANY),
        out_shape=jax.ShapeDtypeStruct(buf0.shape, buf0.dtype),
        scratch_shapes=[pltpu.SemaphoreType.DMA(())],
        input_output_aliases={2: 0},
        compiler_params=pltpu.CompilerParams(
            dimension_semantics=("arbitrary",), vmem_limit_bytes=VMEM_LIMIT),
        name="dispatch",
    )(pos3, xn, buf0)


def _expert_kernel(be_ref, nu_ref, rows_ref, slot_ref, next_ref, x_ref, wg_hbm, wu_hbm, wd_hbm,
                   y_ref, wg_f, wu_f, wd_f, wg_s, wu_s, wd_s, sems, *, layer):
    i = pl.program_id(0)
    used = i < nu_ref[0]
    expert = be_ref[i]
    new_expert = (i == 0) | (expert != be_ref[jnp.maximum(i - 1, 0)])
    slot = slot_ref[i]

    def weight_copies(e, s):
        return (pltpu.make_async_copy(wg_hbm.at[layer, e], wg_f.at[s], sems.at[s, 0]),
                pltpu.make_async_copy(wu_hbm.at[layer, e], wu_f.at[s], sems.at[s, 1]),
                pltpu.make_async_copy(wd_hbm.at[layer, e], wd_f.at[s], sems.at[s, 2]))

    @pl.when(i == 0)
    def _():
        for cp in weight_copies(expert, slot):
            cp.start()

    @pl.when(used & new_expert)
    def _():
        for cp in weight_copies(expert, slot):
            cp.wait()
        wg_s[...] = wg_f[slot].astype(BF16)
        wu_s[...] = wu_f[slot].astype(BF16)
        wd_s[...] = wd_f[slot].astype(BF16)
        nxt = next_ref[i]

        @pl.when(nxt >= 0)
        def _():
            for cp in weight_copies(nxt, 1 - slot):
                cp.start()

    def ffn(nrows):
        x_lo, x_hi = (part.astype(BF16) for part in _unpack_bf16_pairs(x_ref[:nrows, :]))
        g = _mm(x_lo, wg_s[:HALF_MODEL, :]) + _mm(x_hi, wg_s[HALF_MODEL:, :])
        u = _mm(x_lo, wu_s[:HALF_MODEL, :]) + _mm(x_hi, wu_s[HALF_MODEL:, :])
        hmid = (g * jax.nn.sigmoid(g)) * u
        y_ref[:nrows, :] = _pack_bf16_pairs(_mm(hmid.astype(BF16), wd_s[...]))

    half_rows = EXPERT_ROWS // 2
    short = rows_ref[i] <= half_rows

    @pl.when(used & jnp.logical_not(short))
    def _():
        ffn(EXPERT_ROWS)

    @pl.when(used & short)
    def _():
        ffn(half_rows)
        y_ref[half_rows:, :] = jnp.zeros((EXPERT_ROWS - half_rows, HALF_MODEL), y_ref.dtype)

    @pl.when(i >= nu_ref[0])
    def _():
        y_ref[...] = jnp.zeros_like(y_ref)


def _experts(blk_expert, n_used, blk_rows, buf, wg, wu, wd, layer):
    nrows = buf.shape[0]
    nblk = nrows // EXPERT_ROWS
    idx = jnp.arange(nblk, dtype=jnp.int32)
    change = (idx == 0) | (blk_expert != jnp.roll(blk_expert, 1))
    slot = (jnp.cumsum(change.astype(jnp.int32)) - 1) % 2
    later_run = (idx[None, :] > idx[:, None]) & change[None, :] & (idx[None, :] < n_used[0])
    nxt = jnp.where(jnp.any(later_run, axis=1), blk_expert[jnp.argmax(later_run, axis=1)], -1)
    block = lambda i, *_: (i, 0)
    any_space = pl.BlockSpec(memory_space=pl.ANY)
    return pl.pallas_call(
        functools.partial(_expert_kernel, layer=layer),
        grid_spec=pltpu.PrefetchScalarGridSpec(
            num_scalar_prefetch=5,
            grid=(nblk,),
            in_specs=[pl.BlockSpec((EXPERT_ROWS, HALF_MODEL), block),
                      any_space, any_space, any_space],
            out_specs=pl.BlockSpec((EXPERT_ROWS, HALF_MODEL), block),
            scratch_shapes=[
                pltpu.VMEM((2, D_MODEL, D_EXPERT), F32),
                pltpu.VMEM((2, D_MODEL, D_EXPERT), F32),
                pltpu.VMEM((2, D_EXPERT, D_MODEL), F32),
                pltpu.VMEM((D_MODEL, D_EXPERT), BF16),
                pltpu.VMEM((D_MODEL, D_EXPERT), BF16),
                pltpu.VMEM((D_EXPERT, D_MODEL), BF16),
                pltpu.SemaphoreType.DMA((2, 3)),
            ],
        ),
        out_shape=jax.ShapeDtypeStruct((nrows, HALF_MODEL), jnp.uint32),
        compiler_params=pltpu.CompilerParams(
            dimension_semantics=("arbitrary",), vmem_limit_bytes=VMEM_LIMIT),
        name="experts",
    )(blk_expert, n_used, blk_rows, slot.astype(jnp.int32), nxt.astype(jnp.int32),
      buf, wg, wu, wd)


def _combine_ple_kernel(pos_ref, pos_next_ref, h1_ref, p_ref, rf_ref, yb_ref, wple_ref, gple_ref,
                        wpg_ref, gfin_ref, o_ref, ybuf, sems, *, final_norm):
    tm = h1_ref.shape[0]
    i = pl.program_id(0)
    last = pl.num_programs(0) - 1
    slot = lax.rem(i, 2)

    def row_copy(tile_pos_ref, s, r, k):
        src = tile_pos_ref[0, 0, 2 * r + k]
        return pltpu.make_async_copy(
            yb_ref.at[pl.ds(src, 1)], ybuf.at[s, k, pl.ds(r, 1)], sems.at[s])

    def looped(tile_pos_ref, s, op):
        def body(g, carry):
            for u in range(DMA_UNROLL):
                op(row_copy(tile_pos_ref, s, g * DMA_UNROLL + u, 0))
                op(row_copy(tile_pos_ref, s, g * DMA_UNROLL + u, 1))
            return carry
        lax.fori_loop(0, tm // DMA_UNROLL, body, 0)

    @pl.when(i == 0)
    def _():
        looped(pos_ref, 0, lambda cp: cp.start())

    looped(pos_ref, slot, lambda cp: cp.wait())
    for r in range(tm):
        row_copy(pos_next_ref, 1 - slot, r, 0).start()
        row_copy(pos_next_ref, 1 - slot, r, 1).start()
    e = _rms(_mm(p_ref[...].astype(BF16), wple_ref[...]), gple_ref[...])

    rf = rf_ref[...]
    y0_lo, y0_hi = _unpack_bf16_pairs(ybuf[slot, 0])
    y1_lo, y1_hi = _unpack_bf16_pairs(ybuf[slot, 1])
    g0, g1 = rf[:, 0:1], rf[:, 1:2]
    moe = jnp.concatenate([y0_lo * g0 + y1_lo * g1, y0_hi * g0 + y1_hi * g1], axis=-1)
    h2 = h1_ref[...] + moe
    h3 = h2 + e * jax.nn.sigmoid(_mm(h2.astype(BF16), wpg_ref[...]))
    if final_norm:
        h3 = _rms(h3, gfin_ref[...])
    o_ref[...] = h3

    @pl.when(i == last)
    def _():
        looped(pos_next_ref, 1 - slot, lambda cp: cp.wait())


def _combine_ple(pos3, h1, p, rf, yb, wple, gple, wpg, gfin, layer, final_norm):
    t = h1.shape[0]
    tm = TOKEN_TILE
    row = lambda w: pl.BlockSpec((tm, w), lambda i: (i, 0))
    full = lambda a, b: pl.BlockSpec((a, b), lambda i: (0, 0))
    return pl.pallas_call(
        functools.partial(_combine_ple_kernel, final_norm=final_norm),
        grid=(t // tm,),
        in_specs=[
            pl.BlockSpec((1, 1, 2 * tm), lambda i: (i, 0, 0), memory_space=pltpu.SMEM),
            pl.BlockSpec((1, 1, 2 * tm), lambda i: (jnp.minimum(i + 1, t // tm - 1), 0, 0),
                         memory_space=pltpu.SMEM),
            row(D_MODEL),
            pl.BlockSpec((tm, PLE_DIM), lambda i: (layer * (t // tm) + i, 0)),
            row(LANES),
            pl.BlockSpec(memory_space=pl.ANY),
            full(PLE_DIM, D_MODEL), full(1, D_MODEL), full(D_MODEL, D_MODEL), full(1, D_MODEL),
        ],
        out_specs=row(D_MODEL),
        out_shape=jax.ShapeDtypeStruct((t, D_MODEL), F32),
        scratch_shapes=[pltpu.VMEM((2, 2, tm, HALF_MODEL), jnp.uint32),
                        pltpu.SemaphoreType.DMA((2,))],
        compiler_params=pltpu.CompilerParams(
            dimension_semantics=("arbitrary",), vmem_limit_bytes=VMEM_LIMIT),
        name="combine_ple",
    )(pos3, pos3, h1, p, rf, yb, wple, gple, wpg, gfin)


def _rel_bucket(dist):
    n = jnp.maximum(dist, 0)
    max_exact = REL_BUCKETS // 2
    nf = jnp.maximum(n, 1).astype(F32)
    large = max_exact + (jnp.log(nf / max_exact) / math.log(REL_MAX_DIST / max_exact)
                         * (REL_BUCKETS - max_exact)).astype(jnp.int32)
    large = jnp.minimum(large, REL_BUCKETS - 1)
    return jnp.where(n < max_exact, n, large)


def _bias_tiles(rel_bias):
    qi = jnp.arange(WINDOW)[:, None]
    kj = jnp.arange(2 * WINDOW)[None, :]
    bucket = _rel_bucket(qi + WINDOW - kj)
    onehot = (bucket[..., None] == jnp.arange(REL_BUCKETS)).astype(F32)
    nat = jnp.einsum('qkb,bh->hqk', onehot, rel_bias, precision=HIGHEST)
    dsa = nat[:DSA_HEADS]
    far = jnp.broadcast_to(rel_bias[REL_BUCKETS - 1, :DSA_HEADS][:, None, None],
                           (DSA_HEADS, WINDOW, WINDOW))
    dist = qi + WINDOW - kj
    in_window = (dist >= 0) & (dist < WINDOW)
    swa = nat[DSA_HEADS:]
    swa_tiles = jnp.stack([jnp.where(in_window, swa, -jnp.inf),
                           jnp.where(in_window & (kj >= WINDOW), swa, -jnp.inf)], axis=0)
    return jnp.stack([far, dsa[:, :, :WINDOW], dsa[:, :, WINDOW:]], axis=1), swa_tiles


def _pack_w_in(w):
    sizes = (128, 128, 256, 256, 16, 256, 64, 64, 512, 64, 8, 512, 128, 128)
    offs = [0]
    for s in sizes:
        offs.append(offs[-1] + s)
    wt = w.T
    gq, gk, gv, gg, ga, dq, dk, dv, iq, ik, iw, sq, sk, sv = (
        wt[offs[n]:offs[n + 1]] for n in range(len(sizes)))
    z = lambda n: jnp.zeros((n, w.shape[0]), w.dtype)
    packed = jnp.concatenate(
        [iq, sq, gv, gg, dq, gq, gk, ga, z(128 - GLA_RANK), dk, dv, ik, iw,
         z(128 - IDX_DIM - IDX_HEADS), sk, sv], axis=0)
    return packed.astype(BF16)


def kernel(x, p, rel_bias, g_mix, w_in, gla_w_alpha, gla_b_alpha, gla_g_norm, swa_sinks, w_out,
           g_ffn, w_router_group, b_router_group, w_router_expert, b_router_expert, w_expert_gate,
           w_expert_up, w_expert_down, w_ple, g_ple, w_ple_gate, g_final):
    bsz, seq, d = x.shape
    depth = w_in.shape[0]
    t = bsz * seq
    assert d == D_MODEL and t % DISPATCH_TILE == 0 and seq % GLA_BLOCK == 0 and seq % KC == 0
    n_blocks = -(-(2 * t) // EXPERT_ROWS) + N_EXPERTS
    dsa_tiles, swa_bias = _bias_tiles(rel_bias)

    h = x.reshape(t, d)
    for i in range(depth):
        proj = _inproj(h, g_mix[i][None, :], _pack_w_in(w_in[i]))
        proj3 = proj.reshape(bsz, seq, D_PROJ)

        wal = jnp.zeros((128, 128), F32).at[:GLA_RANK].set(gla_w_alpha[i])
        og = _gla(proj3, wal, gla_b_alpha[i][None, :],
                  jnp.tile(gla_g_norm[i], GLA_HEADS)[None, :])
        od = _dsa(proj3, dsa_tiles)
        os_ = _swa(proj3, swa_sinks[i], swa_bias)

        w_r = jnp.concatenate([w_router_group[i], w_router_expert[i]], axis=1)
        w_r = jnp.pad(w_r, ((0, 0), (0, LANES - w_r.shape[1])))
        wr_hi, wr_lo = _split_bf16(w_r)
        b_r = jnp.pad(jnp.concatenate([b_router_group[i], b_router_expert[i]]),
                      (0, LANES - N_GROUPS - N_EXPERTS))[None, :]
        h1, xn, ri, rf, cnt = _outproj_router(
            h, og.reshape(t, -1), od.reshape(t, -1), os_.reshape(t, -1), w_out[i].astype(BF16),
            g_ffn[i][None, :], wr_hi, wr_lo, b_r)

        counts = cnt[ROUTE_OFF:ROUTE_OFF + N_EXPERTS, 0].astype(jnp.int32)
        padded = (counts + EXPERT_ROWS - 1) // EXPERT_ROWS * EXPERT_ROWS
        pad_end = jnp.cumsum(padded)
        pad_start = pad_end - padded
        expert_ids = jnp.arange(N_EXPERTS, dtype=jnp.int32)
        slot_start = jnp.sum(
            jnp.where(ri[0:2, :, None] == expert_ids, pad_start, 0), axis=-1)
        pos3 = (slot_start + ri[2:4]).T.reshape(t // TOKEN_TILE, 1, 2 * TOKEN_TILE)
        gates = jnp.pad(rf[0:2].T, ((0, 0), (0, LANES - 2)))
        blk_start = jnp.arange(n_blocks, dtype=jnp.int32) * EXPERT_ROWS
        blk_expert = jnp.minimum(
            jnp.sum((pad_end[None, :] <= blk_start[:, None]).astype(jnp.int32), axis=1),
            N_EXPERTS - 1)
        n_used = (pad_end[-1:] // EXPERT_ROWS).astype(jnp.int32)
        onehot_e = blk_expert[:, None] == expert_ids[None, :]
        rows_before = blk_start - jnp.sum(jnp.where(onehot_e, pad_start[None, :], 0), axis=1)
        blk_rows = jnp.clip(jnp.sum(jnp.where(onehot_e, counts[None, :], 0), axis=1) - rows_before,
                            0, EXPERT_ROWS).astype(jnp.int32)

        buf = _dispatch(pos3, xn, jnp.zeros((n_blocks * EXPERT_ROWS, HALF_MODEL), jnp.uint32))
        yb = _experts(blk_expert, n_used, blk_rows, buf, w_expert_gate, w_expert_up,
                      w_expert_down, i)
        h = _combine_ple(pos3, h1, p.reshape(depth * t, PLE_DIM), gates, yb, w_ple[i].astype(BF16),
                         g_ple[i][None, :], w_ple_gate[i].astype(BF16), g_final[None, :],
                         layer=i, final_norm=(i == depth - 1))
    return h.reshape(bsz, seq, d)
```

```python
import functools
import math

import jax
import jax.numpy as jnp
from jax import lax
from jax.experimental import pallas as pl
from jax.experimental.pallas import tpu as pltpu

F32 = jnp.float32
BF16 = jnp.bfloat16
HIGHEST = lax.Precision.HIGHEST

D_MODEL = 1024
HEAD_DIM = 64
GLA_HEADS = 4
GLA_DK = 32
GLA_DV = 64
GLA_RANK = 16
GLA_TAU = 16.0
GLA_CHUNK = 64
DSA_HEADS = 4
IDX_HEADS = 8
IDX_DIM = 64
DSA_TOPK_MAX = 256
SWA_HEADS = 8
SWA_KV_HEADS = 2
SWA_GROUP = SWA_HEADS // SWA_KV_HEADS
WINDOW = 128
REL_BUCKETS = 32
REL_MAX_DIST = 128
N_GROUPS = 4
EXPERTS_PER_GROUP = 8
N_EXPERTS = N_GROUPS * EXPERTS_PER_GROUP
D_EXPERT = 512
PLE_DIM = 256
EPS = 1e-6

LANES = 128
SUBLANES = 8
PACK16 = 16

COL_IQ, COL_SQ = 0, 512
COL_GV, COL_GG, COL_DQ = 1024, 1280, 1536
COL_GQ, COL_GK, COL_GA, COL_DKV, COL_IKW, COL_SK, COL_SV = 1792, 1920, 2048, 2176, 2304, 2432, 2560
D_PROJ = 2688

TOKEN_TILE = 512
INPROJ_TILE = 512
ROUTER_TILE = 512
DISPATCH_TILE = 1024
EXPERT_ROWS = 512
GLA_BLOCK = 256
QB = 256
KC = 256
DMA_UNROLL = 8
NEG_BIG = -1e30
VMEM_LIMIT = 48 * 1024 * 1024


def _nt(a, b, precision=None):
    return lax.dot_general(a, b, (((1,), (1,)), ((), ())), precision=precision,
                           preferred_element_type=F32)


def _mm(a, b, precision=None):
    return jnp.dot(a, b, precision=precision, preferred_element_type=F32)


def _eye(n, dtype):
    r = lax.broadcasted_iota(jnp.int32, (n, n), 0)
    c = lax.broadcasted_iota(jnp.int32, (n, n), 1)
    return jnp.where(r == c, 1.0, 0.0).astype(dtype)


def _rms(x, g):
    return x * lax.rsqrt(jnp.mean(x * x, axis=-1, keepdims=True) + EPS) * g


HALF_MODEL = D_MODEL // 2
HIGH16 = 0xFFFF0000


def _pack_bf16_pairs(x):
    bits = pltpu.bitcast(x.astype(BF16).astype(F32), jnp.uint32)
    return (bits[:, HALF_MODEL:] & jnp.uint32(HIGH16)) | (bits[:, :HALF_MODEL] >> 16)


def _unpack_bf16_pairs(w):
    return pltpu.bitcast(w << 16, F32), pltpu.bitcast(w & jnp.uint32(HIGH16), F32)


def _inproj_kernel(h_ref, g_ref, w_ref, o_ref):
    a = _rms(h_ref[...], g_ref[...])
    o_ref[...] = _nt(a.astype(BF16), w_ref[...])


def _inproj(h, g, w):
    t = h.shape[0]
    return pl.pallas_call(
        _inproj_kernel,
        grid=(t // INPROJ_TILE,),
        in_specs=[
            pl.BlockSpec((INPROJ_TILE, D_MODEL), lambda i: (i, 0)),
            pl.BlockSpec((1, D_MODEL), lambda i: (0, 0)),
            pl.BlockSpec((D_PROJ, D_MODEL), lambda i: (0, 0)),
        ],
        out_specs=pl.BlockSpec((INPROJ_TILE, D_PROJ), lambda i: (i, 0)),
        out_shape=jax.ShapeDtypeStruct((t, D_PROJ), F32),
        compiler_params=pltpu.CompilerParams(
            dimension_semantics=("arbitrary",), vmem_limit_bytes=VMEM_LIMIT),
        name="inproj",
    )(h, g, w)


def _gla_kernel(q_ref, k_ref, v_ref, gg_ref, ga_ref, wal_ref, bal_ref, gn_ref, o_ref,
                state_ref, sc_all_ref, p_all_ref, b_ref):
    hk = GLA_HEADS * GLA_DK
    hv = GLA_HEADS * GLA_DV
    c = GLA_CHUNK
    half_c = c // 2
    assert 2 * c == LANES

    @pl.when(pl.program_id(1) == 0)
    def _():
        state_ref[...] = jnp.zeros_like(state_ref)

    @pl.when((pl.program_id(0) == 0) & (pl.program_id(1) == 0))
    def _():
        p_all_ref[...] = jnp.zeros_like(p_all_ref)

    pj = lax.broadcasted_iota(jnp.int32, (half_c, LANES), 0)
    pl_ = lax.broadcasted_iota(jnp.int32, (half_c, LANES), 1)
    pair_causal = jnp.where(pl_ < c, pl_, pl_ - c) <= jnp.where(pl_ < c, pj, pj + half_c)
    rv = lax.broadcasted_iota(jnp.int32, (hv, hk), 0) // GLA_DV
    ck = lax.broadcasted_iota(jnp.int32, (hv, hk), 1) // GLA_DK
    blockdiag_t = jnp.where(rv == ck, 1.0, 0.0).astype(F32)
    hr = lax.broadcasted_iota(jnp.int32, (SUBLANES, hk), 0)
    hl = lax.broadcasted_iota(jnp.int32, (SUBLANES, hk), 1) // GLA_DK
    head_rows = jnp.where(hr == hl, 1.0, 0.0).astype(BF16)
    eye_v = _eye(hv, BF16)

    ga_hi, ga_lo = _split_bf16(ga_ref[0])
    w_hi, w_lo = _split_bf16(wal_ref[...])
    z = _mm(ga_hi, w_hi) + _mm(ga_hi, w_lo) + _mm(ga_lo, w_hi) + bal_ref[...]
    log_a = (jnp.minimum(z, 0.0) - jnp.log1p(jnp.exp(-jnp.abs(z)))) * (1.0 / GLA_TAU)
    la_hi = log_a.astype(BF16)
    la_r1 = log_a - la_hi.astype(F32)
    la_mid = la_r1.astype(BF16)
    la_lo = (la_r1 - la_mid.astype(F32)).astype(BF16)
    rb = lax.broadcasted_iota(jnp.int32, (GLA_BLOCK, GLA_BLOCK), 0)
    cb = lax.broadcasted_iota(jnp.int32, (GLA_BLOCK, GLA_BLOCK), 1)
    tril = jnp.where((rb >= cb) & (rb // c == cb // c), 1.0, 0.0).astype(BF16)
    b_ref[...] = _mm(tril, la_hi) + _mm(tril, la_mid) + _mm(tril, la_lo)

    def chunk(ci):
        rows = slice(ci * c, (ci + 1) * c)
        p_ref, sc_ref = p_all_ref.at[ci], sc_all_ref.at[ci]
        q = q_ref[0, rows, :] * (GLA_DK ** -0.5)
        k = k_ref[0, rows, :]
        v = v_ref[0, rows, :]
        b = b_ref[rows, :]
        state_t = state_ref[...]
        o_inter = _nt((q * jnp.exp(b)).astype(BF16), state_t.astype(BF16))

        for t in range(c):
            ns = PACK16 * (t // PACK16 + 1)
            slot = 2 * (t % half_c) + t // half_c
            p = jnp.exp(b[t:t + 1, :] - b[:ns, :]) * k[:ns, :] * q[t:t + 1, :]
            p_ref[slot * c:slot * c + ns, :] = p.astype(BF16)
        rows_hs = _nt(head_rows, p_ref[...])
        for jj in range(half_c):
            for h in range(GLA_HEADS):
                sc_ref[h, jj:jj + 1, :] = rows_hs[h:h + 1, jj * LANES:(jj + 1) * LANES]
        vb = v.astype(BF16)
        zero_v = jnp.zeros((c, GLA_DV), BF16)
        o_heads = []
        for h in range(GLA_HEADS):
            vh = vb[:, h * GLA_DV:(h + 1) * GLA_DV]
            v_pair = jnp.concatenate([jnp.concatenate([vh, zero_v], axis=1),
                                      jnp.concatenate([zero_v, vh], axis=1)], axis=0)
            scores = jnp.where(pair_causal, sc_ref[h], 0.0).astype(BF16)
            o_pair = _mm(scores, v_pair)
            o_heads.append(jnp.concatenate([o_pair[:, :GLA_DV], o_pair[:, GLA_DV:]], axis=0))
        o = o_inter + jnp.concatenate(o_heads, axis=-1)

        b_last = b[c - 1:c, :]
        kd = (k * jnp.exp(b_last - b)).astype(BF16)
        v_t = _nt(eye_v, vb).astype(BF16)
        upd_t = _mm(v_t, kd)
        state_ref[...] = jnp.exp(b_last) * state_t + upd_t * blockdiag_t

        gg = gg_ref[0, rows, :]
        outs = []
        for h in range(GLA_HEADS):
            oh = o[:, h * GLA_DV:(h + 1) * GLA_DV]
            ms = jnp.mean(oh * oh, axis=-1, keepdims=True)
            outs.append(oh * lax.rsqrt(ms + EPS))
        on = jnp.concatenate(outs, axis=-1) * gn_ref[...]
        o_ref[0, rows, :] = (on * (gg * jax.nn.sigmoid(gg))).astype(o_ref.dtype)

    for ci in range(GLA_BLOCK // c):
        chunk(ci)


def _gla(proj3, wal, bal, gn):
    bsz, seq, _ = proj3.shape
    nb = seq // GLA_BLOCK

    def col(width, off):
        return pl.BlockSpec((1, GLA_BLOCK, width), lambda b, i: (b, i, off // width))

    return pl.pallas_call(
        _gla_kernel,
        grid=(bsz, nb),
        in_specs=[
            col(128, COL_GQ), col(128, COL_GK), col(256, COL_GV), col(256, COL_GG), col(128, COL_GA),
            pl.BlockSpec((128, 128), lambda b, i: (0, 0)),
            pl.BlockSpec((1, 128), lambda b, i: (0, 0)),
            pl.BlockSpec((1, 256), lambda b, i: (0, 0)),
        ],
        out_specs=pl.BlockSpec((1, GLA_BLOCK, 256), lambda b, i: (b, i, 0)),
        out_shape=jax.ShapeDtypeStruct((bsz, seq, GLA_HEADS * GLA_DV), BF16),
        scratch_shapes=[
            pltpu.VMEM((GLA_HEADS * GLA_DV, GLA_HEADS * GLA_DK), F32),
            pltpu.VMEM((GLA_BLOCK // GLA_CHUNK, GLA_HEADS, GLA_CHUNK // 2, 2 * GLA_CHUNK), F32),
            pltpu.VMEM((GLA_BLOCK // GLA_CHUNK, GLA_CHUNK * GLA_CHUNK, GLA_HEADS * GLA_DK), BF16),
            pltpu.VMEM((GLA_BLOCK, GLA_HEADS * GLA_DK), F32),
        ],
        compiler_params=pltpu.CompilerParams(
            dimension_semantics=("arbitrary", "arbitrary"), vmem_limit_bytes=VMEM_LIMIT),
        name="gla",
    )(proj3, proj3, proj3, proj3, proj3, wal, bal, gn)


DSA_BISECT_STEPS = 14
DSA_ACCUMULATORS = 4
DSA_WALK_UNCONDITIONAL = 2
DSA_PV_PARTS = 4


def _dsa_kernel(dq_ref, kv_ref, iq_ref, ikw_ref, btile_ref, o_ref,
                sc_ref, lg_ref, p_ref, kb_ref, vb_ref, mp_ref, lp_ref, acc_ref, cut_ref, *, n_sel):
    j = pl.program_id(1)
    nkc = ((j + 1) * QB + KC - 1) // KC
    seq = kv_ref.shape[1]
    ksel = float(n_sel)

    @pl.when(j == 0)
    def _():
        kv = kv_ref[0]
        kb_ref[...] = kv[:, :HEAD_DIM].astype(BF16)
        vb_ref[...] = kv[:, HEAD_DIM:].astype(BF16)

    def rows(c):
        return pl.ds(pl.multiple_of(c * KC, KC), KC)

    wide = DSA_ACCUMULATORS * SUBLANES

    def fold(op, x):
        return op(x.reshape(KC // wide, wide, QB), axis=0)

    def all8(op, xw, roll_op):
        x8 = op(xw.reshape(DSA_ACCUMULATORS, SUBLANES, QB), axis=0)
        for shift in (4, 2, 1):
            x8 = roll_op(x8, pltpu.roll(x8, shift, 0))
        return x8

    def widen(x8):
        return jnp.concatenate([x8] * DSA_ACCUMULATORS, axis=0)

    zeros8 = jnp.zeros((SUBLANES, QB), F32)
    zerosw = jnp.zeros((wide, QB), F32)
    infw = jnp.full((wide, QB), jnp.inf, F32)

    ikw_q = ikw_ref[0, pl.ds(pl.multiple_of(j * QB, QB), QB), :]
    w_t = ikw_q.T[IDX_DIM:IDX_DIM + IDX_HEADS, :] * (IDX_HEADS ** -0.5 * IDX_DIM ** -0.5)
    iq = iq_ref[0].astype(BF16)
    iq_heads = [iq[:, h * IDX_DIM:(h + 1) * IDX_DIM] for h in range(IDX_HEADS)]

    s_loc = lax.broadcasted_iota(jnp.int32, (KC, QB), 0)
    t_glob = j * QB + lax.broadcasted_iota(jnp.int32, (KC, QB), 1)
    s_loc_f = s_loc.astype(F32)

    def score_chunk(c, st):
        mn8, mx8, ge8, gt8 = st
        ik = ikw_ref[0, rows(c), :][:, :IDX_DIM].astype(BF16)
        acc = jnp.zeros((KC, QB), F32)
        for h in range(IDX_HEADS):
            rel = _nt(ik, iq_heads[h])
            acc = acc + jnp.maximum(rel, 0.0) * w_t[h:h + 1, :]
        adm = s_loc + c * KC <= t_glob
        blk = jnp.where(adm, acc, -jnp.inf)
        sc_ref[rows(c), :] = blk
        mn8 = jnp.minimum(mn8, fold(jnp.min, jnp.where(adm, acc, jnp.inf)))
        mx8 = jnp.maximum(mx8, fold(jnp.max, blk))
        ge8 = ge8 + fold(jnp.sum, jnp.where(blk >= 0.0, 1.0, 0.0))
        gt8 = gt8 + fold(jnp.sum, jnp.where(blk > 0.0, 1.0, 0.0))
        return mn8, mx8, ge8, gt8

    def paired(body, init):
        def two(i, st):
            return body(2 * i + 1, body(2 * i, st))
        st = lax.fori_loop(0, nkc // 2, two, init)
        return lax.cond(nkc % 2 == 1, lambda s: body(nkc - 1, s), lambda s: s, st)

    mn8, mx8, ge8, gt8 = paired(score_chunk, (infw, -infw, zerosw, zerosw))
    row_min = all8(jnp.min, mn8, jnp.minimum)
    row_max = all8(jnp.max, mx8, jnp.maximum)
    c_ge0, c_gt0 = all8(jnp.sum, ge8, jnp.add), all8(jnp.sum, gt8, jnp.add)

    def blocks(c):
        return sc_ref[rows(c), :].reshape(KC // wide, wide, QB)

    def count_ge(x8):
        xw = widen(x8)[None]

        def body(c, cw):
            return cw + jnp.sum(jnp.where(blocks(c) >= xw, 1.0, 0.0), axis=0)
        return all8(jnp.sum, lax.fori_loop(0, nkc, body, zerosw), jnp.add)

    def min_ge(x8):
        xw = widen(x8)[None]

        def body(c, mw):
            blk = blocks(c)
            return jnp.minimum(mw, jnp.min(jnp.where(blk >= xw, blk, jnp.inf), axis=0))
        return all8(jnp.min, lax.fori_loop(0, nkc, body, infw), jnp.minimum)

    def pass_gt(x8):
        xw = widen(x8)[None]

        def body(c, st):
            cw, mw = st
            blk = blocks(c)
            hit = blk > xw
            return (cw + jnp.sum(jnp.where(hit, 1.0, 0.0), axis=0),
                    jnp.minimum(mw, jnp.min(jnp.where(hit, blk, jnp.inf), axis=0)))
        cw, mw = lax.fori_loop(0, nkc, body, (zerosw, infw))
        return all8(jnp.sum, cw, jnp.add), all8(jnp.min, mw, jnp.minimum)

    n_adm = (j * QB + lax.broadcasted_iota(jnp.int32, (SUBLANES, QB), 1) + 1).astype(F32)
    at_zero = (c_gt0 < ksel) & (c_ge0 >= ksel)
    above = c_gt0 >= ksel
    lo = jnp.where(above | at_zero, 0.0, row_min)
    c_lo = jnp.where(above | at_zero, c_ge0, n_adm)
    settled = at_zero | (n_adm <= ksel)
    hi = jnp.where(settled, lo, jnp.where(above, row_max, 0.0))

    def bisect(_, st):
        lo, hi, c_lo = st
        mid = lo + (hi - lo) * 0.5
        cnt = count_ge(mid)
        up = cnt >= ksel
        return jnp.where(up, mid, lo), jnp.where(up, hi, mid), jnp.where(up, cnt, c_lo)

    lo, _, c_lo = lax.fori_loop(0, DSA_BISECT_STEPS, bisect, (lo, hi, c_lo))

    v0 = min_ge(lo)
    done0 = jnp.where(settled | (c_lo == ksel), 1.0, 0.0)
    c_gt_init = jnp.where(at_zero, c_gt0, 0.0)

    def walk_cond(st):
        return jnp.min(st[3]) < 0.5

    def walk_body(st):
        v, c_ge, c_gt, dn = st
        cnt, vnext = pass_gt(v)
        live = dn < 0.5
        fin = live & (cnt < ksel)
        step = live & (cnt >= ksel)
        c_gt = jnp.where(fin, cnt, c_gt)
        dn = jnp.where(fin, 1.0, dn)
        v = jnp.where(step, vnext, v)
        c_ge = jnp.where(step, cnt, c_ge)
        return v, c_ge, c_gt, dn

    walk = (v0, c_lo, c_gt_init, done0)
    for _ in range(DSA_WALK_UNCONDITIONAL):
        walk = walk_body(walk)
    tau8, c_ge, c_gt, _ = lax.while_loop(walk_cond, walk_body, walk)

    need = ksel - c_gt
    cut_ref[...] = jnp.full(cut_ref.shape, float(seq), F32)

    @pl.when(jnp.max(c_ge) > ksel)
    def _():
        s_grp = lax.broadcasted_iota(jnp.int32, (KC // wide, wide, QB), 0) * wide
        s_sub = lax.broadcasted_iota(jnp.int32, (KC // wide, wide, QB), 1)
        s_idx = (s_grp + s_sub).astype(F32)
        tauw = widen(tau8)[None]

        def count_ties_below(m8):
            mw = widen(m8)[None]

            def body(c, cw):
                hit = (blocks(c) == tauw) & (s_idx + (c * KC).astype(F32) < mw)
                return cw + jnp.sum(jnp.where(hit, 1.0, 0.0), axis=0)
            return all8(jnp.sum, lax.fori_loop(0, nkc, body, zerosw), jnp.add)

        def idx_bisect(_, lh):
            lo_m, hi_m = lh
            mid = jnp.floor((lo_m + hi_m) * 0.5)
            ok = count_ties_below(mid) >= need
            return jnp.where(ok, lo_m, mid), jnp.where(ok, mid, hi_m)

        _, hi_m = lax.fori_loop(0, int(math.log2(seq)) + 1, idx_bisect,
                                (zeros8, zeros8 + float(seq)))
        cut_ref[...] = jnp.where(c_ge > ksel, hi_m, float(seq))

    tau = tau8[0:1, :]
    cut = cut_ref[0:1, :]

    q = (dq_ref[0] * (HEAD_DIM ** -0.5)).astype(BF16)
    q4 = jnp.concatenate([q[:, h * HEAD_DIM:(h + 1) * HEAD_DIM] for h in range(DSA_HEADS)],
                         axis=0)
    mp_ref[...] = jnp.full(mp_ref.shape, NEG_BIG, F32)
    lp_ref[...] = jnp.zeros(lp_ref.shape, F32)

    pv_parts = min(DSA_PV_PARTS, seq // KC)
    part = seq // pv_parts

    def cols(c):
        return pl.ds(pl.multiple_of(c * KC, KC), KC)

    def lane_fold(op, x):
        out = x[:, :LANES]
        for g in range(1, KC // LANES):
            out = op(out, x[:, g * LANES:(g + 1) * LANES])
        return out

    def logits_chunk(c, carry):
        blk = sc_ref[rows(c), :]
        sidx = s_loc_f + (c * KC).astype(F32)
        sel = (blk > tau) | ((blk == tau) & (sidx < cut))
        neg_t = jnp.where(sel, 0.0, NEG_BIG).T
        lg4 = _nt(q4, kb_ref[rows(c), :])
        which = [[jnp.clip((c * (KC // WINDOW) + b) - (j * (QB // WINDOW) + a) + 2, 0, 2)
                  for b in range(KC // WINDOW)] for a in range(QB // WINDOW)]
        for h in range(DSA_HEADS):
            hq = slice(h * QB, (h + 1) * QB)
            bias = jnp.concatenate(
                [jnp.concatenate([btile_ref[h, w] for w in which_a], axis=1)
                 for which_a in which], axis=0)
            lg = lg4[hq, :] + bias + neg_t
            lg_ref[hq, cols(c)] = lg
            mp_ref[h] = jnp.maximum(mp_ref[h], lane_fold(jnp.maximum, lg))
        return carry

    paired(logits_chunk, 0)
    m_rows = [jnp.max(mp_ref[h], axis=-1, keepdims=True) for h in range(DSA_HEADS)]

    def probs_chunk(c, carry):
        for h in range(DSA_HEADS):
            hq = slice(h * QB, (h + 1) * QB)
            p = jnp.exp(lg_ref[hq, cols(c)] - m_rows[h])
            lp_ref[h] = lp_ref[h] + lane_fold(jnp.add, p)
            p_ref[hq, cols(c)] = p.astype(BF16)
        return carry

    def zero_chunk(c, carry):
        p_ref[:, cols(c)] = jnp.zeros((DSA_HEADS * QB, KC), BF16)
        return carry

    lax.fori_loop(0, nkc, probs_chunk, 0)
    n_parts = ((j + 1) * QB + part - 1) // part
    lax.fori_loop(nkc, n_parts * (part // KC), zero_chunk, 0)
    acc_ref[...] = _mm(p_ref[:, :part], vb_ref[:part, :])
    for k in range(1, pv_parts):
        @pl.when(k < n_parts)
        def _(k=k):
            acc_ref[...] = acc_ref[...] + _mm(p_ref[:, k * part:(k + 1) * part],
                                              vb_ref[k * part:(k + 1) * part, :])

    outs = []
    for h in range(DSA_HEADS):
        l = jnp.sum(lp_ref[h], axis=-1, keepdims=True)
        outs.append(acc_ref[h * QB:(h + 1) * QB, :] / l)
    o_ref[0] = jnp.concatenate(outs, axis=-1).astype(o_ref.dtype)


def _dsa(proj3, btiles):
    bsz, seq, _ = proj3.shape
    n_sel = min(DSA_TOPK_MAX, seq // 4)
    return pl.pallas_call(
        functools.partial(_dsa_kernel, n_sel=n_sel),
        grid=(bsz, seq // QB),
        in_specs=[
            pl.BlockSpec((1, QB, 256), lambda b, j: (b, j, COL_DQ // 256)),
            pl.BlockSpec((1, seq, 128), lambda b, j: (b, 0, COL_DKV // 128)),
            pl.BlockSpec((1, QB, 512), lambda b, j: (b, j, COL_IQ // 512)),
            pl.BlockSpec((1, seq, 128), lambda b, j: (b, 0, COL_IKW // 128)),
            pl.BlockSpec((DSA_HEADS, 3, WINDOW, WINDOW), lambda b, j: (0, 0, 0, 0)),
        ],
        out_specs=pl.BlockSpec((1, QB, 256), lambda b, j: (b, j, 0)),
        out_shape=jax.ShapeDtypeStruct((bsz, seq, DSA_HEADS * HEAD_DIM), BF16),
        scratch_shapes=[
            pltpu.VMEM((seq, QB), F32),
            pltpu.VMEM((DSA_HEADS * QB, seq), F32),
            pltpu.VMEM((DSA_HEADS * QB, seq), BF16),
            pltpu.VMEM((seq, HEAD_DIM), BF16),
            pltpu.VMEM((seq, HEAD_DIM), BF16),
            pltpu.VMEM((DSA_HEADS, QB, LANES), F32),
            pltpu.VMEM((DSA_HEADS, QB, LANES), F32),
            pltpu.VMEM((DSA_HEADS * QB, HEAD_DIM), F32),
            pltpu.VMEM((SUBLANES, QB), F32),
        ],
        compiler_params=pltpu.CompilerParams(
            dimension_semantics=("arbitrary", "arbitrary"), vmem_limit_bytes=VMEM_LIMIT),
        name="dsa",
    )(proj3, proj3, proj3, proj3, btiles)


def _swa_kernel(sink_ref, q_ref, kc_ref, kp_ref, vc_ref, vp_ref, bias_ref, o_ref):
    n = pl.program_id(1)
    q = (q_ref[0] * (HEAD_DIM ** -0.5)).astype(BF16)
    k2 = jnp.concatenate([kp_ref[0], kc_ref[0]], axis=0).astype(BF16)
    v2 = jnp.concatenate([vp_ref[0], vc_ref[0]], axis=0).astype(BF16)
    ones = jnp.ones((2 * WINDOW, HEAD_DIM), BF16)
    first = jnp.where(n == 0, 1, 0)
    outs = []
    for h in range(SWA_HEADS):
        kvh = h // SWA_GROUP
        kh = k2[:, kvh * HEAD_DIM:(kvh + 1) * HEAD_DIM]
        v_ones = jnp.concatenate([v2[:, kvh * HEAD_DIM:(kvh + 1) * HEAD_DIM], ones], axis=1)
        lg = _nt(q[:, h * HEAD_DIM:(h + 1) * HEAD_DIM], kh) + bias_ref[first, h]
        sink = sink_ref[h]
        m = jnp.maximum(jnp.max(lg, axis=-1, keepdims=True), sink)
        num_den = _mm(jnp.exp(lg - m).astype(BF16), v_ones)
        outs.append(num_den[:, :HEAD_DIM] / (num_den[:, HEAD_DIM:] + jnp.exp(sink - m)))
    o_ref[0] = jnp.concatenate(outs, axis=-1).astype(o_ref.dtype)


def _swa(proj3, sinks, bias_nat):
    bsz, seq, _ = proj3.shape
    return pl.pallas_call(
        _swa_kernel,
        grid=(bsz, seq // WINDOW),
        in_specs=[
            pl.BlockSpec(memory_space=pltpu.SMEM),
            pl.BlockSpec((1, WINDOW, 512), lambda b, n: (b, n, COL_SQ // 512)),
            pl.BlockSpec((1, WINDOW, 128), lambda b, n: (b, n, COL_SK // 128)),
            pl.BlockSpec((1, WINDOW, 128), lambda b, n: (b, jnp.maximum(n - 1, 0), COL_SK // 128)),
            pl.BlockSpec((1, WINDOW, 128), lambda b, n: (b, n, COL_SV // 128)),
            pl.BlockSpec((1, WINDOW, 128), lambda b, n: (b, jnp.maximum(n - 1, 0), COL_SV // 128)),
            pl.BlockSpec((2, SWA_HEADS, WINDOW, 2 * WINDOW), lambda b, n: (0, 0, 0, 0)),
        ],
        out_specs=pl.BlockSpec((1, WINDOW, 512), lambda b, n: (b, n, 0)),
        out_shape=jax.ShapeDtypeStruct((bsz, seq, SWA_HEADS * HEAD_DIM), BF16),
        compiler_params=pltpu.CompilerParams(
            dimension_semantics=("arbitrary", "arbitrary"), vmem_limit_bytes=VMEM_LIMIT),
        name="swa",
    )(sinks, proj3, proj3, proj3, proj3, proj3, bias_nat)


ROUTE_OFF = N_GROUPS
ROUTE_ROWS = 48


def _split_bf16(x):
    hi = x.astype(BF16)
    lo = (x - hi.astype(F32)).astype(BF16)
    return hi, lo


def _outproj_router_kernel(h_ref, og_ref, od_ref, os_ref, wo_ref, gffn_ref, wr_hi_ref, wr_lo_ref,
                           br_ref, h1_ref, xn_ref, ri_ref, rf_ref, cnt_ref, run_ref):
    tm = h_ref.shape[0]

    @pl.when(pl.program_id(0) == 0)
    def _():
        run_ref[...] = jnp.zeros_like(run_ref)

    o = jnp.concatenate([og_ref[...], od_ref[...], os_ref[...]], axis=-1)
    h1 = h_ref[...] + _mm(o, wo_ref[...])
    h1_ref[...] = h1
    xn = _rms(h1, gffn_ref[...])
    xn_ref[...] = _pack_bf16_pairs(xn)

    x_hi, x_lo = _split_bf16(xn)
    lg = (_mm(x_hi, wr_hi_ref[...]) + _mm(x_lo, wr_hi_ref[...]) + _mm(x_hi, wr_lo_ref[...])
          + br_ref[...]).T[:ROUTE_ROWS, :]

    row = lax.broadcasted_iota(jnp.int32, lg.shape, 0)
    row_f = row.astype(F32)
    ninf = -jnp.inf

    def first_max(x):
        m = jnp.max(x, axis=0, keepdims=True)
        idx = jnp.min(jnp.where(x == m, row_f, float(ROUTE_ROWS)), axis=0, keepdims=True)
        return m, idx

    gl = jnp.where(row < N_GROUPS, lg, ninf)
    gmax, gsel = first_max(gl)
    g_w = 1.0 / jnp.sum(jnp.exp(gl - gmax), axis=0, keepdims=True)
    e_lo = ROUTE_OFF + EXPERTS_PER_GROUP * gsel
    el = jnp.where((row_f >= e_lo) & (row_f < e_lo + EXPERTS_PER_GROUP), lg, ninf)
    m1, i1 = first_max(el)
    eden = jnp.sum(jnp.exp(el - m1), axis=0, keepdims=True)
    m2, i2 = first_max(jnp.where(row_f == i1, ninf, el))
    p1 = 1.0 / eden
    p2 = jnp.exp(m2 - m1) / eden
    gate1 = g_w * p1 / (p1 + p2)
    gate2 = g_w * p2 / (p1 + p2)

    onehot = jnp.where((row_f == i1) | (row_f == i2), 1.0, 0.0)
    rr = lax.broadcasted_iota(jnp.int32, (tm, tm), 0)
    cc = lax.broadcasted_iota(jnp.int32, (tm, tm), 1)
    earlier = jnp.where(rr < cc, 1.0, 0.0).astype(BF16)
    run = run_ref[...]
    before = _mm(onehot.astype(BF16), earlier) + jnp.concatenate([run] * (tm // LANES), axis=1)
    rank1 = jnp.sum(jnp.where(row_f == i1, before, 0.0), axis=0, keepdims=True)
    rank2 = jnp.sum(jnp.where(row_f == i2, before, 0.0), axis=0, keepdims=True)
    run_ref[...] = run + jnp.sum(onehot, axis=1, keepdims=True)
    cnt_ref[...] = run_ref[...]

    out_row = lax.broadcasted_iota(jnp.int32, (SUBLANES, tm), 0)
    ints = jnp.where(out_row == 0, i1 - ROUTE_OFF,
                     jnp.where(out_row == 1, i2 - ROUTE_OFF,
                               jnp.where(out_row == 2, rank1,
                                         jnp.where(out_row == 3, rank2, 0.0))))
    ri_ref[...] = ints.astype(jnp.int32)
    rf_ref[...] = jnp.where(out_row == 0, gate1, jnp.where(out_row == 1, gate2, 0.0))


def _outproj_router(h, og, od, os_, wo, gffn, wr_hi, wr_lo, br):
    t = h.shape[0]
    tm = ROUTER_TILE
    row = lambda w: pl.BlockSpec((tm, w), lambda i: (i, 0))
    col = lambda r: pl.BlockSpec((r, tm), lambda i: (0, i))
    full = lambda a, b: pl.BlockSpec((a, b), lambda i: (0, 0))
    return pl.pallas_call(
        _outproj_router_kernel,
        grid=(t // tm,),
        in_specs=[row(D_MODEL), row(256), row(256), row(512), full(D_MODEL, D_MODEL),
                  full(1, D_MODEL), full(D_MODEL, LANES), full(D_MODEL, LANES), full(1, LANES)],
        out_specs=[row(D_MODEL), row(HALF_MODEL), col(SUBLANES), col(SUBLANES),
                   full(ROUTE_ROWS, LANES)],
        out_shape=[
            jax.ShapeDtypeStruct((t, D_MODEL), F32),
            jax.ShapeDtypeStruct((t, HALF_MODEL), jnp.uint32),
            jax.ShapeDtypeStruct((SUBLANES, t), jnp.int32),
            jax.ShapeDtypeStruct((SUBLANES, t), F32),
            jax.ShapeDtypeStruct((ROUTE_ROWS, LANES), F32),
        ],
        scratch_shapes=[pltpu.VMEM((ROUTE_ROWS, LANES), F32)],
        compiler_params=pltpu.CompilerParams(
            dimension_semantics=("arbitrary",), vmem_limit_bytes=VMEM_LIMIT),
        name="outproj_router",
    )(h, og, od, os_, wo, gffn, wr_hi, wr_lo, br)


def _dispatch_kernel(pos_ref, xn_ref, buf_in_ref, buf_ref, sem):
    del buf_in_ref
    tm = xn_ref.shape[0]

    def row_copy(r, k):
        dst = pos_ref[0, 0, 2 * r + k]
        return pltpu.make_async_copy(xn_ref.at[pl.ds(r, 1)], buf_ref.at[pl.ds(dst, 1)], sem)

    def issue(g, carry):
        for u in range(DMA_UNROLL):
            row_copy(g * DMA_UNROLL + u, 0).start()
            row_copy(g * DMA_UNROLL + u, 1).start()
        return carry

    def drain(g, carry):
        for u in range(DMA_UNROLL):
            row_copy(g * DMA_UNROLL + u, 0).wait()
            row_copy(g * DMA_UNROLL + u, 1).wait()
        return carry

    lax.fori_loop(0, tm // DMA_UNROLL, issue, 0)
    lax.fori_loop(0, tm // DMA_UNROLL, drain, 0)


def _dispatch(pos3, xn, buf0):
    t = xn.shape[0]
    tm = DISPATCH_TILE
    pos3 = pos3.reshape(t // tm, 1, 2 * tm)
    return pl.pallas_call(
        _dispatch_kernel,
        grid=(t // tm,),
        in_specs=[
            pl.BlockSpec((1, 1, 2 * tm), lambda i: (i, 0, 0), memory_space=pltpu.SMEM),
            pl.BlockSpec((tm, HALF_MODEL), lambda i: (i, 0)),
            pl.BlockSpec(memory_space=pl.ANY),
        ],
        out_specs=pl.BlockSpec(memory_space=pl.ANY),
        out_shape=jax.ShapeDtypeStruct(buf0.shape, buf0.dtype),
        scratch_shapes=[pltpu.SemaphoreType.DMA(())],
        input_output_aliases={2: 0},
        compiler_params=pltpu.CompilerParams(
            dimension_semantics=("arbitrary",), vmem_limit_bytes=VMEM_LIMIT),
        name="dispatch",
    )(pos3, xn, buf0)


def _expert_kernel(be_ref, nu_ref, rows_ref, slot_ref, next_ref, x_ref, wg_hbm, wu_hbm, wd_hbm,
                   y_ref, wg_f, wu_f, wd_f, wg_s, wu_s, wd_s, sems, *, layer):
    i = pl.program_id(0)
    used = i < nu_ref[0]
    expert = be_ref[i]
    new_expert = (i == 0) | (expert != be_ref[jnp.maximum(i - 1, 0)])
    slot = slot_ref[i]

    def weight_copies(e, s):
        return (pltpu.make_async_copy(wg_hbm.at[layer, e], wg_f.at[s], sems.at[s, 0]),
                pltpu.make_async_copy(wu_hbm.at[layer, e], wu_f.at[s], sems.at[s, 1]),
                pltpu.make_async_copy(wd_hbm.at[layer, e], wd_f.at[s], sems.at[s, 2]))

    @pl.when(i == 0)
    def _():
        for cp in weight_copies(expert, slot):
            cp.start()

    @pl.when(used & new_expert)
    def _():
        for cp in weight_copies(expert, slot):
            cp.wait()
        wg_s[...] = wg_f[slot].astype(BF16)
        wu_s[...] = wu_f[slot].astype(BF16)
        wd_s[...] = wd_f[slot].astype(BF16)
        nxt = next_ref[i]

        @pl.when(nxt >= 0)
        def _():
            for cp in weight_copies(nxt, 1 - slot):
                cp.start()

    def ffn(nrows):
        x_lo, x_hi = (part.astype(BF16) for part in _unpack_bf16_pairs(x_ref[:nrows, :]))
        g = _mm(x_lo, wg_s[:HALF_MODEL, :]) + _mm(x_hi, wg_s[HALF_MODEL:, :])
        u = _mm(x_lo, wu_s[:HALF_MODEL, :]) + _mm(x_hi, wu_s[HALF_MODEL:, :])
        hmid = (g * jax.nn.sigmoid(g)) * u
        y_ref[:nrows, :] = _pack_bf16_pairs(_mm(hmid.astype(BF16), wd_s[...]))

    half_rows = EXPERT_ROWS // 2
    short = rows_ref[i] <= half_rows

    @pl.when(used & jnp.logical_not(short))
    def _():
        ffn(EXPERT_ROWS)

    @pl.when(used & short)
    def _():
        ffn(half_rows)
        y_ref[half_rows:, :] = jnp.zeros((EXPERT_ROWS - half_rows, HALF_MODEL), y_ref.dtype)

    @pl.when(i >= nu_ref[0])
    def _():
        y_ref[...] = jnp.zeros_like(y_ref)


def _experts(blk_expert, n_used, blk_rows, buf, wg, wu, wd, layer):
    nrows = buf.shape[0]
    nblk = nrows // EXPERT_ROWS
    idx = jnp.arange(nblk, dtype=jnp.int32)
    change = (idx == 0) | (blk_expert != jnp.roll(blk_expert, 1))
    slot = (jnp.cumsum(change.astype(jnp.int32)) - 1) % 2
    later_run = (idx[None, :] > idx[:, None]) & change[None, :] & (idx[None, :] < n_used[0])
    nxt = jnp.where(jnp.any(later_run, axis=1), blk_expert[jnp.argmax(later_run, axis=1)], -1)
    block = lambda i, *_: (i, 0)
    any_space = pl.BlockSpec(memory_space=pl.ANY)
    return pl.pallas_call(
        functools.partial(_expert_kernel, layer=layer),
        grid_spec=pltpu.PrefetchScalarGridSpec(
            num_scalar_prefetch=5,
            grid=(nblk,),
            in_specs=[pl.BlockSpec((EXPERT_ROWS, HALF_MODEL), block),
                      any_space, any_space, any_space],
            out_specs=pl.BlockSpec((EXPERT_ROWS, HALF_MODEL), block),
            scratch_shapes=[
                pltpu.VMEM((2, D_MODEL, D_EXPERT), F32),
                pltpu.VMEM((2, D_MODEL, D_EXPERT), F32),
                pltpu.VMEM((2, D_EXPERT, D_MODEL), F32),
                pltpu.VMEM((D_MODEL, D_EXPERT), BF16),
                pltpu.VMEM((D_MODEL, D_EXPERT), BF16),
                pltpu.VMEM((D_EXPERT, D_MODEL), BF16),
                pltpu.SemaphoreType.DMA((2, 3)),
            ],
        ),
        out_shape=jax.ShapeDtypeStruct((nrows, HALF_MODEL), jnp.uint32),
        compiler_params=pltpu.CompilerParams(
            dimension_semantics=("arbitrary",), vmem_limit_bytes=VMEM_LIMIT),
        name="experts",
    )(blk_expert, n_used, blk_rows, slot.astype(jnp.int32), nxt.astype(jnp.int32),
      buf, wg, wu, wd)


def _combine_ple_kernel(pos_ref, pos_next_ref, h1_ref, p_ref, rf_ref, yb_ref, wple_ref, gple_ref,
                        wpg_ref, gfin_ref, o_ref, ybuf, sems, *, final_norm):
    tm = h1_ref.shape[0]
    i = pl.program_id(0)
    last = pl.num_programs(0) - 1
    slot = lax.rem(i, 2)

    def row_copy(tile_pos_ref, s, r, k):
        src = tile_pos_ref[0, 0, 2 * r + k]
        return pltpu.make_async_copy(
            yb_ref.at[pl.ds(src, 1)], ybuf.at[s, k, pl.ds(r, 1)], sems.at[s])

    def looped(tile_pos_ref, s, op):
        def body(g, carry):
            for u in range(DMA_UNROLL):
                op(row_copy(tile_pos_ref, s, g * DMA_UNROLL + u, 0))
                op(row_copy(tile_pos_ref, s, g * DMA_UNROLL + u, 1))
            return carry
        lax.fori_loop(0, tm // DMA_UNROLL, body, 0)

    @pl.when(i == 0)
    def _():
        looped(pos_ref, 0, lambda cp: cp.start())

    looped(pos_ref, slot, lambda cp: cp.wait())
    for r in range(tm):
        row_copy(pos_next_ref, 1 - slot, r, 0).start()
        row_copy(pos_next_ref, 1 - slot, r, 1).start()
    e = _rms(_mm(p_ref[...].astype(BF16), wple_ref[...]), gple_ref[...])

    rf = rf_ref[...]
    y0_lo, y0_hi = _unpack_bf16_pairs(ybuf[slot, 0])
    y1_lo, y1_hi = _unpack_bf16_pairs(ybuf[slot, 1])
    g0, g1 = rf[:, 0:1], rf[:, 1:2]
    moe = jnp.concatenate([y0_lo * g0 + y1_lo * g1, y0_hi * g0 + y1_hi * g1], axis=-1)
    h2 = h1_ref[...] + moe
    h3 = h2 + e * jax.nn.sigmoid(_mm(h2.astype(BF16), wpg_ref[...]))
    if final_norm:
        h3 = _rms(h3, gfin_ref[...])
    o_ref[...] = h3

    @pl.when(i == last)
    def _():
        looped(pos_next_ref, 1 - slot, lambda cp: cp.wait())


def _combine_ple(pos3, h1, p, rf, yb, wple, gple, wpg, gfin, layer, final_norm):
    t = h1.shape[0]
    tm = TOKEN_TILE
    row = lambda w: pl.BlockSpec((tm, w), lambda i: (i, 0))
    full = lambda a, b: pl.BlockSpec((a, b), lambda i: (0, 0))
    return pl.pallas_call(
        functools.partial(_combine_ple_kernel, final_norm=final_norm),
        grid=(t // tm,),
        in_specs=[
            pl.BlockSpec((1, 1, 2 * tm), lambda i: (i, 0, 0), memory_space=pltpu.SMEM),
            pl.BlockSpec((1, 1, 2 * tm), lambda i: (jnp.minimum(i + 1, t // tm - 1), 0, 0),
                         memory_space=pltpu.SMEM),
            row(D_MODEL),
            pl.BlockSpec((tm, PLE_DIM), lambda i: (layer * (t // tm) + i, 0)),
            row(LANES),
            pl.BlockSpec(memory_space=pl.ANY),
            full(PLE_DIM, D_MODEL), full(1, D_MODEL), full(D_MODEL, D_MODEL), full(1, D_MODEL),
        ],
        out_specs=row(D_MODEL),
        out_shape=jax.ShapeDtypeStruct((t, D_MODEL), F32),
        scratch_shapes=[pltpu.VMEM((2, 2, tm, HALF_MODEL), jnp.uint32),
                        pltpu.SemaphoreType.DMA((2,))],
        compiler_params=pltpu.CompilerParams(
            dimension_semantics=("arbitrary",), vmem_limit_bytes=VMEM_LIMIT),
        name="combine_ple",
    )(pos3, pos3, h1, p, rf, yb, wple, gple, wpg, gfin)


def _rel_bucket(dist):
    n = jnp.maximum(dist, 0)
    max_exact = REL_BUCKETS // 2
    nf = jnp.maximum(n, 1).astype(F32)
    large = max_exact + (jnp.log(nf / max_exact) / math.log(REL_MAX_DIST / max_exact)
                         * (REL_BUCKETS - max_exact)).astype(jnp.int32)
    large = jnp.minimum(large, REL_BUCKETS - 1)
    return jnp.where(n < max_exact, n, large)


def _bias_tiles(rel_bias):
    qi = jnp.arange(WINDOW)[:, None]
    kj = jnp.arange(2 * WINDOW)[None, :]
    bucket = _rel_bucket(qi + WINDOW - kj)
    onehot = (bucket[..., None] == jnp.arange(REL_BUCKETS)).astype(F32)
    nat = jnp.einsum('qkb,bh->hqk', onehot, rel_bias, precision=HIGHEST)
    dsa = nat[:DSA_HEADS]
    far = jnp.broadcast_to(rel_bias[REL_BUCKETS - 1, :DSA_HEADS][:, None, None],
                           (DSA_HEADS, WINDOW, WINDOW))
    dist = qi + WINDOW - kj
    in_window = (dist >= 0) & (dist < WINDOW)
    swa = nat[DSA_HEADS:]
    swa_tiles = jnp.stack([jnp.where(in_window, swa, -jnp.inf),
                           jnp.where(in_window & (kj >= WINDOW), swa, -jnp.inf)], axis=0)
    return jnp.stack([far, dsa[:, :, :WINDOW], dsa[:, :, WINDOW:]], axis=1), swa_tiles


def _pack_w_in(w):
    sizes = (128, 128, 256, 256, 16, 256, 64, 64, 512, 64, 8, 512, 128, 128)
    offs = [0]
    for s in sizes:
        offs.append(offs[-1] + s)
    wt = w.T
    gq, gk, gv, gg, ga, dq, dk, dv, iq, ik, iw, sq, sk, sv = (
        wt[offs[n]:offs[n + 1]] for n in range(len(sizes)))
    z = lambda n: jnp.zeros((n, w.shape[0]), w.dtype)
    packed = jnp.concatenate(
        [iq, sq, gv, gg, dq, gq, gk, ga, z(128 - GLA_RANK), dk, dv, ik, iw,
         z(128 - IDX_DIM - IDX_HEADS), sk, sv], axis=0)
    return packed.astype(BF16)


def kernel(x, p, rel_bias, g_mix, w_in, gla_w_alpha, gla_b_alpha, gla_g_norm, swa_sinks, w_out,
           g_ffn, w_router_group, b_router_group, w_router_expert, b_router_expert, w_expert_gate,
           w_expert_up, w_expert_down, w_ple, g_ple, w_ple_gate, g_final):
    bsz, seq, d = x.shape
    depth = w_in.shape[0]
    t = bsz * seq
    assert d == D_MODEL and t % DISPATCH_TILE == 0 and seq % GLA_BLOCK == 0 and seq % KC == 0
    n_blocks = -(-(2 * t) // EXPERT_ROWS) + N_EXPERTS
    dsa_tiles, swa_bias = _bias_tiles(rel_bias)

    h = x.reshape(t, d)
    for i in range(depth):
        proj = _inproj(h, g_mix[i][None, :], _pack_w_in(w_in[i]))
        proj3 = proj.reshape(bsz, seq, D_PROJ)

        wal = jnp.zeros((128, 128), F32).at[:GLA_RANK].set(gla_w_alpha[i])
        og = _gla(proj3, wal, gla_b_alpha[i][None, :],
                  jnp.tile(gla_g_norm[i], GLA_HEADS)[None, :])
        od = _dsa(proj3, dsa_tiles)
        os_ = _swa(proj3, swa_sinks[i], swa_bias)

        w_r = jnp.concatenate([w_router_group[i], w_router_expert[i]], axis=1)
        w_r = jnp.pad(w_r, ((0, 0), (0, LANES - w_r.shape[1])))
        wr_hi, wr_lo = _split_bf16(w_r)
        b_r = jnp.pad(jnp.concatenate([b_router_group[i], b_router_expert[i]]),
                      (0, LANES - N_GROUPS - N_EXPERTS))[None, :]
        h1, xn, ri, rf, cnt = _outproj_router(
            h, og.reshape(t, -1), od.reshape(t, -1), os_.reshape(t, -1), w_out[i].astype(BF16),
            g_ffn[i][None, :], wr_hi, wr_lo, b_r)

        counts = cnt[ROUTE_OFF:ROUTE_OFF + N_EXPERTS, 0].astype(jnp.int32)
        padded = (counts + EXPERT_ROWS - 1) // EXPERT_ROWS * EXPERT_ROWS
        pad_end = jnp.cumsum(padded)
        pad_start = pad_end - padded
        expert_ids = jnp.arange(N_EXPERTS, dtype=jnp.int32)
        slot_start = jnp.sum(
            jnp.where(ri[0:2, :, None] == expert_ids, pad_start, 0), axis=-1)
        pos3 = (slot_start + ri[2:4]).T.reshape(t // TOKEN_TILE, 1, 2 * TOKEN_TILE)
        gates = jnp.pad(rf[0:2].T, ((0, 0), (0, LANES - 2)))
        blk_start = jnp.arange(n_blocks, dtype=jnp.int32) * EXPERT_ROWS
        blk_expert = jnp.minimum(
            jnp.sum((pad_end[None, :] <= blk_start[:, None]).astype(jnp.int32), axis=1),
            N_EXPERTS - 1)
        n_used = (pad_end[-1:] // EXPERT_ROWS).astype(jnp.int32)
        onehot_e = blk_expert[:, None] == expert_ids[None, :]
        rows_before = blk_start - jnp.sum(jnp.where(onehot_e, pad_start[None, :], 0), axis=1)
        blk_rows = jnp.clip(jnp.sum(jnp.where(onehot_e, counts[None, :], 0), axis=1) - rows_before,
                            0, EXPERT_ROWS).astype(jnp.int32)

        buf = _dispatch(pos3, xn, jnp.zeros((n_blocks * EXPERT_ROWS, HALF_MODEL), jnp.uint32))
        yb = _experts(blk_expert, n_used, blk_rows, buf, w_expert_gate, w_expert_up,
                      w_expert_down, i)
        h = _combine_ple(pos3, h1, p.reshape(depth * t, PLE_DIM), gates, yb, w_ple[i].astype(BF16),
                         g_ple[i][None, :], w_ple_gate[i].astype(BF16), g_final[None, :],
                         layer=i, final_norm=(i == depth - 1))
    return h.reshape(bsz, seq, d)
```

```python
import functools
import math

import jax
import jax.numpy as jnp
from jax import lax
from jax.experimental import pallas as pl
from jax.experimental.pallas import tpu as pltpu

F32 = jnp.float32
BF16 = jnp.bfloat16
HIGHEST = lax.Precision.HIGHEST

D_MODEL = 1024
HEAD_DIM = 64
GLA_HEADS = 4
GLA_DK = 32
GLA_DV = 64
GLA_RANK = 16
GLA_TAU = 16.0
GLA_CHUNK = 64
DSA_HEADS = 4
IDX_HEADS = 8
IDX_DIM = 64
DSA_TOPK_MAX = 256
SWA_HEADS = 8
SWA_KV_HEADS = 2
SWA_GROUP = SWA_HEADS // SWA_KV_HEADS
WINDOW = 128
REL_BUCKETS = 32
REL_MAX_DIST = 128
N_GROUPS = 4
EXPERTS_PER_GROUP = 8
N_EXPERTS = N_GROUPS * EXPERTS_PER_GROUP
D_EXPERT = 512
PLE_DIM = 256
EPS = 1e-6

LANES = 128
SUBLANES = 8
PACK16 = 16

COL_IQ, COL_SQ = 0, 512
COL_GV, COL_GG, COL_DQ = 1024, 1280, 1536
COL_GQ, COL_GK, COL_GA, COL_DKV, COL_IKW, COL_SK, COL_SV = 1792, 1920, 2048, 2176, 2304, 2432, 2560
D_PROJ = 2688

TOKEN_TILE = 512
INPROJ_TILE = 512
ROUTER_TILE = 512
DISPATCH_TILE = 1024
EXPERT_ROWS = 512
GLA_BLOCK = 256
GLA_BATCH = 2
QB = 256
KC = 256
DMA_UNROLL = 8
NEG_BIG = -1e30
LOG2_E = math.log2(math.e)
VMEM_LIMIT = 48 * 1024 * 1024


def _nt(a, b, precision=None):
    return lax.dot_general(a, b, (((1,), (1,)), ((), ())), precision=precision,
                           preferred_element_type=F32)


def _mm(a, b, precision=None):
    return jnp.dot(a, b, precision=precision, preferred_element_type=F32)


def _eye(n, dtype):
    r = lax.broadcasted_iota(jnp.int32, (n, n), 0)
    c = lax.broadcasted_iota(jnp.int32, (n, n), 1)
    return jnp.where(r == c, 1.0, 0.0).astype(dtype)


def _rms(x, g):
    return x * lax.rsqrt(jnp.mean(x * x, axis=-1, keepdims=True) + EPS) * g


HALF_MODEL = D_MODEL // 2
HIGH16 = 0xFFFF0000


def _pack_bf16_pairs(x):
    bits = pltpu.bitcast(x.astype(BF16).astype(F32), jnp.uint32)
    return (bits[:, HALF_MODEL:] & jnp.uint32(HIGH16)) | (bits[:, :HALF_MODEL] >> 16)


def _unpack_bf16_pairs(w):
    return pltpu.bitcast(w << 16, F32), pltpu.bitcast(w & jnp.uint32(HIGH16), F32)


def _inproj_kernel(h_ref, g_ref, w_ref, o_ref):
    a = _rms(h_ref[...], g_ref[...])
    o_ref[...] = _nt(a.astype(BF16), w_ref[...])


def _inproj(h, g, w):
    t = h.shape[0]
    return pl.pallas_call(
        _inproj_kernel,
        grid=(t // INPROJ_TILE,),
        in_specs=[
            pl.BlockSpec((INPROJ_TILE, D_MODEL), lambda i: (i, 0)),
            pl.BlockSpec((1, D_MODEL), lambda i: (0, 0)),
            pl.BlockSpec((D_PROJ, D_MODEL), lambda i: (0, 0)),
        ],
        out_specs=pl.BlockSpec((INPROJ_TILE, D_PROJ), lambda i: (i, 0)),
        out_shape=jax.ShapeDtypeStruct((t, D_PROJ), F32),
        compiler_params=pltpu.CompilerParams(
            dimension_semantics=("arbitrary",), vmem_limit_bytes=VMEM_LIMIT),
        name="inproj",
    )(h, g, w)


def _gla_kernel(q_ref, k_ref, v_ref, gg_ref, ga_ref, wal_ref, bal_ref, gn_ref, o_ref,
                state_ref, sc_all_ref, p_all_ref, b_ref):
    @pl.when(pl.program_id(1) == 0)
    def _():
        state_ref[...] = jnp.zeros_like(state_ref)

    @pl.when((pl.program_id(0) == 0) & (pl.program_id(1) == 0))
    def _():
        p_all_ref[...] = jnp.zeros_like(p_all_ref)

    chunk_fns = []
    for bi in range(GLA_BATCH):
        one = pl.ds(bi, 1)
        chunk_fns.append(_gla_sequence(
            q_ref.at[one], k_ref.at[one], v_ref.at[one], gg_ref.at[one], ga_ref.at[one],
            wal_ref, bal_ref, gn_ref, o_ref.at[one], state_ref.at[bi],
            sc_all_ref.at[bi], p_all_ref.at[bi], b_ref.at[bi]))
    for ci in range(GLA_BLOCK // GLA_CHUNK):
        for chunk in chunk_fns:
            chunk(ci)


def _gla_sequence(q_ref, k_ref, v_ref, gg_ref, ga_ref, wal_ref, bal_ref, gn_ref, o_ref,
                  state_ref, sc_all_ref, p_all_ref, b_ref):
    hk = GLA_HEADS * GLA_DK
    hv = GLA_HEADS * GLA_DV
    c = GLA_CHUNK
    half_c = c // 2
    assert 2 * c == LANES

    pj = lax.broadcasted_iota(jnp.int32, (half_c, LANES), 0)
    pl_ = lax.broadcasted_iota(jnp.int32, (half_c, LANES), 1)
    pair_causal = jnp.where(pl_ < c, pl_, pl_ - c) <= jnp.where(pl_ < c, pj, pj + half_c)
    rv = lax.broadcasted_iota(jnp.int32, (hv, hk), 0) // GLA_DV
    ck = lax.broadcasted_iota(jnp.int32, (hv, hk), 1) // GLA_DK
    blockdiag_t = jnp.where(rv == ck, 1.0, 0.0).astype(F32)
    hr = lax.broadcasted_iota(jnp.int32, (SUBLANES, hk), 0)
    hl = lax.broadcasted_iota(jnp.int32, (SUBLANES, hk), 1) // GLA_DK
    head_rows = jnp.where(hr == hl, 1.0, 0.0).astype(BF16)
    eye_v = _eye(hv, BF16)

    ga_hi, ga_lo = _split_bf16(ga_ref[0])
    w_hi, w_lo = _split_bf16(wal_ref[...])
    z = _mm(ga_hi, w_hi) + _mm(ga_hi, w_lo) + _mm(ga_lo, w_hi) + bal_ref[...]
    log_a = (jnp.minimum(z, 0.0) - jnp.log1p(jnp.exp(-jnp.abs(z)))) * (1.0 / GLA_TAU)
    la_hi = log_a.astype(BF16)
    la_r1 = log_a - la_hi.astype(F32)
    la_mid = la_r1.astype(BF16)
    la_lo = (la_r1 - la_mid.astype(F32)).astype(BF16)
    rb = lax.broadcasted_iota(jnp.int32, (GLA_BLOCK, GLA_BLOCK), 0)
    cb = lax.broadcasted_iota(jnp.int32, (GLA_BLOCK, GLA_BLOCK), 1)
    tril = jnp.where((rb >= cb) & (rb // c == cb // c), 1.0, 0.0).astype(BF16)
    b_ref[...] = _mm(tril, la_hi) + _mm(tril, la_mid) + _mm(tril, la_lo)

    def chunk(ci):
        rows = slice(ci * c, (ci + 1) * c)
        p_ref, sc_ref = p_all_ref.at[ci], sc_all_ref.at[ci]
        q = q_ref[0, rows, :] * (GLA_DK ** -0.5)
        k = k_ref[0, rows, :]
        v = v_ref[0, rows, :]
        b = b_ref[rows, :]
        state_t = state_ref[...]
        o_inter = _nt((q * jnp.exp(b)).astype(BF16), state_t.astype(BF16))

        b2 = b * LOG2_E
        for t in range(c):
            ns = PACK16 * (t // PACK16 + 1)
            slot = 2 * (t % half_c) + t // half_c
            p = jnp.exp2(b2[t:t + 1, :] - b2[:ns, :]) * k[:ns, :] * q[t:t + 1, :]
            p_ref[slot * c:slot * c + ns, :] = p.astype(BF16)
        rows_hs = _nt(head_rows, p_ref[...])
        for jj in range(half_c):
            for h in range(GLA_HEADS):
                sc_ref[h, jj:jj + 1, :] = rows_hs[h:h + 1, jj * LANES:(jj + 1) * LANES]
        vb = v.astype(BF16)
        zero_v = jnp.zeros((c, GLA_DV), BF16)
        o_heads = []
        for h in range(GLA_HEADS):
            vh = vb[:, h * GLA_DV:(h + 1) * GLA_DV]
            v_pair = jnp.concatenate([jnp.concatenate([vh, zero_v], axis=1),
                                      jnp.concatenate([zero_v, vh], axis=1)], axis=0)
            scores = jnp.where(pair_causal, sc_ref[h], 0.0).astype(BF16)
            o_pair = _mm(scores, v_pair)
            o_heads.append(jnp.concatenate([o_pair[:, :GLA_DV], o_pair[:, GLA_DV:]], axis=0))
        o = o_inter + jnp.concatenate(o_heads, axis=-1)

        b_last = b[c - 1:c, :]
        kd = (k * jnp.exp(b_last - b)).astype(BF16)
        v_t = _nt(eye_v, vb).astype(BF16)
        upd_t = _mm(v_t, kd)
        state_ref[...] = jnp.exp(b_last) * state_t + upd_t * blockdiag_t

        gg = gg_ref[0, rows, :]
        outs = []
        for h in range(GLA_HEADS):
            oh = o[:, h * GLA_DV:(h + 1) * GLA_DV]
            ms = jnp.mean(oh * oh, axis=-1, keepdims=True)
            outs.append(oh * lax.rsqrt(ms + EPS))
        on = jnp.concatenate(outs, axis=-1) * gn_ref[...]
        o_ref[0, rows, :] = (on * (gg * jax.nn.sigmoid(gg))).astype(o_ref.dtype)

    return chunk


def _gla(proj3, wal, bal, gn):
    bsz, seq, _ = proj3.shape
    nb = seq // GLA_BLOCK

    def col(width, off):
        return pl.BlockSpec((GLA_BATCH, GLA_BLOCK, width), lambda b, i: (b, i, off // width))

    per_seq = lambda *shape: (GLA_BATCH,) + shape
    return pl.pallas_call(
        _gla_kernel,
        grid=(bsz // GLA_BATCH, nb),
        in_specs=[
            col(128, COL_GQ), col(128, COL_GK), col(256, COL_GV), col(256, COL_GG), col(128, COL_GA),
            pl.BlockSpec((128, 128), lambda b, i: (0, 0)),
            pl.BlockSpec((1, 128), lambda b, i: (0, 0)),
            pl.BlockSpec((1, 256), lambda b, i: (0, 0)),
        ],
        out_specs=pl.BlockSpec((GLA_BATCH, GLA_BLOCK, 256), lambda b, i: (b, i, 0)),
        out_shape=jax.ShapeDtypeStruct((bsz, seq, GLA_HEADS * GLA_DV), BF16),
        scratch_shapes=[
            pltpu.VMEM(per_seq(GLA_HEADS * GLA_DV, GLA_HEADS * GLA_DK), F32),
            pltpu.VMEM(per_seq(GLA_BLOCK // GLA_CHUNK, GLA_HEADS, GLA_CHUNK // 2, 2 * GLA_CHUNK),
                       F32),
            pltpu.VMEM(per_seq(GLA_BLOCK // GLA_CHUNK, GLA_CHUNK * GLA_CHUNK,
                               GLA_HEADS * GLA_DK), BF16),
            pltpu.VMEM(per_seq(GLA_BLOCK, GLA_HEADS * GLA_DK), F32),
        ],
        compiler_params=pltpu.CompilerParams(
            dimension_semantics=("arbitrary", "arbitrary"), vmem_limit_bytes=VMEM_LIMIT),
        name="gla",
    )(proj3, proj3, proj3, proj3, proj3, wal, bal, gn)


DSA_BISECT_STEPS = 14
DSA_ACCUMULATORS = 4
DSA_WALK_UNCONDITIONAL = 1
DSA_PV_PARTS = 4


def _dsa_kernel(dq_ref, kv_ref, iq_ref, ikw_ref, btile_ref, o_ref,
                sc_ref, lg_ref, p_ref, kb_ref, vb_ref, mp_ref, lp_ref, acc_ref, cut_ref, *, n_sel):
    j = pl.program_id(1)
    nkc = ((j + 1) * QB + KC - 1) // KC
    seq = kv_ref.shape[1]
    ksel = float(n_sel)

    @pl.when(j == 0)
    def _():
        kv = kv_ref[0]
        kb_ref[...] = kv[:, :HEAD_DIM].astype(BF16)
        vb_ref[...] = kv[:, HEAD_DIM:].astype(BF16)

    def rows(c):
        return pl.ds(pl.multiple_of(c * KC, KC), KC)

    wide = DSA_ACCUMULATORS * SUBLANES

    def fold(op, x):
        return op(x.reshape(KC // wide, wide, QB), axis=0)

    def all8(op, xw, roll_op):
        x8 = op(xw.reshape(DSA_ACCUMULATORS, SUBLANES, QB), axis=0)
        for shift in (4, 2, 1):
            x8 = roll_op(x8, pltpu.roll(x8, shift, 0))
        return x8

    def widen(x8):
        return jnp.concatenate([x8] * DSA_ACCUMULATORS, axis=0)

    zeros8 = jnp.zeros((SUBLANES, QB), F32)
    zerosw = jnp.zeros((wide, QB), F32)
    infw = jnp.full((wide, QB), jnp.inf, F32)

    ikw_q = ikw_ref[0, pl.ds(pl.multiple_of(j * QB, QB), QB), :]
    w_t = ikw_q.T[IDX_DIM:IDX_DIM + IDX_HEADS, :] * (IDX_HEADS ** -0.5 * IDX_DIM ** -0.5)
    iq = iq_ref[0].astype(BF16)
    iq_heads = [iq[:, h * IDX_DIM:(h + 1) * IDX_DIM] for h in range(IDX_HEADS)]

    s_loc = lax.broadcasted_iota(jnp.int32, (KC, QB), 0)
    t_glob = j * QB + lax.broadcasted_iota(jnp.int32, (KC, QB), 1)
    s_loc_f = s_loc.astype(F32)

    def score_chunk(c, st):
        mn8, mx8, ge8, gt8 = st
        ik = ikw_ref[0, rows(c), :][:, :IDX_DIM].astype(BF16)
        acc = jnp.zeros((KC, QB), F32)
        for h in range(IDX_HEADS):
            rel = _nt(ik, iq_heads[h])
            acc = acc + jnp.maximum(rel, 0.0) * w_t[h:h + 1, :]
        adm = s_loc + c * KC <= t_glob
        blk = jnp.where(adm, acc, -jnp.inf)
        sc_ref[rows(c), :] = blk
        mn8 = jnp.minimum(mn8, fold(jnp.min, jnp.where(adm, acc, jnp.inf)))
        mx8 = jnp.maximum(mx8, fold(jnp.max, blk))
        ge8 = ge8 + fold(jnp.sum, jnp.where(blk >= 0.0, 1.0, 0.0))
        gt8 = gt8 + fold(jnp.sum, jnp.where(blk > 0.0, 1.0, 0.0))
        return mn8, mx8, ge8, gt8

    def paired(body, init):
        def two(i, st):
            return body(2 * i + 1, body(2 * i, st))
        st = lax.fori_loop(0, nkc // 2, two, init)
        return lax.cond(nkc % 2 == 1, lambda s: body(nkc - 1, s), lambda s: s, st)

    mn8, mx8, ge8, gt8 = paired(score_chunk, (infw, -infw, zerosw, zerosw))
    row_min = all8(jnp.min, mn8, jnp.minimum)
    row_max = all8(jnp.max, mx8, jnp.maximum)
    c_ge0, c_gt0 = all8(jnp.sum, ge8, jnp.add), all8(jnp.sum, gt8, jnp.add)

    def blocks(c):
        return sc_ref[rows(c), :].reshape(KC // wide, wide, QB)

    def count_ge(x8):
        xw = widen(x8)[None]

        def body(c, cw):
            return cw + jnp.sum(jnp.where(blocks(c) >= xw, 1.0, 0.0), axis=0)
        return all8(jnp.sum, lax.fori_loop(0, nkc, body, zerosw), jnp.add)

    def min_ge(x8):
        xw = widen(x8)[None]

        def body(c, mw):
            blk = blocks(c)
            return jnp.minimum(mw, jnp.min(jnp.where(blk >= xw, blk, jnp.inf), axis=0))
        return all8(jnp.min, lax.fori_loop(0, nkc, body, infw), jnp.minimum)

    def pass_gt(x8):
        xw = widen(x8)[None]

        def body(c, st):
            cw, mw = st
            blk = blocks(c)
            hit = blk > xw
            return (cw + jnp.sum(jnp.where(hit, 1.0, 0.0), axis=0),
                    jnp.minimum(mw, jnp.min(jnp.where(hit, blk, jnp.inf), axis=0)))
        cw, mw = lax.fori_loop(0, nkc, body, (zerosw, infw))
        return all8(jnp.sum, cw, jnp.add), all8(jnp.min, mw, jnp.minimum)

    n_adm = (j * QB + lax.broadcasted_iota(jnp.int32, (SUBLANES, QB), 1) + 1).astype(F32)
    at_zero = (c_gt0 < ksel) & (c_ge0 >= ksel)
    above = c_gt0 >= ksel
    lo = jnp.where(above | at_zero, 0.0, row_min)
    c_lo = jnp.where(above | at_zero, c_ge0, n_adm)
    settled = at_zero | (n_adm <= ksel)
    hi = jnp.where(settled, lo, jnp.where(above, row_max, 0.0))

    def bisect(_, st):
        lo, hi, c_lo = st
        mid = lo + (hi - lo) * 0.5
        cnt = count_ge(mid)
        up = cnt >= ksel
        return jnp.where(up, mid, lo), jnp.where(up, hi, mid), jnp.where(up, cnt, c_lo)

    lo, _, c_lo = lax.fori_loop(0, DSA_BISECT_STEPS, bisect, (lo, hi, c_lo))

    v0 = min_ge(lo)
    done0 = jnp.where(settled | (c_lo == ksel), 1.0, 0.0)
    c_gt_init = jnp.where(at_zero, c_gt0, 0.0)

    def walk_cond(st):
        return jnp.min(st[3]) < 0.5

    def walk_body(st):
        v, c_ge, c_gt, dn = st
        cnt, vnext = pass_gt(v)
        live = dn < 0.5
        fin = live & (cnt < ksel)
        step = live & (cnt >= ksel)
        c_gt = jnp.where(fin, cnt, c_gt)
        v = jnp.where(step, vnext, v)
        c_ge = jnp.where(step, cnt, c_ge)
        dn = jnp.where(fin | (step & (cnt == ksel)), 1.0, dn)
        return v, c_ge, c_gt, dn

    walk = (v0, c_lo, c_gt_init, done0)
    for _ in range(DSA_WALK_UNCONDITIONAL):
        walk = walk_body(walk)
    tau8, c_ge, c_gt, _ = lax.while_loop(walk_cond, walk_body, walk)

    need = ksel - c_gt
    cut_ref[...] = jnp.full(cut_ref.shape, float(seq), F32)

    @pl.when(jnp.max(c_ge) > ksel)
    def _():
        s_grp = lax.broadcasted_iota(jnp.int32, (KC // wide, wide, QB), 0) * wide
        s_sub = lax.broadcasted_iota(jnp.int32, (KC // wide, wide, QB), 1)
        s_idx = (s_grp + s_sub).astype(F32)
        tauw = widen(tau8)[None]

        def count_ties_below(m8):
            mw = widen(m8)[None]

            def body(c, cw):
                hit = (blocks(c) == tauw) & (s_idx + (c * KC).astype(F32) < mw)
                return cw + jnp.sum(jnp.where(hit, 1.0, 0.0), axis=0)
            return all8(jnp.sum, lax.fori_loop(0, nkc, body, zerosw), jnp.add)

        def idx_bisect(_, lh):
            lo_m, hi_m = lh
            mid = jnp.floor((lo_m + hi_m) * 0.5)
            ok = count_ties_below(mid) >= need
            return jnp.where(ok, lo_m, mid), jnp.where(ok, mid, hi_m)

        _, hi_m = lax.fori_loop(0, int(math.log2(seq)) + 1, idx_bisect,
                                (zeros8, zeros8 + float(seq)))
        cut_ref[...] = jnp.where(c_ge > ksel, hi_m, float(seq))

    tau = tau8[0:1, :]
    cut = cut_ref[0:1, :]

    q = (dq_ref[0] * (HEAD_DIM ** -0.5)).astype(BF16)
    q4 = jnp.concatenate([q[:, h * HEAD_DIM:(h + 1) * HEAD_DIM] for h in range(DSA_HEADS)],
                         axis=0)
    mp_ref[...] = jnp.full(mp_ref.shape, NEG_BIG, F32)
    lp_ref[...] = jnp.zeros(lp_ref.shape, F32)

    pv_parts = min(DSA_PV_PARTS, seq // KC)
    part = seq // pv_parts

    def cols(c):
        return pl.ds(pl.multiple_of(c * KC, KC), KC)

    def lane_fold(op, x):
        out = x[:, :LANES]
        for g in range(1, KC // LANES):
            out = op(out, x[:, g * LANES:(g + 1) * LANES])
        return out

    def logits_chunk(c, carry):
        blk = sc_ref[rows(c), :]
        sidx = s_loc_f + (c * KC).astype(F32)
        sel = (blk > tau) | ((blk == tau) & (sidx < cut))
        neg_t = jnp.where(sel, 0.0, NEG_BIG).T
        lg4 = _nt(q4, kb_ref[rows(c), :])
        which = [[jnp.clip((c * (KC // WINDOW) + b) - (j * (QB // WINDOW) + a) + 2, 0, 2)
                  for b in range(KC // WINDOW)] for a in range(QB // WINDOW)]
        for h in range(DSA_HEADS):
            hq = slice(h * QB, (h + 1) * QB)
            bias = jnp.concatenate(
                [jnp.concatenate([btile_ref[h, w] for w in which_a], axis=1)
                 for which_a in which], axis=0)
            lg = lg4[hq, :] + bias + neg_t
            lg_ref[hq, cols(c)] = lg
            mp_ref[h] = jnp.maximum(mp_ref[h], lane_fold(jnp.maximum, lg))
        return carry

    paired(logits_chunk, 0)
    m_rows = [jnp.max(mp_ref[h], axis=-1, keepdims=True) for h in range(DSA_HEADS)]

    def probs_chunk(c, carry):
        for h in range(DSA_HEADS):
            hq = slice(h * QB, (h + 1) * QB)
            p = jnp.exp(lg_ref[hq, cols(c)] - m_rows[h])
            lp_ref[h] = lp_ref[h] + lane_fold(jnp.add, p)
            p_ref[hq, cols(c)] = p.astype(BF16)
        return carry

    def zero_chunk(c, carry):
        p_ref[:, cols(c)] = jnp.zeros((DSA_HEADS * QB, KC), BF16)
        return carry

    lax.fori_loop(0, nkc, probs_chunk, 0)
    n_parts = ((j + 1) * QB + part - 1) // part
    lax.fori_loop(nkc, n_parts * (part // KC), zero_chunk, 0)
    acc_ref[...] = _mm(p_ref[:, :part], vb_ref[:part, :])
    for k in range(1, pv_parts):
        @pl.when(k < n_parts)
        def _(k=k):
            acc_ref[...] = acc_ref[...] + _mm(p_ref[:, k * part:(k + 1) * part],
                                              vb_ref[k * part:(k + 1) * part, :])

    outs = []
    for h in range(DSA_HEADS):
        l = jnp.sum(lp_ref[h], axis=-1, keepdims=True)
        outs.append(acc_ref[h * QB:(h + 1) * QB, :] / l)
    o_ref[0] = jnp.concatenate(outs, axis=-1).astype(o_ref.dtype)


def _dsa(proj3, btiles):
    bsz, seq, _ = proj3.shape
    n_sel = min(DSA_TOPK_MAX, seq // 4)
    return pl.pallas_call(
        functools.partial(_dsa_kernel, n_sel=n_sel),
        grid=(bsz, seq // QB),
        in_specs=[
            pl.BlockSpec((1, QB, 256), lambda b, j: (b, j, COL_DQ // 256)),
            pl.BlockSpec((1, seq, 128), lambda b, j: (b, 0, COL_DKV // 128)),
            pl.BlockSpec((1, QB, 512), lambda b, j: (b, j, COL_IQ // 512)),
            pl.BlockSpec((1, seq, 128), lambda b, j: (b, 0, COL_IKW // 128)),
            pl.BlockSpec((DSA_HEADS, 3, WINDOW, WINDOW), lambda b, j: (0, 0, 0, 0)),
        ],
        out_specs=pl.BlockSpec((1, QB, 256), lambda b, j: (b, j, 0)),
        out_shape=jax.ShapeDtypeStruct((bsz, seq, DSA_HEADS * HEAD_DIM), BF16),
        scratch_shapes=[
            pltpu.VMEM((seq, QB), F32),
            pltpu.VMEM((DSA_HEADS * QB, seq), F32),
            pltpu.VMEM((DSA_HEADS * QB, seq), BF16),
            pltpu.VMEM((seq, HEAD_DIM), BF16),
            pltpu.VMEM((seq, HEAD_DIM), BF16),
            pltpu.VMEM((DSA_HEADS, QB, LANES), F32),
            pltpu.VMEM((DSA_HEADS, QB, LANES), F32),
            pltpu.VMEM((DSA_HEADS * QB, HEAD_DIM), F32),
            pltpu.VMEM((SUBLANES, QB), F32),
        ],
        compiler_params=pltpu.CompilerParams(
            dimension_semantics=("arbitrary", "arbitrary"), vmem_limit_bytes=VMEM_LIMIT),
        name="dsa",
    )(proj3, proj3, proj3, proj3, btiles)


def _swa_kernel(sink_ref, q_ref, kc_ref, kp_ref, vc_ref, vp_ref, bias_ref, o_ref):
    n = pl.program_id(1)
    q = (q_ref[0] * (HEAD_DIM ** -0.5)).astype(BF16)
    k2 = jnp.concatenate([kp_ref[0], kc_ref[0]], axis=0).astype(BF16)
    v2 = jnp.concatenate([vp_ref[0], vc_ref[0]], axis=0).astype(BF16)
    ones = jnp.ones((2 * WINDOW, HEAD_DIM), BF16)
    first = jnp.where(n == 0, 1, 0)
    outs = []
    for h in range(SWA_HEADS):
        kvh = h // SWA_GROUP
        kh = k2[:, kvh * HEAD_DIM:(kvh + 1) * HEAD_DIM]
        v_ones = jnp.concatenate([v2[:, kvh * HEAD_DIM:(kvh + 1) * HEAD_DIM], ones], axis=1)
        lg = _nt(q[:, h * HEAD_DIM:(h + 1) * HEAD_DIM], kh) + bias_ref[first, h]
        sink = sink_ref[h]
        m = jnp.maximum(jnp.max(lg, axis=-1, keepdims=True), sink)
        num_den = _mm(jnp.exp(lg - m).astype(BF16), v_ones)
        outs.append(num_den[:, :HEAD_DIM] / (num_den[:, HEAD_DIM:] + jnp.exp(sink - m)))
    o_ref[0] = jnp.concatenate(outs, axis=-1).astype(o_ref.dtype)


def _swa(proj3, sinks, bias_nat):
    bsz, seq, _ = proj3.shape
    return pl.pallas_call(
        _swa_kernel,
        grid=(bsz, seq // WINDOW),
        in_specs=[
            pl.BlockSpec(memory_space=pltpu.SMEM),
            pl.BlockSpec((1, WINDOW, 512), lambda b, n: (b, n, COL_SQ // 512)),
            pl.BlockSpec((1, WINDOW, 128), lambda b, n: (b, n, COL_SK // 128)),
            pl.BlockSpec((1, WINDOW, 128), lambda b, n: (b, jnp.maximum(n - 1, 0), COL_SK // 128)),
            pl.BlockSpec((1, WINDOW, 128), lambda b, n: (b, n, COL_SV // 128)),
            pl.BlockSpec((1, WINDOW, 128), lambda b, n: (b, jnp.maximum(n - 1, 0), COL_SV // 128)),
            pl.BlockSpec((2, SWA_HEADS, WINDOW, 2 * WINDOW), lambda b, n: (0, 0, 0, 0)),
        ],
        out_specs=pl.BlockSpec((1, WINDOW, 512), lambda b, n: (b, n, 0)),
        out_shape=jax.ShapeDtypeStruct((bsz, seq, SWA_HEADS * HEAD_DIM), BF16),
        compiler_params=pltpu.CompilerParams(
            dimension_semantics=("arbitrary", "arbitrary"), vmem_limit_bytes=VMEM_LIMIT),
        name="swa",
    )(sinks, proj3, proj3, proj3, proj3, proj3, bias_nat)


ROUTE_OFF = N_GROUPS
ROUTE_ROWS = 48


def _split_bf16(x):
    hi = x.astype(BF16)
    lo = (x - hi.astype(F32)).astype(BF16)
    return hi, lo


def _outproj_router_kernel(h_ref, og_ref, od_ref, os_ref, wo_ref, gffn_ref, wr_hi_ref, wr_lo_ref,
                           br_ref, h1_ref, xn_ref, ri_ref, rf_ref, cnt_ref, run_ref):
    tm = h_ref.shape[0]

    @pl.when(pl.program_id(0) == 0)
    def _():
        run_ref[...] = jnp.zeros_like(run_ref)

    o = jnp.concatenate([og_ref[...], od_ref[...], os_ref[...]], axis=-1)
    h1 = h_ref[...] + _mm(o, wo_ref[...])
    h1_ref[...] = h1
    xn = _rms(h1, gffn_ref[...])
    xn_ref[...] = _pack_bf16_pairs(xn)

    x_hi, x_lo = _split_bf16(xn)
    lg = (_mm(x_hi, wr_hi_ref[...]) + _mm(x_lo, wr_hi_ref[...]) + _mm(x_hi, wr_lo_ref[...])
          + br_ref[...]).T[:ROUTE_ROWS, :]

    row = lax.broadcasted_iota(jnp.int32, lg.shape, 0)
    row_f = row.astype(F32)
    ninf = -jnp.inf

    def first_max(x):
        m = jnp.max(x, axis=0, keepdims=True)
        idx = jnp.min(jnp.where(x == m, row_f, float(ROUTE_ROWS)), axis=0, keepdims=True)
        return m, idx

    gl = jnp.where(row < N_GROUPS, lg, ninf)
    gmax, gsel = first_max(gl)
    g_w = 1.0 / jnp.sum(jnp.exp(gl - gmax), axis=0, keepdims=True)
    e_lo = ROUTE_OFF + EXPERTS_PER_GROUP * gsel
    el = jnp.where((row_f >= e_lo) & (row_f < e_lo + EXPERTS_PER_GROUP), lg, ninf)
    m1, i1 = first_max(el)
    eden = jnp.sum(jnp.exp(el - m1), axis=0, keepdims=True)
    m2, i2 = first_max(jnp.where(row_f == i1, ninf, el))
    p1 = 1.0 / eden
    p2 = jnp.exp(m2 - m1) / eden
    gate1 = g_w * p1 / (p1 + p2)
    gate2 = g_w * p2 / (p1 + p2)

    onehot = jnp.where((row_f == i1) | (row_f == i2), 1.0, 0.0)
    rr = lax.broadcasted_iota(jnp.int32, (tm, tm), 0)
    cc = lax.broadcasted_iota(jnp.int32, (tm, tm), 1)
    earlier = jnp.where(rr < cc, 1.0, 0.0).astype(BF16)
    run = run_ref[...]
    before = _mm(onehot.astype(BF16), earlier) + jnp.concatenate([run] * (tm // LANES), axis=1)
    rank1 = jnp.sum(jnp.where(row_f == i1, before, 0.0), axis=0, keepdims=True)
    rank2 = jnp.sum(jnp.where(row_f == i2, before, 0.0), axis=0, keepdims=True)
    run_ref[...] = run + jnp.sum(onehot, axis=1, keepdims=True)
    cnt_ref[...] = run_ref[...]

    out_row = lax.broadcasted_iota(jnp.int32, (SUBLANES, tm), 0)
    ints = jnp.where(out_row == 0, i1 - ROUTE_OFF,
                     jnp.where(out_row == 1, i2 - ROUTE_OFF,
                               jnp.where(out_row == 2, rank1,
                                         jnp.where(out_row == 3, rank2, 0.0))))
    ri_ref[...] = ints.astype(jnp.int32)
    rf_ref[...] = jnp.where(out_row == 0, gate1, jnp.where(out_row == 1, gate2, 0.0))


def _outproj_router(h, og, od, os_, wo, gffn, wr_hi, wr_lo, br):
    t = h.shape[0]
    tm = ROUTER_TILE
    row = lambda w: pl.BlockSpec((tm, w), lambda i: (i, 0))
    col = lambda r: pl.BlockSpec((r, tm), lambda i: (0, i))
    full = lambda a, b: pl.BlockSpec((a, b), lambda i: (0, 0))
    return pl.pallas_call(
        _outproj_router_kernel,
        grid=(t // tm,),
        in_specs=[row(D_MODEL), row(256), row(256), row(512), full(D_MODEL, D_MODEL),
                  full(1, D_MODEL), full(D_MODEL, LANES), full(D_MODEL, LANES), full(1, LANES)],
        out_specs=[row(D_MODEL), row(HALF_MODEL), col(SUBLANES), col(SUBLANES),
                   full(ROUTE_ROWS, LANES)],
        out_shape=[
            jax.ShapeDtypeStruct((t, D_MODEL), F32),
            jax.ShapeDtypeStruct((t, HALF_MODEL), jnp.uint32),
            jax.ShapeDtypeStruct((SUBLANES, t), jnp.int32),
            jax.ShapeDtypeStruct((SUBLANES, t), F32),
            jax.ShapeDtypeStruct((ROUTE_ROWS, LANES), F32),
        ],
        scratch_shapes=[pltpu.VMEM((ROUTE_ROWS, LANES), F32)],
        compiler_params=pltpu.CompilerParams(
            dimension_semantics=("arbitrary",), vmem_limit_bytes=VMEM_LIMIT),
        name="outproj_router",
    )(h, og, od, os_, wo, gffn, wr_hi, wr_lo, br)


def _dispatch_kernel(pos_ref, xn_ref, buf_in_ref, buf_ref, sem):
    del buf_in_ref
    tm = xn_ref.shape[0]

    def row_copy(r, k):
        dst = pos_ref[0, 0, 2 * r + k]
        return pltpu.make_async_copy(xn_ref.at[pl.ds(r, 1)], buf_ref.at[pl.ds(dst, 1)], sem)

    def issue(g, carry):
        for u in range(DMA_UNROLL):
            row_copy(g * DMA_UNROLL + u, 0).start()
            row_copy(g * DMA_UNROLL + u, 1).start()
        return carry

    def drain(g, carry):
        for u in range(DMA_UNROLL):
            row_copy(g * DMA_UNROLL + u, 0).wait()
            row_copy(g * DMA_UNROLL + u, 1).wait()
        return carry

    lax.fori_loop(0, tm // DMA_UNROLL, issue, 0)
    lax.fori_loop(0, tm // DMA_UNROLL, drain, 0)


def _dispatch(pos3, xn, buf0):
    t = xn.shape[0]
    tm = DISPATCH_TILE
    pos3 = pos3.reshape(t // tm, 1, 2 * tm)
    return pl.pallas_call(
        _dispatch_kernel,
        grid=(t // tm,),
        in_specs=[
            pl.BlockSpec((1, 1, 2 * tm), lambda i: (i, 0, 0), memory_space=pltpu.SMEM),
            pl.BlockSpec((tm, HALF_MODEL), lambda i: (i, 0)),
            pl.BlockSpec(memory_space=pl.ANY),
        ],
        out_specs=pl.BlockSpec(memory_space=pl.ANY),
        out_shape=jax.ShapeDtypeStruct(buf0.shape, buf0.dtype),
        scratch_shapes=[pltpu.SemaphoreType.DMA(())],
        input_output_aliases={2: 0},
        compiler_params=pltpu.CompilerParams(
            dimension_semantics=("arbitrary",), vmem_limit_bytes=VMEM_LIMIT),
        name="dispatch",
    )(pos3, xn, buf0)


def _expert_kernel(be_ref, nu_ref, rows_ref, slot_ref, next_ref, x_ref, wg_hbm, wu_hbm, wd_hbm,
                   y_ref, wg_f, wu_f, wd_f, wg_s, wu_s, wd_s, sems, *, layer):
    i = pl.program_id(0)
    used = i < nu_ref[0]
    expert = be_ref[i]
    new_expert = (i == 0) | (expert != be_ref[jnp.maximum(i - 1, 0)])
    slot = slot_ref[i]

    def weight_copies(e, s):
        return (pltpu.make_async_copy(wg_hbm.at[layer, e], wg_f.at[s], sems.at[s, 0]),
                pltpu.make_async_copy(wu_hbm.at[layer, e], wu_f.at[s], sems.at[s, 1]),
                pltpu.make_async_copy(wd_hbm.at[layer, e], wd_f.at[s], sems.at[s, 2]))

    @pl.when(i == 0)
    def _():
        for cp in weight_copies(expert, slot):
            cp.start()

    @pl.when(used & new_expert)
    def _():
        for cp in weight_copies(expert, slot):
            cp.wait()
        wg_s[...] = wg_f[slot].astype(BF16)
        wu_s[...] = wu_f[slot].astype(BF16)
        wd_s[...] = wd_f[slot].astype(BF16)
        nxt = next_ref[i]

        @pl.when(nxt >= 0)
        def _():
            for cp in weight_copies(nxt, 1 - slot):
                cp.start()

    def ffn(nrows):
        x_lo, x_hi = (part.astype(BF16) for part in _unpack_bf16_pairs(x_ref[:nrows, :]))
        g = _mm(x_lo, wg_s[:HALF_MODEL, :]) + _mm(x_hi, wg_s[HALF_MODEL:, :])
        u = _mm(x_lo, wu_s[:HALF_MODEL, :]) + _mm(x_hi, wu_s[HALF_MODEL:, :])
        hmid = (g * jax.nn.sigmoid(g)) * u
        y_ref[:nrows, :] = _pack_bf16_pairs(_mm(hmid.astype(BF16), wd_s[...]))

    half_rows = EXPERT_ROWS // 2
    short = rows_ref[i] <= half_rows

    @pl.when(used & jnp.logical_not(short))
    def _():
        ffn(EXPERT_ROWS)

    @pl.when(used & short)
    def _():
        ffn(half_rows)
        y_ref[half_rows:, :] = jnp.zeros((EXPERT_ROWS - half_rows, HALF_MODEL), y_ref.dtype)

    @pl.when(i >= nu_ref[0])
    def _():
        y_ref[...] = jnp.zeros_like(y_ref)


def _experts(blk_expert, n_used, blk_rows, buf, wg, wu, wd, layer):
    nrows = buf.shape[0]
    nblk = nrows // EXPERT_ROWS
    idx = jnp.arange(nblk, dtype=jnp.int32)
    change = (idx == 0) | (blk_expert != jnp.roll(blk_expert, 1))
    slot = (jnp.cumsum(change.astype(jnp.int32)) - 1) % 2
    later_run = (idx[None, :] > idx[:, None]) & change[None, :] & (idx[None, :] < n_used[0])
    nxt = jnp.where(jnp.any(later_run, axis=1), blk_expert[jnp.argmax(later_run, axis=1)], -1)
    block = lambda i, *_: (i, 0)
    any_space = pl.BlockSpec(memory_space=pl.ANY)
    return pl.pallas_call(
        functools.partial(_expert_kernel, layer=layer),
        grid_spec=pltpu.PrefetchScalarGridSpec(
            num_scalar_prefetch=5,
            grid=(nblk,),
            in_specs=[pl.BlockSpec((EXPERT_ROWS, HALF_MODEL), block),
                      any_space, any_space, any_space],
            out_specs=pl.BlockSpec((EXPERT_ROWS, HALF_MODEL), block),
            scratch_shapes=[
                pltpu.VMEM((2, D_MODEL, D_EXPERT), F32),
                pltpu.VMEM((2, D_MODEL, D_EXPERT), F32),
                pltpu.VMEM((2, D_EXPERT, D_MODEL), F32),
                pltpu.VMEM((D_MODEL, D_EXPERT), BF16),
                pltpu.VMEM((D_MODEL, D_EXPERT), BF16),
                pltpu.VMEM((D_EXPERT, D_MODEL), BF16),
                pltpu.SemaphoreType.DMA((2, 3)),
            ],
        ),
        out_shape=jax.ShapeDtypeStruct((nrows, HALF_MODEL), jnp.uint32),
        compiler_params=pltpu.CompilerParams(
            dimension_semantics=("arbitrary",), vmem_limit_bytes=VMEM_LIMIT),
        name="experts",
    )(blk_expert, n_used, blk_rows, slot.astype(jnp.int32), nxt.astype(jnp.int32),
      buf, wg, wu, wd)


def _combine_ple_kernel(pos_ref, pos_next_ref, h1_ref, p_ref, rf_ref, yb_ref, wple_ref, gple_ref,
                        wpg_ref, gfin_ref, o_ref, ybuf, sems, *, final_norm):
    tm = h1_ref.shape[0]
    i = pl.program_id(0)
    last = pl.num_programs(0) - 1
    slot = lax.rem(i, 2)

    def row_copy(tile_pos_ref, s, r, k):
        src = tile_pos_ref[0, 0, 2 * r + k]
        return pltpu.make_async_copy(
            yb_ref.at[pl.ds(src, 1)], ybuf.at[s, k, pl.ds(r, 1)], sems.at[s])

    def looped(tile_pos_ref, s, op):
        def body(g, carry):
            for u in range(DMA_UNROLL):
                op(row_copy(tile_pos_ref, s, g * DMA_UNROLL + u, 0))
                op(row_copy(tile_pos_ref, s, g * DMA_UNROLL + u, 1))
            return carry
        lax.fori_loop(0, tm // DMA_UNROLL, body, 0)

    @pl.when(i == 0)
    def _():
        looped(pos_ref, 0, lambda cp: cp.start())

    looped(pos_ref, slot, lambda cp: cp.wait())
    for r in range(tm):
        row_copy(pos_next_ref, 1 - slot, r, 0).start()
        row_copy(pos_next_ref, 1 - slot, r, 1).start()
    e = _rms(_mm(p_ref[...].astype(BF16), wple_ref[...]), gple_ref[...])

    rf = rf_ref[...]
    y0_lo, y0_hi = _unpack_bf16_pairs(ybuf[slot, 0])
    y1_lo, y1_hi = _unpack_bf16_pairs(ybuf[slot, 1])
    g0, g1 = rf[:, 0:1], rf[:, 1:2]
    moe = jnp.concatenate([y0_lo * g0 + y1_lo * g1, y0_hi * g0 + y1_hi * g1], axis=-1)
    h2 = h1_ref[...] + moe
    h3 = h2 + e * jax.nn.sigmoid(_mm(h2.astype(BF16), wpg_ref[...]))
    if final_norm:
        h3 = _rms(h3, gfin_ref[...])
    o_ref[...] = h3

    @pl.when(i == last)
    def _():
        looped(pos_next_ref, 1 - slot, lambda cp: cp.wait())


def _combine_ple(pos3, h1, p, rf, yb, wple, gple, wpg, gfin, layer, final_norm):
    t = h1.shape[0]
    tm = TOKEN_TILE
    row = lambda w: pl.BlockSpec((tm, w), lambda i: (i, 0))
    full = lambda a, b: pl.BlockSpec((a, b), lambda i: (0, 0))
    return pl.pallas_call(
        functools.partial(_combine_ple_kernel, final_norm=final_norm),
        grid=(t // tm,),
        in_specs=[
            pl.BlockSpec((1, 1, 2 * tm), lambda i: (i, 0, 0), memory_space=pltpu.SMEM),
            pl.BlockSpec((1, 1, 2 * tm), lambda i: (jnp.minimum(i + 1, t // tm - 1), 0, 0),
                         memory_space=pltpu.SMEM),
            row(D_MODEL),
            pl.BlockSpec((tm, PLE_DIM), lambda i: (layer * (t // tm) + i, 0)),
            row(LANES),
            pl.BlockSpec(memory_space=pl.ANY),
            full(PLE_DIM, D_MODEL), full(1, D_MODEL), full(D_MODEL, D_MODEL), full(1, D_MODEL),
        ],
        out_specs=row(D_MODEL),
        out_shape=jax.ShapeDtypeStruct((t, D_MODEL), F32),
        scratch_shapes=[pltpu.VMEM((2, 2, tm, HALF_MODEL), jnp.uint32),
                        pltpu.SemaphoreType.DMA((2,))],
        compiler_params=pltpu.CompilerParams(
            dimension_semantics=("arbitrary",), vmem_limit_bytes=VMEM_LIMIT),
        name="combine_ple",
    )(pos3, pos3, h1, p, rf, yb, wple, gple, wpg, gfin)


def _rel_bucket(dist):
    n = jnp.maximum(dist, 0)
    max_exact = REL_BUCKETS // 2
    nf = jnp.maximum(n, 1).astype(F32)
    large = max_exact + (jnp.log(nf / max_exact) / math.log(REL_MAX_DIST / max_exact)
                         * (REL_BUCKETS - max_exact)).astype(jnp.int32)
    large = jnp.minimum(large, REL_BUCKETS - 1)
    return jnp.where(n < max_exact, n, large)


def _bias_tiles(rel_bias):
    qi = jnp.arange(WINDOW)[:, None]
    kj = jnp.arange(2 * WINDOW)[None, :]
    bucket = _rel_bucket(qi + WINDOW - kj)
    onehot = (bucket[..., None] == jnp.arange(REL_BUCKETS)).astype(F32)
    nat = jnp.einsum('qkb,bh->hqk', onehot, rel_bias, precision=HIGHEST)
    dsa = nat[:DSA_HEADS]
    far = jnp.broadcast_to(rel_bias[REL_BUCKETS - 1, :DSA_HEADS][:, None, None],
                           (DSA_HEADS, WINDOW, WINDOW))
    dist = qi + WINDOW - kj
    in_window = (dist >= 0) & (dist < WINDOW)
    swa = nat[DSA_HEADS:]
    swa_tiles = jnp.stack([jnp.where(in_window, swa, -jnp.inf),
                           jnp.where(in_window & (kj >= WINDOW), swa, -jnp.inf)], axis=0)
    return jnp.stack([far, dsa[:, :, :WINDOW], dsa[:, :, WINDOW:]], axis=1), swa_tiles


def _pack_w_in(w):
    sizes = (128, 128, 256, 256, 16, 256, 64, 64, 512, 64, 8, 512, 128, 128)
    offs = [0]
    for s in sizes:
        offs.append(offs[-1] + s)
    wt = w.T
    gq, gk, gv, gg, ga, dq, dk, dv, iq, ik, iw, sq, sk, sv = (
        wt[offs[n]:offs[n + 1]] for n in range(len(sizes)))
    z = lambda n: jnp.zeros((n, w.shape[0]), w.dtype)
    packed = jnp.concatenate(
        [iq, sq, gv, gg, dq, gq, gk, ga, z(128 - GLA_RANK), dk, dv, ik, iw,
         z(128 - IDX_DIM - IDX_HEADS), sk, sv], axis=0)
    return packed.astype(BF16)


def kernel(x, p, rel_bias, g_mix, w_in, gla_w_alpha, gla_b_alpha, gla_g_norm, swa_sinks, w_out,
           g_ffn, w_router_group, b_router_group, w_router_expert, b_router_expert, w_expert_gate,
           w_expert_up, w_expert_down, w_ple, g_ple, w_ple_gate, g_final):
    bsz, seq, d = x.shape
    depth = w_in.shape[0]
    t = bsz * seq
    assert d == D_MODEL and t % DISPATCH_TILE == 0 and seq % GLA_BLOCK == 0 and seq % KC == 0
    n_blocks = -(-(2 * t) // EXPERT_ROWS) + N_EXPERTS
    dsa_tiles, swa_bias = _bias_tiles(rel_bias)

    h = x.reshape(t, d)
    for i in range(depth):
        proj = _inproj(h, g_mix[i][None, :], _pack_w_in(w_in[i]))
        proj3 = proj.reshape(bsz, seq, D_PROJ)

        wal = jnp.zeros((128, 128), F32).at[:GLA_RANK].set(gla_w_alpha[i])
        og = _gla(proj3, wal, gla_b_alpha[i][None, :],
                  jnp.tile(gla_g_norm[i], GLA_HEADS)[None, :])
        od = _dsa(proj3, dsa_tiles)
        os_ = _swa(proj3, swa_sinks[i], swa_bias)

        w_r = jnp.concatenate([w_router_group[i], w_router_expert[i]], axis=1)
        w_r = jnp.pad(w_r, ((0, 0), (0, LANES - w_r.shape[1])))
        wr_hi, wr_lo = _split_bf16(w_r)
        b_r = jnp.pad(jnp.concatenate([b_router_group[i], b_router_expert[i]]),
                      (0, LANES - N_GROUPS - N_EXPERTS))[None, :]
        h1, xn, ri, rf, cnt = _outproj_router(
            h, og.reshape(t, -1), od.reshape(t, -1), os_.reshape(t, -1), w_out[i].astype(BF16),
            g_ffn[i][None, :], wr_hi, wr_lo, b_r)

        counts = cnt[ROUTE_OFF:ROUTE_OFF + N_EXPERTS, 0].astype(jnp.int32)
        padded = (counts + EXPERT_ROWS - 1) // EXPERT_ROWS * EXPERT_ROWS
        pad_end = jnp.cumsum(padded)
        pad_start = pad_end - padded
        expert_ids = jnp.arange(N_EXPERTS, dtype=jnp.int32)
        slot_start = jnp.sum(
            jnp.where(ri[0:2, :, None] == expert_ids, pad_start, 0), axis=-1)
        pos3 = (slot_start + ri[2:4]).T.reshape(t // TOKEN_TILE, 1, 2 * TOKEN_TILE)
        gates = jnp.pad(rf[0:2].T, ((0, 0), (0, LANES - 2)))
        blk_start = jnp.arange(n_blocks, dtype=jnp.int32) * EXPERT_ROWS
        blk_expert = jnp.minimum(
            jnp.sum((pad_end[None, :] <= blk_start[:, None]).astype(jnp.int32), axis=1),
            N_EXPERTS - 1)
        n_used = (pad_end[-1:] // EXPERT_ROWS).astype(jnp.int32)
        onehot_e = blk_expert[:, None] == expert_ids[None, :]
        rows_before = blk_start - jnp.sum(jnp.where(onehot_e, pad_start[None, :], 0), axis=1)
        blk_rows = jnp.clip(jnp.sum(jnp.where(onehot_e, counts[None, :], 0), axis=1) - rows_before,
                            0, EXPERT_ROWS).astype(jnp.int32)

        buf = _dispatch(pos3, xn, jnp.zeros((n_blocks * EXPERT_ROWS, HALF_MODEL), jnp.uint32))
        yb = _experts(blk_expert, n_used, blk_rows, buf, w_expert_gate, w_expert_up,
                      w_expert_down, i)
        h = _combine_ple(pos3, h1, p.reshape(depth * t, PLE_DIM), gates, yb, w_ple[i].astype(BF16),
                         g_ple[i][None, :], w_ple_gate[i].astype(BF16), g_final[None, :],
                         layer=i, final_norm=(i == depth - 1))
    return h.reshape(bsz, seq, d)
```

```python
import functools
import math

import jax
import jax.numpy as jnp
from jax import lax
from jax.experimental import pallas as pl
from jax.experimental.pallas import tpu as pltpu

F32 = jnp.float32
BF16 = jnp.bfloat16
HIGHEST = lax.Precision.HIGHEST

D_MODEL = 1024
HEAD_DIM = 64
GLA_HEADS = 4
GLA_DK = 32
GLA_DV = 64
GLA_RANK = 16
GLA_TAU = 16.0
GLA_CHUNK = 64
DSA_HEADS = 4
IDX_HEADS = 8
IDX_DIM = 64
DSA_TOPK_MAX = 256
SWA_HEADS = 8
SWA_KV_HEADS = 2
SWA_GROUP = SWA_HEADS // SWA_KV_HEADS
WINDOW = 128
REL_BUCKETS = 32
REL_MAX_DIST = 128
N_GROUPS = 4
EXPERTS_PER_GROUP = 8
N_EXPERTS = N_GROUPS * EXPERTS_PER_GROUP
D_EXPERT = 512
PLE_DIM = 256
EPS = 1e-6

LANES = 128
SUBLANES = 8
PACK16 = 16

COL_IQ, COL_SQ = 0, 512
COL_GV, COL_GG, COL_DQ = 1024, 1280, 1536
COL_GQ, COL_GK, COL_GA, COL_DKV, COL_IKW, COL_SK, COL_SV = 1792, 1920, 2048, 2176, 2304, 2432, 2560
D_PROJ = 2688

TOKEN_TILE = 512
INPROJ_TILE = 512
ROUTER_TILE = 512
DISPATCH_TILE = 1024
EXPERT_ROWS = 512
GLA_BLOCK = 256
GLA_BATCH = 2
SWA_BATCH = 2
QB = 256
KC = 256
DMA_UNROLL = 8
NEG_BIG = -1e30
LOG2_E = math.log2(math.e)
VMEM_LIMIT = 48 * 1024 * 1024


def _nt(a, b, precision=None):
    return lax.dot_general(a, b, (((1,), (1,)), ((), ())), precision=precision,
                           preferred_element_type=F32)


def _mm(a, b, precision=None):
    return jnp.dot(a, b, precision=precision, preferred_element_type=F32)


def _eye(n, dtype):
    r = lax.broadcasted_iota(jnp.int32, (n, n), 0)
    c = lax.broadcasted_iota(jnp.int32, (n, n), 1)
    return jnp.where(r == c, 1.0, 0.0).astype(dtype)


def _rms(x, g):
    return x * lax.rsqrt(jnp.mean(x * x, axis=-1, keepdims=True) + EPS) * g


HALF_MODEL = D_MODEL // 2
HIGH16 = 0xFFFF0000


def _pack_bf16_pairs(x):
    bits = pltpu.bitcast(x.astype(BF16).astype(F32), jnp.uint32)
    return (bits[:, HALF_MODEL:] & jnp.uint32(HIGH16)) | (bits[:, :HALF_MODEL] >> 16)


def _unpack_bf16_pairs(w):
    return pltpu.bitcast(w << 16, F32), pltpu.bitcast(w & jnp.uint32(HIGH16), F32)


def _inproj_kernel(h_ref, g_ref, w_ref, o_ref):
    a = _rms(h_ref[...], g_ref[...])
    o_ref[...] = _nt(a.astype(BF16), w_ref[...])


def _inproj(h, g, w):
    t = h.shape[0]
    return pl.pallas_call(
        _inproj_kernel,
        grid=(t // INPROJ_TILE,),
        in_specs=[
            pl.BlockSpec((INPROJ_TILE, D_MODEL), lambda i: (i, 0)),
            pl.BlockSpec((1, D_MODEL), lambda i: (0, 0)),
            pl.BlockSpec((D_PROJ, D_MODEL), lambda i: (0, 0)),
        ],
        out_specs=pl.BlockSpec((INPROJ_TILE, D_PROJ), lambda i: (i, 0)),
        out_shape=jax.ShapeDtypeStruct((t, D_PROJ), F32),
        compiler_params=pltpu.CompilerParams(
            dimension_semantics=("arbitrary",), vmem_limit_bytes=VMEM_LIMIT),
        name="inproj",
    )(h, g, w)


def _gla_kernel(q_ref, k_ref, v_ref, gg_ref, ga_ref, wal_ref, bal_ref, gn_ref, o_ref,
                state_ref, sc_all_ref, p_all_ref, b_ref):
    @pl.when(pl.program_id(1) == 0)
    def _():
        state_ref[...] = jnp.zeros_like(state_ref)

    @pl.when((pl.program_id(0) == 0) & (pl.program_id(1) == 0))
    def _():
        p_all_ref[...] = jnp.zeros_like(p_all_ref)

    chunk_fns = []
    for bi in range(GLA_BATCH):
        one = pl.ds(bi, 1)
        chunk_fns.append(_gla_sequence(
            q_ref.at[one], k_ref.at[one], v_ref.at[one], gg_ref.at[one], ga_ref.at[one],
            wal_ref, bal_ref, gn_ref, o_ref.at[one], state_ref.at[bi],
            sc_all_ref.at[bi], p_all_ref.at[bi], b_ref.at[bi]))
    for ci in range(GLA_BLOCK // GLA_CHUNK):
        for chunk in chunk_fns:
            chunk(ci)


def _gla_sequence(q_ref, k_ref, v_ref, gg_ref, ga_ref, wal_ref, bal_ref, gn_ref, o_ref,
                  state_ref, sc_all_ref, p_all_ref, b_ref):
    hk = GLA_HEADS * GLA_DK
    hv = GLA_HEADS * GLA_DV
    c = GLA_CHUNK
    half_c = c // 2
    assert 2 * c == LANES

    pj = lax.broadcasted_iota(jnp.int32, (half_c, LANES), 0)
    pl_ = lax.broadcasted_iota(jnp.int32, (half_c, LANES), 1)
    pair_causal = jnp.where(pl_ < c, pl_, pl_ - c) <= jnp.where(pl_ < c, pj, pj + half_c)
    rv = lax.broadcasted_iota(jnp.int32, (hv, hk), 0) // GLA_DV
    ck = lax.broadcasted_iota(jnp.int32, (hv, hk), 1) // GLA_DK
    blockdiag_t = jnp.where(rv == ck, 1.0, 0.0).astype(F32)
    hr = lax.broadcasted_iota(jnp.int32, (SUBLANES, hk), 0)
    hl = lax.broadcasted_iota(jnp.int32, (SUBLANES, hk), 1) // GLA_DK
    head_rows = jnp.where(hr == hl, 1.0, 0.0).astype(BF16)
    eye_v = _eye(hv, BF16)

    ga_hi, ga_lo = _split_bf16(ga_ref[0])
    w_hi, w_lo = _split_bf16(wal_ref[...])
    z = _mm(ga_hi, w_hi) + _mm(ga_hi, w_lo) + _mm(ga_lo, w_hi) + bal_ref[...]
    log_a = (jnp.minimum(z, 0.0) - jnp.log1p(jnp.exp(-jnp.abs(z)))) * (1.0 / GLA_TAU)
    la_hi = log_a.astype(BF16)
    la_r1 = log_a - la_hi.astype(F32)
    la_mid = la_r1.astype(BF16)
    la_lo = (la_r1 - la_mid.astype(F32)).astype(BF16)
    rb = lax.broadcasted_iota(jnp.int32, (GLA_BLOCK, GLA_BLOCK), 0)
    cb = lax.broadcasted_iota(jnp.int32, (GLA_BLOCK, GLA_BLOCK), 1)
    tril = jnp.where((rb >= cb) & (rb // c == cb // c), 1.0, 0.0).astype(BF16)
    b_ref[...] = _mm(tril, la_hi) + _mm(tril, la_mid) + _mm(tril, la_lo)

    def chunk(ci):
        rows = slice(ci * c, (ci + 1) * c)
        p_ref, sc_ref = p_all_ref.at[ci], sc_all_ref.at[ci]
        q = q_ref[0, rows, :] * (GLA_DK ** -0.5)
        k = k_ref[0, rows, :]
        v = v_ref[0, rows, :]
        b = b_ref[rows, :]
        state_t = state_ref[...]
        o_inter = _nt((q * jnp.exp(b)).astype(BF16), state_t.astype(BF16))

        b2 = b * LOG2_E
        for t in range(c):
            ns = PACK16 * (t // PACK16 + 1)
            slot = 2 * (t % half_c) + t // half_c
            p = jnp.exp2(b2[t:t + 1, :] - b2[:ns, :]) * k[:ns, :] * q[t:t + 1, :]
            p_ref[slot * c:slot * c + ns, :] = p.astype(BF16)
        rows_hs = _nt(head_rows, p_ref[...])
        for jj in range(half_c):
            for h in range(GLA_HEADS):
                sc_ref[h, jj:jj + 1, :] = rows_hs[h:h + 1, jj * LANES:(jj + 1) * LANES]
        vb = v.astype(BF16)
        zero_v = jnp.zeros((c, GLA_DV), BF16)
        o_heads = []
        for h in range(GLA_HEADS):
            vh = vb[:, h * GLA_DV:(h + 1) * GLA_DV]
            v_pair = jnp.concatenate([jnp.concatenate([vh, zero_v], axis=1),
                                      jnp.concatenate([zero_v, vh], axis=1)], axis=0)
            scores = jnp.where(pair_causal, sc_ref[h], 0.0).astype(BF16)
            o_pair = _mm(scores, v_pair)
            o_heads.append(jnp.concatenate([o_pair[:, :GLA_DV], o_pair[:, GLA_DV:]], axis=0))
        o = o_inter + jnp.concatenate(o_heads, axis=-1)

        b_last = b[c - 1:c, :]
        kd = (k * jnp.exp(b_last - b)).astype(BF16)
        v_t = _nt(eye_v, vb).astype(BF16)
        upd_t = _mm(v_t, kd)
        state_ref[...] = jnp.exp(b_last) * state_t + upd_t * blockdiag_t

        gg = gg_ref[0, rows, :]
        outs = []
        for h in range(GLA_HEADS):
            oh = o[:, h * GLA_DV:(h + 1) * GLA_DV]
            ms = jnp.mean(oh * oh, axis=-1, keepdims=True)
            outs.append(oh * lax.rsqrt(ms + EPS))
        on = jnp.concatenate(outs, axis=-1) * gn_ref[...]
        o_ref[0, rows, :] = (on * (gg * jax.nn.sigmoid(gg))).astype(o_ref.dtype)

    return chunk


def _gla(proj3, wal, bal, gn):
    bsz, seq, _ = proj3.shape
    nb = seq // GLA_BLOCK

    def col(width, off):
        return pl.BlockSpec((GLA_BATCH, GLA_BLOCK, width), lambda b, i: (b, i, off // width))

    per_seq = lambda *shape: (GLA_BATCH,) + shape
    return pl.pallas_call(
        _gla_kernel,
        grid=(bsz // GLA_BATCH, nb),
        in_specs=[
            col(128, COL_GQ), col(128, COL_GK), col(256, COL_GV), col(256, COL_GG), col(128, COL_GA),
            pl.BlockSpec((128, 128), lambda b, i: (0, 0)),
            pl.BlockSpec((1, 128), lambda b, i: (0, 0)),
            pl.BlockSpec((1, 256), lambda b, i: (0, 0)),
        ],
        out_specs=pl.BlockSpec((GLA_BATCH, GLA_BLOCK, 256), lambda b, i: (b, i, 0)),
        out_shape=jax.ShapeDtypeStruct((bsz, seq, GLA_HEADS * GLA_DV), BF16),
        scratch_shapes=[
            pltpu.VMEM(per_seq(GLA_HEADS * GLA_DV, GLA_HEADS * GLA_DK), F32),
            pltpu.VMEM(per_seq(GLA_BLOCK // GLA_CHUNK, GLA_HEADS, GLA_CHUNK // 2, 2 * GLA_CHUNK),
                       F32),
            pltpu.VMEM(per_seq(GLA_BLOCK // GLA_CHUNK, GLA_CHUNK * GLA_CHUNK,
                               GLA_HEADS * GLA_DK), BF16),
            pltpu.VMEM(per_seq(GLA_BLOCK, GLA_HEADS * GLA_DK), F32),
        ],
        compiler_params=pltpu.CompilerParams(
            dimension_semantics=("arbitrary", "arbitrary"), vmem_limit_bytes=VMEM_LIMIT),
        name="gla",
    )(proj3, proj3, proj3, proj3, proj3, wal, bal, gn)


DSA_BISECT_STEPS = 14
DSA_ACCUMULATORS = 2
DSA_WALK_UNCONDITIONAL = 1
DSA_PV_PARTS = 4


def _dsa_kernel(dq_ref, kv_ref, iq_ref, ikw_ref, btile_ref, o_ref,
                sc_ref, lg_ref, p_ref, kb_ref, vb_ref, mp_ref, lp_ref, acc_ref, cut_ref, *, n_sel):
    j = pl.program_id(1)
    nkc = ((j + 1) * QB + KC - 1) // KC
    seq = kv_ref.shape[1]
    ksel = float(n_sel)

    @pl.when(j == 0)
    def _():
        kv = kv_ref[0]
        kb_ref[...] = kv[:, :HEAD_DIM].astype(BF16)
        vb_ref[...] = kv[:, HEAD_DIM:].astype(BF16)

    def rows(c):
        return pl.ds(pl.multiple_of(c * KC, KC), KC)

    wide = DSA_ACCUMULATORS * SUBLANES

    def fold(op, x):
        return op(x.reshape(KC // wide, wide, QB), axis=0)

    def all8(op, xw, roll_op):
        x8 = op(xw.reshape(DSA_ACCUMULATORS, SUBLANES, QB), axis=0)
        for shift in (4, 2, 1):
            x8 = roll_op(x8, pltpu.roll(x8, shift, 0))
        return x8

    def widen(x8):
        return jnp.concatenate([x8] * DSA_ACCUMULATORS, axis=0)

    zeros8 = jnp.zeros((SUBLANES, QB), F32)
    zerosw = jnp.zeros((wide, QB), F32)
    infw = jnp.full((wide, QB), jnp.inf, F32)

    ikw_q = ikw_ref[0, pl.ds(pl.multiple_of(j * QB, QB), QB), :]
    w_t = ikw_q.T[IDX_DIM:IDX_DIM + IDX_HEADS, :] * (IDX_HEADS ** -0.5 * IDX_DIM ** -0.5)
    iq = iq_ref[0].astype(BF16)
    iq_heads = [iq[:, h * IDX_DIM:(h + 1) * IDX_DIM] for h in range(IDX_HEADS)]

    s_loc = lax.broadcasted_iota(jnp.int32, (KC, QB), 0)
    t_glob = j * QB + lax.broadcasted_iota(jnp.int32, (KC, QB), 1)
    s_loc_f = s_loc.astype(F32)

    def score_chunk(c, st):
        mn8, mx8, ge8, gt8 = st
        ik = ikw_ref[0, rows(c), :][:, :IDX_DIM].astype(BF16)
        acc = jnp.zeros((KC, QB), F32)
        for h in range(IDX_HEADS):
            rel = _nt(ik, iq_heads[h])
            acc = acc + jnp.maximum(rel, 0.0) * w_t[h:h + 1, :]
        adm = s_loc + c * KC <= t_glob
        blk = jnp.where(adm, acc, -jnp.inf)
        sc_ref[rows(c), :] = blk
        mn8 = jnp.minimum(mn8, fold(jnp.min, jnp.where(adm, acc, jnp.inf)))
        mx8 = jnp.maximum(mx8, fold(jnp.max, blk))
        ge8 = ge8 + fold(jnp.sum, jnp.where(blk >= 0.0, 1.0, 0.0))
        gt8 = gt8 + fold(jnp.sum, jnp.where(blk > 0.0, 1.0, 0.0))
        return mn8, mx8, ge8, gt8

    def paired(body, init):
        def two(i, st):
            return body(2 * i + 1, body(2 * i, st))
        st = lax.fori_loop(0, nkc // 2, two, init)
        return lax.cond(nkc % 2 == 1, lambda s: body(nkc - 1, s), lambda s: s, st)

    mn8, mx8, ge8, gt8 = paired(score_chunk, (infw, -infw, zerosw, zerosw))
    row_min = all8(jnp.min, mn8, jnp.minimum)
    row_max = all8(jnp.max, mx8, jnp.maximum)
    c_ge0, c_gt0 = all8(jnp.sum, ge8, jnp.add), all8(jnp.sum, gt8, jnp.add)

    def blocks(c):
        return sc_ref[rows(c), :].reshape(KC // wide, wide, QB)

    def count_ge(x8):
        xw = widen(x8)[None]

        def body(c, cw):
            return cw + jnp.sum(jnp.where(blocks(c) >= xw, 1.0, 0.0), axis=0)
        return all8(jnp.sum, paired(body, zerosw), jnp.add)

    def min_ge(x8):
        xw = widen(x8)[None]

        def body(c, mw):
            blk = blocks(c)
            return jnp.minimum(mw, jnp.min(jnp.where(blk >= xw, blk, jnp.inf), axis=0))
        return all8(jnp.min, lax.fori_loop(0, nkc, body, infw), jnp.minimum)

    def pass_gt(x8):
        xw = widen(x8)[None]

        def body(c, st):
            cw, mw = st
            blk = blocks(c)
            hit = blk > xw
            return (cw + jnp.sum(jnp.where(hit, 1.0, 0.0), axis=0),
                    jnp.minimum(mw, jnp.min(jnp.where(hit, blk, jnp.inf), axis=0)))
        cw, mw = lax.fori_loop(0, nkc, body, (zerosw, infw))
        return all8(jnp.sum, cw, jnp.add), all8(jnp.min, mw, jnp.minimum)

    n_adm = (j * QB + lax.broadcasted_iota(jnp.int32, (SUBLANES, QB), 1) + 1).astype(F32)
    at_zero = (c_gt0 < ksel) & (c_ge0 >= ksel)
    above = c_gt0 >= ksel
    lo = jnp.where(above | at_zero, 0.0, row_min)
    c_lo = jnp.where(above | at_zero, c_ge0, n_adm)
    settled = at_zero | (n_adm <= ksel)
    hi = jnp.where(settled, lo, jnp.where(above, row_max, 0.0))

    def bisect(_, st):
        lo, hi, c_lo = st
        mid = lo + (hi - lo) * 0.5
        cnt = count_ge(mid)
        up = cnt >= ksel
        return jnp.where(up, mid, lo), jnp.where(up, hi, mid), jnp.where(up, cnt, c_lo)

    lo, _, c_lo = lax.fori_loop(0, DSA_BISECT_STEPS, bisect, (lo, hi, c_lo))

    v0 = min_ge(lo)
    done0 = jnp.where(settled | (c_lo == ksel), 1.0, 0.0)
    c_gt_init = jnp.where(at_zero, c_gt0, 0.0)

    def walk_cond(st):
        return jnp.min(st[3]) < 0.5

    def walk_body(st):
        v, c_ge, c_gt, dn = st
        cnt, vnext = pass_gt(v)
        live = dn < 0.5
        fin = live & (cnt < ksel)
        step = live & (cnt >= ksel)
        c_gt = jnp.where(fin, cnt, c_gt)
        v = jnp.where(step, vnext, v)
        c_ge = jnp.where(step, cnt, c_ge)
        dn = jnp.where(fin | (step & (cnt == ksel)), 1.0, dn)
        return v, c_ge, c_gt, dn

    walk = (v0, c_lo, c_gt_init, done0)
    for _ in range(DSA_WALK_UNCONDITIONAL):
        walk = walk_body(walk)
    tau8, c_ge, c_gt, _ = lax.while_loop(walk_cond, walk_body, walk)

    need = ksel - c_gt
    cut_ref[...] = jnp.full(cut_ref.shape, float(seq), F32)

    @pl.when(jnp.max(c_ge) > ksel)
    def _():
        s_grp = lax.broadcasted_iota(jnp.int32, (KC // wide, wide, QB), 0) * wide
        s_sub = lax.broadcasted_iota(jnp.int32, (KC // wide, wide, QB), 1)
        s_idx = (s_grp + s_sub).astype(F32)
        tauw = widen(tau8)[None]

        def count_ties_below(m8):
            mw = widen(m8)[None]

            def body(c, cw):
                hit = (blocks(c) == tauw) & (s_idx + (c * KC).astype(F32) < mw)
                return cw + jnp.sum(jnp.where(hit, 1.0, 0.0), axis=0)
            return all8(jnp.sum, lax.fori_loop(0, nkc, body, zerosw), jnp.add)

        def idx_bisect(_, lh):
            lo_m, hi_m = lh
            mid = jnp.floor((lo_m + hi_m) * 0.5)
            ok = count_ties_below(mid) >= need
            return jnp.where(ok, lo_m, mid), jnp.where(ok, mid, hi_m)

        _, hi_m = lax.fori_loop(0, int(math.log2(seq)) + 1, idx_bisect,
                                (zeros8, zeros8 + float(seq)))
        cut_ref[...] = jnp.where(c_ge > ksel, hi_m, float(seq))

    tau = tau8[0:1, :]
    cut = cut_ref[0:1, :]

    q = (dq_ref[0] * (HEAD_DIM ** -0.5)).astype(BF16)
    q4 = jnp.concatenate([q[:, h * HEAD_DIM:(h + 1) * HEAD_DIM] for h in range(DSA_HEADS)],
                         axis=0)
    mp_ref[...] = jnp.full(mp_ref.shape, NEG_BIG, F32)
    lp_ref[...] = jnp.zeros(lp_ref.shape, F32)

    pv_parts = min(DSA_PV_PARTS, seq // KC)
    part = seq // pv_parts

    def cols(c):
        return pl.ds(pl.multiple_of(c * KC, KC), KC)

    def lane_fold(op, x):
        out = x[:, :LANES]
        for g in range(1, KC // LANES):
            out = op(out, x[:, g * LANES:(g + 1) * LANES])
        return out

    def logits_chunk(c, carry):
        blk = sc_ref[rows(c), :]
        sidx = s_loc_f + (c * KC).astype(F32)
        sel = (blk > tau) | ((blk == tau) & (sidx < cut))
        neg_t = jnp.where(sel, 0.0, NEG_BIG).T
        lg4 = _nt(q4, kb_ref[rows(c), :])
        which = [[jnp.clip((c * (KC // WINDOW) + b) - (j * (QB // WINDOW) + a) + 2, 0, 2)
                  for b in range(KC // WINDOW)] for a in range(QB // WINDOW)]
        for h in range(DSA_HEADS):
            hq = slice(h * QB, (h + 1) * QB)
            bias = jnp.concatenate(
                [jnp.concatenate([btile_ref[h, w] for w in which_a], axis=1)
                 for which_a in which], axis=0)
            lg = lg4[hq, :] + bias + neg_t
            lg_ref[hq, cols(c)] = lg
            mp_ref[h] = jnp.maximum(mp_ref[h], lane_fold(jnp.maximum, lg))
        return carry

    paired(logits_chunk, 0)
    m_rows = [jnp.max(mp_ref[h], axis=-1, keepdims=True) for h in range(DSA_HEADS)]

    def probs_chunk(c, carry):
        for h in range(DSA_HEADS):
            hq = slice(h * QB, (h + 1) * QB)
            p = jnp.exp(lg_ref[hq, cols(c)] - m_rows[h])
            lp_ref[h] = lp_ref[h] + lane_fold(jnp.add, p)
            p_ref[hq, cols(c)] = p.astype(BF16)
        return carry

    def zero_chunk(c, carry):
        p_ref[:, cols(c)] = jnp.zeros((DSA_HEADS * QB, KC), BF16)
        return carry

    lax.fori_loop(0, nkc, probs_chunk, 0)
    n_parts = ((j + 1) * QB + part - 1) // part
    lax.fori_loop(nkc, n_parts * (part // KC), zero_chunk, 0)
    acc_ref[...] = _mm(p_ref[:, :part], vb_ref[:part, :])
    for k in range(1, pv_parts):
        @pl.when(k < n_parts)
        def _(k=k):
            acc_ref[...] = acc_ref[...] + _mm(p_ref[:, k * part:(k + 1) * part],
                                              vb_ref[k * part:(k + 1) * part, :])

    outs = []
    for h in range(DSA_HEADS):
        l = jnp.sum(lp_ref[h], axis=-1, keepdims=True)
        outs.append(acc_ref[h * QB:(h + 1) * QB, :] / l)
    o_ref[0] = jnp.concatenate(outs, axis=-1).astype(o_ref.dtype)


def _dsa(proj3, btiles):
    bsz, seq, _ = proj3.shape
    n_sel = min(DSA_TOPK_MAX, seq // 4)
    return pl.pallas_call(
        functools.partial(_dsa_kernel, n_sel=n_sel),
        grid=(bsz, seq // QB),
        in_specs=[
            pl.BlockSpec((1, QB, 256), lambda b, j: (b, j, COL_DQ // 256)),
            pl.BlockSpec((1, seq, 128), lambda b, j: (b, 0, COL_DKV // 128)),
            pl.BlockSpec((1, QB, 512), lambda b, j: (b, j, COL_IQ // 512)),
            pl.BlockSpec((1, seq, 128), lambda b, j: (b, 0, COL_IKW // 128)),
            pl.BlockSpec((DSA_HEADS, 3, WINDOW, WINDOW), lambda b, j: (0, 0, 0, 0)),
        ],
        out_specs=pl.BlockSpec((1, QB, 256), lambda b, j: (b, j, 0)),
        out_shape=jax.ShapeDtypeStruct((bsz, seq, DSA_HEADS * HEAD_DIM), BF16),
        scratch_shapes=[
            pltpu.VMEM((seq, QB), F32),
            pltpu.VMEM((DSA_HEADS * QB, seq), F32),
            pltpu.VMEM((DSA_HEADS * QB, seq), BF16),
            pltpu.VMEM((seq, HEAD_DIM), BF16),
            pltpu.VMEM((seq, HEAD_DIM), BF16),
            pltpu.VMEM((DSA_HEADS, QB, LANES), F32),
            pltpu.VMEM((DSA_HEADS, QB, LANES), F32),
            pltpu.VMEM((DSA_HEADS * QB, HEAD_DIM), F32),
            pltpu.VMEM((SUBLANES, QB), F32),
        ],
        compiler_params=pltpu.CompilerParams(
            dimension_semantics=("arbitrary", "arbitrary"), vmem_limit_bytes=VMEM_LIMIT),
        name="dsa",
    )(proj3, proj3, proj3, proj3, btiles)


def _swa_kernel(sink_ref, q_ref, kc_ref, kp_ref, vc_ref, vp_ref, bias_ref, o_ref):
    n = pl.program_id(1)
    nseq = q_ref.shape[0]
    q = [(q_ref[s] * (HEAD_DIM ** -0.5)).astype(BF16) for s in range(nseq)]
    k2 = [jnp.concatenate([kp_ref[s], kc_ref[s]], axis=0).astype(BF16) for s in range(nseq)]
    v2 = [jnp.concatenate([vp_ref[s], vc_ref[s]], axis=0).astype(BF16) for s in range(nseq)]
    ones = jnp.ones((2 * WINDOW, HEAD_DIM), BF16)
    first = jnp.where(n == 0, 1, 0)
    outs = [[] for _ in range(nseq)]
    for h in range(SWA_HEADS):
        kvh = h // SWA_GROUP
        sink = sink_ref[h]
        for s in range(nseq):
            kh = k2[s][:, kvh * HEAD_DIM:(kvh + 1) * HEAD_DIM]
            v_ones = jnp.concatenate([v2[s][:, kvh * HEAD_DIM:(kvh + 1) * HEAD_DIM], ones], axis=1)
            lg = _nt(q[s][:, h * HEAD_DIM:(h + 1) * HEAD_DIM], kh) + bias_ref[first, h]
            m = jnp.maximum(jnp.max(lg, axis=-1, keepdims=True), sink)
            num_den = _mm(jnp.exp(lg - m).astype(BF16), v_ones)
            outs[s].append(num_den[:, :HEAD_DIM] / (num_den[:, HEAD_DIM:] + jnp.exp(sink - m)))
    for s in range(nseq):
        o_ref[s] = jnp.concatenate(outs[s], axis=-1).astype(o_ref.dtype)


def _swa(proj3, sinks, bias_nat):
    bsz, seq, _ = proj3.shape
    return pl.pallas_call(
        _swa_kernel,
        grid=(bsz // SWA_BATCH, seq // WINDOW),
        in_specs=[
            pl.BlockSpec(memory_space=pltpu.SMEM),
            pl.BlockSpec((SWA_BATCH, WINDOW, 512), lambda b, n: (b, n, COL_SQ // 512)),
            pl.BlockSpec((SWA_BATCH, WINDOW, 128), lambda b, n: (b, n, COL_SK // 128)),
            pl.BlockSpec((SWA_BATCH, WINDOW, 128),
                         lambda b, n: (b, jnp.maximum(n - 1, 0), COL_SK // 128)),
            pl.BlockSpec((SWA_BATCH, WINDOW, 128), lambda b, n: (b, n, COL_SV // 128)),
            pl.BlockSpec((SWA_BATCH, WINDOW, 128),
                         lambda b, n: (b, jnp.maximum(n - 1, 0), COL_SV // 128)),
            pl.BlockSpec((2, SWA_HEADS, WINDOW, 2 * WINDOW), lambda b, n: (0, 0, 0, 0)),
        ],
        out_specs=pl.BlockSpec((SWA_BATCH, WINDOW, 512), lambda b, n: (b, n, 0)),
        out_shape=jax.ShapeDtypeStruct((bsz, seq, SWA_HEADS * HEAD_DIM), BF16),
        compiler_params=pltpu.CompilerParams(
            dimension_semantics=("arbitrary", "arbitrary"), vmem_limit_bytes=VMEM_LIMIT),
        name="swa",
    )(sinks, proj3, proj3, proj3, proj3, proj3, bias_nat)


ROUTE_OFF = N_GROUPS
ROUTE_ROWS = 48


def _split_bf16(x):
    hi = x.astype(BF16)
    lo = (x - hi.astype(F32)).astype(BF16)
    return hi, lo


def _outproj_router_kernel(h_ref, og_ref, od_ref, os_ref, wo_ref, gffn_ref, wr_hi_ref, wr_lo_ref,
                           br_ref, h1_ref, xn_ref, ri_ref, rf_ref, cnt_ref, run_ref):
    tm = h_ref.shape[0]

    @pl.when(pl.program_id(0) == 0)
    def _():
        run_ref[...] = jnp.zeros_like(run_ref)

    o = jnp.concatenate([og_ref[...], od_ref[...], os_ref[...]], axis=-1)
    h1 = h_ref[...] + _mm(o, wo_ref[...])
    h1_ref[...] = h1
    xn = _rms(h1, gffn_ref[...])
    xn_ref[...] = _pack_bf16_pairs(xn)

    x_hi, x_lo = _split_bf16(xn)
    lg = (_mm(x_hi, wr_hi_ref[...]) + _mm(x_lo, wr_hi_ref[...]) + _mm(x_hi, wr_lo_ref[...])
          + br_ref[...]).T[:ROUTE_ROWS, :]

    row = lax.broadcasted_iota(jnp.int32, lg.shape, 0)
    row_f = row.astype(F32)
    ninf = -jnp.inf

    def first_max(x):
        m = jnp.max(x, axis=0, keepdims=True)
        idx = jnp.min(jnp.where(x == m, row_f, float(ROUTE_ROWS)), axis=0, keepdims=True)
        return m, idx

    gl = jnp.where(row < N_GROUPS, lg, ninf)
    gmax, gsel = first_max(gl)
    g_w = 1.0 / jnp.sum(jnp.exp(gl - gmax), axis=0, keepdims=True)
    e_lo = ROUTE_OFF + EXPERTS_PER_GROUP * gsel
    el = jnp.where((row_f >= e_lo) & (row_f < e_lo + EXPERTS_PER_GROUP), lg, ninf)
    m1, i1 = first_max(el)
    eden = jnp.sum(jnp.exp(el - m1), axis=0, keepdims=True)
    m2, i2 = first_max(jnp.where(row_f == i1, ninf, el))
    p1 = 1.0 / eden
    p2 = jnp.exp(m2 - m1) / eden
    gate1 = g_w * p1 / (p1 + p2)
    gate2 = g_w * p2 / (p1 + p2)

    onehot = jnp.where((row_f == i1) | (row_f == i2), 1.0, 0.0)
    rr = lax.broadcasted_iota(jnp.int32, (tm, tm), 0)
    cc = lax.broadcasted_iota(jnp.int32, (tm, tm), 1)
    earlier = jnp.where(rr < cc, 1.0, 0.0).astype(BF16)
    run = run_ref[...]
    before = _mm(onehot.astype(BF16), earlier) + jnp.concatenate([run] * (tm // LANES), axis=1)
    rank1 = jnp.sum(jnp.where(row_f == i1, before, 0.0), axis=0, keepdims=True)
    rank2 = jnp.sum(jnp.where(row_f == i2, before, 0.0), axis=0, keepdims=True)
    run_ref[...] = run + jnp.sum(onehot, axis=1, keepdims=True)
    cnt_ref[...] = run_ref[...]

    out_row = lax.broadcasted_iota(jnp.int32, (SUBLANES, tm), 0)
    ints = jnp.where(out_row == 0, i1 - ROUTE_OFF,
                     jnp.where(out_row == 1, i2 - ROUTE_OFF,
                               jnp.where(out_row == 2, rank1,
                                         jnp.where(out_row == 3, rank2, 0.0))))
    ri_ref[...] = ints.astype(jnp.int32)
    rf_ref[...] = jnp.where(out_row == 0, gate1, jnp.where(out_row == 1, gate2, 0.0))


def _outproj_router(h, og, od, os_, wo, gffn, wr_hi, wr_lo, br):
    t = h.shape[0]
    tm = ROUTER_TILE
    row = lambda w: pl.BlockSpec((tm, w), lambda i: (i, 0))
    col = lambda r: pl.BlockSpec((r, tm), lambda i: (0, i))
    full = lambda a, b: pl.BlockSpec((a, b), lambda i: (0, 0))
    return pl.pallas_call(
        _outproj_router_kernel,
        grid=(t // tm,),
        in_specs=[row(D_MODEL), row(256), row(256), row(512), full(D_MODEL, D_MODEL),
                  full(1, D_MODEL), full(D_MODEL, LANES), full(D_MODEL, LANES), full(1, LANES)],
        out_specs=[row(D_MODEL), row(HALF_MODEL), col(SUBLANES), col(SUBLANES),
                   full(ROUTE_ROWS, LANES)],
        out_shape=[
            jax.ShapeDtypeStruct((t, D_MODEL), F32),
            jax.ShapeDtypeStruct((t, HALF_MODEL), jnp.uint32),
            jax.ShapeDtypeStruct((SUBLANES, t), jnp.int32),
            jax.ShapeDtypeStruct((SUBLANES, t), F32),
            jax.ShapeDtypeStruct((ROUTE_ROWS, LANES), F32),
        ],
        scratch_shapes=[pltpu.VMEM((ROUTE_ROWS, LANES), F32)],
        compiler_params=pltpu.CompilerParams(
            dimension_semantics=("arbitrary",), vmem_limit_bytes=VMEM_LIMIT),
        name="outproj_router",
    )(h, og, od, os_, wo, gffn, wr_hi, wr_lo, br)


def _dispatch_kernel(pos_ref, xn_ref, buf_in_ref, buf_ref, sem):
    del buf_in_ref
    tm = xn_ref.shape[0]

    def row_copy(r, k):
        dst = pos_ref[0, 0, 2 * r + k]
        return pltpu.make_async_copy(xn_ref.at[pl.ds(r, 1)], buf_ref.at[pl.ds(dst, 1)], sem)

    def issue(g, carry):
        for u in range(DMA_UNROLL):
            row_copy(g * DMA_UNROLL + u, 0).start()
            row_copy(g * DMA_UNROLL + u, 1).start()
        return carry

    def drain(g, carry):
        for u in range(DMA_UNROLL):
            row_copy(g * DMA_UNROLL + u, 0).wait()
            row_copy(g * DMA_UNROLL + u, 1).wait()
        return carry

    lax.fori_loop(0, tm // DMA_UNROLL, issue, 0)
    lax.fori_loop(0, tm // DMA_UNROLL, drain, 0)


def _dispatch(pos3, xn, buf0):
    t = xn.shape[0]
    tm = DISPATCH_TILE
    pos3 = pos3.reshape(t // tm, 1, 2 * tm)
    return pl.pallas_call(
        _dispatch_kernel,
        grid=(t // tm,),
        in_specs=[
            pl.BlockSpec((1, 1, 2 * tm), lambda i: (i, 0, 0), memory_space=pltpu.SMEM),
            pl.BlockSpec((tm, HALF_MODEL), lambda i: (i, 0)),
            pl.BlockSpec(memory_space=pl.ANY),
        ],
        out_specs=pl.BlockSpec(memory_space=pl.ANY),
        out_shape=jax.ShapeDtypeStruct(buf0.shape, buf0.dtype),
        scratch_shapes=[pltpu.SemaphoreType.DMA(())],
        input_output_aliases={2: 0},
        compiler_params=pltpu.CompilerParams(
            dimension_semantics=("arbitrary",), vmem_limit_bytes=VMEM_LIMIT),
        name="dispatch",
    )(pos3, xn, buf0)


def _expert_kernel(be_ref, nu_ref, rows_ref, slot_ref, next_ref, x_ref, wg_hbm, wu_hbm, wd_hbm,
                   y_ref, wg_f, wu_f, wd_f, wg_s, wu_s, wd_s, sems, *, layer):
    i = pl.program_id(0)
    used = i < nu_ref[0]
    expert = be_ref[i]
    new_expert = (i == 0) | (expert != be_ref[jnp.maximum(i - 1, 0)])
    slot = slot_ref[i]

    def weight_copies(e, s):
        return (pltpu.make_async_copy(wg_hbm.at[layer, e], wg_f.at[s], sems.at[s, 0]),
                pltpu.make_async_copy(wu_hbm.at[layer, e], wu_f.at[s], sems.at[s, 1]),
                pltpu.make_async_copy(wd_hbm.at[layer, e], wd_f.at[s], sems.at[s, 2]))

    @pl.when(i == 0)
    def _():
        for cp in weight_copies(expert, slot):
            cp.start()

    @pl.when(used & new_expert)
    def _():
        for cp in weight_copies(expert, slot):
            cp.wait()
        wg_s[...] = wg_f[slot].astype(BF16)
        wu_s[...] = wu_f[slot].astype(BF16)
        wd_s[...] = wd_f[slot].astype(BF16)
        nxt = next_ref[i]

        @pl.when(nxt >= 0)
        def _():
            for cp in weight_copies(nxt, 1 - slot):
                cp.start()

    def ffn(nrows):
        x_lo, x_hi = (part.astype(BF16) for part in _unpack_bf16_pairs(x_ref[:nrows, :]))
        g = _mm(x_lo, wg_s[:HALF_MODEL, :]) + _mm(x_hi, wg_s[HALF_MODEL:, :])
        u = _mm(x_lo, wu_s[:HALF_MODEL, :]) + _mm(x_hi, wu_s[HALF_MODEL:, :])
        hmid = (g * jax.nn.sigmoid(g)) * u
        y_ref[:nrows, :] = _pack_bf16_pairs(_mm(hmid.astype(BF16), wd_s[...]))

    half_rows = EXPERT_ROWS // 2
    short = rows_ref[i] <= half_rows

    @pl.when(used & jnp.logical_not(short))
    def _():
        ffn(EXPERT_ROWS)

    @pl.when(used & short)
    def _():
        ffn(half_rows)
        y_ref[half_rows:, :] = jnp.zeros((EXPERT_ROWS - half_rows, HALF_MODEL), y_ref.dtype)

    @pl.when(i >= nu_ref[0])
    def _():
        y_ref[...] = jnp.zeros_like(y_ref)


def _experts(blk_expert, n_used, blk_rows, buf, wg, wu, wd, layer):
    nrows = buf.shape[0]
    nblk = nrows // EXPERT_ROWS
    idx = jnp.arange(nblk, dtype=jnp.int32)
    change = (idx == 0) | (blk_expert != jnp.roll(blk_expert, 1))
    slot = (jnp.cumsum(change.astype(jnp.int32)) - 1) % 2
    later_run = (idx[None, :] > idx[:, None]) & change[None, :] & (idx[None, :] < n_used[0])
    nxt = jnp.where(jnp.any(later_run, axis=1), blk_expert[jnp.argmax(later_run, axis=1)], -1)
    block = lambda i, *_: (i, 0)
    any_space = pl.BlockSpec(memory_space=pl.ANY)
    return pl.pallas_call(
        functools.partial(_expert_kernel, layer=layer),
        grid_spec=pltpu.PrefetchScalarGridSpec(
            num_scalar_prefetch=5,
            grid=(nblk,),
            in_specs=[pl.BlockSpec((EXPERT_ROWS, HALF_MODEL), block),
                      any_space, any_space, any_space],
            out_specs=pl.BlockSpec((EXPERT_ROWS, HALF_MODEL), block),
            scratch_shapes=[
                pltpu.VMEM((2, D_MODEL, D_EXPERT), F32),
                pltpu.VMEM((2, D_MODEL, D_EXPERT), F32),
                pltpu.VMEM((2, D_EXPERT, D_MODEL), F32),
                pltpu.VMEM((D_MODEL, D_EXPERT), BF16),
                pltpu.VMEM((D_MODEL, D_EXPERT), BF16),
                pltpu.VMEM((D_EXPERT, D_MODEL), BF16),
                pltpu.SemaphoreType.DMA((2, 3)),
            ],
        ),
        out_shape=jax.ShapeDtypeStruct((nrows, HALF_MODEL), jnp.uint32),
        compiler_params=pltpu.CompilerParams(
            dimension_semantics=("arbitrary",), vmem_limit_bytes=VMEM_LIMIT),
        name="experts",
    )(blk_expert, n_used, blk_rows, slot.astype(jnp.int32), nxt.astype(jnp.int32),
      buf, wg, wu, wd)


def _combine_ple_kernel(pos_ref, pos_next_ref, h1_ref, p_ref, rf_ref, yb_ref, wple_ref, gple_ref,
                        wpg_ref, gfin_ref, o_ref, ybuf, sems, *, final_norm):
    tm = h1_ref.shape[0]
    i = pl.program_id(0)
    last = pl.num_programs(0) - 1
    slot = lax.rem(i, 2)

    def row_copy(tile_pos_ref, s, r, k):
        src = tile_pos_ref[0, 0, 2 * r + k]
        return pltpu.make_async_copy(
            yb_ref.at[pl.ds(src, 1)], ybuf.at[s, k, pl.ds(r, 1)], sems.at[s])

    def looped(tile_pos_ref, s, op):
        def body(g, carry):
            for u in range(DMA_UNROLL):
                op(row_copy(tile_pos_ref, s, g * DMA_UNROLL + u, 0))
                op(row_copy(tile_pos_ref, s, g * DMA_UNROLL + u, 1))
            return carry
        lax.fori_loop(0, tm // DMA_UNROLL, body, 0)

    @pl.when(i == 0)
    def _():
        looped(pos_ref, 0, lambda cp: cp.start())

    looped(pos_ref, slot, lambda cp: cp.wait())
    for r in range(tm):
        row_copy(pos_next_ref, 1 - slot, r, 0).start()
        row_copy(pos_next_ref, 1 - slot, r, 1).start()
    e = _rms(_mm(p_ref[...].astype(BF16), wple_ref[...]), gple_ref[...])

    rf = rf_ref[...]
    y0_lo, y0_hi = _unpack_bf16_pairs(ybuf[slot, 0])
    y1_lo, y1_hi = _unpack_bf16_pairs(ybuf[slot, 1])
    g0, g1 = rf[:, 0:1], rf[:, 1:2]
    moe = jnp.concatenate([y0_lo * g0 + y1_lo * g1, y0_hi * g0 + y1_hi * g1], axis=-1)
    h2 = h1_ref[...] + moe
    h3 = h2 + e * jax.nn.sigmoid(_mm(h2.astype(BF16), wpg_ref[...]))
    if final_norm:
        h3 = _rms(h3, gfin_ref[...])
    o_ref[...] = h3

    @pl.when(i == last)
    def _():
        looped(pos_next_ref, 1 - slot, lambda cp: cp.wait())


def _combine_ple(pos3, h1, p, rf, yb, wple, gple, wpg, gfin, layer, final_norm):
    t = h1.shape[0]
    tm = TOKEN_TILE
    row = lambda w: pl.BlockSpec((tm, w), lambda i: (i, 0))
    full = lambda a, b: pl.BlockSpec((a, b), lambda i: (0, 0))
    return pl.pallas_call(
        functools.partial(_combine_ple_kernel, final_norm=final_norm),
        grid=(t // tm,),
        in_specs=[
            pl.BlockSpec((1, 1, 2 * tm), lambda i: (i, 0, 0), memory_space=pltpu.SMEM),
            pl.BlockSpec((1, 1, 2 * tm), lambda i: (jnp.minimum(i + 1, t // tm - 1), 0, 0),
                         memory_space=pltpu.SMEM),
            row(D_MODEL),
            pl.BlockSpec((tm, PLE_DIM), lambda i: (layer * (t // tm) + i, 0)),
            row(LANES),
            pl.BlockSpec(memory_space=pl.ANY),
            full(PLE_DIM, D_MODEL), full(1, D_MODEL), full(D_MODEL, D_MODEL), full(1, D_MODEL),
        ],
        out_specs=row(D_MODEL),
        out_shape=jax.ShapeDtypeStruct((t, D_MODEL), F32),
        scratch_shapes=[pltpu.VMEM((2, 2, tm, HALF_MODEL), jnp.uint32),
                        pltpu.SemaphoreType.DMA((2,))],
        compiler_params=pltpu.CompilerParams(
            dimension_semantics=("arbitrary",), vmem_limit_bytes=VMEM_LIMIT),
        name="combine_ple",
    )(pos3, pos3, h1, p, rf, yb, wple, gple, wpg, gfin)


def _rel_bucket(dist):
    n = jnp.maximum(dist, 0)
    max_exact = REL_BUCKETS // 2
    nf = jnp.maximum(n, 1).astype(F32)
    large = max_exact + (jnp.log(nf / max_exact) / math.log(REL_MAX_DIST / max_exact)
                         * (REL_BUCKETS - max_exact)).astype(jnp.int32)
    large = jnp.minimum(large, REL_BUCKETS - 1)
    return jnp.where(n < max_exact, n, large)


def _bias_tiles(rel_bias):
    qi = jnp.arange(WINDOW)[:, None]
    kj = jnp.arange(2 * WINDOW)[None, :]
    bucket = _rel_bucket(qi + WINDOW - kj)
    onehot = (bucket[..., None] == jnp.arange(REL_BUCKETS)).astype(F32)
    nat = jnp.einsum('qkb,bh->hqk', onehot, rel_bias, precision=HIGHEST)
    dsa = nat[:DSA_HEADS]
    far = jnp.broadcast_to(rel_bias[REL_BUCKETS - 1, :DSA_HEADS][:, None, None],
                           (DSA_HEADS, WINDOW, WINDOW))
    dist = qi + WINDOW - kj
    in_window = (dist >= 0) & (dist < WINDOW)
    swa = nat[DSA_HEADS:]
    swa_tiles = jnp.stack([jnp.where(in_window, swa, -jnp.inf),
                           jnp.where(in_window & (kj >= WINDOW), swa, -jnp.inf)], axis=0)
    return jnp.stack([far, dsa[:, :, :WINDOW], dsa[:, :, WINDOW:]], axis=1), swa_tiles


def _pack_w_in(w):
    sizes = (128, 128, 256, 256, 16, 256, 64, 64, 512, 64, 8, 512, 128, 128)
    offs = [0]
    for s in sizes:
        offs.append(offs[-1] + s)
    wt = w.T
    gq, gk, gv, gg, ga, dq, dk, dv, iq, ik, iw, sq, sk, sv = (
        wt[offs[n]:offs[n + 1]] for n in range(len(sizes)))
    z = lambda n: jnp.zeros((n, w.shape[0]), w.dtype)
    packed = jnp.concatenate(
        [iq, sq, gv, gg, dq, gq, gk, ga, z(128 - GLA_RANK), dk, dv, ik, iw,
         z(128 - IDX_DIM - IDX_HEADS), sk, sv], axis=0)
    return packed.astype(BF16)


def kernel(x, p, rel_bias, g_mix, w_in, gla_w_alpha, gla_b_alpha, gla_g_norm, swa_sinks, w_out,
           g_ffn, w_router_group, b_router_group, w_router_expert, b_router_expert, w_expert_gate,
           w_expert_up, w_expert_down, w_ple, g_ple, w_ple_gate, g_final):
    bsz, seq, d = x.shape
    depth = w_in.shape[0]
    t = bsz * seq
    assert d == D_MODEL and t % DISPATCH_TILE == 0 and seq % GLA_BLOCK == 0 and seq % KC == 0
    n_blocks = -(-(2 * t) // EXPERT_ROWS) + N_EXPERTS
    dsa_tiles, swa_bias = _bias_tiles(rel_bias)

    h = x.reshape(t, d)
    for i in range(depth):
        proj = _inproj(h, g_mix[i][None, :], _pack_w_in(w_in[i]))
        proj3 = proj.reshape(bsz, seq, D_PROJ)

        wal = jnp.zeros((128, 128), F32).at[:GLA_RANK].set(gla_w_alpha[i])
        og = _gla(proj3, wal, gla_b_alpha[i][None, :],
                  jnp.tile(gla_g_norm[i], GLA_HEADS)[None, :])
        od = _dsa(proj3, dsa_tiles)
        os_ = _swa(proj3, swa_sinks[i], swa_bias)

        w_r = jnp.concatenate([w_router_group[i], w_router_expert[i]], axis=1)
        w_r = jnp.pad(w_r, ((0, 0), (0, LANES - w_r.shape[1])))
        wr_hi, wr_lo = _split_bf16(w_r)
        b_r = jnp.pad(jnp.concatenate([b_router_group[i], b_router_expert[i]]),
                      (0, LANES - N_GROUPS - N_EXPERTS))[None, :]
        h1, xn, ri, rf, cnt = _outproj_router(
            h, og.reshape(t, -1), od.reshape(t, -1), os_.reshape(t, -1), w_out[i].astype(BF16),
            g_ffn[i][None, :], wr_hi, wr_lo, b_r)

        counts = cnt[ROUTE_OFF:ROUTE_OFF + N_EXPERTS, 0].astype(jnp.int32)
        padded = (counts + EXPERT_ROWS - 1) // EXPERT_ROWS * EXPERT_ROWS
        pad_end = jnp.cumsum(padded)
        pad_start = pad_end - padded
        expert_ids = jnp.arange(N_EXPERTS, dtype=jnp.int32)
        slot_start = jnp.sum(
            jnp.where(ri[0:2, :, None] == expert_ids, pad_start, 0), axis=-1)
        pos3 = (slot_start + ri[2:4]).T.reshape(t // TOKEN_TILE, 1, 2 * TOKEN_TILE)
        gates = jnp.pad(rf[0:2].T, ((0, 0), (0, LANES - 2)))
        blk_start = jnp.arange(n_blocks, dtype=jnp.int32) * EXPERT_ROWS
        blk_expert = jnp.minimum(
            jnp.sum((pad_end[None, :] <= blk_start[:, None]).astype(jnp.int32), axis=1),
            N_EXPERTS - 1)
        n_used = (pad_end[-1:] // EXPERT_ROWS).astype(jnp.int32)
        onehot_e = blk_expert[:, None] == expert_ids[None, :]
        rows_before = blk_start - jnp.sum(jnp.where(onehot_e, pad_start[None, :], 0), axis=1)
        blk_rows = jnp.clip(jnp.sum(jnp.where(onehot_e, counts[None, :], 0), axis=1) - rows_before,
                            0, EXPERT_ROWS).astype(jnp.int32)

        buf = _dispatch(pos3, xn, jnp.zeros((n_blocks * EXPERT_ROWS, HALF_MODEL), jnp.uint32))
        yb = _experts(blk_expert, n_used, blk_rows, buf, w_expert_gate, w_expert_up,
                      w_expert_down, i)
        h = _combine_ple(pos3, h1, p.reshape(depth * t, PLE_DIM), gates, yb, w_ple[i].astype(BF16),
                         g_ple[i][None, :], w_ple_gate[i].astype(BF16), g_final[None, :],
                         layer=i, final_norm=(i == depth - 1))
    return h.reshape(bsz, seq, d)
```

```python
import functools
import math

import jax
import jax.numpy as jnp
from jax import lax
from jax.experimental import pallas as pl
from jax.experimental.pallas import tpu as pltpu

F32 = jnp.float32
BF16 = jnp.bfloat16
HIGHEST = lax.Precision.HIGHEST

D_MODEL = 1024
HEAD_DIM = 64
GLA_HEADS = 4
GLA_DK = 32
GLA_DV = 64
GLA_RANK = 16
GLA_TAU = 16.0
GLA_CHUNK = 64
DSA_HEADS = 4
IDX_HEADS = 8
IDX_DIM = 64
DSA_TOPK_MAX = 256
SWA_HEADS = 8
SWA_KV_HEADS = 2
SWA_GROUP = SWA_HEADS // SWA_KV_HEADS
WINDOW = 128
REL_BUCKETS = 32
REL_MAX_DIST = 128
N_GROUPS = 4
EXPERTS_PER_GROUP = 8
N_EXPERTS = N_GROUPS * EXPERTS_PER_GROUP
D_EXPERT = 512
PLE_DIM = 256
EPS = 1e-6

LANES = 128
SUBLANES = 8
PACK16 = 16

COL_IQ, COL_SQ = 0, 512
COL_GV, COL_GG, COL_DQ = 1024, 1280, 1536
COL_GQ, COL_GK, COL_GA, COL_DKV, COL_IKW, COL_SK, COL_SV = 1792, 1920, 2048, 2176, 2304, 2432, 2560
D_PROJ = 2688

TOKEN_TILE = 512
INPROJ_TILE = 512
ROUTER_TILE = 512
DISPATCH_TILE = 1024
EXPERT_ROWS = 512
GLA_BLOCK = 256
GLA_BATCH = 4
SWA_BATCH = 2
QB = 256
KC = 256
DMA_UNROLL = 8
NEG_BIG = -1e30
LOG2_E = math.log2(math.e)
VMEM_LIMIT = 48 * 1024 * 1024


def _nt(a, b, precision=None):
    return lax.dot_general(a, b, (((1,), (1,)), ((), ())), precision=precision,
                           preferred_element_type=F32)


def _mm(a, b, precision=None):
    return jnp.dot(a, b, precision=precision, preferred_element_type=F32)


def _eye(n, dtype):
    r = lax.broadcasted_iota(jnp.int32, (n, n), 0)
    c = lax.broadcasted_iota(jnp.int32, (n, n), 1)
    return jnp.where(r == c, 1.0, 0.0).astype(dtype)


def _rms(x, g):
    return x * lax.rsqrt(jnp.mean(x * x, axis=-1, keepdims=True) + EPS) * g


HALF_MODEL = D_MODEL // 2
HIGH16 = 0xFFFF0000


def _pack_bf16_pairs(x):
    bits = pltpu.bitcast(x.astype(BF16).astype(F32), jnp.uint32)
    return (bits[:, HALF_MODEL:] & jnp.uint32(HIGH16)) | (bits[:, :HALF_MODEL] >> 16)


def _unpack_bf16_pairs(w):
    return pltpu.bitcast(w << 16, F32), pltpu.bitcast(w & jnp.uint32(HIGH16), F32)


def _inproj_kernel(h_ref, g_ref, w_ref, o_ref):
    a = _rms(h_ref[...], g_ref[...])
    o_ref[...] = _nt(a.astype(BF16), w_ref[...])


def _inproj(h, g, w):
    t = h.shape[0]
    return pl.pallas_call(
        _inproj_kernel,
        grid=(t // INPROJ_TILE,),
        in_specs=[
            pl.BlockSpec((INPROJ_TILE, D_MODEL), lambda i: (i, 0)),
            pl.BlockSpec((1, D_MODEL), lambda i: (0, 0)),
            pl.BlockSpec((D_PROJ, D_MODEL), lambda i: (0, 0)),
        ],
        out_specs=pl.BlockSpec((INPROJ_TILE, D_PROJ), lambda i: (i, 0)),
        out_shape=jax.ShapeDtypeStruct((t, D_PROJ), F32),
        compiler_params=pltpu.CompilerParams(
            dimension_semantics=("arbitrary",), vmem_limit_bytes=VMEM_LIMIT),
        name="inproj",
    )(h, g, w)


def _gla_kernel(q_ref, k_ref, v_ref, gg_ref, ga_ref, wal_ref, bal_ref, gn_ref, o_ref,
                state_ref, sc_all_ref, p_all_ref, b_ref):
    @pl.when(pl.program_id(1) == 0)
    def _():
        state_ref[...] = jnp.zeros_like(state_ref)

    @pl.when((pl.program_id(0) == 0) & (pl.program_id(1) == 0))
    def _():
        p_all_ref[...] = jnp.zeros_like(p_all_ref)

    chunk_fns = []
    for bi in range(GLA_BATCH):
        one = pl.ds(bi, 1)
        chunk_fns.append(_gla_sequence(
            q_ref.at[one], k_ref.at[one], v_ref.at[one], gg_ref.at[one], ga_ref.at[one],
            wal_ref, bal_ref, gn_ref, o_ref.at[one], state_ref.at[bi],
            sc_all_ref.at[bi], p_all_ref.at[bi], b_ref.at[bi]))
    for ci in range(GLA_BLOCK // GLA_CHUNK):
        for chunk in chunk_fns:
            chunk(ci)


def _gla_sequence(q_ref, k_ref, v_ref, gg_ref, ga_ref, wal_ref, bal_ref, gn_ref, o_ref,
                  state_ref, sc_all_ref, p_all_ref, b_ref):
    hk = GLA_HEADS * GLA_DK
    hv = GLA_HEADS * GLA_DV
    c = GLA_CHUNK
    half_c = c // 2
    assert 2 * c == LANES

    pj = lax.broadcasted_iota(jnp.int32, (half_c, LANES), 0)
    pl_ = lax.broadcasted_iota(jnp.int32, (half_c, LANES), 1)
    pair_causal = jnp.where(pl_ < c, pl_, pl_ - c) <= jnp.where(pl_ < c, pj, pj + half_c)
    rv = lax.broadcasted_iota(jnp.int32, (hv, hk), 0) // GLA_DV
    ck = lax.broadcasted_iota(jnp.int32, (hv, hk), 1) // GLA_DK
    blockdiag_t = jnp.where(rv == ck, 1.0, 0.0).astype(F32)
    hr = lax.broadcasted_iota(jnp.int32, (SUBLANES, hk), 0)
    hl = lax.broadcasted_iota(jnp.int32, (SUBLANES, hk), 1) // GLA_DK
    head_rows = jnp.where(hr == hl, 1.0, 0.0).astype(BF16)
    eye_v = _eye(hv, BF16)

    ga_hi, ga_lo = _split_bf16(ga_ref[0])
    w_hi, w_lo = _split_bf16(wal_ref[...])
    z = _mm(ga_hi, w_hi) + _mm(ga_hi, w_lo) + _mm(ga_lo, w_hi) + bal_ref[...]
    log_a = (jnp.minimum(z, 0.0) - jnp.log1p(jnp.exp(-jnp.abs(z)))) * (1.0 / GLA_TAU)
    la_hi = log_a.astype(BF16)
    la_r1 = log_a - la_hi.astype(F32)
    la_mid = la_r1.astype(BF16)
    la_lo = (la_r1 - la_mid.astype(F32)).astype(BF16)
    rb = lax.broadcasted_iota(jnp.int32, (GLA_BLOCK, GLA_BLOCK), 0)
    cb = lax.broadcasted_iota(jnp.int32, (GLA_BLOCK, GLA_BLOCK), 1)
    tril = jnp.where((rb >= cb) & (rb // c == cb // c), 1.0, 0.0).astype(BF16)
    b_ref[...] = _mm(tril, la_hi) + _mm(tril, la_mid) + _mm(tril, la_lo)

    def chunk(ci):
        rows = slice(ci * c, (ci + 1) * c)
        p_ref, sc_ref = p_all_ref.at[ci], sc_all_ref.at[ci]
        q = q_ref[0, rows, :] * (GLA_DK ** -0.5)
        k = k_ref[0, rows, :]
        v = v_ref[0, rows, :]
        b = b_ref[rows, :]
        state_t = state_ref[...]
        o_inter = _nt((q * jnp.exp(b)).astype(BF16), state_t.astype(BF16))

        b2 = b * LOG2_E
        for t in range(c):
            ns = PACK16 * (t // PACK16 + 1)
            slot = 2 * (t % half_c) + t // half_c
            p = jnp.exp2(b2[t:t + 1, :] - b2[:ns, :]) * k[:ns, :] * q[t:t + 1, :]
            p_ref[slot * c:slot * c + ns, :] = p.astype(BF16)
        rows_hs = _nt(head_rows, p_ref[...])
        for jj in range(half_c):
            for h in range(GLA_HEADS):
                sc_ref[h, jj:jj + 1, :] = rows_hs[h:h + 1, jj * LANES:(jj + 1) * LANES]
        vb = v.astype(BF16)
        zero_v = jnp.zeros((c, GLA_DV), BF16)
        o_heads = []
        for h in range(GLA_HEADS):
            vh = vb[:, h * GLA_DV:(h + 1) * GLA_DV]
            v_pair = jnp.concatenate([jnp.concatenate([vh, zero_v], axis=1),
                                      jnp.concatenate([zero_v, vh], axis=1)], axis=0)
            scores = jnp.where(pair_causal, sc_ref[h], 0.0).astype(BF16)
            o_pair = _mm(scores, v_pair)
            o_heads.append(jnp.concatenate([o_pair[:, :GLA_DV], o_pair[:, GLA_DV:]], axis=0))
        o = o_inter + jnp.concatenate(o_heads, axis=-1)

        b_last = b[c - 1:c, :]
        kd = (k * jnp.exp(b_last - b)).astype(BF16)
        v_t = _nt(eye_v, vb).astype(BF16)
        upd_t = _mm(v_t, kd)
        state_ref[...] = jnp.exp(b_last) * state_t + upd_t * blockdiag_t

        gg = gg_ref[0, rows, :]
        outs = []
        for h in range(GLA_HEADS):
            oh = o[:, h * GLA_DV:(h + 1) * GLA_DV]
            ms = jnp.mean(oh * oh, axis=-1, keepdims=True)
            outs.append(oh * lax.rsqrt(ms + EPS))
        on = jnp.concatenate(outs, axis=-1) * gn_ref[...]
        o_ref[0, rows, :] = (on * (gg * jax.nn.sigmoid(gg))).astype(o_ref.dtype)

    return chunk


def _gla(proj3, wal, bal, gn):
    bsz, seq, _ = proj3.shape
    nb = seq // GLA_BLOCK

    def col(width, off):
        return pl.BlockSpec((GLA_BATCH, GLA_BLOCK, width), lambda b, i: (b, i, off // width))

    per_seq = lambda *shape: (GLA_BATCH,) + shape
    return pl.pallas_call(
        _gla_kernel,
        grid=(bsz // GLA_BATCH, nb),
        in_specs=[
            col(128, COL_GQ), col(128, COL_GK), col(256, COL_GV), col(256, COL_GG), col(128, COL_GA),
            pl.BlockSpec((128, 128), lambda b, i: (0, 0)),
            pl.BlockSpec((1, 128), lambda b, i: (0, 0)),
            pl.BlockSpec((1, 256), lambda b, i: (0, 0)),
        ],
        out_specs=pl.BlockSpec((GLA_BATCH, GLA_BLOCK, 256), lambda b, i: (b, i, 0)),
        out_shape=jax.ShapeDtypeStruct((bsz, seq, GLA_HEADS * GLA_DV), BF16),
        scratch_shapes=[
            pltpu.VMEM(per_seq(GLA_HEADS * GLA_DV, GLA_HEADS * GLA_DK), F32),
            pltpu.VMEM(per_seq(GLA_BLOCK // GLA_CHUNK, GLA_HEADS, GLA_CHUNK // 2, 2 * GLA_CHUNK),
                       F32),
            pltpu.VMEM(per_seq(GLA_BLOCK // GLA_CHUNK, GLA_CHUNK * GLA_CHUNK,
                               GLA_HEADS * GLA_DK), BF16),
            pltpu.VMEM(per_seq(GLA_BLOCK, GLA_HEADS * GLA_DK), F32),
        ],
        compiler_params=pltpu.CompilerParams(
            dimension_semantics=("arbitrary", "arbitrary"), vmem_limit_bytes=VMEM_LIMIT),
        name="gla",
    )(proj3, proj3, proj3, proj3, proj3, wal, bal, gn)


DSA_BISECT_STEPS = 14
DSA_ACCUMULATORS = 2
DSA_WALK_UNCONDITIONAL = 1
DSA_PV_PARTS = 4


def _dsa_kernel(dq_ref, kv_ref, iq_ref, ikw_ref, btile_ref, o_ref,
                sc_ref, lg_ref, p_ref, kb_ref, vb_ref, mp_ref, lp_ref, acc_ref, cut_ref, *, n_sel):
    j = pl.program_id(1)
    nkc = ((j + 1) * QB + KC - 1) // KC
    seq = kv_ref.shape[1]
    ksel = float(n_sel)

    @pl.when(j == 0)
    def _():
        kv = kv_ref[0]
        kb_ref[...] = kv[:, :HEAD_DIM].astype(BF16)
        vb_ref[...] = kv[:, HEAD_DIM:].astype(BF16)

    def rows(c):
        return pl.ds(pl.multiple_of(c * KC, KC), KC)

    wide = DSA_ACCUMULATORS * SUBLANES

    def fold(op, x):
        return op(x.reshape(KC // wide, wide, QB), axis=0)

    def all8(op, xw, roll_op):
        x8 = op(xw.reshape(DSA_ACCUMULATORS, SUBLANES, QB), axis=0)
        for shift in (4, 2, 1):
            x8 = roll_op(x8, pltpu.roll(x8, shift, 0))
        return x8

    def widen(x8):
        return jnp.concatenate([x8] * DSA_ACCUMULATORS, axis=0)

    zeros8 = jnp.zeros((SUBLANES, QB), F32)
    zerosw = jnp.zeros((wide, QB), F32)
    infw = jnp.full((wide, QB), jnp.inf, F32)

    ikw_q = ikw_ref[0, pl.ds(pl.multiple_of(j * QB, QB), QB), :]
    w_t = ikw_q.T[IDX_DIM:IDX_DIM + IDX_HEADS, :] * (IDX_HEADS ** -0.5 * IDX_DIM ** -0.5)
    iq = iq_ref[0].astype(BF16)
    iq_heads = [iq[:, h * IDX_DIM:(h + 1) * IDX_DIM] for h in range(IDX_HEADS)]

    s_loc = lax.broadcasted_iota(jnp.int32, (KC, QB), 0)
    t_glob = j * QB + lax.broadcasted_iota(jnp.int32, (KC, QB), 1)
    s_loc_f = s_loc.astype(F32)

    def score_chunk(c, st):
        mn8, mx8, ge8, gt8 = st
        ik = ikw_ref[0, rows(c), :][:, :IDX_DIM].astype(BF16)
        acc = jnp.zeros((KC, QB), F32)
        for h in range(IDX_HEADS):
            rel = _nt(ik, iq_heads[h])
            acc = acc + jnp.maximum(rel, 0.0) * w_t[h:h + 1, :]
        adm = s_loc + c * KC <= t_glob
        blk = jnp.where(adm, acc, -jnp.inf)
        sc_ref[rows(c), :] = blk
        mn8 = jnp.minimum(mn8, fold(jnp.min, jnp.where(adm, acc, jnp.inf)))
        mx8 = jnp.maximum(mx8, fold(jnp.max, blk))
        ge8 = ge8 + fold(jnp.sum, jnp.where(blk >= 0.0, 1.0, 0.0))
        gt8 = gt8 + fold(jnp.sum, jnp.where(blk > 0.0, 1.0, 0.0))
        return mn8, mx8, ge8, gt8

    def paired(body, init):
        def two(i, st):
            return body(2 * i + 1, body(2 * i, st))
        st = lax.fori_loop(0, nkc // 2, two, init)
        return lax.cond(nkc % 2 == 1, lambda s: body(nkc - 1, s), lambda s: s, st)

    mn8, mx8, ge8, gt8 = paired(score_chunk, (infw, -infw, zerosw, zerosw))
    row_min = all8(jnp.min, mn8, jnp.minimum)
    row_max = all8(jnp.max, mx8, jnp.maximum)
    c_ge0, c_gt0 = all8(jnp.sum, ge8, jnp.add), all8(jnp.sum, gt8, jnp.add)

    def blocks(c):
        return sc_ref[rows(c), :].reshape(KC // wide, wide, QB)

    def count_ge(x8):
        xw = widen(x8)[None]

        def body(c, cw):
            return cw + jnp.sum(jnp.where(blocks(c) >= xw, 1.0, 0.0), axis=0)
        return all8(jnp.sum, paired(body, zerosw), jnp.add)

    def min_ge(x8):
        xw = widen(x8)[None]

        def body(c, mw):
            blk = blocks(c)
            return jnp.minimum(mw, jnp.min(jnp.where(blk >= xw, blk, jnp.inf), axis=0))
        return all8(jnp.min, paired(body, infw), jnp.minimum)

    def pass_gt(x8):
        xw = widen(x8)[None]

        def body(c, st):
            cw, mw = st
            blk = blocks(c)
            hit = blk > xw
            return (cw + jnp.sum(jnp.where(hit, 1.0, 0.0), axis=0),
                    jnp.minimum(mw, jnp.min(jnp.where(hit, blk, jnp.inf), axis=0)))
        cw, mw = paired(body, (zerosw, infw))
        return all8(jnp.sum, cw, jnp.add), all8(jnp.min, mw, jnp.minimum)

    n_adm = (j * QB + lax.broadcasted_iota(jnp.int32, (SUBLANES, QB), 1) + 1).astype(F32)
    at_zero = (c_gt0 < ksel) & (c_ge0 >= ksel)
    above = c_gt0 >= ksel
    lo = jnp.where(above | at_zero, 0.0, row_min)
    c_lo = jnp.where(above | at_zero, c_ge0, n_adm)
    settled = at_zero | (n_adm <= ksel)
    hi = jnp.where(settled, lo, jnp.where(above, row_max, 0.0))

    def bisect(_, st):
        lo, hi, c_lo = st
        mid = lo + (hi - lo) * 0.5
        cnt = count_ge(mid)
        up = cnt >= ksel
        return jnp.where(up, mid, lo), jnp.where(up, hi, mid), jnp.where(up, cnt, c_lo)

    lo, _, c_lo = lax.fori_loop(0, DSA_BISECT_STEPS, bisect, (lo, hi, c_lo))

    v0 = min_ge(lo)
    done0 = jnp.where(settled | (c_lo == ksel), 1.0, 0.0)
    c_gt_init = jnp.where(at_zero, c_gt0, 0.0)

    def walk_cond(st):
        return jnp.min(st[3]) < 0.5

    def walk_body(st):
        v, c_ge, c_gt, dn = st
        cnt, vnext = pass_gt(v)
        live = dn < 0.5
        fin = live & (cnt < ksel)
        step = live & (cnt >= ksel)
        c_gt = jnp.where(fin, cnt, c_gt)
        v = jnp.where(step, vnext, v)
        c_ge = jnp.where(step, cnt, c_ge)
        dn = jnp.where(fin | (step & (cnt == ksel)), 1.0, dn)
        return v, c_ge, c_gt, dn

    walk = (v0, c_lo, c_gt_init, done0)
    for _ in range(DSA_WALK_UNCONDITIONAL):
        walk = walk_body(walk)
    tau8, c_ge, c_gt, _ = lax.while_loop(walk_cond, walk_body, walk)

    need = ksel - c_gt
    cut_ref[...] = jnp.full(cut_ref.shape, float(seq), F32)

    @pl.when(jnp.max(c_ge) > ksel)
    def _():
        s_grp = lax.broadcasted_iota(jnp.int32, (KC // wide, wide, QB), 0) * wide
        s_sub = lax.broadcasted_iota(jnp.int32, (KC // wide, wide, QB), 1)
        s_idx = (s_grp + s_sub).astype(F32)
        tauw = widen(tau8)[None]

        def count_ties_below(m8):
            mw = widen(m8)[None]

            def body(c, cw):
                hit = (blocks(c) == tauw) & (s_idx + (c * KC).astype(F32) < mw)
                return cw + jnp.sum(jnp.where(hit, 1.0, 0.0), axis=0)
            return all8(jnp.sum, lax.fori_loop(0, nkc, body, zerosw), jnp.add)

        def idx_bisect(_, lh):
            lo_m, hi_m = lh
            mid = jnp.floor((lo_m + hi_m) * 0.5)
            ok = count_ties_below(mid) >= need
            return jnp.where(ok, lo_m, mid), jnp.where(ok, mid, hi_m)

        _, hi_m = lax.fori_loop(0, int(math.log2(seq)) + 1, idx_bisect,
                                (zeros8, zeros8 + float(seq)))
        cut_ref[...] = jnp.where(c_ge > ksel, hi_m, float(seq))

    tau = tau8[0:1, :]
    cut = cut_ref[0:1, :]

    q = (dq_ref[0] * (HEAD_DIM ** -0.5)).astype(BF16)
    q4 = jnp.concatenate([q[:, h * HEAD_DIM:(h + 1) * HEAD_DIM] for h in range(DSA_HEADS)],
                         axis=0)
    mp_ref[...] = jnp.full(mp_ref.shape, NEG_BIG, F32)
    lp_ref[...] = jnp.zeros(lp_ref.shape, F32)

    pv_parts = min(DSA_PV_PARTS, seq // KC)
    part = seq // pv_parts

    def cols(c):
        return pl.ds(pl.multiple_of(c * KC, KC), KC)

    def lane_fold(op, x):
        out = x[:, :LANES]
        for g in range(1, KC // LANES):
            out = op(out, x[:, g * LANES:(g + 1) * LANES])
        return out

    def logits_chunk(c, carry):
        blk = sc_ref[rows(c), :]
        sidx = s_loc_f + (c * KC).astype(F32)
        sel = (blk > tau) | ((blk == tau) & (sidx < cut))
        neg_t = jnp.where(sel, 0.0, NEG_BIG).T
        lg4 = _nt(q4, kb_ref[rows(c), :])
        which = [[jnp.clip((c * (KC // WINDOW) + b) - (j * (QB // WINDOW) + a) + 2, 0, 2)
                  for b in range(KC // WINDOW)] for a in range(QB // WINDOW)]
        for h in range(DSA_HEADS):
            hq = slice(h * QB, (h + 1) * QB)
            bias = jnp.concatenate(
                [jnp.concatenate([btile_ref[h, w] for w in which_a], axis=1)
                 for which_a in which], axis=0)
            lg = lg4[hq, :] + bias + neg_t
            lg_ref[hq, cols(c)] = lg
            mp_ref[h] = jnp.maximum(mp_ref[h], lane_fold(jnp.maximum, lg))
        return carry

    paired(logits_chunk, 0)
    m_rows = [jnp.max(mp_ref[h], axis=-1, keepdims=True) for h in range(DSA_HEADS)]

    def probs_chunk(c, carry):
        for h in range(DSA_HEADS):
            hq = slice(h * QB, (h + 1) * QB)
            p = jnp.exp(lg_ref[hq, cols(c)] - m_rows[h])
            lp_ref[h] = lp_ref[h] + lane_fold(jnp.add, p)
            p_ref[hq, cols(c)] = p.astype(BF16)
        return carry

    def zero_chunk(c, carry):
        p_ref[:, cols(c)] = jnp.zeros((DSA_HEADS * QB, KC), BF16)
        return carry

    lax.fori_loop(0, nkc, probs_chunk, 0)
    n_parts = ((j + 1) * QB + part - 1) // part
    lax.fori_loop(nkc, n_parts * (part // KC), zero_chunk, 0)
    acc_ref[...] = _mm(p_ref[:, :part], vb_ref[:part, :])
    for k in range(1, pv_parts):
        @pl.when(k < n_parts)
        def _(k=k):
            acc_ref[...] = acc_ref[...] + _mm(p_ref[:, k * part:(k + 1) * part],
                                              vb_ref[k * part:(k + 1) * part, :])

    outs = []
    for h in range(DSA_HEADS):
        l = jnp.sum(lp_ref[h], axis=-1, keepdims=True)
        outs.append(acc_ref[h * QB:(h + 1) * QB, :] / l)
    o_ref[0] = jnp.concatenate(outs, axis=-1).astype(o_ref.dtype)


def _dsa(proj3, btiles):
    bsz, seq, _ = proj3.shape
    n_sel = min(DSA_TOPK_MAX, seq // 4)
    return pl.pallas_call(
        functools.partial(_dsa_kernel, n_sel=n_sel),
        grid=(bsz, seq // QB),
        in_specs=[
            pl.BlockSpec((1, QB, 256), lambda b, j: (b, j, COL_DQ // 256)),
            pl.BlockSpec((1, seq, 128), lambda b, j: (b, 0, COL_DKV // 128)),
            pl.BlockSpec((1, QB, 512), lambda b, j: (b, j, COL_IQ // 512)),
            pl.BlockSpec((1, seq, 128), lambda b, j: (b, 0, COL_IKW // 128)),
            pl.BlockSpec((DSA_HEADS, 3, WINDOW, WINDOW), lambda b, j: (0, 0, 0, 0)),
        ],
        out_specs=pl.BlockSpec((1, QB, 256), lambda b, j: (b, j, 0)),
        out_shape=jax.ShapeDtypeStruct((bsz, seq, DSA_HEADS * HEAD_DIM), BF16),
        scratch_shapes=[
            pltpu.VMEM((seq, QB), F32),
            pltpu.VMEM((DSA_HEADS * QB, seq), F32),
            pltpu.VMEM((DSA_HEADS * QB, seq), BF16),
            pltpu.VMEM((seq, HEAD_DIM), BF16),
            pltpu.VMEM((seq, HEAD_DIM), BF16),
            pltpu.VMEM((DSA_HEADS, QB, LANES), F32),
            pltpu.VMEM((DSA_HEADS, QB, LANES), F32),
            pltpu.VMEM((DSA_HEADS * QB, HEAD_DIM), F32),
            pltpu.VMEM((SUBLANES, QB), F32),
        ],
        compiler_params=pltpu.CompilerParams(
            dimension_semantics=("arbitrary", "arbitrary"), vmem_limit_bytes=VMEM_LIMIT),
        name="dsa",
    )(proj3, proj3, proj3, proj3, btiles)


def _swa_kernel(sink_ref, q_ref, kc_ref, kp_ref, vc_ref, vp_ref, bias_ref, o_ref):
    n = pl.program_id(1)
    nseq = q_ref.shape[0]
    q = [(q_ref[s] * (HEAD_DIM ** -0.5)).astype(BF16) for s in range(nseq)]
    k2 = [jnp.concatenate([kp_ref[s], kc_ref[s]], axis=0).astype(BF16) for s in range(nseq)]
    v2 = [jnp.concatenate([vp_ref[s], vc_ref[s]], axis=0).astype(BF16) for s in range(nseq)]
    ones = jnp.ones((2 * WINDOW, HEAD_DIM), BF16)
    first = jnp.where(n == 0, 1, 0)
    outs = [[] for _ in range(nseq)]
    for h in range(SWA_HEADS):
        kvh = h // SWA_GROUP
        sink = sink_ref[h]
        for s in range(nseq):
            kh = k2[s][:, kvh * HEAD_DIM:(kvh + 1) * HEAD_DIM]
            v_ones = jnp.concatenate([v2[s][:, kvh * HEAD_DIM:(kvh + 1) * HEAD_DIM], ones], axis=1)
            lg = _nt(q[s][:, h * HEAD_DIM:(h + 1) * HEAD_DIM], kh) + bias_ref[first, h]
            m = jnp.maximum(jnp.max(lg, axis=-1, keepdims=True), sink)
            num_den = _mm(jnp.exp(lg - m).astype(BF16), v_ones)
            outs[s].append(num_den[:, :HEAD_DIM] / (num_den[:, HEAD_DIM:] + jnp.exp(sink - m)))
    for s in range(nseq):
        o_ref[s] = jnp.concatenate(outs[s], axis=-1).astype(o_ref.dtype)


def _swa(proj3, sinks, bias_nat):
    bsz, seq, _ = proj3.shape
    return pl.pallas_call(
        _swa_kernel,
        grid=(bsz // SWA_BATCH, seq // WINDOW),
        in_specs=[
            pl.BlockSpec(memory_space=pltpu.SMEM),
            pl.BlockSpec((SWA_BATCH, WINDOW, 512), lambda b, n: (b, n, COL_SQ // 512)),
            pl.BlockSpec((SWA_BATCH, WINDOW, 128), lambda b, n: (b, n, COL_SK // 128)),
            pl.BlockSpec((SWA_BATCH, WINDOW, 128),
                         lambda b, n: (b, jnp.maximum(n - 1, 0), COL_SK // 128)),
            pl.BlockSpec((SWA_BATCH, WINDOW, 128), lambda b, n: (b, n, COL_SV // 128)),
            pl.BlockSpec((SWA_BATCH, WINDOW, 128),
                         lambda b, n: (b, jnp.maximum(n - 1, 0), COL_SV // 128)),
            pl.BlockSpec((2, SWA_HEADS, WINDOW, 2 * WINDOW), lambda b, n: (0, 0, 0, 0)),
        ],
        out_specs=pl.BlockSpec((SWA_BATCH, WINDOW, 512), lambda b, n: (b, n, 0)),
        out_shape=jax.ShapeDtypeStruct((bsz, seq, SWA_HEADS * HEAD_DIM), BF16),
        compiler_params=pltpu.CompilerParams(
            dimension_semantics=("arbitrary", "arbitrary"), vmem_limit_bytes=VMEM_LIMIT),
        name="swa",
    )(sinks, proj3, proj3, proj3, proj3, proj3, bias_nat)


ROUTE_OFF = N_GROUPS
ROUTE_ROWS = 48


def _split_bf16(x):
    hi = x.astype(BF16)
    lo = (x - hi.astype(F32)).astype(BF16)
    return hi, lo


def _outproj_router_kernel(h_ref, og_ref, od_ref, os_ref, wo_ref, gffn_ref, wr_hi_ref, wr_lo_ref,
                           br_ref, h1_ref, xn_ref, ri_ref, rf_ref, cnt_ref, run_ref):
    tm = h_ref.shape[0]

    @pl.when(pl.program_id(0) == 0)
    def _():
        run_ref[...] = jnp.zeros_like(run_ref)

    o = jnp.concatenate([og_ref[...], od_ref[...], os_ref[...]], axis=-1)
    h1 = h_ref[...] + _mm(o, wo_ref[...])
    h1_ref[...] = h1
    xn = _rms(h1, gffn_ref[...])
    xn_ref[...] = _pack_bf16_pairs(xn)

    x_hi, x_lo = _split_bf16(xn)
    lg = (_mm(x_hi, wr_hi_ref[...]) + _mm(x_lo, wr_hi_ref[...]) + _mm(x_hi, wr_lo_ref[...])
          + br_ref[...]).T[:ROUTE_ROWS, :]

    row = lax.broadcasted_iota(jnp.int32, lg.shape, 0)
    row_f = row.astype(F32)
    ninf = -jnp.inf

    def first_max(x):
        m = jnp.max(x, axis=0, keepdims=True)
        idx = jnp.min(jnp.where(x == m, row_f, float(ROUTE_ROWS)), axis=0, keepdims=True)
        return m, idx

    gl = jnp.where(row < N_GROUPS, lg, ninf)
    gmax, gsel = first_max(gl)
    g_w = 1.0 / jnp.sum(jnp.exp(gl - gmax), axis=0, keepdims=True)
    e_lo = ROUTE_OFF + EXPERTS_PER_GROUP * gsel
    el = jnp.where((row_f >= e_lo) & (row_f < e_lo + EXPERTS_PER_GROUP), lg, ninf)
    m1, i1 = first_max(el)
    eden = jnp.sum(jnp.exp(el - m1), axis=0, keepdims=True)
    m2, i2 = first_max(jnp.where(row_f == i1, ninf, el))
    p1 = 1.0 / eden
    p2 = jnp.exp(m2 - m1) / eden
    gate1 = g_w * p1 / (p1 + p2)
    gate2 = g_w * p2 / (p1 + p2)

    onehot = jnp.where((row_f == i1) | (row_f == i2), 1.0, 0.0)
    rr = lax.broadcasted_iota(jnp.int32, (tm, tm), 0)
    cc = lax.broadcasted_iota(jnp.int32, (tm, tm), 1)
    earlier = jnp.where(rr < cc, 1.0, 0.0).astype(BF16)
    run = run_ref[...]
    before = _mm(onehot.astype(BF16), earlier) + jnp.concatenate([run] * (tm // LANES), axis=1)
    rank1 = jnp.sum(jnp.where(row_f == i1, before, 0.0), axis=0, keepdims=True)
    rank2 = jnp.sum(jnp.where(row_f == i2, before, 0.0), axis=0, keepdims=True)
    run_ref[...] = run + jnp.sum(onehot, axis=1, keepdims=True)
    cnt_ref[...] = run_ref[...]

    out_row = lax.broadcasted_iota(jnp.int32, (SUBLANES, tm), 0)
    ints = jnp.where(out_row == 0, i1 - ROUTE_OFF,
                     jnp.where(out_row == 1, i2 - ROUTE_OFF,
                               jnp.where(out_row == 2, rank1,
                                         jnp.where(out_row == 3, rank2, 0.0))))
    ri_ref[...] = ints.astype(jnp.int32)
    rf_ref[...] = jnp.where(out_row == 0, gate1, jnp.where(out_row == 1, gate2, 0.0))


def _outproj_router(h, og, od, os_, wo, gffn, wr_hi, wr_lo, br):
    t = h.shape[0]
    tm = ROUTER_TILE
    row = lambda w: pl.BlockSpec((tm, w), lambda i: (i, 0))
    col = lambda r: pl.BlockSpec((r, tm), lambda i: (0, i))
    full = lambda a, b: pl.BlockSpec((a, b), lambda i: (0, 0))
    return pl.pallas_call(
        _outproj_router_kernel,
        grid=(t // tm,),
        in_specs=[row(D_MODEL), row(256), row(256), row(512), full(D_MODEL, D_MODEL),
                  full(1, D_MODEL), full(D_MODEL, LANES), full(D_MODEL, LANES), full(1, LANES)],
        out_specs=[row(D_MODEL), row(HALF_MODEL), col(SUBLANES), col(SUBLANES),
                   full(ROUTE_ROWS, LANES)],
        out_shape=[
            jax.ShapeDtypeStruct((t, D_MODEL), F32),
            jax.ShapeDtypeStruct((t, HALF_MODEL), jnp.uint32),
            jax.ShapeDtypeStruct((SUBLANES, t), jnp.int32),
            jax.ShapeDtypeStruct((SUBLANES, t), F32),
            jax.ShapeDtypeStruct((ROUTE_ROWS, LANES), F32),
        ],
        scratch_shapes=[pltpu.VMEM((ROUTE_ROWS, LANES), F32)],
        compiler_params=pltpu.CompilerParams(
            dimension_semantics=("arbitrary",), vmem_limit_bytes=VMEM_LIMIT),
        name="outproj_router",
    )(h, og, od, os_, wo, gffn, wr_hi, wr_lo, br)


def _dispatch_kernel(pos_ref, xn_ref, buf_in_ref, buf_ref, sem):
    del buf_in_ref
    tm = xn_ref.shape[0]

    def row_copy(r, k):
        dst = pos_ref[0, 0, 2 * r + k]
        return pltpu.make_async_copy(xn_ref.at[pl.ds(r, 1)], buf_ref.at[pl.ds(dst, 1)], sem)

    def issue(g, carry):
        for u in range(DMA_UNROLL):
            row_copy(g * DMA_UNROLL + u, 0).start()
            row_copy(g * DMA_UNROLL + u, 1).start()
        return carry

    def drain(g, carry):
        for u in range(DMA_UNROLL):
            row_copy(g * DMA_UNROLL + u, 0).wait()
            row_copy(g * DMA_UNROLL + u, 1).wait()
        return carry

    lax.fori_loop(0, tm // DMA_UNROLL, issue, 0)
    lax.fori_loop(0, tm // DMA_UNROLL, drain, 0)


def _dispatch(pos3, xn, buf0):
    t = xn.shape[0]
    tm = DISPATCH_TILE
    pos3 = pos3.reshape(t // tm, 1, 2 * tm)
    return pl.pallas_call(
        _dispatch_kernel,
        grid=(t // tm,),
        in_specs=[
            pl.BlockSpec((1, 1, 2 * tm), lambda i: (i, 0, 0), memory_space=pltpu.SMEM),
            pl.BlockSpec((tm, HALF_MODEL), lambda i: (i, 0)),
            pl.BlockSpec(memory_space=pl.ANY),
        ],
        out_specs=pl.BlockSpec(memory_space=pl.ANY),
        out_shape=jax.ShapeDtypeStruct(buf0.shape, buf0.dtype),
        scratch_shapes=[pltpu.SemaphoreType.DMA(())],
        input_output_aliases={2: 0},
        compiler_params=pltpu.CompilerParams(
            dimension_semantics=("arbitrary",), vmem_limit_bytes=VMEM_LIMIT),
        name="dispatch",
    )(pos3, xn, buf0)


def _expert_kernel(be_ref, nu_ref, rows_ref, slot_ref, next_ref, x_ref, wg_hbm, wu_hbm, wd_hbm,
                   y_ref, wg_f, wu_f, wd_f, wg_s, wu_s, wd_s, sems, *, layer):
    i = pl.program_id(0)
    used = i < nu_ref[0]
    expert = be_ref[i]
    new_expert = (i == 0) | (expert != be_ref[jnp.maximum(i - 1, 0)])
    slot = slot_ref[i]

    def weight_copies(e, s):
        return (pltpu.make_async_copy(wg_hbm.at[layer, e], wg_f.at[s], sems.at[s, 0]),
                pltpu.make_async_copy(wu_hbm.at[layer, e], wu_f.at[s], sems.at[s, 1]),
                pltpu.make_async_copy(wd_hbm.at[layer, e], wd_f.at[s], sems.at[s, 2]))

    @pl.when(i == 0)
    def _():
        for cp in weight_copies(expert, slot):
            cp.start()

    @pl.when(used & new_expert)
    def _():
        for cp in weight_copies(expert, slot):
            cp.wait()
        wg_s[...] = wg_f[slot].astype(BF16)
        wu_s[...] = wu_f[slot].astype(BF16)
        wd_s[...] = wd_f[slot].astype(BF16)
        nxt = next_ref[i]

        @pl.when(nxt >= 0)
        def _():
            for cp in weight_copies(nxt, 1 - slot):
                cp.start()

    def ffn(nrows):
        x_lo, x_hi = (part.astype(BF16) for part in _unpack_bf16_pairs(x_ref[:nrows, :]))
        g = _mm(x_lo, wg_s[:HALF_MODEL, :]) + _mm(x_hi, wg_s[HALF_MODEL:, :])
        u = _mm(x_lo, wu_s[:HALF_MODEL, :]) + _mm(x_hi, wu_s[HALF_MODEL:, :])
        hmid = (g * jax.nn.sigmoid(g)) * u
        y_ref[:nrows, :] = _pack_bf16_pairs(_mm(hmid.astype(BF16), wd_s[...]))

    half_rows = EXPERT_ROWS // 2
    short = rows_ref[i] <= half_rows

    @pl.when(used & jnp.logical_not(short))
    def _():
        ffn(EXPERT_ROWS)

    @pl.when(used & short)
    def _():
        ffn(half_rows)
        y_ref[half_rows:, :] = jnp.zeros((EXPERT_ROWS - half_rows, HALF_MODEL), y_ref.dtype)

    @pl.when(i >= nu_ref[0])
    def _():
        y_ref[...] = jnp.zeros_like(y_ref)


def _experts(blk_expert, n_used, blk_rows, buf, wg, wu, wd, layer):
    nrows = buf.shape[0]
    nblk = nrows // EXPERT_ROWS
    idx = jnp.arange(nblk, dtype=jnp.int32)
    change = (idx == 0) | (blk_expert != jnp.roll(blk_expert, 1))
    slot = (jnp.cumsum(change.astype(jnp.int32)) - 1) % 2
    later_run = (idx[None, :] > idx[:, None]) & change[None, :] & (idx[None, :] < n_used[0])
    nxt = jnp.where(jnp.any(later_run, axis=1), blk_expert[jnp.argmax(later_run, axis=1)], -1)
    block = lambda i, *_: (i, 0)
    any_space = pl.BlockSpec(memory_space=pl.ANY)
    return pl.pallas_call(
        functools.partial(_expert_kernel, layer=layer),
        grid_spec=pltpu.PrefetchScalarGridSpec(
            num_scalar_prefetch=5,
            grid=(nblk,),
            in_specs=[pl.BlockSpec((EXPERT_ROWS, HALF_MODEL), block),
                      any_space, any_space, any_space],
            out_specs=pl.BlockSpec((EXPERT_ROWS, HALF_MODEL), block),
            scratch_shapes=[
                pltpu.VMEM((2, D_MODEL, D_EXPERT), F32),
                pltpu.VMEM((2, D_MODEL, D_EXPERT), F32),
                pltpu.VMEM((2, D_EXPERT, D_MODEL), F32),
                pltpu.VMEM((D_MODEL, D_EXPERT), BF16),
                pltpu.VMEM((D_MODEL, D_EXPERT), BF16),
                pltpu.VMEM((D_EXPERT, D_MODEL), BF16),
                pltpu.SemaphoreType.DMA((2, 3)),
            ],
        ),
        out_shape=jax.ShapeDtypeStruct((nrows, HALF_MODEL), jnp.uint32),
        compiler_params=pltpu.CompilerParams(
            dimension_semantics=("arbitrary",), vmem_limit_bytes=VMEM_LIMIT),
        name="experts",
    )(blk_expert, n_used, blk_rows, slot.astype(jnp.int32), nxt.astype(jnp.int32),
      buf, wg, wu, wd)


def _combine_ple_kernel(pos_ref, pos_next_ref, h1_ref, p_ref, rf_ref, yb_ref, wple_ref, gple_ref,
                        wpg_ref, gfin_ref, o_ref, ybuf, sems, *, final_norm):
    tm = h1_ref.shape[0]
    i = pl.program_id(0)
    last = pl.num_programs(0) - 1
    slot = lax.rem(i, 2)

    def row_copy(tile_pos_ref, s, r, k):
        src = tile_pos_ref[0, 0, 2 * r + k]
        return pltpu.make_async_copy(
            yb_ref.at[pl.ds(src, 1)], ybuf.at[s, k, pl.ds(r, 1)], sems.at[s])

    def looped(tile_pos_ref, s, op):
        def body(g, carry):
            for u in range(DMA_UNROLL):
                op(row_copy(tile_pos_ref, s, g * DMA_UNROLL + u, 0))
                op(row_copy(tile_pos_ref, s, g * DMA_UNROLL + u, 1))
            return carry
        lax.fori_loop(0, tm // DMA_UNROLL, body, 0)

    @pl.when(i == 0)
    def _():
        looped(pos_ref, 0, lambda cp: cp.start())

    looped(pos_ref, slot, lambda cp: cp.wait())
    for r in range(tm):
        row_copy(pos_next_ref, 1 - slot, r, 0).start()
        row_copy(pos_next_ref, 1 - slot, r, 1).start()
    e = _rms(_mm(p_ref[...].astype(BF16), wple_ref[...]), gple_ref[...])

    rf = rf_ref[...]
    y0_lo, y0_hi = _unpack_bf16_pairs(ybuf[slot, 0])
    y1_lo, y1_hi = _unpack_bf16_pairs(ybuf[slot, 1])
    g0, g1 = rf[:, 0:1], rf[:, 1:2]
    moe = jnp.concatenate([y0_lo * g0 + y1_lo * g1, y0_hi * g0 + y1_hi * g1], axis=-1)
    h2 = h1_ref[...] + moe
    h3 = h2 + e * jax.nn.sigmoid(_mm(h2.astype(BF16), wpg_ref[...]))
    if final_norm:
        h3 = _rms(h3, gfin_ref[...])
    o_ref[...] = h3

    @pl.when(i == last)
    def _():
        looped(pos_next_ref, 1 - slot, lambda cp: cp.wait())


def _combine_ple(pos3, h1, p, rf, yb, wple, gple, wpg, gfin, layer, final_norm):
    t = h1.shape[0]
    tm = TOKEN_TILE
    row = lambda w: pl.BlockSpec((tm, w), lambda i: (i, 0))
    full = lambda a, b: pl.BlockSpec((a, b), lambda i: (0, 0))
    return pl.pallas_call(
        functools.partial(_combine_ple_kernel, final_norm=final_norm),
        grid=(t // tm,),
        in_specs=[
            pl.BlockSpec((1, 1, 2 * tm), lambda i: (i, 0, 0), memory_space=pltpu.SMEM),
            pl.BlockSpec((1, 1, 2 * tm), lambda i: (jnp.minimum(i + 1, t // tm - 1), 0, 0),
                         memory_space=pltpu.SMEM),
            row(D_MODEL),
            pl.BlockSpec((tm, PLE_DIM), lambda i: (layer * (t // tm) + i, 0)),
            row(LANES),
            pl.BlockSpec(memory_space=pl.ANY),
            full(PLE_DIM, D_MODEL), full(1, D_MODEL), full(D_MODEL, D_MODEL), full(1, D_MODEL),
        ],
        out_specs=row(D_MODEL),
        out_shape=jax.ShapeDtypeStruct((t, D_MODEL), F32),
        scratch_shapes=[pltpu.VMEM((2, 2, tm, HALF_MODEL), jnp.uint32),
                        pltpu.SemaphoreType.DMA((2,))],
        compiler_params=pltpu.CompilerParams(
            dimension_semantics=("arbitrary",), vmem_limit_bytes=VMEM_LIMIT),
        name="combine_ple",
    )(pos3, pos3, h1, p, rf, yb, wple, gple, wpg, gfin)


def _rel_bucket(dist):
    n = jnp.maximum(dist, 0)
    max_exact = REL_BUCKETS // 2
    nf = jnp.maximum(n, 1).astype(F32)
    large = max_exact + (jnp.log(nf / max_exact) / math.log(REL_MAX_DIST / max_exact)
                         * (REL_BUCKETS - max_exact)).astype(jnp.int32)
    large = jnp.minimum(large, REL_BUCKETS - 1)
    return jnp.where(n < max_exact, n, large)


def _bias_tiles(rel_bias):
    qi = jnp.arange(WINDOW)[:, None]
    kj = jnp.arange(2 * WINDOW)[None, :]
    bucket = _rel_bucket(qi + WINDOW - kj)
    onehot = (bucket[..., None] == jnp.arange(REL_BUCKETS)).astype(F32)
    nat = jnp.einsum('qkb,bh->hqk', onehot, rel_bias, precision=HIGHEST)
    dsa = nat[:DSA_HEADS]
    far = jnp.broadcast_to(rel_bias[REL_BUCKETS - 1, :DSA_HEADS][:, None, None],
                           (DSA_HEADS, WINDOW, WINDOW))
    dist = qi + WINDOW - kj
    in_window = (dist >= 0) & (dist < WINDOW)
    swa = nat[DSA_HEADS:]
    swa_tiles = jnp.stack([jnp.where(in_window, swa, -jnp.inf),
                           jnp.where(in_window & (kj >= WINDOW), swa, -jnp.inf)], axis=0)
    return jnp.stack([far, dsa[:, :, :WINDOW], dsa[:, :, WINDOW:]], axis=1), swa_tiles


def _pack_w_in(w):
    sizes = (128, 128, 256, 256, 16, 256, 64, 64, 512, 64, 8, 512, 128, 128)
    offs = [0]
    for s in sizes:
        offs.append(offs[-1] + s)
    wt = w.T
    gq, gk, gv, gg, ga, dq, dk, dv, iq, ik, iw, sq, sk, sv = (
        wt[offs[n]:offs[n + 1]] for n in range(len(sizes)))
    z = lambda n: jnp.zeros((n, w.shape[0]), w.dtype)
    packed = jnp.concatenate(
        [iq, sq, gv, gg, dq, gq, gk, ga, z(128 - GLA_RANK), dk, dv, ik, iw,
         z(128 - IDX_DIM - IDX_HEADS), sk, sv], axis=0)
    return packed.astype(BF16)


def kernel(x, p, rel_bias, g_mix, w_in, gla_w_alpha, gla_b_alpha, gla_g_norm, swa_sinks, w_out,
           g_ffn, w_router_group, b_router_group, w_router_expert, b_router_expert, w_expert_gate,
           w_expert_up, w_expert_down, w_ple, g_ple, w_ple_gate, g_final):
    bsz, seq, d = x.shape
    depth = w_in.shape[0]
    t = bsz * seq
    assert d == D_MODEL and t % DISPATCH_TILE == 0 and seq % GLA_BLOCK == 0 and seq % KC == 0
    n_blocks = -(-(2 * t) // EXPERT_ROWS) + N_EXPERTS
    dsa_tiles, swa_bias = _bias_tiles(rel_bias)

    h = x.reshape(t, d)
    for i in range(depth):
        proj = _inproj(h, g_mix[i][None, :], _pack_w_in(w_in[i]))
        proj3 = proj.reshape(bsz, seq, D_PROJ)

        wal = jnp.zeros((128, 128), F32).at[:GLA_RANK].set(gla_w_alpha[i])
        og = _gla(proj3, wal, gla_b_alpha[i][None, :],
                  jnp.tile(gla_g_norm[i], GLA_HEADS)[None, :])
        od = _dsa(proj3, dsa_tiles)
        os_ = _swa(proj3, swa_sinks[i], swa_bias)

        w_r = jnp.concatenate([w_router_group[i], w_router_expert[i]], axis=1)
        w_r = jnp.pad(w_r, ((0, 0), (0, LANES - w_r.shape[1])))
        wr_hi, wr_lo = _split_bf16(w_r)
        b_r = jnp.pad(jnp.concatenate([b_router_group[i], b_router_expert[i]]),
                      (0, LANES - N_GROUPS - N_EXPERTS))[None, :]
        h1, xn, ri, rf, cnt = _outproj_router(
            h, og.reshape(t, -1), od.reshape(t, -1), os_.reshape(t, -1), w_out[i].astype(BF16),
            g_ffn[i][None, :], wr_hi, wr_lo, b_r)

        counts = cnt[ROUTE_OFF:ROUTE_OFF + N_EXPERTS, 0].astype(jnp.int32)
        padded = (counts + EXPERT_ROWS - 1) // EXPERT_ROWS * EXPERT_ROWS
        pad_end = jnp.cumsum(padded)
        pad_start = pad_end - padded
        expert_ids = jnp.arange(N_EXPERTS, dtype=jnp.int32)
        slot_start = jnp.sum(
            jnp.where(ri[0:2, :, None] == expert_ids, pad_start, 0), axis=-1)
        pos3 = (slot_start + ri[2:4]).T.reshape(t // TOKEN_TILE, 1, 2 * TOKEN_TILE)
        gates = jnp.pad(rf[0:2].T, ((0, 0), (0, LANES - 2)))
        blk_start = jnp.arange(n_blocks, dtype=jnp.int32) * EXPERT_ROWS
        blk_expert = jnp.minimum(
            jnp.sum((pad_end[None, :] <= blk_start[:, None]).astype(jnp.int32), axis=1),
            N_EXPERTS - 1)
        n_used = (pad_end[-1:] // EXPERT_ROWS).astype(jnp.int32)
        onehot_e = blk_expert[:, None] == expert_ids[None, :]
        rows_before = blk_start - jnp.sum(jnp.where(onehot_e, pad_start[None, :], 0), axis=1)
        blk_rows = jnp.clip(jnp.sum(jnp.where(onehot_e, counts[None, :], 0), axis=1) - rows_before,
                            0, EXPERT_ROWS).astype(jnp.int32)

        buf = _dispatch(pos3, xn, jnp.zeros((n_blocks * EXPERT_ROWS, HALF_MODEL), jnp.uint32))
        yb = _experts(blk_expert, n_used, blk_rows, buf, w_expert_gate, w_expert_up,
                      w_expert_down, i)
        h = _combine_ple(pos3, h1, p.reshape(depth * t, PLE_DIM), gates, yb, w_ple[i].astype(BF16),
                         g_ple[i][None, :], w_ple_gate[i].astype(BF16), g_final[None, :],
                         layer=i, final_norm=(i == depth - 1))
    return h.reshape(bsz, seq, d)
```
